```python
import math
import jax, jax.numpy as jnp
from jax import lax
import numpy as np

D_MODEL = 1024
BATCH = 8
SEQ = 2048
DEPTH = 2
DEC_BATCH = 128
DEC_SEQ = 1
PAST_LEN = 16384
PAGE_SIZE = 128

N_META = 16
POOL_WIDTH = D_MODEL // 2
POOL_WINDOWS = (2, 4, 8, 16)
POOL_GROUPS = len(POOL_WINDOWS)
POOL_GROUP_IN = POOL_WIDTH // POOL_GROUPS
POOL_GROUP_OUT = D_MODEL // POOL_GROUPS
POOL_HIST = max(POOL_WINDOWS) - 1
SSM_WIDTH = D_MODEL // 2
SSM_GROUP_CH = 16
SSM_GROUPS = SSM_WIDTH // SSM_GROUP_CH
SSM_STATE = 64
D_FF = 2816
CONV_W = 3
RMS_EPS = 1e-6
DT_MIN = 1e-3
DT_MAX = 1e-1
IN_COLS = POOL_WIDTH + SSM_WIDTH + 2 * D_MODEL

kernel_name = "hybrid_pool_s5_convffn_decode_step"


def rmsnorm(x, g):
    xf = x.astype(jnp.float32)
    y = xf * lax.rsqrt(jnp.mean(xf * xf, axis=-1, keepdims=True) + RMS_EPS)
    return (y * g.astype(jnp.float32)).astype(x.dtype)


def pool_mixer(u, hist, start_pos, w_grp, scale):
    L = u.shape[1]
    ext = jnp.concatenate([hist.astype(u.dtype), u], axis=1)
    extf = ext.astype(jnp.float32)
    csum = jnp.pad(jnp.cumsum(extf, axis=1), ((0, 0), (1, 0), (0, 0)))
    ends = csum[:, POOL_HIST + 1:]
    pos = start_pos + jnp.arange(L)
    outs = []
    for g, w in enumerate(POOL_WINDOWS):
        sl = slice(g * POOL_GROUP_IN, (g + 1) * POOL_GROUP_IN)
        begin = csum[:, POOL_HIST + 1 - w:POOL_HIST + 1 - w + L, sl]
        cnt = jnp.minimum(w, pos + 1).astype(jnp.float32)[None, :, None]
        d = (ends[..., sl] - begin) / cnt - extf[:, POOL_HIST:, sl]
        outs.append(jnp.einsum("blc,cd->bld", d, w_grp[g].astype(jnp.float32)))
    y = jnp.concatenate(outs, axis=-1) * scale.astype(jnp.float32)
    return y.astype(u.dtype), ext[:, -POOL_HIST:]


def ssm_mixer(u, h0_re, h0_im, A_re, A_im, log_dt, B_re, B_im, C_re, C_im, D_skip):
    Bsz, L, _ = u.shape
    f32 = jnp.float32
    uf = u.astype(f32).reshape(Bsz, L, SSM_GROUPS, SSM_GROUP_CH)
    dt = jnp.exp(log_dt.astype(f32))[:, None]
    lam_re, lam_im = A_re.astype(f32), A_im.astype(f32)
    mag = jnp.exp(lam_re * dt)
    ab_re = mag * jnp.cos(lam_im * dt)
    ab_im = mag * jnp.sin(lam_im * dt)
    den = lam_re * lam_re + lam_im * lam_im
    nr, ni = ab_re - 1.0, ab_im
    q_re = (nr * lam_re + ni * lam_im) / den
    q_im = (ni * lam_re - nr * lam_im) / den
    Br, Bi = B_re.astype(f32), B_im.astype(f32)
    bb_re = q_re[..., None] * Br - q_im[..., None] * Bi
    bb_im = q_re[..., None] * Bi + q_im[..., None] * Br
    bu_re = jnp.einsum("blgp,gnp->blgn", uf, bb_re)
    bu_im = jnp.einsum("blgp,gnp->blgn", uf, bb_im)
    h0r, h0i = h0_re.astype(f32), h0_im.astype(f32)
    bu_re = bu_re.at[:, 0].add(ab_re * h0r - ab_im * h0i)
    bu_im = bu_im.at[:, 0].add(ab_re * h0i + ab_im * h0r)
    a_re = jnp.broadcast_to(ab_re, bu_re.shape)
    a_im = jnp.broadcast_to(ab_im, bu_im.shape)

    def combine(left, right):
        a1r, a1i, b1r, b1i = left
        a2r, a2i, b2r, b2i = right
        return (a1r * a2r - a1i * a2i, a1r * a2i + a1i * a2r,
                a2r * b1r - a2i * b1i + b2r, a2r * b1i + a2i * b1r + b2i)

    _, _, h_re, h_im = lax.associative_scan(combine, (a_re, a_im, bu_re, bu_im), axis=1)
    y = (jnp.einsum("blgn,gpn->blgp", h_re, C_re.astype(f32))
         - jnp.einsum("blgn,gpn->blgp", h_im, C_im.astype(f32)))
    y = y.reshape(Bsz, L, SSM_WIDTH) + D_skip.astype(f32) * u.astype(f32)
    return y.astype(u.dtype), h_re[:, -1].astype(h0_re.dtype), h_im[:, -1].astype(h0_im.dtype)


def glu_out(y, w_glu):
    z = jax.nn.gelu(y)
    zz = z @ w_glu
    return zz[..., :D_MODEL] * jax.nn.sigmoid(zz[..., D_MODEL:])


def conv_ffn(h, hist, w_up, conv_w, conv_b, w_down):
    L = h.shape[1]
    up = h @ w_up
    g, v = up[..., :D_FF], up[..., D_FF:]
    ext = jnp.concatenate([hist.astype(g.dtype), g], axis=1)
    c = conv_b + conv_w[0] * ext[:, 0:L]
    for k in range(1, CONV_W):
        c = c + conv_w[k] * ext[:, k:k + L]
    out = (jax.nn.gelu(c) * v) @ w_down
    return out, ext[:, -(CONV_W - 1):]


def trunk(x, st_pool, st_re, st_im, st_conv, start_pos, norm1_g, w_in, pool_w, pool_scale,
          ssm_A_re, ssm_A_im, ssm_log_dt, ssm_B_re, ssm_B_im, ssm_C_re, ssm_C_im, ssm_D,
          w_glu, w_out, norm2_g, w_up, conv_w, conv_b, w_down, norm_f_g):
    new_pool, new_re, new_im, new_conv = [], [], [], []
    o_ssm = POOL_WIDTH
    o_gp = POOL_WIDTH + SSM_WIDTH
    o_gs = o_gp + D_MODEL
    for l in range(DEPTH):
        h = rmsnorm(x, norm1_g[l])
        proj = h @ w_in[l]
        u_pool = proj[..., :o_ssm]
        u_ssm = proj[..., o_ssm:o_gp]
        gate_pool = jax.nn.sigmoid(proj[..., o_gp:o_gs])
        gate_ssm = jax.nn.sigmoid(proj[..., o_gs:])
        y_pool, hp = pool_mixer(u_pool, st_pool[l], start_pos, pool_w[l], pool_scale[l])
        y_s, hr, hi = ssm_mixer(u_ssm, st_re[l], st_im[l], ssm_A_re[l], ssm_A_im[l], ssm_log_dt[l],
                                ssm_B_re[l], ssm_B_im[l], ssm_C_re[l], ssm_C_im[l], ssm_D[l])
        y_ssm = glu_out(y_s, w_glu[l])
        x = x + (gate_pool * y_pool + gate_ssm * y_ssm) @ w_out[l]
        h2 = rmsnorm(x, norm2_g[l])
        f, hc = conv_ffn(h2, st_conv[l], w_up[l], conv_w[l], conv_b[l], w_down[l])
        x = x + f
        new_pool.append(hp)
        new_re.append(hr)
        new_im.append(hi)
        new_conv.append(hc)
    y = rmsnorm(x, norm_f_g)
    return y, jnp.stack(new_pool), jnp.stack(new_re), jnp.stack(new_im), jnp.stack(new_conv)


def setup_inputs(seed: int = 0) -> dict:
    key = jax.random.key(seed)
    ks = jax.random.split(key, 32)
    f32 = jnp.float32
    nrm = lambda k, shape, s: jax.random.normal(k, shape, f32) * s
    n_idx = jnp.arange(SSM_STATE, dtype=f32)
    return {
        "x_prompt": nrm(ks[0], (BATCH, SEQ, D_MODEL), 1.0),
        "x_sample": nrm(ks[1], (DEC_BATCH, DEC_SEQ, D_MODEL), 1.0),
        "state_pool": nrm(ks[2], (DEPTH, DEC_BATCH, POOL_HIST, POOL_WIDTH), 1.0),
        "state_ssm_re": nrm(ks[3], (DEPTH, DEC_BATCH, SSM_GROUPS, SSM_STATE), 0.5),
        "state_ssm_im": nrm(ks[4], (DEPTH, DEC_BATCH, SSM_GROUPS, SSM_STATE), 0.5),
        "state_conv": nrm(ks[5], (DEPTH, DEC_BATCH, CONV_W - 1, D_FF), 1.0),
        "meta_tokens": nrm(ks[6], (N_META, D_MODEL), 1.0),
        "norm1_g": 1.0 + nrm(ks[7], (DEPTH, D_MODEL), 0.01),
        "w_in": nrm(ks[8], (DEPTH, D_MODEL, IN_COLS), D_MODEL ** -0.5),
        "pool_w": nrm(ks[9], (DEPTH, POOL_GROUPS, POOL_GROUP_IN, POOL_GROUP_OUT), POOL_GROUP_IN ** -0.5),
        "pool_scale": 1.0 + nrm(ks[10], (DEPTH, D_MODEL), 0.1),
        "ssm_A_re": -0.5 + nrm(ks[11], (DEPTH, SSM_GROUPS, SSM_STATE), 0.01),
        "ssm_A_im": math.pi * n_idx + nrm(ks[12], (DEPTH, SSM_GROUPS, SSM_STATE), 0.01),
        "ssm_log_dt": jax.random.uniform(ks[13], (DEPTH, SSM_GROUPS), f32,
                                         math.log(DT_MIN), math.log(DT_MAX)),
        "ssm_B_re": nrm(ks[14], (DEPTH, SSM_GROUPS, SSM_STATE, SSM_GROUP_CH), SSM_GROUP_CH ** -0.5),
        "ssm_B_im": nrm(ks[15], (DEPTH, SSM_GROUPS, SSM_STATE, SSM_GROUP_CH), SSM_GROUP_CH ** -0.5),
        "ssm_C_re": nrm(ks[16], (DEPTH, SSM_GROUPS, SSM_GROUP_CH, SSM_STATE), SSM_STATE ** -0.5),
        "ssm_C_im": nrm(ks[17], (DEPTH, SSM_GROUPS, SSM_GROUP_CH, SSM_STATE), SSM_STATE ** -0.5),
        "ssm_D": nrm(ks[18], (DEPTH, SSM_WIDTH), 1.0),
        "w_glu": nrm(ks[19], (DEPTH, SSM_WIDTH, 2 * D_MODEL), SSM_WIDTH ** -0.5),
        "w_out": nrm(ks[20], (DEPTH, D_MODEL, D_MODEL), D_MODEL ** -0.5),
        "norm2_g": 1.0 + nrm(ks[21], (DEPTH, D_MODEL), 0.01),
        "w_up": nrm(ks[22], (DEPTH, D_MODEL, 2 * D_FF), D_MODEL ** -0.5),
        "conv_w": nrm(ks[23], (DEPTH, CONV_W, D_FF), CONV_W ** -0.5),
        "conv_b": nrm(ks[24], (DEPTH, D_FF), 0.02),
        "w_down": nrm(ks[25], (DEPTH, D_FF, D_MODEL), D_FF ** -0.5),
        "norm_f_g": 1.0 + nrm(ks[26], (D_MODEL,), 0.01),
    }


def reference(x_prompt, x_sample, state_pool, state_ssm_re, state_ssm_im, state_conv, meta_tokens,
              norm1_g, w_in, pool_w, pool_scale, ssm_A_re, ssm_A_im, ssm_log_dt, ssm_B_re, ssm_B_im,
              ssm_C_re, ssm_C_im, ssm_D, w_glu, w_out, norm2_g, w_up, conv_w, conv_b, w_down, norm_f_g):
    dt = x_prompt.dtype
    bp = x_prompt.shape[0]
    meta = jnp.broadcast_to(meta_tokens.astype(dt)[None], (bp, N_META, D_MODEL))
    xp = jnp.concatenate([meta, x_prompt], axis=1)
    z_pool = jnp.zeros((DEPTH, bp, POOL_HIST, POOL_WIDTH), dt)
    z_ssm = jnp.zeros((DEPTH, bp, SSM_GROUPS, SSM_STATE), dt)
    z_conv = jnp.zeros((DEPTH, bp, CONV_W - 1, D_FF), dt)
    yp, pool_p, re_p, im_p, conv_p = trunk(
        xp, z_pool, z_ssm, z_ssm, z_conv, 0, norm1_g, w_in, pool_w, pool_scale,
        ssm_A_re, ssm_A_im, ssm_log_dt, ssm_B_re, ssm_B_im, ssm_C_re, ssm_C_im, ssm_D,
        w_glu, w_out, norm2_g, w_up, conv_w, conv_b, w_down, norm_f_g)
    y_prompt = yp[:, N_META:]
    y_sample, pool_s, re_s, im_s, conv_s = trunk(
        x_sample, state_pool, state_ssm_re, state_ssm_im, state_conv, PAST_LEN,
        norm1_g, w_in, pool_w, pool_scale,
        ssm_A_re, ssm_A_im, ssm_log_dt, ssm_B_re, ssm_B_im, ssm_C_re, ssm_C_im, ssm_D,
        w_glu, w_out, norm2_g, w_up, conv_w, conv_b, w_down, norm_f_g)
    return (y_prompt, y_sample, pool_p, re_p, im_p, conv_p, pool_s, re_s, im_s, conv_s)
```

```python
import functools
import math

import jax
import jax.numpy as jnp
from jax import lax
from jax.experimental import pallas as pl
from jax.experimental.pallas import tpu as pltpu

D_MODEL = 1024
N_META = 16
POOL_WIDTH = 512
POOL_WINDOWS = (2, 4, 8, 16)
POOL_GROUP_IN = 128
POOL_GROUP_OUT = 256
POOL_KEEP = 16
SSM_WIDTH = 512
SSM_GROUP_CH = 16
SSM_GROUPS = 32
SSM_STATE = 64
SSM_FLAT = SSM_GROUPS * SSM_STATE
SSM_TILE = 512
SSM_TILES = SSM_FLAT // SSM_TILE
SSM_TILE_CH = SSM_TILE // SSM_STATE * SSM_GROUP_CH
D_FF = 2816
FF_TILE = 256
FF_TILES = D_FF // FF_TILE
CONV_W = 3
RMS_EPS = 1e-6
PAST_LEN = 16384

V7X_VMEM_LIMIT_BYTES = 56 * 1024 * 1024

_F32 = jnp.float32
_BF16 = jnp.bfloat16


def _rmsnorm(x, g):
    return x * lax.rsqrt(jnp.mean(x * x, axis=-1, keepdims=True) + RMS_EPS) * g


def _dot(a, b):
    return jnp.dot(a, b, preferred_element_type=_F32)


def _const_spec(shape):
    nd = len(shape)
    return pl.BlockSpec(shape, lambda c: (0,) * nd, pipeline_mode=pl.Buffered(1))


def _disc_kernel(are_ref, aim_ref, ldt_ref, bre_ref, bim_ref,
                 abre_ref, abim_ref, bbre_ref, bbim_ref):
    lam_re = are_ref[...]
    lam_im = aim_ref[...]
    dt = jnp.exp(ldt_ref[...])
    mag = jnp.exp(lam_re * dt)
    ab_re = mag * jnp.cos(lam_im * dt)
    ab_im = mag * jnp.sin(lam_im * dt)
    den = lam_re * lam_re + lam_im * lam_im
    nr = ab_re - 1.0
    ni = ab_im
    q_re = (nr * lam_re + ni * lam_im) / den
    q_im = (ni * lam_re - nr * lam_im) / den
    b_re = bre_ref[...]
    b_im = bim_ref[...]
    abre_ref[...] = ab_re
    abim_ref[...] = ab_im
    bbre_ref[...] = q_re * b_re - q_im * b_im
    bbim_ref[...] = q_re * b_im + q_im * b_re


def _discretise(a_re, a_im, log_dt, b_re, b_im):
    depth = a_re.shape[0]
    flat = lambda a: a.reshape(depth, 1, SSM_FLAT)
    ldt = jnp.broadcast_to(log_dt[:, :, None], (depth, SSM_GROUPS, SSM_STATE))
    bt = lambda b: b.transpose(0, 3, 1, 2).reshape(depth, SSM_GROUP_CH, SSM_FLAT)
    vec = pl.BlockSpec((None, 1, SSM_FLAT), lambda l: (l, 0, 0))
    mat = pl.BlockSpec((None, SSM_GROUP_CH, SSM_FLAT), lambda l: (l, 0, 0))
    return pl.pallas_call(
        _disc_kernel,
        grid=(depth,),
        in_specs=[vec, vec, vec, mat, mat],
        out_specs=[vec, vec, mat, mat],
        out_shape=[jax.ShapeDtypeStruct((depth, 1, SSM_FLAT), _F32)] * 2
        + [jax.ShapeDtypeStruct((depth, SSM_GROUP_CH, SSM_FLAT), _F32)] * 2,
        name="s5_discretise",
    )(flat(a_re), flat(a_im), flat(ldt), bt(b_re), bt(b_im))


def _block_diag_b(bb):
    eye = jnp.eye(SSM_TILE // SSM_STATE, dtype=bb.dtype)
    t = bb.reshape(SSM_GROUP_CH, SSM_TILES, 8, SSM_STATE).transpose(1, 2, 0, 3)
    out = t[:, :, :, None, :] * eye[None, :, None, :, None]
    return out.reshape(SSM_TILES, SSM_TILE_CH, SSM_TILE)


def _block_diag_c(c):
    eye = jnp.eye(SSM_TILE // SSM_STATE, dtype=c.dtype)
    t = c.reshape(SSM_TILES, 8, SSM_GROUP_CH, SSM_STATE).transpose(0, 1, 3, 2)
    out = t[:, :, :, None, :] * eye[None, :, None, :, None]
    return out.reshape(SSM_TILES, SSM_TILE, SSM_TILE_CH)


def _mixer_kernel(x_ref, hist_ref, h0re_ref, h0im_ref, n1g_ref, win_ref, poolw_ref,
                  pscale_ref, are_ref, aim_ref, bbc_ref, cre_ref, cim_ref, dskip_ref,
                  wglu_ref, wout_ref,
                  xo_ref, poolst_ref, hre_o_ref, him_o_ref,
                  ext_ref, bu_ref, hst_ref, *, B, T, start_pos, n_chunks):
    M = B * T
    keep = POOL_KEEP * B
    c = pl.program_id(0)

    @pl.when(c == 0)
    def _():
        ext_ref[0:keep, :] = hist_ref[...]
        hst_ref[0] = h0re_ref[...]
        hst_ref[1] = h0im_ref[...]

    x = x_ref[...]
    hb = _rmsnorm(x, n1g_ref[...]).astype(_BF16)

    u = _dot(hb, win_ref[:, 0:POOL_WIDTH + SSM_WIDTH])
    u_pool = u[:, 0:POOL_WIDTH]
    u_ssm = u[:, POOL_WIDTH:]
    ext_ref[keep:keep + M, :] = u_pool
    u_ssm_b = u_ssm.astype(_BF16)

    y_tiles = []
    for i in range(SSM_TILES):
        sl = slice(i * SSM_TILE, (i + 1) * SSM_TILE)
        bu_ref[...] = _dot(u_ssm_b[:, i * SSM_TILE_CH:(i + 1) * SSM_TILE_CH], bbc_ref[i])
        a_re = jnp.broadcast_to(are_ref[:, sl], (B, SSM_TILE))
        a_im = jnp.broadcast_to(aim_ref[:, sl], (B, SSM_TILE))
        h_re0 = hst_ref[0, :, sl]
        h_im0 = hst_ref[1, :, sl]

        def step(rows, carry, a_re=a_re, a_im=a_im):
            h_re, h_im = carry
            n_re = a_re * h_re - a_im * h_im + bu_ref[rows, 0:SSM_TILE]
            n_im = a_re * h_im + a_im * h_re + bu_ref[rows, SSM_TILE:2 * SSM_TILE]
            bu_ref[rows, 0:SSM_TILE] = n_re
            bu_ref[rows, SSM_TILE:2 * SSM_TILE] = n_im
            return n_re, n_im

        if T == 1:
            h_re, h_im = step(slice(0, B), (h_re0, h_im0))
        else:
            h_re, h_im = lax.fori_loop(
                0, T, lambda t, carry: step(pl.ds(pl.multiple_of(t * B, B), B), carry),
                (h_re0, h_im0), unroll=2)
        hst_ref[0, :, sl] = h_re
        hst_ref[1, :, sl] = h_im
        y_tiles.append(_dot(bu_ref[:, 0:SSM_TILE].astype(_BF16), cre_ref[i])
                       - _dot(bu_ref[:, SSM_TILE:2 * SSM_TILE].astype(_BF16), cim_ref[i]))
    y_s = jnp.concatenate(y_tiles, axis=-1) + dskip_ref[...] * u_ssm
    zz = _dot(jax.nn.gelu(y_s).astype(_BF16), wglu_ref[...])
    y_ssm = zz[:, 0:D_MODEL] * jax.nn.sigmoid(zz[:, D_MODEL:])

    t_glob = (start_pos + c * T
              + lax.broadcasted_iota(jnp.int32, (M, 1), 0) // B)
    pool_tiles = []
    for g, w in enumerate(POOL_WINDOWS):
        ls = slice(g * POOL_GROUP_IN, (g + 1) * POOL_GROUP_IN)
        tok = ext_ref[keep:keep + M, ls]
        win_sum = tok
        for k in range(1, w):
            win_sum = win_sum + ext_ref[keep - k * B:keep - k * B + M, ls]
        cnt = jnp.minimum(w, t_glob + 1).astype(_F32)
        d = win_sum / cnt - tok
        pool_tiles.append(_dot(d.astype(_BF16), poolw_ref[g]))
    y_pool = jnp.concatenate(pool_tiles, axis=-1) * pscale_ref[...]

    gates = _dot(hb, win_ref[:, POOL_WIDTH + SSM_WIDTH:])
    merge = (jax.nn.sigmoid(gates[:, 0:D_MODEL]) * y_pool
             + jax.nn.sigmoid(gates[:, D_MODEL:]) * y_ssm)
    xo_ref[...] = x + _dot(merge.astype(_BF16), wout_ref[...])

    new_hist = ext_ref[M:M + keep, :]
    ext_ref[0:keep, :] = new_hist

    @pl.when(c == n_chunks - 1)
    def _():
        poolst_ref[...] = new_hist
        hre_o_ref[...] = hst_ref[0]
        him_o_ref[...] = hst_ref[1]


def _mixer(x, hist, h0re, h0im, lw, *, B, T, start_pos):
    m_tot = x.shape[0]
    M = B * T
    n_chunks = m_tot // M
    keep = POOL_KEEP * B
    kern = functools.partial(_mixer_kernel, B=B, T=T, start_pos=start_pos, n_chunks=n_chunks)
    consts = [lw["n1g"], lw["w_in"], lw["pool_w"], lw["pool_scale"], lw["a_re"], lw["a_im"],
              lw["bbc"], lw["cre"], lw["cim"], lw["dskip"], lw["w_glu"], lw["w_out"]]
    row_spec = pl.BlockSpec((M, D_MODEL), lambda c: (c, 0))
    return pl.pallas_call(
        kern,
        grid=(n_chunks,),
        in_specs=[row_spec, _const_spec(hist.shape), _const_spec(h0re.shape),
                  _const_spec(h0im.shape)] + [_const_spec(a.shape) for a in consts],
        out_specs=[row_spec, _const_spec((keep, POOL_WIDTH)),
                   _const_spec((B, SSM_FLAT)), _const_spec((B, SSM_FLAT))],
        out_shape=[jax.ShapeDtypeStruct((m_tot, D_MODEL), _F32),
                   jax.ShapeDtypeStruct((keep, POOL_WIDTH), _F32),
                   jax.ShapeDtypeStruct((B, SSM_FLAT), _F32),
                   jax.ShapeDtypeStruct((B, SSM_FLAT), _F32)],
        scratch_shapes=[pltpu.VMEM((keep + M, POOL_WIDTH), _F32),
                        pltpu.VMEM((M, 2 * SSM_TILE), _F32),
                        pltpu.VMEM((2, B, SSM_FLAT), _F32)],
        compiler_params=pltpu.CompilerParams(
            dimension_semantics=("arbitrary",), vmem_limit_bytes=V7X_VMEM_LIMIT_BYTES),
        name="mixer",
    )(x, hist, h0re, h0im, *consts)


def _ffn_kernel(x_ref, chist_ref, n2g_ref, wup_ref, convw_ref, convb_ref, wdown_ref, nfg_ref,
                xo_ref, convst_ref,
                carry_ref, extc_ref, acc_ref, *, B, T, n_chunks, final_norm):
    M = B * T
    keep = (CONV_W - 1) * B
    c = pl.program_id(0)

    @pl.when(c == 0)
    def _():
        carry_ref[...] = chist_ref[...]

    x = x_ref[...]
    hb = _rmsnorm(x, n2g_ref[...]).astype(_BF16)
    for j in range(FF_TILES):
        fs = slice(j * FF_TILE, (j + 1) * FF_TILE)
        g = _dot(hb, wup_ref[:, fs])
        v = _dot(hb, wup_ref[:, D_FF + j * FF_TILE:D_FF + (j + 1) * FF_TILE])
        extc_ref[0:keep, :] = carry_ref[:, fs]
        extc_ref[keep:keep + M, :] = g
        conv = convb_ref[:, fs]
        for k in range(CONV_W):
            conv = conv + convw_ref[k:k + 1, fs] * extc_ref[k * B:k * B + M, :]
        carry_ref[:, fs] = extc_ref[M:M + keep, :]
        act = (jax.nn.gelu(conv) * v).astype(_BF16)
        contrib = _dot(act, wdown_ref[fs, :])
        if j == 0:
            acc_ref[...] = contrib
        else:
            acc_ref[...] += contrib
    xo = x + acc_ref[...]
    if final_norm:
        xo = _rmsnorm(xo, nfg_ref[...])
    xo_ref[...] = xo

    @pl.when(c == n_chunks - 1)
    def _():
        convst_ref[...] = carry_ref[...]


def _ffn(x, chist, lw, nfg, *, B, T, final_norm):
    m_tot = x.shape[0]
    M = B * T
    n_chunks = m_tot // M
    keep = (CONV_W - 1) * B
    kern = functools.partial(_ffn_kernel, B=B, T=T, n_chunks=n_chunks, final_norm=final_norm)
    consts = [lw["n2g"], lw["w_up"], lw["conv_w"], lw["conv_b"], lw["w_down"], nfg]
    row_spec = pl.BlockSpec((M, D_MODEL), lambda c: (c, 0))
    return pl.pallas_call(
        kern,
        grid=(n_chunks,),
        in_specs=[row_spec, _const_spec(chist.shape)] + [_const_spec(a.shape) for a in consts],
        out_specs=[row_spec, _const_spec((keep, D_FF))],
        out_shape=[jax.ShapeDtypeStruct((m_tot, D_MODEL), _F32),
                   jax.ShapeDtypeStruct((keep, D_FF), _F32)],
        scratch_shapes=[pltpu.VMEM((keep, D_FF), _F32),
                        pltpu.VMEM((keep + M, FF_TILE), _F32),
                        pltpu.VMEM((M, D_MODEL), _F32)],
        compiler_params=pltpu.CompilerParams(
            dimension_semantics=("arbitrary",), vmem_limit_bytes=V7X_VMEM_LIMIT_BYTES),
        name="convffn",
    )(x, chist, *consts)


def _trunk(x_tm, st_pool, st_re, st_im, st_conv, layers, nfg, *, B, T, start_pos):
    depth = len(layers)
    pools, res, ims, convs = [], [], [], []
    for l, lw in enumerate(layers):
        x_tm, p, hr, hi = _mixer(x_tm, st_pool[l], st_re[l], st_im[l], lw,
                                 B=B, T=T, start_pos=start_pos)
        x_tm, cv = _ffn(x_tm, st_conv[l], lw, nfg, B=B, T=T, final_norm=(l == depth - 1))
        pools.append(p)
        res.append(hr)
        ims.append(hi)
        convs.append(cv)
    return x_tm, pools, res, ims, convs


def _states_out(pools, res, ims, convs, B):
    pool = jnp.stack([p.reshape(POOL_KEEP, B, POOL_WIDTH)[1:].transpose(1, 0, 2) for p in pools])
    re = jnp.stack([h.reshape(B, SSM_GROUPS, SSM_STATE) for h in res])
    im = jnp.stack([h.reshape(B, SSM_GROUPS, SSM_STATE) for h in ims])
    conv = jnp.stack([cv.reshape(CONV_W - 1, B, D_FF).transpose(1, 0, 2) for cv in convs])
    return pool, re, im, conv


def kernel(x_prompt, x_sample, state_pool, state_ssm_re, state_ssm_im, state_conv, meta_tokens,
           norm1_g, w_in, pool_w, pool_scale, ssm_A_re, ssm_A_im, ssm_log_dt, ssm_B_re, ssm_B_im,
           ssm_C_re, ssm_C_im, ssm_D, w_glu, w_out, norm2_g, w_up, conv_w, conv_b, w_down, norm_f_g):
    depth = w_in.shape[0]
    ab_re, ab_im, bb_re, bb_im = _discretise(ssm_A_re, ssm_A_im, ssm_log_dt, ssm_B_re, ssm_B_im)
    layers = []
    for l in range(depth):
        layers.append(dict(
            n1g=norm1_g[l][None], w_in=w_in[l].astype(_BF16), pool_w=pool_w[l].astype(_BF16),
            pool_scale=pool_scale[l][None], a_re=ab_re[l], a_im=ab_im[l],
            bbc=jnp.concatenate([_block_diag_b(bb_re[l]), _block_diag_b(bb_im[l])],
                                axis=-1).astype(_BF16),
            cre=_block_diag_c(ssm_C_re[l]).astype(_BF16),
            cim=_block_diag_c(ssm_C_im[l]).astype(_BF16),
            dskip=ssm_D[l][None], w_glu=w_glu[l].astype(_BF16), w_out=w_out[l].astype(_BF16),
            n2g=norm2_g[l][None], w_up=w_up[l].astype(_BF16), conv_w=conv_w[l],
            conv_b=conv_b[l][None], w_down=w_down[l].astype(_BF16)))
    nfg = norm_f_g[None]

    bp, seq, _ = x_prompt.shape
    lp = N_META + seq
    meta = jnp.broadcast_to(meta_tokens[:, None, :], (N_META, bp, D_MODEL))
    xp = jnp.concatenate([meta, x_prompt.transpose(1, 0, 2)], axis=0).reshape(lp * bp, D_MODEL)
    zp = jnp.zeros((depth, POOL_KEEP * bp, POOL_WIDTH), _F32)
    zs = jnp.zeros((depth, bp, SSM_FLAT), _F32)
    zc = jnp.zeros((depth, (CONV_W - 1) * bp, D_FF), _F32)
    yp, pools, res, ims, convs = _trunk(xp, zp, zs, zs, zc, layers, nfg,
                                        B=bp, T=86, start_pos=0)
    y_prompt = yp.reshape(lp, bp, D_MODEL)[N_META:].transpose(1, 0, 2)
    pool_p, re_p, im_p, conv_p = _states_out(pools, res, ims, convs, bp)

    bs = x_sample.shape[0]
    xs = x_sample.reshape(bs, D_MODEL)
    sp = jnp.pad(state_pool.transpose(0, 2, 1, 3), ((0, 0), (1, 0), (0, 0), (0, 0)))
    sp = sp.reshape(depth, POOL_KEEP * bs, POOL_WIDTH)
    sre = state_ssm_re.reshape(depth, bs, SSM_FLAT)
    sim = state_ssm_im.reshape(depth, bs, SSM_FLAT)
    sc = state_conv.transpose(0, 2, 1, 3).reshape(depth, (CONV_W - 1) * bs, D_FF)
    ys, pools, res, ims, convs = _trunk(xs, sp, sre, sim, sc, layers, nfg,
                                        B=bs, T=1, start_pos=PAST_LEN)
    y_sample = ys.reshape(bs, 1, D_MODEL)
    pool_s, re_s, im_s, conv_s = _states_out(pools, res, ims, convs, bs)

    return (y_prompt, y_sample, pool_p, re_p, im_p, conv_p, pool_s, re_s, im_s, conv_s)
```

```python
import functools
import math

import jax
import jax.numpy as jnp
from jax import lax
from jax.experimental import pallas as pl
from jax.experimental.pallas import tpu as pltpu

D_MODEL = 1024
N_META = 16
POOL_WIDTH = 512
POOL_WINDOWS = (2, 4, 8, 16)
POOL_GROUP_IN = 128
POOL_GROUP_OUT = 256
POOL_KEEP = 16
SSM_WIDTH = 512
SSM_GROUP_CH = 16
SSM_GROUPS = 32
SSM_STATE = 64
SSM_FLAT = SSM_GROUPS * SSM_STATE
SSM_TILE = 512
SSM_TILES = SSM_FLAT // SSM_TILE
SSM_TILE_CH = SSM_TILE // SSM_STATE * SSM_GROUP_CH
D_FF = 2816
FF_TILE = 256
FF_TILES = D_FF // FF_TILE
CONV_W = 3
RMS_EPS = 1e-6
PAST_LEN = 16384

V7X_VMEM_LIMIT_BYTES = 56 * 1024 * 1024

_F32 = jnp.float32
_BF16 = jnp.bfloat16


def _rmsnorm(x, g):
    return x * lax.rsqrt(jnp.mean(x * x, axis=-1, keepdims=True) + RMS_EPS) * g


def _dot(a, b):
    return jnp.dot(a, b, preferred_element_type=_F32)


def _const_spec(shape):
    nd = len(shape)
    return pl.BlockSpec(shape, lambda c: (0,) * nd, pipeline_mode=pl.Buffered(1))


def _disc_kernel(are_ref, aim_ref, ldt_ref, bre_ref, bim_ref,
                 abre_ref, abim_ref, bbre_ref, bbim_ref):
    lam_re = are_ref[...]
    lam_im = aim_ref[...]
    dt = jnp.exp(ldt_ref[...])
    mag = jnp.exp(lam_re * dt)
    ab_re = mag * jnp.cos(lam_im * dt)
    ab_im = mag * jnp.sin(lam_im * dt)
    den = lam_re * lam_re + lam_im * lam_im
    nr = ab_re - 1.0
    ni = ab_im
    q_re = (nr * lam_re + ni * lam_im) / den
    q_im = (ni * lam_re - nr * lam_im) / den
    b_re = bre_ref[...]
    b_im = bim_ref[...]
    abre_ref[...] = ab_re
    abim_ref[...] = ab_im
    bbre_ref[...] = q_re * b_re - q_im * b_im
    bbim_ref[...] = q_re * b_im + q_im * b_re


def _discretise(a_re, a_im, log_dt, b_re, b_im):
    depth = a_re.shape[0]
    flat = lambda a: a.reshape(depth, 1, SSM_FLAT)
    ldt = jnp.broadcast_to(log_dt[:, :, None], (depth, SSM_GROUPS, SSM_STATE))
    bt = lambda b: b.transpose(0, 3, 1, 2).reshape(depth, SSM_GROUP_CH, SSM_FLAT)
    vec = pl.BlockSpec((None, 1, SSM_FLAT), lambda l: (l, 0, 0))
    mat = pl.BlockSpec((None, SSM_GROUP_CH, SSM_FLAT), lambda l: (l, 0, 0))
    return pl.pallas_call(
        _disc_kernel,
        grid=(depth,),
        in_specs=[vec, vec, vec, mat, mat],
        out_specs=[vec, vec, mat, mat],
        out_shape=[jax.ShapeDtypeStruct((depth, 1, SSM_FLAT), _F32)] * 2
        + [jax.ShapeDtypeStruct((depth, SSM_GROUP_CH, SSM_FLAT), _F32)] * 2,
        name="s5_discretise",
    )(flat(a_re), flat(a_im), flat(ldt), bt(b_re), bt(b_im))


def _block_diag_b(bb):
    eye = jnp.eye(SSM_TILE // SSM_STATE, dtype=bb.dtype)
    t = bb.reshape(SSM_GROUP_CH, SSM_TILES, 8, SSM_STATE).transpose(1, 2, 0, 3)
    out = t[:, :, :, None, :] * eye[None, :, None, :, None]
    return out.reshape(SSM_TILES, SSM_TILE_CH, SSM_TILE)


def _block_diag_c(c):
    eye = jnp.eye(SSM_TILE // SSM_STATE, dtype=c.dtype)
    t = c.reshape(SSM_TILES, 8, SSM_GROUP_CH, SSM_STATE).transpose(0, 1, 3, 2)
    out = t[:, :, :, None, :] * eye[None, :, None, :, None]
    return out.reshape(SSM_TILES, SSM_TILE, SSM_TILE_CH)


def _mixer_kernel(x_ref, hist_ref, h0re_ref, h0im_ref, n1g_ref, win_ref, poolw_ref,
                  pscale_ref, are_ref, aim_ref, bbc_ref, cre_ref, cim_ref, dskip_ref,
                  wglu_ref, wout_ref,
                  xo_ref, poolst_ref, hre_o_ref, him_o_ref,
                  ext_ref, bu_ref, hst_ref, *, B, T, start_pos, n_chunks):
    M = B * T
    keep = POOL_KEEP * B
    c = pl.program_id(0)

    @pl.when(c == 0)
    def _():
        ext_ref[0:keep, :] = hist_ref[...]
        hst_ref[0] = h0re_ref[...]
        hst_ref[1] = h0im_ref[...]

    x = x_ref[...]
    hb = _rmsnorm(x, n1g_ref[...]).astype(_BF16)

    u = _dot(hb, win_ref[:, 0:POOL_WIDTH + SSM_WIDTH])
    u_pool = u[:, 0:POOL_WIDTH]
    u_ssm = u[:, POOL_WIDTH:]
    ext_ref[keep:keep + M, :] = u_pool
    u_ssm_b = u_ssm.astype(_BF16)

    y_tiles = []
    for i in range(SSM_TILES):
        sl = slice(i * SSM_TILE, (i + 1) * SSM_TILE)
        bu_ref[...] = _dot(u_ssm_b[:, i * SSM_TILE_CH:(i + 1) * SSM_TILE_CH], bbc_ref[i])
        a_re = jnp.broadcast_to(are_ref[:, sl], (B, SSM_TILE))
        a_im = jnp.broadcast_to(aim_ref[:, sl], (B, SSM_TILE))
        h_re0 = hst_ref[0, :, sl]
        h_im0 = hst_ref[1, :, sl]

        def step(rows, carry, a_re=a_re, a_im=a_im):
            h_re, h_im = carry
            n_re = a_re * h_re - a_im * h_im + bu_ref[rows, 0:SSM_TILE]
            n_im = a_re * h_im + a_im * h_re + bu_ref[rows, SSM_TILE:2 * SSM_TILE]
            bu_ref[rows, 0:SSM_TILE] = n_re
            bu_ref[rows, SSM_TILE:2 * SSM_TILE] = n_im
            return n_re, n_im

        carry = (h_re0, h_im0)
        for t in range(T):
            carry = step(slice(t * B, (t + 1) * B), carry)
        h_re, h_im = carry
        hst_ref[0, :, sl] = h_re
        hst_ref[1, :, sl] = h_im
        y_tiles.append(_dot(bu_ref[:, 0:SSM_TILE].astype(_BF16), cre_ref[i])
                       - _dot(bu_ref[:, SSM_TILE:2 * SSM_TILE].astype(_BF16), cim_ref[i]))
    y_s = jnp.concatenate(y_tiles, axis=-1) + dskip_ref[...] * u_ssm
    zz = _dot(jax.nn.gelu(y_s).astype(_BF16), wglu_ref[...])
    y_ssm = zz[:, 0:D_MODEL] * jax.nn.sigmoid(zz[:, D_MODEL:])

    t_glob = (start_pos + c * T
              + lax.broadcasted_iota(jnp.int32, (M, 1), 0) // B)
    pool_tiles = []
    for g, w in enumerate(POOL_WINDOWS):
        ls = slice(g * POOL_GROUP_IN, (g + 1) * POOL_GROUP_IN)
        tok = ext_ref[keep:keep + M, ls]
        win_sum = tok
        for k in range(1, w):
            win_sum = win_sum + ext_ref[keep - k * B:keep - k * B + M, ls]
        cnt = jnp.minimum(w, t_glob + 1).astype(_F32)
        d = win_sum / cnt - tok
        pool_tiles.append(_dot(d.astype(_BF16), poolw_ref[g]))
    y_pool = jnp.concatenate(pool_tiles, axis=-1) * pscale_ref[...]

    gates = _dot(hb, win_ref[:, POOL_WIDTH + SSM_WIDTH:])
    merge = (jax.nn.sigmoid(gates[:, 0:D_MODEL]) * y_pool
             + jax.nn.sigmoid(gates[:, D_MODEL:]) * y_ssm)
    xo_ref[...] = x + _dot(merge.astype(_BF16), wout_ref[...])

    new_hist = ext_ref[M:M + keep, :]
    ext_ref[0:keep, :] = new_hist

    @pl.when(c == n_chunks - 1)
    def _():
        poolst_ref[...] = new_hist
        hre_o_ref[...] = hst_ref[0]
        him_o_ref[...] = hst_ref[1]


def _mixer(x, hist, h0re, h0im, lw, *, B, T, start_pos):
    m_tot = x.shape[0]
    M = B * T
    n_chunks = m_tot // M
    keep = POOL_KEEP * B
    kern = functools.partial(_mixer_kernel, B=B, T=T, start_pos=start_pos, n_chunks=n_chunks)
    consts = [lw["n1g"], lw["w_in"], lw["pool_w"], lw["pool_scale"], lw["a_re"], lw["a_im"],
              lw["bbc"], lw["cre"], lw["cim"], lw["dskip"], lw["w_glu"], lw["w_out"]]
    row_spec = pl.BlockSpec((M, D_MODEL), lambda c: (c, 0))
    return pl.pallas_call(
        kern,
        grid=(n_chunks,),
        in_specs=[row_spec, _const_spec(hist.shape), _const_spec(h0re.shape),
                  _const_spec(h0im.shape)] + [_const_spec(a.shape) for a in consts],
        out_specs=[row_spec, _const_spec((keep, POOL_WIDTH)),
                   _const_spec((B, SSM_FLAT)), _const_spec((B, SSM_FLAT))],
        out_shape=[jax.ShapeDtypeStruct((m_tot, D_MODEL), _F32),
                   jax.ShapeDtypeStruct((keep, POOL_WIDTH), _F32),
                   jax.ShapeDtypeStruct((B, SSM_FLAT), _F32),
                   jax.ShapeDtypeStruct((B, SSM_FLAT), _F32)],
        scratch_shapes=[pltpu.VMEM((keep + M, POOL_WIDTH), _F32),
                        pltpu.VMEM((M, 2 * SSM_TILE), _F32),
                        pltpu.VMEM((2, B, SSM_FLAT), _F32)],
        compiler_params=pltpu.CompilerParams(
            dimension_semantics=("arbitrary",), vmem_limit_bytes=V7X_VMEM_LIMIT_BYTES),
        name="mixer",
    )(x, hist, h0re, h0im, *consts)


def _ffn_kernel(x_ref, chist_ref, n2g_ref, wup_ref, convw_ref, convb_ref, wdown_ref, nfg_ref,
                xo_ref, convst_ref,
                carry_ref, extc_ref, acc_ref, *, B, T, n_chunks, final_norm):
    M = B * T
    keep = (CONV_W - 1) * B
    c = pl.program_id(0)

    @pl.when(c == 0)
    def _():
        carry_ref[...] = chist_ref[...]

    x = x_ref[...]
    hb = _rmsnorm(x, n2g_ref[...]).astype(_BF16)
    for j in range(FF_TILES):
        fs = slice(j * FF_TILE, (j + 1) * FF_TILE)
        g = _dot(hb, wup_ref[:, fs])
        v = _dot(hb, wup_ref[:, D_FF + j * FF_TILE:D_FF + (j + 1) * FF_TILE])
        extc_ref[0:keep, :] = carry_ref[:, fs]
        extc_ref[keep:keep + M, :] = g
        conv = convb_ref[:, fs]
        for k in range(CONV_W):
            conv = conv + convw_ref[k:k + 1, fs] * extc_ref[k * B:k * B + M, :]
        carry_ref[:, fs] = extc_ref[M:M + keep, :]
        act = (jax.nn.gelu(conv) * v).astype(_BF16)
        contrib = _dot(act, wdown_ref[fs, :])
        if j == 0:
            acc_ref[...] = contrib
        else:
            acc_ref[...] += contrib
    xo = x + acc_ref[...]
    if final_norm:
        xo = _rmsnorm(xo, nfg_ref[...])
    xo_ref[...] = xo

    @pl.when(c == n_chunks - 1)
    def _():
        convst_ref[...] = carry_ref[...]


def _ffn(x, chist, lw, nfg, *, B, T, final_norm):
    m_tot = x.shape[0]
    M = B * T
    n_chunks = m_tot // M
    keep = (CONV_W - 1) * B
    kern = functools.partial(_ffn_kernel, B=B, T=T, n_chunks=n_chunks, final_norm=final_norm)
    consts = [lw["n2g"], lw["w_up"], lw["conv_w"], lw["conv_b"], lw["w_down"], nfg]
    row_spec = pl.BlockSpec((M, D_MODEL), lambda c: (c, 0))
    return pl.pallas_call(
        kern,
        grid=(n_chunks,),
        in_specs=[row_spec, _const_spec(chist.shape)] + [_const_spec(a.shape) for a in consts],
        out_specs=[row_spec, _const_spec((keep, D_FF))],
        out_shape=[jax.ShapeDtypeStruct((m_tot, D_MODEL), _F32),
                   jax.ShapeDtypeStruct((keep, D_FF), _F32)],
        scratch_shapes=[pltpu.VMEM((keep, D_FF), _F32),
                        pltpu.VMEM((keep + M, FF_TILE), _F32),
                        pltpu.VMEM((M, D_MODEL), _F32)],
        compiler_params=pltpu.CompilerParams(
            dimension_semantics=("arbitrary",), vmem_limit_bytes=V7X_VMEM_LIMIT_BYTES),
        name="convffn",
    )(x, chist, *consts)


def _trunk(x_tm, st_pool, st_re, st_im, st_conv, layers, nfg, *, B, T, start_pos):
    depth = len(layers)
    pools, res, ims, convs = [], [], [], []
    for l, lw in enumerate(layers):
        x_tm, p, hr, hi = _mixer(x_tm, st_pool[l], st_re[l], st_im[l], lw,
                                 B=B, T=T, start_pos=start_pos)
        x_tm, cv = _ffn(x_tm, st_conv[l], lw, nfg, B=B, T=T, final_norm=(l == depth - 1))
        pools.append(p)
        res.append(hr)
        ims.append(hi)
        convs.append(cv)
    return x_tm, pools, res, ims, convs


def _states_out(pools, res, ims, convs, B):
    pool = jnp.stack([p.reshape(POOL_KEEP, B, POOL_WIDTH)[1:].transpose(1, 0, 2) for p in pools])
    re = jnp.stack([h.reshape(B, SSM_GROUPS, SSM_STATE) for h in res])
    im = jnp.stack([h.reshape(B, SSM_GROUPS, SSM_STATE) for h in ims])
    conv = jnp.stack([cv.reshape(CONV_W - 1, B, D_FF).transpose(1, 0, 2) for cv in convs])
    return pool, re, im, conv


def kernel(x_prompt, x_sample, state_pool, state_ssm_re, state_ssm_im, state_conv, meta_tokens,
           norm1_g, w_in, pool_w, pool_scale, ssm_A_re, ssm_A_im, ssm_log_dt, ssm_B_re, ssm_B_im,
           ssm_C_re, ssm_C_im, ssm_D, w_glu, w_out, norm2_g, w_up, conv_w, conv_b, w_down, norm_f_g):
    depth = w_in.shape[0]
    ab_re, ab_im, bb_re, bb_im = _discretise(ssm_A_re, ssm_A_im, ssm_log_dt, ssm_B_re, ssm_B_im)
    layers = []
    for l in range(depth):
        layers.append(dict(
            n1g=norm1_g[l][None], w_in=w_in[l].astype(_BF16), pool_w=pool_w[l].astype(_BF16),
            pool_scale=pool_scale[l][None], a_re=ab_re[l], a_im=ab_im[l],
            bbc=jnp.concatenate([_block_diag_b(bb_re[l]), _block_diag_b(bb_im[l])],
                                axis=-1).astype(_BF16),
            cre=_block_diag_c(ssm_C_re[l]).astype(_BF16),
            cim=_block_diag_c(ssm_C_im[l]).astype(_BF16),
            dskip=ssm_D[l][None], w_glu=w_glu[l].astype(_BF16), w_out=w_out[l].astype(_BF16),
            n2g=norm2_g[l][None], w_up=w_up[l].astype(_BF16), conv_w=conv_w[l],
            conv_b=conv_b[l][None], w_down=w_down[l].astype(_BF16)))
    nfg = norm_f_g[None]

    bp, seq, _ = x_prompt.shape
    lp = N_META + seq
    meta = jnp.broadcast_to(meta_tokens[:, None, :], (N_META, bp, D_MODEL))
    xp = jnp.concatenate([meta, x_prompt.transpose(1, 0, 2)], axis=0).reshape(lp * bp, D_MODEL)
    zp = jnp.zeros((depth, POOL_KEEP * bp, POOL_WIDTH), _F32)
    zs = jnp.zeros((depth, bp, SSM_FLAT), _F32)
    zc = jnp.zeros((depth, (CONV_W - 1) * bp, D_FF), _F32)
    yp, pools, res, ims, convs = _trunk(xp, zp, zs, zs, zc, layers, nfg,
                                        B=bp, T=86, start_pos=0)
    y_prompt = yp.reshape(lp, bp, D_MODEL)[N_META:].transpose(1, 0, 2)
    pool_p, re_p, im_p, conv_p = _states_out(pools, res, ims, convs, bp)

    bs = x_sample.shape[0]
    xs = x_sample.reshape(bs, D_MODEL)
    sp = jnp.pad(state_pool.transpose(0, 2, 1, 3), ((0, 0), (1, 0), (0, 0), (0, 0)))
    sp = sp.reshape(depth, POOL_KEEP * bs, POOL_WIDTH)
    sre = state_ssm_re.reshape(depth, bs, SSM_FLAT)
    sim = state_ssm_im.reshape(depth, bs, SSM_FLAT)
    sc = state_conv.transpose(0, 2, 1, 3).reshape(depth, (CONV_W - 1) * bs, D_FF)
    ys, pools, res, ims, convs = _trunk(xs, sp, sre, sim, sc, layers, nfg,
                                        B=bs, T=1, start_pos=PAST_LEN)
    y_sample = ys.reshape(bs, 1, D_MODEL)
    pool_s, re_s, im_s, conv_s = _states_out(pools, res, ims, convs, bs)

    return (y_prompt, y_sample, pool_p, re_p, im_p, conv_p, pool_s, re_s, im_s, conv_s)
```

```python
import functools
import math

import jax
import jax.numpy as jnp
from jax import lax
from jax.experimental import pallas as pl
from jax.experimental.pallas import tpu as pltpu

D_MODEL = 1024
N_META = 16
POOL_WIDTH = 512
POOL_WINDOWS = (2, 4, 8, 16)
POOL_GROUP_IN = 128
POOL_GROUP_OUT = 256
POOL_KEEP = 16
SSM_WIDTH = 512
SSM_GROUP_CH = 16
SSM_GROUPS = 32
SSM_STATE = 64
SSM_FLAT = SSM_GROUPS * SSM_STATE
SSM_TILE = 512
SSM_TILES = SSM_FLAT // SSM_TILE
SSM_TILE_CH = SSM_TILE // SSM_STATE * SSM_GROUP_CH
D_FF = 2816
FF_TILE = 256
FF_TILES = D_FF // FF_TILE
CONV_W = 3
RMS_EPS = 1e-6
PAST_LEN = 16384

V7X_VMEM_LIMIT_BYTES = 56 * 1024 * 1024

_F32 = jnp.float32
_BF16 = jnp.bfloat16


def _rmsnorm(x, g):
    return x * lax.rsqrt(jnp.mean(x * x, axis=-1, keepdims=True) + RMS_EPS) * g


def _dot(a, b):
    return jnp.dot(a, b, preferred_element_type=_F32)


def _const_spec(shape):
    nd = len(shape)
    return pl.BlockSpec(shape, lambda c: (0,) * nd, pipeline_mode=pl.Buffered(1))


def _disc_kernel(are_ref, aim_ref, ldt_ref, bre_ref, bim_ref,
                 abre_ref, abim_ref, bbre_ref, bbim_ref):
    lam_re = are_ref[...]
    lam_im = aim_ref[...]
    dt = jnp.exp(ldt_ref[...])
    mag = jnp.exp(lam_re * dt)
    ab_re = mag * jnp.cos(lam_im * dt)
    ab_im = mag * jnp.sin(lam_im * dt)
    den = lam_re * lam_re + lam_im * lam_im
    nr = ab_re - 1.0
    ni = ab_im
    q_re = (nr * lam_re + ni * lam_im) / den
    q_im = (ni * lam_re - nr * lam_im) / den
    b_re = bre_ref[...]
    b_im = bim_ref[...]
    abre_ref[...] = ab_re
    abim_ref[...] = ab_im
    bbre_ref[...] = q_re * b_re - q_im * b_im
    bbim_ref[...] = q_re * b_im + q_im * b_re


def _discretise(a_re, a_im, log_dt, b_re, b_im):
    depth = a_re.shape[0]
    flat = lambda a: a.reshape(depth, 1, SSM_FLAT)
    ldt = jnp.broadcast_to(log_dt[:, :, None], (depth, SSM_GROUPS, SSM_STATE))
    bt = lambda b: b.transpose(0, 3, 1, 2).reshape(depth, SSM_GROUP_CH, SSM_FLAT)
    vec = pl.BlockSpec((None, 1, SSM_FLAT), lambda l: (l, 0, 0))
    mat = pl.BlockSpec((None, SSM_GROUP_CH, SSM_FLAT), lambda l: (l, 0, 0))
    return pl.pallas_call(
        _disc_kernel,
        grid=(depth,),
        in_specs=[vec, vec, vec, mat, mat],
        out_specs=[vec, vec, mat, mat],
        out_shape=[jax.ShapeDtypeStruct((depth, 1, SSM_FLAT), _F32)] * 2
        + [jax.ShapeDtypeStruct((depth, SSM_GROUP_CH, SSM_FLAT), _F32)] * 2,
        name="s5_discretise",
    )(flat(a_re), flat(a_im), flat(ldt), bt(b_re), bt(b_im))


def _block_diag_b(bb):
    eye = jnp.eye(SSM_TILE // SSM_STATE, dtype=bb.dtype)
    t = bb.reshape(SSM_GROUP_CH, SSM_TILES, 8, SSM_STATE).transpose(1, 2, 0, 3)
    out = t[:, :, :, None, :] * eye[None, :, None, :, None]
    return out.reshape(SSM_TILES, SSM_TILE_CH, SSM_TILE)


def _block_diag_c(c):
    eye = jnp.eye(SSM_TILE // SSM_STATE, dtype=c.dtype)
    t = c.reshape(SSM_TILES, 8, SSM_GROUP_CH, SSM_STATE).transpose(0, 1, 3, 2)
    out = t[:, :, :, None, :] * eye[None, :, None, :, None]
    return out.reshape(SSM_TILES, SSM_TILE, SSM_TILE_CH)


def _mixer_kernel(x_ref, hist_ref, h0re_ref, h0im_ref, n1g_ref, win_ref, poolw_ref,
                  pscale_ref, are_ref, aim_ref, bbc_ref, cre_ref, cim_ref, dskip_ref,
                  wglu_ref, wout_ref,
                  xo_ref, poolst_ref, hre_o_ref, him_o_ref,
                  ext_ref, hst_ref, *, B, T, start_pos, n_chunks):
    M = B * T
    keep = POOL_KEEP * B
    c = pl.program_id(0)

    @pl.when(c == 0)
    def _():
        ext_ref[0:keep, :] = hist_ref[...]
        hst_ref[0] = h0re_ref[...]
        hst_ref[1] = h0im_ref[...]

    x = x_ref[...]
    hb = _rmsnorm(x, n1g_ref[...]).astype(_BF16)

    u = _dot(hb, win_ref[:, 0:POOL_WIDTH + SSM_WIDTH])
    u_pool = u[:, 0:POOL_WIDTH]
    u_ssm = u[:, POOL_WIDTH:]
    ext_ref[keep:keep + M, :] = u_pool
    u_ssm_b = u_ssm.astype(_BF16)

    bus = [_dot(u_ssm_b[:, i * SSM_TILE_CH:(i + 1) * SSM_TILE_CH], bbc_ref[i])
           for i in range(SSM_TILES)]

    def gate(lo):
        return jax.nn.sigmoid(_dot(hb, win_ref[:, lo:lo + D_MODEL]))

    def ssm_tile(i):
        sl = slice(i * SSM_TILE, (i + 1) * SSM_TILE)
        a_re = jnp.broadcast_to(are_ref[:, sl], (B, SSM_TILE))
        a_im = jnp.broadcast_to(aim_ref[:, sl], (B, SSM_TILE))
        h_re = hst_ref[0, :, sl]
        h_im = hst_ref[1, :, sl]
        hs_re, hs_im = [], []
        for t in range(T):
            rows = slice(t * B, (t + 1) * B)
            n_re = a_re * h_re - a_im * h_im + bus[i][rows, 0:SSM_TILE]
            n_im = a_re * h_im + a_im * h_re + bus[i][rows, SSM_TILE:2 * SSM_TILE]
            h_re, h_im = n_re, n_im
            hs_re.append(h_re)
            hs_im.append(h_im)
        hst_ref[0, :, sl] = h_re
        hst_ref[1, :, sl] = h_im
        return (_dot(jnp.concatenate(hs_re, axis=0).astype(_BF16), cre_ref[i])
                - _dot(jnp.concatenate(hs_im, axis=0).astype(_BF16), cim_ref[i]))

    t_glob = (start_pos + c * T
              + lax.broadcasted_iota(jnp.int32, (M, 1), 0) // B)
    pool_tiles = []
    for g, w in enumerate(POOL_WINDOWS):
        ls = slice(g * POOL_GROUP_IN, (g + 1) * POOL_GROUP_IN)
        tok = ext_ref[keep:keep + M, ls]
        win_sum = tok
        for k in range(1, w):
            win_sum = win_sum + ext_ref[keep - k * B:keep - k * B + M, ls]
        cnt = jnp.minimum(w, t_glob + 1).astype(_F32)
        d = win_sum / cnt - tok
        pool_tiles.append(_dot(d.astype(_BF16), poolw_ref[g]))
    y_pool = jnp.concatenate(pool_tiles, axis=-1) * pscale_ref[...]

    gate_pool = gate(POOL_WIDTH + SSM_WIDTH)
    merge_pool = gate_pool * y_pool
    y_tiles = [ssm_tile(0), ssm_tile(1)]
    gate_ssm = gate(POOL_WIDTH + SSM_WIDTH + D_MODEL)
    y_tiles += [ssm_tile(2), ssm_tile(3)]

    y_s = jnp.concatenate(y_tiles, axis=-1) + dskip_ref[...] * u_ssm
    zz = _dot(jax.nn.gelu(y_s).astype(_BF16), wglu_ref[...])
    y_ssm = zz[:, 0:D_MODEL] * jax.nn.sigmoid(zz[:, D_MODEL:])

    merge = merge_pool + gate_ssm * y_ssm
    xo_ref[...] = x + _dot(merge.astype(_BF16), wout_ref[...])

    new_hist = ext_ref[M:M + keep, :]
    ext_ref[0:keep, :] = new_hist

    @pl.when(c == n_chunks - 1)
    def _():
        poolst_ref[...] = new_hist
        hre_o_ref[...] = hst_ref[0]
        him_o_ref[...] = hst_ref[1]


def _mixer(x, hist, h0re, h0im, lw, *, B, T, start_pos):
    m_tot = x.shape[0]
    M = B * T
    n_chunks = m_tot // M
    keep = POOL_KEEP * B
    kern = functools.partial(_mixer_kernel, B=B, T=T, start_pos=start_pos, n_chunks=n_chunks)
    consts = [lw["n1g"], lw["w_in"], lw["pool_w"], lw["pool_scale"], lw["a_re"], lw["a_im"],
              lw["bbc"], lw["cre"], lw["cim"], lw["dskip"], lw["w_glu"], lw["w_out"]]
    row_spec = pl.BlockSpec((M, D_MODEL), lambda c: (c, 0))
    return pl.pallas_call(
        kern,
        grid=(n_chunks,),
        in_specs=[row_spec, _const_spec(hist.shape), _const_spec(h0re.shape),
                  _const_spec(h0im.shape)] + [_const_spec(a.shape) for a in consts],
        out_specs=[row_spec, _const_spec((keep, POOL_WIDTH)),
                   _const_spec((B, SSM_FLAT)), _const_spec((B, SSM_FLAT))],
        out_shape=[jax.ShapeDtypeStruct((m_tot, D_MODEL), _F32),
                   jax.ShapeDtypeStruct((keep, POOL_WIDTH), _F32),
                   jax.ShapeDtypeStruct((B, SSM_FLAT), _F32),
                   jax.ShapeDtypeStruct((B, SSM_FLAT), _F32)],
        scratch_shapes=[pltpu.VMEM((keep + M, POOL_WIDTH), _F32),
                        pltpu.VMEM((2, B, SSM_FLAT), _F32)],
        compiler_params=pltpu.CompilerParams(
            dimension_semantics=("arbitrary",), vmem_limit_bytes=V7X_VMEM_LIMIT_BYTES),
        name="mixer",
    )(x, hist, h0re, h0im, *consts)


def _ffn_kernel(x_ref, chist_ref, n2g_ref, wup_ref, convw_ref, convb_ref, wdown_ref, nfg_ref,
                xo_ref, convst_ref,
                carry_ref, *, B, T, n_chunks, final_norm):
    M = B * T
    keep = (CONV_W - 1) * B
    c = pl.program_id(0)

    @pl.when(c == 0)
    def _():
        carry_ref[...] = chist_ref[...]

    x = x_ref[...]
    hb = _rmsnorm(x, n2g_ref[...]).astype(_BF16)
    def up(j):
        return (_dot(hb, wup_ref[:, j * FF_TILE:(j + 1) * FF_TILE]),
                _dot(hb, wup_ref[:, D_FF + j * FF_TILE:D_FF + (j + 1) * FF_TILE]))

    acc = None
    g, v = up(0)
    for j in range(FF_TILES):
        fs = slice(j * FF_TILE, (j + 1) * FF_TILE)
        nxt = up(j + 1) if j + 1 < FF_TILES else None
        g_ext = jnp.concatenate([carry_ref[:, fs], g], axis=0)
        conv = convb_ref[:, fs]
        for k in range(CONV_W):
            conv = conv + convw_ref[k:k + 1, fs] * g_ext[k * B:k * B + M, :]
        carry_ref[:, fs] = g_ext[M:M + keep, :]
        act = (jax.nn.gelu(conv) * v).astype(_BF16)
        contrib = _dot(act, wdown_ref[fs, :])
        acc = contrib if acc is None else acc + contrib
        if nxt is not None:
            g, v = nxt
    xo = x + acc
    if final_norm:
        xo = _rmsnorm(xo, nfg_ref[...])
    xo_ref[...] = xo

    @pl.when(c == n_chunks - 1)
    def _():
        convst_ref[...] = carry_ref[...]


def _ffn(x, chist, lw, nfg, *, B, T, final_norm):
    m_tot = x.shape[0]
    M = B * T
    n_chunks = m_tot // M
    keep = (CONV_W - 1) * B
    kern = functools.partial(_ffn_kernel, B=B, T=T, n_chunks=n_chunks, final_norm=final_norm)
    consts = [lw["n2g"], lw["w_up"], lw["conv_w"], lw["conv_b"], lw["w_down"], nfg]
    row_spec = pl.BlockSpec((M, D_MODEL), lambda c: (c, 0))
    return pl.pallas_call(
        kern,
        grid=(n_chunks,),
        in_specs=[row_spec, _const_spec(chist.shape)] + [_const_spec(a.shape) for a in consts],
        out_specs=[row_spec, _const_spec((keep, D_FF))],
        out_shape=[jax.ShapeDtypeStruct((m_tot, D_MODEL), _F32),
                   jax.ShapeDtypeStruct((keep, D_FF), _F32)],
        scratch_shapes=[pltpu.VMEM((keep, D_FF), _F32)],
        compiler_params=pltpu.CompilerParams(
            dimension_semantics=("arbitrary",), vmem_limit_bytes=V7X_VMEM_LIMIT_BYTES),
        name="convffn",
    )(x, chist, *consts)


def _trunk(x_tm, st_pool, st_re, st_im, st_conv, layers, nfg, *, B, T, start_pos):
    depth = len(layers)
    pools, res, ims, convs = [], [], [], []
    for l, lw in enumerate(layers):
        x_tm, p, hr, hi = _mixer(x_tm, st_pool[l], st_re[l], st_im[l], lw,
                                 B=B, T=T, start_pos=start_pos)
        x_tm, cv = _ffn(x_tm, st_conv[l], lw, nfg, B=B, T=T, final_norm=(l == depth - 1))
        pools.append(p)
        res.append(hr)
        ims.append(hi)
        convs.append(cv)
    return x_tm, pools, res, ims, convs


def _states_out(pools, res, ims, convs, B):
    pool = jnp.stack([p.reshape(POOL_KEEP, B, POOL_WIDTH)[1:].transpose(1, 0, 2) for p in pools])
    re = jnp.stack([h.reshape(B, SSM_GROUPS, SSM_STATE) for h in res])
    im = jnp.stack([h.reshape(B, SSM_GROUPS, SSM_STATE) for h in ims])
    conv = jnp.stack([cv.reshape(CONV_W - 1, B, D_FF).transpose(1, 0, 2) for cv in convs])
    return pool, re, im, conv


def kernel(x_prompt, x_sample, state_pool, state_ssm_re, state_ssm_im, state_conv, meta_tokens,
           norm1_g, w_in, pool_w, pool_scale, ssm_A_re, ssm_A_im, ssm_log_dt, ssm_B_re, ssm_B_im,
           ssm_C_re, ssm_C_im, ssm_D, w_glu, w_out, norm2_g, w_up, conv_w, conv_b, w_down, norm_f_g):
    depth = w_in.shape[0]
    ab_re, ab_im, bb_re, bb_im = _discretise(ssm_A_re, ssm_A_im, ssm_log_dt, ssm_B_re, ssm_B_im)
    layers = []
    for l in range(depth):
        layers.append(dict(
            n1g=norm1_g[l][None], w_in=w_in[l].astype(_BF16), pool_w=pool_w[l].astype(_BF16),
            pool_scale=pool_scale[l][None], a_re=ab_re[l], a_im=ab_im[l],
            bbc=jnp.concatenate([_block_diag_b(bb_re[l]), _block_diag_b(bb_im[l])],
                                axis=-1).astype(_BF16),
            cre=_block_diag_c(ssm_C_re[l]).astype(_BF16),
            cim=_block_diag_c(ssm_C_im[l]).astype(_BF16),
            dskip=ssm_D[l][None], w_glu=w_glu[l].astype(_BF16), w_out=w_out[l].astype(_BF16),
            n2g=norm2_g[l][None], w_up=w_up[l].astype(_BF16), conv_w=conv_w[l],
            conv_b=conv_b[l][None], w_down=w_down[l].astype(_BF16)))
    nfg = norm_f_g[None]

    bp, seq, _ = x_prompt.shape
    lp = N_META + seq
    meta = jnp.broadcast_to(meta_tokens[:, None, :], (N_META, bp, D_MODEL))
    xp = jnp.concatenate([meta, x_prompt.transpose(1, 0, 2)], axis=0).reshape(lp * bp, D_MODEL)
    zp = jnp.zeros((depth, POOL_KEEP * bp, POOL_WIDTH), _F32)
    zs = jnp.zeros((depth, bp, SSM_FLAT), _F32)
    zc = jnp.zeros((depth, (CONV_W - 1) * bp, D_FF), _F32)
    yp, pools, res, ims, convs = _trunk(xp, zp, zs, zs, zc, layers, nfg,
                                        B=bp, T=86, start_pos=0)
    y_prompt = yp.reshape(lp, bp, D_MODEL)[N_META:].transpose(1, 0, 2)
    pool_p, re_p, im_p, conv_p = _states_out(pools, res, ims, convs, bp)

    bs = x_sample.shape[0]
    xs = x_sample.reshape(bs, D_MODEL)
    sp = jnp.pad(state_pool.transpose(0, 2, 1, 3), ((0, 0), (1, 0), (0, 0), (0, 0)))
    sp = sp.reshape(depth, POOL_KEEP * bs, POOL_WIDTH)
    sre = state_ssm_re.reshape(depth, bs, SSM_FLAT)
    sim = state_ssm_im.reshape(depth, bs, SSM_FLAT)
    sc = state_conv.transpose(0, 2, 1, 3).reshape(depth, (CONV_W - 1) * bs, D_FF)
    ys, pools, res, ims, convs = _trunk(xs, sp, sre, sim, sc, layers, nfg,
                                        B=bs, T=1, start_pos=PAST_LEN)
    y_sample = ys.reshape(bs, 1, D_MODEL)
    pool_s, re_s, im_s, conv_s = _states_out(pools, res, ims, convs, bs)

    return (y_prompt, y_sample, pool_p, re_p, im_p, conv_p, pool_s, re_s, im_s, conv_s)
```

```python
import functools

import jax
import jax.numpy as jnp
from jax import lax
from jax.experimental import pallas as pl
from jax.experimental.pallas import tpu as pltpu

D_MODEL = 1024
N_META = 16
POOL_WIDTH = 512
POOL_WINDOWS = (2, 4, 8, 16)
POOL_GROUP_IN = 128
POOL_HIST = 15
POOL_KEEP = 16
SSM_WIDTH = 512
SSM_GROUP_CH = 16
SSM_GROUPS = 32
SSM_STATE = 64
SSM_FLAT = SSM_GROUPS * SSM_STATE
SSM_TILE = 512
SSM_TILES = SSM_FLAT // SSM_TILE
SSM_TILE_CH = SSM_TILE // SSM_STATE * SSM_GROUP_CH
D_FF = 2816
FF_TILE = 256
FF_TILES = D_FF // FF_TILE
CONV_W = 3
CONV_HIST = CONV_W - 1
RMS_EPS = 1e-6
PAST_LEN = 16384

V7X_LANES = 128
D_SLABS = D_MODEL // V7X_LANES
POOL_SLABS = POOL_WIDTH // V7X_LANES
FF_SLABS = D_FF // V7X_LANES
FF_TILE_SLABS = FF_TILE // V7X_LANES
V7X_VMEM_LIMIT_BYTES = 56 * 1024 * 1024
PROMPT_CHUNK_STEPS = 64

_F32 = jnp.float32
_BF16 = jnp.bfloat16


def _rmsnorm(x, g):
    return x * lax.rsqrt(jnp.mean(x * x, axis=-1, keepdims=True) + RMS_EPS) * g


def _dot(a, b):
    return jnp.dot(a, b, preferred_element_type=_F32)


def _const_spec(shape):
    nd = len(shape)
    return pl.BlockSpec(shape, lambda c: (0,) * nd, pipeline_mode=pl.Buffered(1))


def _compiler_params():
    return pltpu.CompilerParams(dimension_semantics=("arbitrary",),
                                vmem_limit_bytes=V7X_VMEM_LIMIT_BYTES)


def _disc_kernel(are_ref, aim_ref, ldt_ref, bre_ref, bim_ref,
                 abre_ref, abim_ref, bbre_ref, bbim_ref):
    lam_re = are_ref[...]
    lam_im = aim_ref[...]
    dt = jnp.exp(ldt_ref[...])
    mag = jnp.exp(lam_re * dt)
    ab_re = mag * jnp.cos(lam_im * dt)
    ab_im = mag * jnp.sin(lam_im * dt)
    den = lam_re * lam_re + lam_im * lam_im
    nr = ab_re - 1.0
    ni = ab_im
    q_re = (nr * lam_re + ni * lam_im) / den
    q_im = (ni * lam_re - nr * lam_im) / den
    b_re = bre_ref[...]
    b_im = bim_ref[...]
    abre_ref[...] = ab_re
    abim_ref[...] = ab_im
    bbre_ref[...] = q_re * b_re - q_im * b_im
    bbim_ref[...] = q_re * b_im + q_im * b_re


def _discretise(a_re, a_im, log_dt, b_re, b_im):
    depth = a_re.shape[0]
    flat = lambda a: a.reshape(depth, 1, SSM_FLAT)
    ldt = jnp.broadcast_to(log_dt[:, :, None], (depth, SSM_GROUPS, SSM_STATE))
    bt = lambda b: b.transpose(0, 3, 1, 2).reshape(depth, SSM_GROUP_CH, SSM_FLAT)
    vec = pl.BlockSpec((None, 1, SSM_FLAT), lambda l: (l, 0, 0))
    mat = pl.BlockSpec((None, SSM_GROUP_CH, SSM_FLAT), lambda l: (l, 0, 0))
    return pl.pallas_call(
        _disc_kernel,
        grid=(depth,),
        in_specs=[vec, vec, vec, mat, mat],
        out_specs=[vec, vec, mat, mat],
        out_shape=[jax.ShapeDtypeStruct((depth, 1, SSM_FLAT), _F32)] * 2
        + [jax.ShapeDtypeStruct((depth, SSM_GROUP_CH, SSM_FLAT), _F32)] * 2,
        name="s5_discretise",
    )(flat(a_re), flat(a_im), flat(ldt), bt(b_re), bt(b_im))


def _block_diag_b(bb):
    eye = jnp.eye(SSM_TILE // SSM_STATE, dtype=bb.dtype)
    t = bb.reshape(SSM_GROUP_CH, SSM_TILES, 8, SSM_STATE).transpose(1, 2, 0, 3)
    out = t[:, :, :, None, :] * eye[None, :, None, :, None]
    return out.reshape(SSM_TILES, SSM_TILE_CH, SSM_TILE)


def _block_diag_c(c):
    eye = jnp.eye(SSM_TILE // SSM_STATE, dtype=c.dtype)
    t = c.reshape(SSM_TILES, 8, SSM_GROUP_CH, SSM_STATE).transpose(0, 1, 3, 2)
    out = t[:, :, :, None, :] * eye[None, :, None, :, None]
    return out.reshape(SSM_TILES, SSM_TILE, SSM_TILE_CH)


def _pool_branch(ext, n_hist, B, T, pos0, poolw_ref, pscale):
    M = B * T
    base = n_hist * B
    pos = pos0 + lax.broadcasted_iota(jnp.int32, (M, 1), 0) // B
    tiles = []
    for g, w in enumerate(POOL_WINDOWS):
        ls = slice(g * POOL_GROUP_IN, (g + 1) * POOL_GROUP_IN)
        tok = ext[base:base + M, ls]
        win_sum = tok
        for k in range(1, w):
            win_sum = win_sum + ext[base - k * B:base - k * B + M, ls]
        cnt = jnp.minimum(w, pos + 1).astype(_F32)
        d = win_sum / cnt - tok
        tiles.append(_dot(d.astype(_BF16), poolw_ref[g]))
    return jnp.concatenate(tiles, axis=-1) * pscale


def _ssm_inputs(u_ssm_b, bbc_ref):
    return [_dot(u_ssm_b[:, i * SSM_TILE_CH:(i + 1) * SSM_TILE_CH], bbc_ref[i])
            for i in range(SSM_TILES)]


def _ssm_tile(i, bu, h_re, h_im, B, T, are_ref, aim_ref, cre_ref, cim_ref):
    sl = slice(i * SSM_TILE, (i + 1) * SSM_TILE)
    a_re = jnp.broadcast_to(are_ref[:, sl], (B, SSM_TILE))
    a_im = jnp.broadcast_to(aim_ref[:, sl], (B, SSM_TILE))
    hs_re, hs_im = [], []
    for t in range(T):
        rows = slice(t * B, (t + 1) * B)
        n_re = a_re * h_re - a_im * h_im + bu[rows, 0:SSM_TILE]
        n_im = a_re * h_im + a_im * h_re + bu[rows, SSM_TILE:2 * SSM_TILE]
        h_re, h_im = n_re, n_im
        hs_re.append(h_re)
        hs_im.append(h_im)
    y = (_dot(jnp.concatenate(hs_re, axis=0).astype(_BF16), cre_ref[i])
         - _dot(jnp.concatenate(hs_im, axis=0).astype(_BF16), cim_ref[i]))
    return y, h_re, h_im


def _glu(y_tiles, u_ssm, dskip, wglu_ref):
    y_s = jnp.concatenate(y_tiles, axis=-1) + dskip * u_ssm
    zz = _dot(jax.nn.gelu(y_s).astype(_BF16), wglu_ref[...])
    return zz[:, 0:D_MODEL] * jax.nn.sigmoid(zz[:, D_MODEL:])


def _gate(hb, win_ref, lo):
    return jax.nn.sigmoid(_dot(hb, win_ref[:, lo:lo + D_MODEL]))


_MIXER_WEIGHTS = ("n1g", "w_in", "pool_w", "pool_scale", "a_re", "a_im", "bbc", "cre", "cim",
                  "dskip", "w_glu", "w_out")


def _mixer_main_kernel(x_ref, hist_ref, h0re_ref, h0im_ref, n1g_ref, win_ref, poolw_ref,
                       pscale_ref, are_ref, aim_ref, bbc_ref, cre_ref, cim_ref, dskip_ref,
                       wglu_ref, wout_ref,
                       xo_ref, poolst_ref, hre_o_ref, him_o_ref,
                       pool_ref, hst_ref, *slab_refs, B, T, start_pos, n_chunks, batch_major_in):
    M = B * T
    c = pl.program_id(0)

    @pl.when(c == 0)
    def _():
        pool_ref[...] = hist_ref[...]
        hst_ref[0] = h0re_ref[...]
        hst_ref[1] = h0im_ref[...]

    if batch_major_in:
        (xs_ref,) = slab_refs
        for b in range(B):
            for j in range(D_SLABS):
                xs_ref[j, pl.ds(b, T, stride=B), :] = x_ref[b, :, j * V7X_LANES:(j + 1) * V7X_LANES]
        x = jnp.concatenate([xs_ref[j] for j in range(D_SLABS)], axis=-1)
    else:
        x = x_ref[...]
    hb = _rmsnorm(x, n1g_ref[...]).astype(_BF16)

    u = _dot(hb, win_ref[:, 0:POOL_WIDTH + SSM_WIDTH])
    u_ssm = u[:, POOL_WIDTH:]
    ext = jnp.concatenate([pool_ref[...], u[:, 0:POOL_WIDTH]], axis=0)
    pool_ref[...] = ext[M:M + POOL_KEEP * B, :]

    bus = _ssm_inputs(u_ssm.astype(_BF16), bbc_ref)
    y_pool = _pool_branch(ext, POOL_KEEP, B, T, start_pos + c * T, poolw_ref, pscale_ref[...])
    merge_pool = _gate(hb, win_ref, POOL_WIDTH + SSM_WIDTH) * y_pool

    def tile(i):
        sl = slice(i * SSM_TILE, (i + 1) * SSM_TILE)
        y, h_re, h_im = _ssm_tile(i, bus[i], hst_ref[0, :, sl], hst_ref[1, :, sl], B, T,
                                  are_ref, aim_ref, cre_ref, cim_ref)
        hst_ref[0, :, sl] = h_re
        hst_ref[1, :, sl] = h_im
        return y

    y_tiles = [tile(0), tile(1)]
    gate_ssm = _gate(hb, win_ref, POOL_WIDTH + SSM_WIDTH + D_MODEL)
    y_tiles += [tile(2), tile(3)]
    merge = merge_pool + gate_ssm * _glu(y_tiles, u_ssm, dskip_ref[...], wglu_ref)
    xo_ref[...] = x + _dot(merge.astype(_BF16), wout_ref[...])

    @pl.when(c == n_chunks - 1)
    def _():
        poolst_ref[...] = pool_ref[...]
        hre_o_ref[...] = hst_ref[0]
        him_o_ref[...] = hst_ref[1]


def _mixer_main(x, hist, h0re, h0im, lw, *, B, T, start_pos, batch_major_in):
    M = B * T
    if batch_major_in:
        n_chunks = x.shape[1] // T
        x_spec = pl.BlockSpec((B, T, D_MODEL), lambda c: (0, c, 0))
        slabs = [pltpu.VMEM((D_SLABS, M, V7X_LANES), _F32)]
    else:
        n_chunks = x.shape[0] // M
        x_spec = pl.BlockSpec((M, D_MODEL), lambda c: (c, 0))
        slabs = []
    keep = POOL_KEEP * B
    kern = functools.partial(_mixer_main_kernel, B=B, T=T, start_pos=start_pos,
                             n_chunks=n_chunks, batch_major_in=batch_major_in)
    consts = [lw[k] for k in _MIXER_WEIGHTS]
    return pl.pallas_call(
        kern,
        grid=(n_chunks,),
        in_specs=[x_spec, _const_spec(hist.shape), _const_spec(h0re.shape),
                  _const_spec(h0im.shape)] + [_const_spec(a.shape) for a in consts],
        out_specs=[pl.BlockSpec((M, D_MODEL), lambda c: (c, 0)), _const_spec((keep, POOL_WIDTH)),
                   _const_spec((B, SSM_FLAT)), _const_spec((B, SSM_FLAT))],
        out_shape=[jax.ShapeDtypeStruct((n_chunks * M, D_MODEL), _F32),
                   jax.ShapeDtypeStruct((keep, POOL_WIDTH), _F32),
                   jax.ShapeDtypeStruct((B, SSM_FLAT), _F32),
                   jax.ShapeDtypeStruct((B, SSM_FLAT), _F32)],
        scratch_shapes=[pltpu.VMEM((keep, POOL_WIDTH), _F32),
                        pltpu.VMEM((2, B, SSM_FLAT), _F32)] + slabs,
        compiler_params=_compiler_params(),
        name="mixer_main",
    )(x, hist, h0re, h0im, *consts)


def _mixer_small_kernel(x_ref, pool_ref, sre_ref, sim_ref, n1g_ref, win_ref, poolw_ref,
                        pscale_ref, are_ref, aim_ref, bbc_ref, cre_ref, cim_ref, dskip_ref,
                        wglu_ref, wout_ref,
                        xo_ref, pool_o_ref, sre_o_ref, sim_o_ref, mpool_o_ref, mre_o_ref,
                        mim_o_ref, *, Bs, Bm, Tm):
    Ms, Mm = Bs, Bm * Tm
    x = x_ref[...]
    hb = _rmsnorm(x, n1g_ref[...]).astype(_BF16)
    u = _dot(hb, win_ref[:, 0:POOL_WIDTH + SSM_WIDTH])
    u_pool, u_ssm = u[:, 0:POOL_WIDTH], u[:, POOL_WIDTH:]
    gate_pool = _gate(hb, win_ref, POOL_WIDTH + SSM_WIDTH)
    gate_ssm = _gate(hb, win_ref, POOL_WIDTH + SSM_WIDTH + D_MODEL)
    bus = _ssm_inputs(u_ssm.astype(_BF16), bbc_ref)

    stride = POOL_HIST * POOL_SLABS
    hist = [jnp.concatenate([pool_ref[pl.ds(j * POOL_SLABS + q, Bs, stride=stride), :]
                             for q in range(POOL_SLABS)], axis=-1) for j in range(POOL_HIST)]
    new_hist = hist[1:] + [u_pool[0:Ms]]
    for j in range(POOL_HIST):
        for q in range(POOL_SLABS):
            pool_o_ref[pl.ds(j * POOL_SLABS + q, Bs, stride=stride), :] = (
                new_hist[j][:, q * V7X_LANES:(q + 1) * V7X_LANES])
    y_pool_s = _pool_branch(jnp.concatenate(hist + [u_pool[0:Ms]], axis=0), POOL_HIST, Bs, 1,
                            PAST_LEN, poolw_ref, pscale_ref[...])
    ext_m = jnp.concatenate([jnp.zeros((POOL_KEEP * Bm, POOL_WIDTH), _F32), u_pool[Ms:]], axis=0)
    y_pool_m = _pool_branch(ext_m, POOL_KEEP, Bm, Tm, 0, poolw_ref, pscale_ref[...])
    mpool_o_ref[...] = ext_m[Mm:Mm + POOL_KEEP * Bm, :]
    y_pool = jnp.concatenate([y_pool_s, y_pool_m], axis=0)

    y_tiles = []
    for i in range(SSM_TILES):
        sl = slice(i * SSM_TILE, (i + 1) * SSM_TILE)
        ys, h_re, h_im = _ssm_tile(i, bus[i][0:Ms], sre_ref[:, sl], sim_ref[:, sl], Bs, 1,
                                   are_ref, aim_ref, cre_ref, cim_ref)
        sre_o_ref[:, sl] = h_re
        sim_o_ref[:, sl] = h_im
        zero = jnp.zeros((Bm, SSM_TILE), _F32)
        ym, h_re, h_im = _ssm_tile(i, bus[i][Ms:], zero, zero, Bm, Tm,
                                   are_ref, aim_ref, cre_ref, cim_ref)
        mre_o_ref[:, sl] = h_re
        mim_o_ref[:, sl] = h_im
        y_tiles.append(jnp.concatenate([ys, ym], axis=0))
    merge = gate_pool * y_pool + gate_ssm * _glu(y_tiles, u_ssm, dskip_ref[...], wglu_ref)
    xo_ref[...] = x + _dot(merge.astype(_BF16), wout_ref[...])


def _mixer_small(x, pool_slabs, sre, sim, lw, *, Bs, Bm, Tm):
    kern = functools.partial(_mixer_small_kernel, Bs=Bs, Bm=Bm, Tm=Tm)
    consts = [lw[k] for k in _MIXER_WEIGHTS]
    ins = [x, pool_slabs, sre, sim] + consts
    out_shape = [jax.ShapeDtypeStruct(x.shape, _F32),
                 jax.ShapeDtypeStruct(pool_slabs.shape, _F32),
                 jax.ShapeDtypeStruct(sre.shape, _F32),
                 jax.ShapeDtypeStruct(sim.shape, _F32),
                 jax.ShapeDtypeStruct((POOL_KEEP * Bm, POOL_WIDTH), _F32),
                 jax.ShapeDtypeStruct((Bm, SSM_FLAT), _F32),
                 jax.ShapeDtypeStruct((Bm, SSM_FLAT), _F32)]
    return pl.pallas_call(
        kern,
        grid=(1,),
        in_specs=[_const_spec(a.shape) for a in ins],
        out_specs=[_const_spec(s.shape) for s in out_shape],
        out_shape=out_shape,
        compiler_params=_compiler_params(),
        name="mixer_small",
    )(*ins)


def _up_tile(hb, wup_ref, j):
    return (_dot(hb, wup_ref[:, j * FF_TILE:(j + 1) * FF_TILE]),
            _dot(hb, wup_ref[:, D_FF + j * FF_TILE:D_FF + (j + 1) * FF_TILE]))


def _conv_tile(g_ext, B, M, convw_ref, convb_ref, fs):
    conv = convb_ref[:, fs]
    for k in range(CONV_W):
        conv = conv + convw_ref[k:k + 1, fs] * g_ext[k * B:k * B + M, :]
    return conv


_FFN_WEIGHTS = ("n2g", "w_up", "conv_w", "conv_b", "w_down")


def _ffn_main_kernel(x_ref, chist_ref, n2g_ref, wup_ref, convw_ref, convb_ref, wdown_ref, nfg_ref,
                     xo_ref, convst_ref,
                     carry_ref, *slab_refs, B, T, n_chunks, final_norm, batch_major_out):
    M = B * T
    keep = CONV_HIST * B
    c = pl.program_id(0)

    @pl.when(c == 0)
    def _():
        carry_ref[...] = chist_ref[...]

    x = x_ref[...]
    hb = _rmsnorm(x, n2g_ref[...]).astype(_BF16)
    acc = None
    g, v = _up_tile(hb, wup_ref, 0)
    for j in range(FF_TILES):
        fs = slice(j * FF_TILE, (j + 1) * FF_TILE)
        nxt = _up_tile(hb, wup_ref, j + 1) if j + 1 < FF_TILES else None
        g_ext = jnp.concatenate([carry_ref[:, fs], g], axis=0)
        conv = _conv_tile(g_ext, B, M, convw_ref, convb_ref, fs)
        carry_ref[:, fs] = g_ext[M:M + keep, :]
        act = (jax.nn.gelu(conv) * v).astype(_BF16)
        contrib = _dot(act, wdown_ref[fs, :])
        acc = contrib if acc is None else acc + contrib
        if nxt is not None:
            g, v = nxt
    xo = x + acc
    if final_norm:
        xo = _rmsnorm(xo, nfg_ref[...])
    if batch_major_out:
        (ys_ref,) = slab_refs
        for j in range(D_SLABS):
            ys_ref[j] = xo[:, j * V7X_LANES:(j + 1) * V7X_LANES]
        for b in range(B):
            for j in range(D_SLABS):
                xo_ref[b, :, j * V7X_LANES:(j + 1) * V7X_LANES] = ys_ref[j, pl.ds(b, T, stride=B), :]
    else:
        xo_ref[...] = xo

    @pl.when(c == n_chunks - 1)
    def _():
        convst_ref[...] = carry_ref[...]


def _ffn_main(x, chist, lw, nfg, *, B, T, final_norm, batch_major_out):
    M = B * T
    n_chunks = x.shape[0] // M
    keep = CONV_HIST * B
    kern = functools.partial(_ffn_main_kernel, B=B, T=T, n_chunks=n_chunks,
                             final_norm=final_norm, batch_major_out=batch_major_out)
    consts = [lw[k] for k in _FFN_WEIGHTS] + [nfg]
    if batch_major_out:
        o_spec = pl.BlockSpec((B, T, D_MODEL), lambda c: (0, c, 0))
        o_shape = jax.ShapeDtypeStruct((B, n_chunks * T, D_MODEL), _F32)
        slabs = [pltpu.VMEM((D_SLABS, M, V7X_LANES), _F32)]
    else:
        o_spec = pl.BlockSpec((M, D_MODEL), lambda c: (c, 0))
        o_shape = jax.ShapeDtypeStruct((n_chunks * M, D_MODEL), _F32)
        slabs = []
    return pl.pallas_call(
        kern,
        grid=(n_chunks,),
        in_specs=[pl.BlockSpec((M, D_MODEL), lambda c: (c, 0)), _const_spec(chist.shape)]
        + [_const_spec(a.shape) for a in consts],
        out_specs=[o_spec, _const_spec((keep, D_FF))],
        out_shape=[o_shape, jax.ShapeDtypeStruct((keep, D_FF), _F32)],
        scratch_shapes=[pltpu.VMEM((keep, D_FF), _F32)] + slabs,
        compiler_params=_compiler_params(),
        name="convffn_main",
    )(x, chist, *consts)


def _ffn_small_kernel(x_ref, conv_ref, n2g_ref, wup_ref, convw_ref, convb_ref, wdown_ref, nfg_ref,
                      xo_ref, conv_o_ref, mconv_o_ref, *, Bs, Bm, Tm, final_norm):
    Ms, Mm = Bs, Bm * Tm
    x = x_ref[...]
    hb = _rmsnorm(x, n2g_ref[...]).astype(_BF16)
    stride = CONV_HIST * FF_SLABS
    acc = None
    for jt in range(FF_TILES):
        fs = slice(jt * FF_TILE, (jt + 1) * FF_TILE)
        g, v = _up_tile(hb, wup_ref, jt)
        slab0 = jt * FF_TILE_SLABS
        hist = [jnp.concatenate([conv_ref[pl.ds(j * FF_SLABS + slab0 + q, Bs, stride=stride), :]
                                 for q in range(FF_TILE_SLABS)], axis=-1)
                for j in range(CONV_HIST)]
        g_ext_s = jnp.concatenate(hist + [g[0:Ms]], axis=0)
        new_hist = [hist[1], g[0:Ms]]
        for j in range(CONV_HIST):
            for q in range(FF_TILE_SLABS):
                conv_o_ref[pl.ds(j * FF_SLABS + slab0 + q, Bs, stride=stride), :] = (
                    new_hist[j][:, q * V7X_LANES:(q + 1) * V7X_LANES])
        g_ext_m = jnp.concatenate([jnp.zeros((CONV_HIST * Bm, FF_TILE), _F32), g[Ms:]], axis=0)
        mconv_o_ref[:, fs] = g_ext_m[Mm:Mm + CONV_HIST * Bm, :]
        conv = jnp.concatenate([_conv_tile(g_ext_s, Bs, Ms, convw_ref, convb_ref, fs),
                                _conv_tile(g_ext_m, Bm, Mm, convw_ref, convb_ref, fs)], axis=0)
        act = (jax.nn.gelu(conv) * v).astype(_BF16)
        contrib = _dot(act, wdown_ref[fs, :])
        acc = contrib if acc is None else acc + contrib
    xo = x + acc
    if final_norm:
        xo = _rmsnorm(xo, nfg_ref[...])
    xo_ref[...] = xo


def _ffn_small(x, conv_slabs, lw, nfg, *, Bs, Bm, Tm, final_norm):
    kern = functools.partial(_ffn_small_kernel, Bs=Bs, Bm=Bm, Tm=Tm, final_norm=final_norm)
    ins = [x, conv_slabs] + [lw[k] for k in _FFN_WEIGHTS] + [nfg]
    out_shape = [jax.ShapeDtypeStruct(x.shape, _F32),
                 jax.ShapeDtypeStruct(conv_slabs.shape, _F32),
                 jax.ShapeDtypeStruct((CONV_HIST * Bm, D_FF), _F32)]
    return pl.pallas_call(
        kern,
        grid=(1,),
        in_specs=[_const_spec(a.shape) for a in ins],
        out_specs=[_const_spec(s.shape) for s in out_shape],
        out_shape=out_shape,
        compiler_params=_compiler_params(),
        name="convffn_small",
    )(*ins)


def kernel(x_prompt, x_sample, state_pool, state_ssm_re, state_ssm_im, state_conv, meta_tokens,
           norm1_g, w_in, pool_w, pool_scale, ssm_A_re, ssm_A_im, ssm_log_dt, ssm_B_re, ssm_B_im,
           ssm_C_re, ssm_C_im, ssm_D, w_glu, w_out, norm2_g, w_up, conv_w, conv_b, w_down, norm_f_g):
    depth = w_in.shape[0]
    ab_re, ab_im, bb_re, bb_im = _discretise(ssm_A_re, ssm_A_im, ssm_log_dt, ssm_B_re, ssm_B_im)
    layers = []
    for l in range(depth):
        layers.append(dict(
            n1g=norm1_g[l][None], w_in=w_in[l].astype(_BF16), pool_w=pool_w[l].astype(_BF16),
            pool_scale=pool_scale[l][None], a_re=ab_re[l], a_im=ab_im[l],
            bbc=jnp.concatenate([_block_diag_b(bb_re[l]), _block_diag_b(bb_im[l])],
                                axis=-1).astype(_BF16),
            cre=_block_diag_c(ssm_C_re[l]).astype(_BF16),
            cim=_block_diag_c(ssm_C_im[l]).astype(_BF16),
            dskip=ssm_D[l][None], w_glu=w_glu[l].astype(_BF16), w_out=w_out[l].astype(_BF16),
            n2g=norm2_g[l][None], w_up=w_up[l].astype(_BF16), conv_w=conv_w[l],
            conv_b=conv_b[l][None], w_down=w_down[l].astype(_BF16)))
    nfg = norm_f_g[None]

    bp, seq, _ = x_prompt.shape
    bs = x_sample.shape[0]
    assert seq % PROMPT_CHUNK_STEPS == 0

    xs = jnp.concatenate([x_sample.reshape(bs, D_MODEL), jnp.repeat(meta_tokens, bp, axis=0)], axis=0)
    pool_sl = state_pool.reshape(depth, bs * POOL_HIST * POOL_SLABS, V7X_LANES)
    conv_sl = state_conv.reshape(depth, bs * CONV_HIST * FF_SLABS, V7X_LANES)
    sre = state_ssm_re.reshape(depth, bs, SSM_FLAT)
    sim = state_ssm_im.reshape(depth, bs, SSM_FLAT)
    small = dict(Bs=bs, Bm=bp, Tm=N_META)
    pool_s, re_s, im_s, conv_s, meta_state = [], [], [], [], []
    for l, lw in enumerate(layers):
        xs, p, hr, hi, mp, mr, mi = _mixer_small(xs, pool_sl[l], sre[l], sim[l], lw, **small)
        xs, cv, mc = _ffn_small(xs, conv_sl[l], lw, nfg, final_norm=(l == depth - 1), **small)
        pool_s.append(p.reshape(bs, POOL_HIST, POOL_WIDTH))
        re_s.append(hr.reshape(bs, SSM_GROUPS, SSM_STATE))
        im_s.append(hi.reshape(bs, SSM_GROUPS, SSM_STATE))
        conv_s.append(cv.reshape(bs, CONV_HIST, D_FF))
        meta_state.append((mp, mr, mi, mc))
    y_sample = xs[0:bs].reshape(bs, 1, D_MODEL)

    xp = x_prompt
    pool_p, re_p, im_p, conv_p = [], [], [], []
    main = dict(B=bp, T=PROMPT_CHUNK_STEPS)
    for l, lw in enumerate(layers):
        mp, mr, mi, mc = meta_state[l]
        xp, p, hr, hi = _mixer_main(xp, mp, mr, mi, lw, start_pos=N_META,
                                    batch_major_in=(l == 0), **main)
        last = l == depth - 1
        xp, cv = _ffn_main(xp, mc, lw, nfg, final_norm=last, batch_major_out=last, **main)
        pool_p.append(p.reshape(POOL_KEEP, bp, POOL_WIDTH)[1:].transpose(1, 0, 2))
        re_p.append(hr.reshape(bp, SSM_GROUPS, SSM_STATE))
        im_p.append(hi.reshape(bp, SSM_GROUPS, SSM_STATE))
        conv_p.append(cv.reshape(CONV_HIST, bp, D_FF).transpose(1, 0, 2))
    y_prompt = xp

    st = jnp.stack
    return (y_prompt, y_sample, st(pool_p), st(re_p), st(im_p), st(conv_p),
            st(pool_s), st(re_s), st(im_s), st(conv_s))
```

```python
import functools

import jax
import jax.numpy as jnp
from jax import lax
from jax.experimental import pallas as pl
from jax.experimental.pallas import tpu as pltpu

D_MODEL = 1024
N_META = 16
POOL_WIDTH = 512
POOL_WINDOWS = (2, 4, 8, 16)
POOL_GROUPS = len(POOL_WINDOWS)
POOL_GROUP_IN = 128
POOL_GROUP_OUT = 256
POOL_HIST = 15
POOL_KEEP = 16
SSM_WIDTH = 512
SSM_GROUP_CH = 16
SSM_GROUPS = 32
SSM_STATE = 64
SSM_FLAT = SSM_GROUPS * SSM_STATE
SSM_TILE = 512
SSM_TILES = SSM_FLAT // SSM_TILE
SSM_TILE_GROUPS = SSM_TILE // SSM_STATE
SSM_TILE_CH = SSM_TILE_GROUPS * SSM_GROUP_CH
IN_COLS = POOL_WIDTH + SSM_WIDTH + 2 * D_MODEL
D_FF = 2816
FF_TILE = 256
FF_TILES = D_FF // FF_TILE
CONV_W = 3
CONV_HIST = CONV_W - 1
RMS_EPS = 1e-6
PAST_LEN = 16384

V7X_LANES = 128
D_SLABS = D_MODEL // V7X_LANES
V7X_VMEM_LIMIT_BYTES = 56 * 1024 * 1024
PROMPT_CHUNK_STEPS = 64

_F32 = jnp.float32
_BF16 = jnp.bfloat16


def _rmsnorm(x, g):
    return x * lax.rsqrt(jnp.mean(x * x, axis=-1, keepdims=True) + RMS_EPS) * g


def _dot(a, b):
    return jnp.dot(a, b, preferred_element_type=_F32)


def _whole(shape):
    nd = len(shape)
    return pl.BlockSpec(shape, lambda *_: (0,) * nd, pipeline_mode=pl.Buffered(1))


def _layer(arr, l):
    nd = arr.ndim - 1
    return pl.BlockSpec((None,) + arr.shape[1:], lambda *_: (l,) + (0,) * nd,
                        pipeline_mode=pl.Buffered(1))


def _compiler_params():
    return pltpu.CompilerParams(dimension_semantics=("arbitrary",),
                                vmem_limit_bytes=V7X_VMEM_LIMIT_BYTES)


def _disc_kernel(are_ref, aim_ref, ldt_ref, bre_ref, bim_ref, ctre_ref, ctim_ref,
                 abre_ref, abim_ref, bbc_ref, cre_ref, cim_ref):
    lam_re = are_ref[...]
    lam_im = aim_ref[...]
    dt = jnp.exp(ldt_ref[...])
    mag = jnp.exp(lam_re * dt)
    ab_re = mag * jnp.cos(lam_im * dt)
    ab_im = mag * jnp.sin(lam_im * dt)
    den = lam_re * lam_re + lam_im * lam_im
    nr = ab_re - 1.0
    ni = ab_im
    q_re = (nr * lam_re + ni * lam_im) / den
    q_im = (ni * lam_re - nr * lam_im) / den
    b_re = bre_ref[...]
    b_im = bim_ref[...]
    abre_ref[...] = ab_re
    abim_ref[...] = ab_im
    bb_re = (q_re * b_re - q_im * b_im).astype(_BF16)
    bb_im = (q_re * b_im + q_im * b_re).astype(_BF16)
    ct_re = ctre_ref[...].astype(_BF16)
    ct_im = ctim_ref[...].astype(_BF16)
    bbc_ref[...] = jnp.zeros(bbc_ref.shape, _BF16)
    cre_ref[...] = jnp.zeros(cre_ref.shape, _BF16)
    cim_ref[...] = jnp.zeros(cim_ref.shape, _BF16)
    for i in range(SSM_TILES):
        for gl in range(SSM_TILE_GROUPS):
            g = i * SSM_TILE_GROUPS + gl
            ch = slice(gl * SSM_GROUP_CH, (gl + 1) * SSM_GROUP_CH)
            st = slice(gl * SSM_STATE, (gl + 1) * SSM_STATE)
            st_im = slice(SSM_TILE + gl * SSM_STATE, SSM_TILE + (gl + 1) * SSM_STATE)
            src_st = slice(g * SSM_STATE, (g + 1) * SSM_STATE)
            src_ch = slice(g * SSM_GROUP_CH, (g + 1) * SSM_GROUP_CH)
            bbc_ref[i, ch, st] = bb_re[:, src_st]
            bbc_ref[i, ch, st_im] = bb_im[:, src_st]
            cre_ref[i, st, ch] = ct_re[:, src_ch]
            cim_ref[i, st, ch] = ct_im[:, src_ch]


def _discretise(a_re, a_im, log_dt, b_re, b_im, c_re, c_im):
    depth = a_re.shape[0]
    flat = lambda a: a.reshape(depth, 1, SSM_FLAT)
    ldt = jnp.broadcast_to(log_dt[:, :, None], (depth, SSM_GROUPS, SSM_STATE))
    bt = lambda b: b.transpose(0, 3, 1, 2).reshape(depth, SSM_GROUP_CH, SSM_FLAT)
    ct = lambda c: c.transpose(0, 3, 1, 2).reshape(depth, SSM_STATE, SSM_WIDTH)
    ins = [flat(a_re), flat(a_im), flat(ldt), bt(b_re), bt(b_im), ct(c_re), ct(c_im)]
    out_shape = [jax.ShapeDtypeStruct((depth, 1, SSM_FLAT), _F32)] * 2 + [
        jax.ShapeDtypeStruct((depth, SSM_TILES, SSM_TILE_CH, 2 * SSM_TILE), _BF16),
        jax.ShapeDtypeStruct((depth, SSM_TILES, SSM_TILE, SSM_TILE_CH), _BF16),
        jax.ShapeDtypeStruct((depth, SSM_TILES, SSM_TILE, SSM_TILE_CH), _BF16)]
    per_layer = lambda s: pl.BlockSpec((None,) + s.shape[1:],
                                       lambda l: (l,) + (0,) * (len(s.shape) - 1))
    return pl.pallas_call(
        _disc_kernel,
        grid=(depth,),
        in_specs=[per_layer(a) for a in ins],
        out_specs=[per_layer(s) for s in out_shape],
        out_shape=out_shape,
        name="s5_discretise",
    )(*ins)


def _pool_branch(ext, n_hist, B, T, pos0, poolw, pscale):
    M = B * T
    base = n_hist * B
    pos = pos0 + lax.broadcasted_iota(jnp.int32, (M, 1), 0) // B
    tiles = []
    for g, w in enumerate(POOL_WINDOWS):
        ls = slice(g * POOL_GROUP_IN, (g + 1) * POOL_GROUP_IN)
        tok = ext[base:base + M, ls]
        win_sum = tok
        for k in range(1, w):
            win_sum = win_sum + ext[base - k * B:base - k * B + M, ls]
        cnt = jnp.minimum(w, pos + 1).astype(_F32)
        d = win_sum / cnt - tok
        tiles.append(_dot(d.astype(_BF16), poolw(g)))
    return jnp.concatenate(tiles, axis=-1) * pscale


def _ssm_inputs(u_ssm_b, bbc_ref):
    return [_dot(u_ssm_b[:, i * SSM_TILE_CH:(i + 1) * SSM_TILE_CH], bbc_ref[i])
            for i in range(SSM_TILES)]


def _ssm_tile(i, bu, h_re, h_im, B, T, are_ref, aim_ref, cre_ref, cim_ref):
    sl = slice(i * SSM_TILE, (i + 1) * SSM_TILE)
    a_re = jnp.broadcast_to(are_ref[:, sl], (B, SSM_TILE))
    a_im = jnp.broadcast_to(aim_ref[:, sl], (B, SSM_TILE))
    hs_re, hs_im = [], []
    for t in range(T):
        rows = slice(t * B, (t + 1) * B)
        n_re = a_re * h_re - a_im * h_im + bu[rows, 0:SSM_TILE]
        n_im = a_re * h_im + a_im * h_re + bu[rows, SSM_TILE:2 * SSM_TILE]
        h_re, h_im = n_re, n_im
        hs_re.append(h_re)
        hs_im.append(h_im)
    y = (_dot(jnp.concatenate(hs_re, axis=0).astype(_BF16), cre_ref[i])
         - _dot(jnp.concatenate(hs_im, axis=0).astype(_BF16), cim_ref[i]))
    return y, h_re, h_im


def _glu(y_tiles, u_ssm, dskip, wglu):
    y_s = jnp.concatenate(y_tiles, axis=-1) + dskip * u_ssm
    zz = _dot(jax.nn.gelu(y_s).astype(_BF16), wglu)
    return zz[:, 0:D_MODEL] * jax.nn.sigmoid(zz[:, D_MODEL:])


def _mixer_main_kernel(x_ref, hist_ref, h0re_ref, h0im_ref, n1g_ref, pscale_ref, are_ref, aim_ref,
                       bbc_ref, cre_ref, cim_ref, dskip_ref, win_ref, poolw_ref, wglu_ref, wout_ref,
                       xo_ref, poolst_ref, hre_o_ref, him_o_ref,
                       pool_ref, hst_ref, *slab_refs, B, T, start_pos, n_chunks, batch_major_in):
    M = B * T
    c = pl.program_id(0)

    @pl.when(c == 0)
    def _():
        pool_ref[...] = hist_ref[...]
        hst_ref[0] = h0re_ref[...]
        hst_ref[1] = h0im_ref[...]

    if batch_major_in:
        (xs_ref,) = slab_refs
        for b in range(B):
            for j in range(D_SLABS):
                xs_ref[j, pl.ds(b, T, stride=B), :] = x_ref[b, :, j * V7X_LANES:(j + 1) * V7X_LANES]
        x = jnp.concatenate([xs_ref[j] for j in range(D_SLABS)], axis=-1)
    else:
        x = x_ref[...]
    hb = _rmsnorm(x, n1g_ref[...]).astype(_BF16)

    def gate(lo):
        return jax.nn.sigmoid(_dot(hb, win_ref[:, lo:lo + D_MODEL]))

    u = _dot(hb, win_ref[:, 0:POOL_WIDTH + SSM_WIDTH])
    u_ssm = u[:, POOL_WIDTH:]
    ext = jnp.concatenate([pool_ref[...], u[:, 0:POOL_WIDTH]], axis=0)
    pool_ref[...] = ext[M:M + POOL_KEEP * B, :]

    bus = _ssm_inputs(u_ssm.astype(_BF16), bbc_ref)
    y_pool = _pool_branch(ext, POOL_KEEP, B, T, start_pos + c * T, lambda g: poolw_ref[g],
                          pscale_ref[...])
    merge_pool = gate(POOL_WIDTH + SSM_WIDTH) * y_pool

    def tile(i):
        sl = slice(i * SSM_TILE, (i + 1) * SSM_TILE)
        y, h_re, h_im = _ssm_tile(i, bus[i], hst_ref[0, :, sl], hst_ref[1, :, sl], B, T,
                                  are_ref, aim_ref, cre_ref, cim_ref)
        hst_ref[0, :, sl] = h_re
        hst_ref[1, :, sl] = h_im
        return y

    y_tiles = [tile(0), tile(1)]
    gate_ssm = gate(POOL_WIDTH + SSM_WIDTH + D_MODEL)
    y_tiles += [tile(2), tile(3)]
    merge = merge_pool + gate_ssm * _glu(y_tiles, u_ssm, dskip_ref[...], wglu_ref[...])
    xo_ref[...] = x + _dot(merge.astype(_BF16), wout_ref[...])

    @pl.when(c == n_chunks - 1)
    def _():
        poolst_ref[...] = pool_ref[...]
        hre_o_ref[...] = hst_ref[0]
        him_o_ref[...] = hst_ref[1]


def _mixer_main(x, meta_state, p, l, wb, *, B, T, start_pos, batch_major_in):
    M = B * T
    if batch_major_in:
        n_chunks = x.shape[1] // T
        x_spec = pl.BlockSpec((B, T, D_MODEL), lambda c: (0, c, 0))
        slabs = [pltpu.VMEM((D_SLABS, M, V7X_LANES), _F32)]
    else:
        n_chunks = x.shape[0] // M
        x_spec = pl.BlockSpec((M, D_MODEL), lambda c: (c, 0))
        slabs = []
    keep = POOL_KEEP * B
    kern = functools.partial(_mixer_main_kernel, B=B, T=T, start_pos=start_pos,
                             n_chunks=n_chunks, batch_major_in=batch_major_in)
    stacked = [p["n1g"], p["pool_scale"], p["a_re"], p["a_im"], p["bbc"], p["cre"], p["cim"],
               p["dskip"]]
    return pl.pallas_call(
        kern,
        grid=(n_chunks,),
        in_specs=[x_spec] + [_whole(a.shape) for a in meta_state]
        + [_layer(a, l) for a in stacked] + [_whole(a.shape) for a in wb],
        out_specs=[pl.BlockSpec((M, D_MODEL), lambda c: (c, 0)), _whole((keep, POOL_WIDTH)),
                   _whole((B, SSM_FLAT)), _whole((B, SSM_FLAT))],
        out_shape=[jax.ShapeDtypeStruct((n_chunks * M, D_MODEL), _F32),
                   jax.ShapeDtypeStruct((keep, POOL_WIDTH), _F32),
                   jax.ShapeDtypeStruct((B, SSM_FLAT), _F32),
                   jax.ShapeDtypeStruct((B, SSM_FLAT), _F32)],
        scratch_shapes=[pltpu.VMEM((keep, POOL_WIDTH), _F32),
                        pltpu.VMEM((2, B, SSM_FLAT), _F32)] + slabs,
        compiler_params=_compiler_params(),
        name="mixer_main",
    )(x, *meta_state, *stacked, *wb)


def _mixer_small_kernel(x_ref, pool_ref, sre_ref, sim_ref, n1g_ref, pscale_ref, are_ref, aim_ref,
                        bbc_ref, cre_ref, cim_ref, dskip_ref, win_ref, poolw_ref, wglu_ref,
                        wout_ref, *refs, Bs, Bm, Tm, n_alias):
    (xo_ref, pool_o_ref, sre_o_ref, sim_o_ref, mpool_o_ref, mre_o_ref, mim_o_ref,
     win_o_ref, poolw_o_ref, wglu_o_ref, wout_o_ref) = refs[n_alias:]
    Ms, Mm = Bs, Bm * Tm
    win_o_ref[...] = win_ref[...].astype(_BF16)
    poolw_o_ref[...] = poolw_ref[...].astype(_BF16)
    wglu_o_ref[...] = wglu_ref[...].astype(_BF16)
    wout_o_ref[...] = wout_ref[...].astype(_BF16)

    x = x_ref[...]
    hb = _rmsnorm(x, n1g_ref[...]).astype(_BF16)
    proj = _dot(hb, win_o_ref[...])
    u_pool, u_ssm = proj[:, 0:POOL_WIDTH], proj[:, POOL_WIDTH:POOL_WIDTH + SSM_WIDTH]
    gate_pool = jax.nn.sigmoid(proj[:, POOL_WIDTH + SSM_WIDTH:POOL_WIDTH + SSM_WIDTH + D_MODEL])
    gate_ssm = jax.nn.sigmoid(proj[:, POOL_WIDTH + SSM_WIDTH + D_MODEL:])
    bus = _ssm_inputs(u_ssm.astype(_BF16), bbc_ref)
    poolw = lambda g: poolw_o_ref[g]

    hist = [pool_ref[:, j, :] for j in range(POOL_HIST)]
    new_hist = hist[1:] + [u_pool[0:Ms]]
    for j in range(POOL_HIST):
        pool_o_ref[:, j, :] = new_hist[j]
    y_pool_s = _pool_branch(jnp.concatenate(hist + [u_pool[0:Ms]], axis=0), POOL_HIST, Bs, 1,
                            PAST_LEN, poolw, pscale_ref[...])
    ext_m = jnp.concatenate([jnp.zeros((POOL_KEEP * Bm, POOL_WIDTH), _F32), u_pool[Ms:]], axis=0)
    y_pool_m = _pool_branch(ext_m, POOL_KEEP, Bm, Tm, 0, poolw, pscale_ref[...])
    mpool_o_ref[...] = ext_m[Mm:Mm + POOL_KEEP * Bm, :]
    y_pool = jnp.concatenate([y_pool_s, y_pool_m], axis=0)

    y_tiles = []
    for i in range(SSM_TILES):
        sl = slice(i * SSM_TILE, (i + 1) * SSM_TILE)
        ys, h_re, h_im = _ssm_tile(i, bus[i][0:Ms], sre_ref[:, sl], sim_ref[:, sl], Bs, 1,
                                   are_ref, aim_ref, cre_ref, cim_ref)
        sre_o_ref[:, sl] = h_re
        sim_o_ref[:, sl] = h_im
        zero = jnp.zeros((Bm, SSM_TILE), _F32)
        ym, h_re, h_im = _ssm_tile(i, bus[i][Ms:], zero, zero, Bm, Tm,
                                   are_ref, aim_ref, cre_ref, cim_ref)
        mre_o_ref[:, sl] = h_re
        mim_o_ref[:, sl] = h_im
        y_tiles.append(jnp.concatenate([ys, ym], axis=0))
    merge = gate_pool * y_pool + gate_ssm * _glu(y_tiles, u_ssm, dskip_ref[...], wglu_o_ref[...])
    xo_ref[...] = x + _dot(merge.astype(_BF16), wout_o_ref[...])


def _mixer_small(x, p, l, prev, *, Bs, Bm, Tm):
    kern = functools.partial(_mixer_small_kernel, Bs=Bs, Bm=Bm, Tm=Tm, n_alias=len(prev))
    stacked = [p["state_pool"], p["state_re"], p["state_im"], p["n1g"], p["pool_scale"], p["a_re"],
               p["a_im"], p["bbc"], p["cre"], p["cim"], p["dskip"], p["w_in"], p["pool_w"],
               p["w_glu"], p["w_out"]]
    sds = jax.ShapeDtypeStruct
    state_shapes = [sds(p["state_pool"].shape, _F32), sds(p["state_re"].shape, _F32),
                    sds(p["state_im"].shape, _F32)]
    plain_shapes = [sds((POOL_KEEP * Bm, POOL_WIDTH), _F32), sds((Bm, SSM_FLAT), _F32),
                    sds((Bm, SSM_FLAT), _F32), sds((D_MODEL, IN_COLS), _BF16),
                    sds((POOL_GROUPS, POOL_GROUP_IN, POOL_GROUP_OUT), _BF16),
                    sds((SSM_WIDTH, 2 * D_MODEL), _BF16), sds((D_MODEL, D_MODEL), _BF16)]
    n_in = 1 + len(stacked)
    return pl.pallas_call(
        kern,
        grid=(1,),
        in_specs=[_whole(x.shape)] + [_layer(a, l) for a in stacked]
        + [pl.BlockSpec(memory_space=pl.ANY)] * len(prev),
        out_specs=[_whole(x.shape)] + [_layer(s, l) for s in state_shapes]
        + [_whole(s.shape) for s in plain_shapes],
        out_shape=[sds(x.shape, _F32)] + state_shapes + plain_shapes,
        input_output_aliases={n_in + k: 1 + k for k in range(len(prev))},
        compiler_params=_compiler_params(),
        name="mixer_small",
    )(x, *stacked, *prev)


def _conv_tile(g_ext, B, M, convw, convb):
    conv = convb
    for k in range(CONV_W):
        conv = conv + convw[k:k + 1, :] * g_ext[k * B:k * B + M, :]
    return conv


def _ffn_main_kernel(x_ref, chist_ref, n2g_ref, convw_ref, convb_ref, nfg_ref, wg_ref, wv_ref,
                     wdown_ref, xo_ref, convst_ref,
                     carry_ref, *slab_refs, B, T, n_chunks, final_norm, batch_major_out):
    M = B * T
    keep = CONV_HIST * B
    c = pl.program_id(0)

    @pl.when(c == 0)
    def _():
        carry_ref[...] = chist_ref[...]

    x = x_ref[...]
    hb = _rmsnorm(x, n2g_ref[...]).astype(_BF16)

    def up(j):
        fs = slice(j * FF_TILE, (j + 1) * FF_TILE)
        return _dot(hb, wg_ref[:, fs]), _dot(hb, wv_ref[:, fs])

    acc = None
    g, v = up(0)
    for j in range(FF_TILES):
        fs = slice(j * FF_TILE, (j + 1) * FF_TILE)
        nxt = up(j + 1) if j + 1 < FF_TILES else None
        g_ext = jnp.concatenate([carry_ref[:, fs], g], axis=0)
        conv = _conv_tile(g_ext, B, M, convw_ref[:, fs], convb_ref[:, fs])
        carry_ref[:, fs] = g_ext[M:M + keep, :]
        act = (jax.nn.gelu(conv) * v).astype(_BF16)
        contrib = _dot(act, wdown_ref[fs, :])
        acc = contrib if acc is None else acc + contrib
        if nxt is not None:
            g, v = nxt
    xo = x + acc
    if final_norm:
        xo = _rmsnorm(xo, nfg_ref[...])
    if batch_major_out:
        (ys_ref,) = slab_refs
        for j in range(D_SLABS):
            ys_ref[j] = xo[:, j * V7X_LANES:(j + 1) * V7X_LANES]
        for b in range(B):
            for j in range(D_SLABS):
                xo_ref[b, :, j * V7X_LANES:(j + 1) * V7X_LANES] = ys_ref[j, pl.ds(b, T, stride=B), :]
    else:
        xo_ref[...] = xo

    @pl.when(c == n_chunks - 1)
    def _():
        convst_ref[...] = carry_ref[...]


def _ffn_main(x, chist, p, l, wb, *, B, T, final_norm, batch_major_out):
    M = B * T
    n_chunks = x.shape[0] // M
    keep = CONV_HIST * B
    kern = functools.partial(_ffn_main_kernel, B=B, T=T, n_chunks=n_chunks,
                             final_norm=final_norm, batch_major_out=batch_major_out)
    stacked = [p["n2g"], p["conv_w"], p["conv_b"]]
    if batch_major_out:
        o_spec = pl.BlockSpec((B, T, D_MODEL), lambda c: (0, c, 0))
        o_shape = jax.ShapeDtypeStruct((B, n_chunks * T, D_MODEL), _F32)
        slabs = [pltpu.VMEM((D_SLABS, M, V7X_LANES), _F32)]
    else:
        o_spec = pl.BlockSpec((M, D_MODEL), lambda c: (c, 0))
        o_shape = jax.ShapeDtypeStruct((n_chunks * M, D_MODEL), _F32)
        slabs = []
    return pl.pallas_call(
        kern,
        grid=(n_chunks,),
        in_specs=[pl.BlockSpec((M, D_MODEL), lambda c: (c, 0)), _whole(chist.shape)]
        + [_layer(a, l) for a in stacked] + [_whole(p["nfg"].shape)]
        + [_whole(a.shape) for a in wb],
        out_specs=[o_spec, _whole((keep, D_FF))],
        out_shape=[o_shape, jax.ShapeDtypeStruct((keep, D_FF), _F32)],
        scratch_shapes=[pltpu.VMEM((keep, D_FF), _F32)] + slabs,
        compiler_params=_compiler_params(),
        name="convffn_main",
    )(x, chist, *stacked, p["nfg"], *wb)


def _ffn_small_kernel(x_ref, conv_ref, n2g_ref, convw_ref, convb_ref, nfg_ref, wg_ref, wv_ref,
                      wdown_ref, *refs, Bs, Bm, Tm, final_norm, n_alias):
    (xo_ref, conv_o_ref, mconv_o_ref, wg_o_ref, wv_o_ref, wdown_o_ref,
     hb_ref, acc_ref) = refs[n_alias:]
    Ms, Mm = Bs, Bm * Tm
    jt = pl.program_id(0)

    @pl.when(jt == 0)
    def _():
        hb_ref[...] = _rmsnorm(x_ref[...], n2g_ref[...]).astype(_BF16)
        acc_ref[...] = jnp.zeros(acc_ref.shape, _F32)

    wg_o_ref[...] = wg_ref[...].astype(_BF16)
    wv_o_ref[...] = wv_ref[...].astype(_BF16)
    wdown_o_ref[...] = wdown_ref[...].astype(_BF16)
    hb = hb_ref[...]
    g = _dot(hb, wg_o_ref[...])
    v = _dot(hb, wv_o_ref[...])
    hist = [conv_ref[:, j, :] for j in range(CONV_HIST)]
    new_hist = hist[1:] + [g[0:Ms]]
    for j in range(CONV_HIST):
        conv_o_ref[:, j, :] = new_hist[j]
    g_ext_s = jnp.concatenate(hist + [g[0:Ms]], axis=0)
    g_ext_m = jnp.concatenate([jnp.zeros((CONV_HIST * Bm, FF_TILE), _F32), g[Ms:]], axis=0)
    mconv_o_ref[...] = g_ext_m[Mm:Mm + CONV_HIST * Bm, :]
    conv = jnp.concatenate([_conv_tile(g_ext_s, Bs, Ms, convw_ref[...], convb_ref[...]),
                            _conv_tile(g_ext_m, Bm, Mm, convw_ref[...], convb_ref[...])], axis=0)
    act = (jax.nn.gelu(conv) * v).astype(_BF16)
    acc_ref[...] += _dot(act, wdown_o_ref[...])

    @pl.when(jt == FF_TILES - 1)
    def _():
        xo = x_ref[...] + acc_ref[...]
        if final_norm:
            xo = _rmsnorm(xo, nfg_ref[...])
        xo_ref[...] = xo


def _ffn_small(x, p, l, prev, *, Bs, Bm, Tm, final_norm):
    kern = functools.partial(_ffn_small_kernel, Bs=Bs, Bm=Bm, Tm=Tm, final_norm=final_norm,
                             n_alias=len(prev))
    sds = jax.ShapeDtypeStruct
    tile_of = lambda a, idx: pl.BlockSpec((None,) + idx[0], idx[1])
    n_rows = x.shape[0]
    in_specs = [
        _whole(x.shape),
        pl.BlockSpec((None, Bs, CONV_HIST, FF_TILE), lambda j: (l, 0, 0, j)),
        _layer(p["n2g"], l),
        pl.BlockSpec((None, CONV_W, FF_TILE), lambda j: (l, 0, j)),
        pl.BlockSpec((None, 1, FF_TILE), lambda j: (l, 0, j)),
        _whole(p["nfg"].shape),
        pl.BlockSpec((None, D_MODEL, FF_TILE), lambda j: (l, 0, j)),
        pl.BlockSpec((None, D_MODEL, FF_TILE), lambda j: (l, 0, FF_TILES + j)),
        pl.BlockSpec((None, FF_TILE, D_MODEL), lambda j: (l, j, 0)),
    ] + [pl.BlockSpec(memory_space=pl.ANY)] * len(prev)
    out_shape = [sds(x.shape, _F32), sds(p["state_conv"].shape, _F32),
                 sds((CONV_HIST * Bm, D_FF), _F32), sds((D_MODEL, D_FF), _BF16),
                 sds((D_MODEL, D_FF), _BF16), sds((D_FF, D_MODEL), _BF16)]
    out_specs = [
        _whole(x.shape),
        pl.BlockSpec((None, Bs, CONV_HIST, FF_TILE), lambda j: (l, 0, 0, j)),
        pl.BlockSpec((CONV_HIST * Bm, FF_TILE), lambda j: (0, j)),
        pl.BlockSpec((D_MODEL, FF_TILE), lambda j: (0, j)),
        pl.BlockSpec((D_MODEL, FF_TILE), lambda j: (0, j)),
        pl.BlockSpec((FF_TILE, D_MODEL), lambda j: (j, 0)),
    ]
    n_in = 9
    return pl.pallas_call(
        kern,
        grid=(FF_TILES,),
        in_specs=in_specs,
        out_specs=out_specs,
        out_shape=out_shape,
        scratch_shapes=[pltpu.VMEM((n_rows, D_MODEL), _BF16), pltpu.VMEM((n_rows, D_MODEL), _F32)],
        input_output_aliases={n_in + k: 1 + k for k in range(len(prev))},
        compiler_params=_compiler_params(),
        name="convffn_small",
    )(x, p["state_conv"], p["n2g"], p["conv_w"], p["conv_b"], p["nfg"], p["w_up"], p["w_up"],
      p["w_down"], *prev)


def kernel(x_prompt, x_sample, state_pool, state_ssm_re, state_ssm_im, state_conv, meta_tokens,
           norm1_g, w_in, pool_w, pool_scale, ssm_A_re, ssm_A_im, ssm_log_dt, ssm_B_re, ssm_B_im,
           ssm_C_re, ssm_C_im, ssm_D, w_glu, w_out, norm2_g, w_up, conv_w, conv_b, w_down, norm_f_g):
    depth = w_in.shape[0]
    bp, seq, _ = x_prompt.shape
    bs = x_sample.shape[0]
    assert seq % PROMPT_CHUNK_STEPS == 0
    ab_re, ab_im, bbc, cre, cim = _discretise(ssm_A_re, ssm_A_im, ssm_log_dt, ssm_B_re, ssm_B_im,
                                              ssm_C_re, ssm_C_im)
    row = lambda a: a[:, None, :]
    p = dict(state_pool=state_pool, state_conv=state_conv,
             state_re=state_ssm_re.reshape(depth, bs, SSM_FLAT),
             state_im=state_ssm_im.reshape(depth, bs, SSM_FLAT),
             n1g=row(norm1_g), pool_scale=row(pool_scale), a_re=ab_re, a_im=ab_im, bbc=bbc,
             cre=cre, cim=cim, dskip=row(ssm_D), w_in=w_in, pool_w=pool_w, w_glu=w_glu,
             w_out=w_out, n2g=row(norm2_g), conv_w=conv_w, conv_b=row(conv_b), w_up=w_up,
             w_down=w_down, nfg=norm_f_g[None])

    xs = jnp.concatenate([x_sample.reshape(bs, D_MODEL), jnp.repeat(meta_tokens, bp, axis=0)], axis=0)
    small = dict(Bs=bs, Bm=bp, Tm=N_META)
    mix_prev, ffn_prev, meta_state, mix_w, ffn_w = [], [], [], [], []
    for l in range(depth):
        xs, pool_s, re_s, im_s, mp, mr, mi, *wb = _mixer_small(xs, p, l, mix_prev, **small)
        mix_prev = [pool_s, re_s, im_s]
        mix_w.append(wb)
        xs, conv_s, mc, *wb = _ffn_small(xs, p, l, ffn_prev, final_norm=(l == depth - 1), **small)
        ffn_prev = [conv_s]
        ffn_w.append(wb)
        meta_state.append(((mp, mr, mi), mc))
    y_sample = xs[0:bs].reshape(bs, 1, D_MODEL)

    xp = x_prompt
    pool_p, re_p, im_p, conv_p = [], [], [], []
    main = dict(B=bp, T=PROMPT_CHUNK_STEPS)
    for l in range(depth):
        mix_state, mc = meta_state[l]
        xp, pp, hr, hi = _mixer_main(xp, mix_state, p, l, mix_w[l], start_pos=N_META,
                                     batch_major_in=(l == 0), **main)
        last = l == depth - 1
        xp, cv = _ffn_main(xp, mc, p, l, ffn_w[l], final_norm=last, batch_major_out=last, **main)
        pool_p.append(pp)
        re_p.append(hr)
        im_p.append(hi)
        conv_p.append(cv)
    st = jnp.stack
    pool_p = st(pool_p).reshape(depth, POOL_KEEP, bp, POOL_WIDTH)[:, 1:].transpose(0, 2, 1, 3)
    conv_p = st(conv_p).reshape(depth, CONV_HIST, bp, D_FF).transpose(0, 2, 1, 3)
    group = lambda h: h.reshape(h.shape[:-1] + (SSM_GROUPS, SSM_STATE))
    return (xp, y_sample, pool_p, group(st(re_p)), group(st(im_p)), conv_p,
            pool_s, group(re_s), group(im_s), conv_s)
```

```python
import functools

import jax
import jax.numpy as jnp
from jax import lax
from jax.experimental import pallas as pl
from jax.experimental.pallas import tpu as pltpu

D_MODEL = 1024
N_META = 16
POOL_WIDTH = 512
POOL_WINDOWS = (2, 4, 8, 16)
POOL_GROUPS = len(POOL_WINDOWS)
POOL_GROUP_IN = 128
POOL_GROUP_OUT = 256
POOL_HIST = 15
POOL_KEEP = 16
SSM_WIDTH = 512
SSM_GROUP_CH = 16
SSM_GROUPS = 32
SSM_STATE = 64
SSM_FLAT = SSM_GROUPS * SSM_STATE
SSM_TILE = 512
SSM_TILES = SSM_FLAT // SSM_TILE
SSM_TILE_GROUPS = SSM_TILE // SSM_STATE
SSM_TILE_CH = SSM_TILE_GROUPS * SSM_GROUP_CH
IN_COLS = POOL_WIDTH + SSM_WIDTH + 2 * D_MODEL
D_FF = 2816
FF_TILE = 256
FF_TILES = D_FF // FF_TILE
FF_LOOKAHEAD = 2
FF_SLOTS = FF_LOOKAHEAD + 1
CONV_W = 3
CONV_HIST = CONV_W - 1
RMS_EPS = 1e-6
PAST_LEN = 16384

V7X_LANES = 128
D_SLABS = D_MODEL // V7X_LANES
V7X_VMEM_LIMIT_BYTES = 56 * 1024 * 1024
PROMPT_CHUNK_STEPS = 64

_F32 = jnp.float32
_BF16 = jnp.bfloat16


def _rmsnorm(x, g):
    return x * lax.rsqrt(jnp.mean(x * x, axis=-1, keepdims=True) + RMS_EPS) * g


def _dot(a, b):
    return jnp.dot(a, b, preferred_element_type=_F32)


def _whole(shape):
    nd = len(shape)
    return pl.BlockSpec(shape, lambda *_: (0,) * nd, pipeline_mode=pl.Buffered(1))


def _layer(arr, l):
    nd = arr.ndim - 1
    return pl.BlockSpec((None,) + arr.shape[1:], lambda *_: (l,) + (0,) * nd,
                        pipeline_mode=pl.Buffered(1))


def _compiler_params():
    return pltpu.CompilerParams(dimension_semantics=("arbitrary",),
                                vmem_limit_bytes=V7X_VMEM_LIMIT_BYTES)


def _disc_kernel(are_ref, aim_ref, ldt_ref, bre_ref, bim_ref, ctre_ref, ctim_ref,
                 abre_ref, abim_ref, bbc_ref, cre_ref, cim_ref):
    lam_re = are_ref[...]
    lam_im = aim_ref[...]
    dt = jnp.exp(ldt_ref[...])
    mag = jnp.exp(lam_re * dt)
    ab_re = mag * jnp.cos(lam_im * dt)
    ab_im = mag * jnp.sin(lam_im * dt)
    den = lam_re * lam_re + lam_im * lam_im
    nr = ab_re - 1.0
    ni = ab_im
    q_re = (nr * lam_re + ni * lam_im) / den
    q_im = (ni * lam_re - nr * lam_im) / den
    b_re = bre_ref[...]
    b_im = bim_ref[...]
    abre_ref[...] = ab_re
    abim_ref[...] = ab_im
    bb_re = (q_re * b_re - q_im * b_im).astype(_BF16)
    bb_im = (q_re * b_im + q_im * b_re).astype(_BF16)
    ct_re = ctre_ref[...].astype(_BF16)
    ct_im = ctim_ref[...].astype(_BF16)
    bbc_ref[...] = jnp.zeros(bbc_ref.shape, _BF16)
    cre_ref[...] = jnp.zeros(cre_ref.shape, _BF16)
    cim_ref[...] = jnp.zeros(cim_ref.shape, _BF16)
    for i in range(SSM_TILES):
        for gl in range(SSM_TILE_GROUPS):
            g = i * SSM_TILE_GROUPS + gl
            ch = slice(gl * SSM_GROUP_CH, (gl + 1) * SSM_GROUP_CH)
            st = slice(gl * SSM_STATE, (gl + 1) * SSM_STATE)
            st_im = slice(SSM_TILE + gl * SSM_STATE, SSM_TILE + (gl + 1) * SSM_STATE)
            src_st = slice(g * SSM_STATE, (g + 1) * SSM_STATE)
            src_ch = slice(g * SSM_GROUP_CH, (g + 1) * SSM_GROUP_CH)
            bbc_ref[i, ch, st] = bb_re[:, src_st]
            bbc_ref[i, ch, st_im] = bb_im[:, src_st]
            cre_ref[i, st, ch] = ct_re[:, src_ch]
            cim_ref[i, st, ch] = ct_im[:, src_ch]


def _discretise(a_re, a_im, log_dt, b_re, b_im, c_re, c_im):
    depth = a_re.shape[0]
    flat = lambda a: a.reshape(depth, 1, SSM_FLAT)
    ldt = jnp.broadcast_to(log_dt[:, :, None], (depth, SSM_GROUPS, SSM_STATE))
    bt = lambda b: b.transpose(0, 3, 1, 2).reshape(depth, SSM_GROUP_CH, SSM_FLAT)
    ct = lambda c: c.transpose(0, 3, 1, 2).reshape(depth, SSM_STATE, SSM_WIDTH)
    ins = [flat(a_re), flat(a_im), flat(ldt), bt(b_re), bt(b_im), ct(c_re), ct(c_im)]
    out_shape = [jax.ShapeDtypeStruct((depth, 1, SSM_FLAT), _F32)] * 2 + [
        jax.ShapeDtypeStruct((depth, SSM_TILES, SSM_TILE_CH, 2 * SSM_TILE), _BF16),
        jax.ShapeDtypeStruct((depth, SSM_TILES, SSM_TILE, SSM_TILE_CH), _BF16),
        jax.ShapeDtypeStruct((depth, SSM_TILES, SSM_TILE, SSM_TILE_CH), _BF16)]
    per_layer = lambda s: pl.BlockSpec((None,) + s.shape[1:],
                                       lambda l: (l,) + (0,) * (len(s.shape) - 1))
    return pl.pallas_call(
        _disc_kernel,
        grid=(depth,),
        in_specs=[per_layer(a) for a in ins],
        out_specs=[per_layer(s) for s in out_shape],
        out_shape=out_shape,
        name="s5_discretise",
    )(*ins)


def _pool_branch(ext, n_hist, B, T, pos0, poolw, pscale):
    M = B * T
    base = n_hist * B
    pos = pos0 + lax.broadcasted_iota(jnp.int32, (M, 1), 0) // B
    tiles = []
    for g, w in enumerate(POOL_WINDOWS):
        ls = slice(g * POOL_GROUP_IN, (g + 1) * POOL_GROUP_IN)
        tok = ext[base:base + M, ls]
        win_sum = tok
        for k in range(1, w):
            win_sum = win_sum + ext[base - k * B:base - k * B + M, ls]
        cnt = jnp.minimum(w, pos + 1).astype(_F32)
        d = win_sum / cnt - tok
        tiles.append(_dot(d.astype(_BF16), poolw(g)))
    return jnp.concatenate(tiles, axis=-1) * pscale


def _ssm_inputs(u_ssm_b, bbc_ref):
    return [_dot(u_ssm_b[:, i * SSM_TILE_CH:(i + 1) * SSM_TILE_CH], bbc_ref[i])
            for i in range(SSM_TILES)]


def _ssm_tile(i, bu, h_re, h_im, B, T, are_ref, aim_ref, cre_ref, cim_ref):
    sl = slice(i * SSM_TILE, (i + 1) * SSM_TILE)
    a_re = jnp.broadcast_to(are_ref[:, sl], (B, SSM_TILE))
    a_im = jnp.broadcast_to(aim_ref[:, sl], (B, SSM_TILE))
    hs_re, hs_im = [], []
    for t in range(T):
        rows = slice(t * B, (t + 1) * B)
        n_re = a_re * h_re - a_im * h_im + bu[rows, 0:SSM_TILE]
        n_im = a_re * h_im + a_im * h_re + bu[rows, SSM_TILE:2 * SSM_TILE]
        h_re, h_im = n_re, n_im
        hs_re.append(h_re)
        hs_im.append(h_im)
    y = (_dot(jnp.concatenate(hs_re, axis=0).astype(_BF16), cre_ref[i])
         - _dot(jnp.concatenate(hs_im, axis=0).astype(_BF16), cim_ref[i]))
    return y, h_re, h_im


def _glu(y_tiles, u_ssm, dskip, wglu):
    y_s = jnp.concatenate(y_tiles, axis=-1) + dskip * u_ssm
    zz = _dot(jax.nn.gelu(y_s).astype(_BF16), wglu)
    return zz[:, 0:D_MODEL] * jax.nn.sigmoid(zz[:, D_MODEL:])


def _mixer_main_kernel(x_ref, hist_ref, h0re_ref, h0im_ref, n1g_ref, pscale_ref, are_ref, aim_ref,
                       bbc_ref, cre_ref, cim_ref, dskip_ref, win_ref, poolw_ref, wglu_ref, wout_ref,
                       xo_ref, poolst_ref, hre_o_ref, him_o_ref,
                       pool_ref, hst_ref, *scratch_refs, B, T, start_pos, n_chunks,
                       batch_major_in):
    M = B * T
    c = pl.program_id(0)

    @pl.when(c == 0)
    def _():
        pool_ref[...] = hist_ref[...]
        hst_ref[0] = h0re_ref[...]
        hst_ref[1] = h0im_ref[...]

    bu_refs, slab_refs = scratch_refs[:SSM_TILES], scratch_refs[SSM_TILES:]
    if batch_major_in:
        (xs_ref,) = slab_refs
        for b in range(B):
            for j in range(D_SLABS):
                xs_ref[j, pl.ds(b, T, stride=B), :] = x_ref[b, :, j * V7X_LANES:(j + 1) * V7X_LANES]
        x = jnp.concatenate([xs_ref[j] for j in range(D_SLABS)], axis=-1)
    else:
        x = x_ref[...]
    hb = _rmsnorm(x, n1g_ref[...]).astype(_BF16)

    u = _dot(hb, win_ref[:, 0:POOL_WIDTH + SSM_WIDTH])
    u_ssm = u[:, POOL_WIDTH:]
    u_ssm_b = u_ssm.astype(_BF16)
    ext = jnp.concatenate([pool_ref[...], u[:, 0:POOL_WIDTH]], axis=0)
    pool_ref[...] = ext[M:M + POOL_KEEP * B, :]

    gate_cols = 2 * D_MODEL // SSM_TILES
    staged_rows = pl.ds(pl.multiple_of(jnp.minimum(c, 0), B), M)
    gates = []
    for i in range(SSM_TILES):
        bu_refs[i][staged_rows, :] = _dot(u_ssm_b[:, i * SSM_TILE_CH:(i + 1) * SSM_TILE_CH],
                                          bbc_ref[i])
        lo = POOL_WIDTH + SSM_WIDTH + i * gate_cols
        gates.append(jax.nn.sigmoid(_dot(hb, win_ref[:, lo:lo + gate_cols])))
    gate_pool = jnp.concatenate(gates[0:SSM_TILES // 2], axis=-1)
    gate_ssm = jnp.concatenate(gates[SSM_TILES // 2:], axis=-1)
    y_pool = _pool_branch(ext, POOL_KEEP, B, T, start_pos + c * T, lambda g: poolw_ref[g],
                          pscale_ref[...])
    merge_pool = gate_pool * y_pool

    def tile(i):
        sl = slice(i * SSM_TILE, (i + 1) * SSM_TILE)
        y, h_re, h_im = _ssm_tile(i, bu_refs[i], hst_ref[0, :, sl], hst_ref[1, :, sl], B, T,
                                  are_ref, aim_ref, cre_ref, cim_ref)
        hst_ref[0, :, sl] = h_re
        hst_ref[1, :, sl] = h_im
        return y

    y_tiles = [tile(i) for i in range(SSM_TILES)]
    merge = merge_pool + gate_ssm * _glu(y_tiles, u_ssm, dskip_ref[...], wglu_ref[...])
    xo_ref[...] = x + _dot(merge.astype(_BF16), wout_ref[...])

    @pl.when(c == n_chunks - 1)
    def _():
        poolst_ref[...] = pool_ref[...]
        hre_o_ref[...] = hst_ref[0]
        him_o_ref[...] = hst_ref[1]


def _mixer_main(x, meta_state, p, l, wb, *, B, T, start_pos, batch_major_in):
    M = B * T
    if batch_major_in:
        n_chunks = x.shape[1] // T
        x_spec = pl.BlockSpec((B, T, D_MODEL), lambda c: (0, c, 0))
        slabs = [pltpu.VMEM((D_SLABS, M, V7X_LANES), _F32)]
    else:
        n_chunks = x.shape[0] // M
        x_spec = pl.BlockSpec((M, D_MODEL), lambda c: (c, 0))
        slabs = []
    keep = POOL_KEEP * B
    kern = functools.partial(_mixer_main_kernel, B=B, T=T, start_pos=start_pos,
                             n_chunks=n_chunks, batch_major_in=batch_major_in)
    stacked = [p["n1g"], p["pool_scale"], p["a_re"], p["a_im"], p["bbc"], p["cre"], p["cim"],
               p["dskip"]]
    return pl.pallas_call(
        kern,
        grid=(n_chunks,),
        in_specs=[x_spec] + [_whole(a.shape) for a in meta_state]
        + [_layer(a, l) for a in stacked] + [_whole(a.shape) for a in wb],
        out_specs=[pl.BlockSpec((M, D_MODEL), lambda c: (c, 0)), _whole((keep, POOL_WIDTH)),
                   _whole((B, SSM_FLAT)), _whole((B, SSM_FLAT))],
        out_shape=[jax.ShapeDtypeStruct((n_chunks * M, D_MODEL), _F32),
                   jax.ShapeDtypeStruct((keep, POOL_WIDTH), _F32),
                   jax.ShapeDtypeStruct((B, SSM_FLAT), _F32),
                   jax.ShapeDtypeStruct((B, SSM_FLAT), _F32)],
        scratch_shapes=[pltpu.VMEM((keep, POOL_WIDTH), _F32),
                        pltpu.VMEM((2, B, SSM_FLAT), _F32),
                        ] + [pltpu.VMEM((M, 2 * SSM_TILE), _F32)] * SSM_TILES + slabs,
        compiler_params=_compiler_params(),
        name="mixer_main",
    )(x, *meta_state, *stacked, *wb)


def _mixer_small_kernel(x_ref, pool_ref, sre_ref, sim_ref, n1g_ref, pscale_ref, are_ref, aim_ref,
                        bbc_ref, cre_ref, cim_ref, dskip_ref, win_ref, poolw_ref, wglu_ref,
                        wout_ref, *refs, Bs, Bm, Tm, n_alias):
    (xo_ref, pool_o_ref, sre_o_ref, sim_o_ref, mpool_o_ref, mre_o_ref, mim_o_ref,
     win_o_ref, poolw_o_ref, wglu_o_ref, wout_o_ref) = refs[n_alias:]
    Ms, Mm = Bs, Bm * Tm
    win_o_ref[...] = win_ref[...].astype(_BF16)
    poolw_o_ref[...] = poolw_ref[...].astype(_BF16)
    wglu_o_ref[...] = wglu_ref[...].astype(_BF16)
    wout_o_ref[...] = wout_ref[...].astype(_BF16)

    x = x_ref[...]
    hb = _rmsnorm(x, n1g_ref[...]).astype(_BF16)
    proj = _dot(hb, win_o_ref[...])
    u_pool, u_ssm = proj[:, 0:POOL_WIDTH], proj[:, POOL_WIDTH:POOL_WIDTH + SSM_WIDTH]
    gate_pool = jax.nn.sigmoid(proj[:, POOL_WIDTH + SSM_WIDTH:POOL_WIDTH + SSM_WIDTH + D_MODEL])
    gate_ssm = jax.nn.sigmoid(proj[:, POOL_WIDTH + SSM_WIDTH + D_MODEL:])
    bus = _ssm_inputs(u_ssm.astype(_BF16), bbc_ref)
    poolw = lambda g: poolw_o_ref[g]

    hist = [pool_ref[:, j, :] for j in range(POOL_HIST)]
    new_hist = hist[1:] + [u_pool[0:Ms]]
    for j in range(POOL_HIST):
        pool_o_ref[:, j, :] = new_hist[j]
    y_pool_s = _pool_branch(jnp.concatenate(hist + [u_pool[0:Ms]], axis=0), POOL_HIST, Bs, 1,
                            PAST_LEN, poolw, pscale_ref[...])
    ext_m = jnp.concatenate([jnp.zeros((POOL_KEEP * Bm, POOL_WIDTH), _F32), u_pool[Ms:]], axis=0)
    y_pool_m = _pool_branch(ext_m, POOL_KEEP, Bm, Tm, 0, poolw, pscale_ref[...])
    mpool_o_ref[...] = ext_m[Mm:Mm + POOL_KEEP * Bm, :]
    y_pool = jnp.concatenate([y_pool_s, y_pool_m], axis=0)

    y_tiles = []
    for i in range(SSM_TILES):
        sl = slice(i * SSM_TILE, (i + 1) * SSM_TILE)
        ys, h_re, h_im = _ssm_tile(i, bus[i][0:Ms], sre_ref[:, sl], sim_ref[:, sl], Bs, 1,
                                   are_ref, aim_ref, cre_ref, cim_ref)
        sre_o_ref[:, sl] = h_re
        sim_o_ref[:, sl] = h_im
        zero = jnp.zeros((Bm, SSM_TILE), _F32)
        ym, h_re, h_im = _ssm_tile(i, bus[i][Ms:], zero, zero, Bm, Tm,
                                   are_ref, aim_ref, cre_ref, cim_ref)
        mre_o_ref[:, sl] = h_re
        mim_o_ref[:, sl] = h_im
        y_tiles.append(jnp.concatenate([ys, ym], axis=0))
    merge = gate_pool * y_pool + gate_ssm * _glu(y_tiles, u_ssm, dskip_ref[...], wglu_o_ref[...])
    xo_ref[...] = x + _dot(merge.astype(_BF16), wout_o_ref[...])


def _mixer_small(x, p, l, prev, *, Bs, Bm, Tm):
    kern = functools.partial(_mixer_small_kernel, Bs=Bs, Bm=Bm, Tm=Tm, n_alias=len(prev))
    stacked = [p["state_pool"], p["state_re"], p["state_im"], p["n1g"], p["pool_scale"], p["a_re"],
               p["a_im"], p["bbc"], p["cre"], p["cim"], p["dskip"], p["w_in"], p["pool_w"],
               p["w_glu"], p["w_out"]]
    sds = jax.ShapeDtypeStruct
    state_shapes = [sds(p["state_pool"].shape, _F32), sds(p["state_re"].shape, _F32),
                    sds(p["state_im"].shape, _F32)]
    plain_shapes = [sds((POOL_KEEP * Bm, POOL_WIDTH), _F32), sds((Bm, SSM_FLAT), _F32),
                    sds((Bm, SSM_FLAT), _F32), sds((D_MODEL, IN_COLS), _BF16),
                    sds((POOL_GROUPS, POOL_GROUP_IN, POOL_GROUP_OUT), _BF16),
                    sds((SSM_WIDTH, 2 * D_MODEL), _BF16), sds((D_MODEL, D_MODEL), _BF16)]
    n_in = 1 + len(stacked)
    return pl.pallas_call(
        kern,
        grid=(1,),
        in_specs=[_whole(x.shape)] + [_layer(a, l) for a in stacked]
        + [pl.BlockSpec(memory_space=pl.ANY)] * len(prev),
        out_specs=[_whole(x.shape)] + [_layer(s, l) for s in state_shapes]
        + [_whole(s.shape) for s in plain_shapes],
        out_shape=[sds(x.shape, _F32)] + state_shapes + plain_shapes,
        input_output_aliases={n_in + k: 1 + k for k in range(len(prev))},
        compiler_params=_compiler_params(),
        name="mixer_small",
    )(x, *stacked, *prev)


GELU_C1 = 0.7978845608028654
GELU_C2 = GELU_C1 * 0.044715


def _gelu2_mul(x, v):
    t = jnp.tanh(x * (GELU_C1 + GELU_C2 * (x * x)))
    return (x * v) * (1.0 + t)


def _conv_tile(g_ext, B, M, convw, convb):
    conv = convb
    for k in range(CONV_W):
        conv = conv + convw[k:k + 1, :] * g_ext[k * B:k * B + M, :]
    return conv


def _ffn_main_kernel(x_ref, chist_ref, n2g_ref, convw_ref, convb_ref, nfg_ref, wgv_ref,
                     wdown_ref, xo_ref, convst_ref,
                     carry_ref, *scratch_refs, B, T, n_chunks, final_norm, batch_major_out):
    M = B * T
    keep = CONV_HIST * B
    c = pl.program_id(0)

    @pl.when(c == 0)
    def _():
        carry_ref[...] = chist_ref[...]

    x = x_ref[...]
    hb = _rmsnorm(x, n2g_ref[...]).astype(_BF16)

    staged_rows = pl.ds(pl.multiple_of(keep + jnp.minimum(c, 0), B), M)

    gv_refs, slab_refs = scratch_refs[:FF_SLOTS], scratch_refs[FF_SLOTS:]

    def up(j):
        gv_refs[j % FF_SLOTS][staged_rows, :] = _dot(
            hb, wgv_ref[:, 2 * j * FF_TILE:2 * (j + 1) * FF_TILE])

    acc = None
    for j in range(FF_LOOKAHEAD):
        up(j)
    for j in range(FF_TILES):
        fs = slice(j * FF_TILE, (j + 1) * FF_TILE)
        if j + FF_LOOKAHEAD < FF_TILES:
            up(j + FF_LOOKAHEAD)
        slot = gv_refs[j % FF_SLOTS]
        slot[0:keep, 0:FF_TILE] = carry_ref[:, fs]
        conv = _conv_tile(slot.at[:, 0:FF_TILE], B, M, convw_ref[:, fs], convb_ref[:, fs])
        carry_ref[:, fs] = slot[M:M + keep, 0:FF_TILE]
        v = slot[keep:keep + M, FF_TILE:]
        contrib = _dot(_gelu2_mul(conv, v).astype(_BF16), wdown_ref[fs, :])
        acc = contrib if acc is None else acc + contrib
    xo = x + acc
    if final_norm:
        xo = _rmsnorm(xo, nfg_ref[...])
    if batch_major_out:
        (ys_ref,) = slab_refs
        for j in range(D_SLABS):
            ys_ref[j] = xo[:, j * V7X_LANES:(j + 1) * V7X_LANES]
        for b in range(B):
            for j in range(D_SLABS):
                xo_ref[b, :, j * V7X_LANES:(j + 1) * V7X_LANES] = ys_ref[j, pl.ds(b, T, stride=B), :]
    else:
        xo_ref[...] = xo

    @pl.when(c == n_chunks - 1)
    def _():
        convst_ref[...] = carry_ref[...]


def _ffn_main(x, chist, p, l, wb, *, B, T, final_norm, batch_major_out):
    M = B * T
    n_chunks = x.shape[0] // M
    keep = CONV_HIST * B
    kern = functools.partial(_ffn_main_kernel, B=B, T=T, n_chunks=n_chunks,
                             final_norm=final_norm, batch_major_out=batch_major_out)
    stacked = [p["n2g"], p["conv_w"], p["conv_b"]]
    if batch_major_out:
        o_spec = pl.BlockSpec((B, T, D_MODEL), lambda c: (0, c, 0))
        o_shape = jax.ShapeDtypeStruct((B, n_chunks * T, D_MODEL), _F32)
        slabs = [pltpu.VMEM((D_SLABS, M, V7X_LANES), _F32)]
    else:
        o_spec = pl.BlockSpec((M, D_MODEL), lambda c: (c, 0))
        o_shape = jax.ShapeDtypeStruct((n_chunks * M, D_MODEL), _F32)
        slabs = []
    return pl.pallas_call(
        kern,
        grid=(n_chunks,),
        in_specs=[pl.BlockSpec((M, D_MODEL), lambda c: (c, 0)), _whole(chist.shape)]
        + [_layer(a, l) for a in stacked] + [_whole(p["nfg"].shape)]
        + [_whole(a.shape) for a in wb],
        out_specs=[o_spec, _whole((keep, D_FF))],
        out_shape=[o_shape, jax.ShapeDtypeStruct((keep, D_FF), _F32)],
        scratch_shapes=[pltpu.VMEM((keep, D_FF), _F32),
                        ] + [pltpu.VMEM((keep + M, 2 * FF_TILE), _F32)] * FF_SLOTS + slabs,
        compiler_params=_compiler_params(),
        name="convffn_main",
    )(x, chist, *stacked, p["nfg"], *wb)


def _ffn_small_kernel(x_ref, conv_ref, n2g_ref, convw_ref, convb_ref, nfg_ref, wg_ref, wv_ref,
                      wdown_ref, *refs, Bs, Bm, Tm, final_norm, n_alias):
    (xo_ref, conv_o_ref, mconv_o_ref, wgv_o_ref, wdown_o_ref,
     hb_ref, acc_ref) = refs[n_alias:]
    Ms, Mm = Bs, Bm * Tm
    jt = pl.program_id(0)

    @pl.when(jt == 0)
    def _():
        hb_ref[...] = _rmsnorm(x_ref[...], n2g_ref[...]).astype(_BF16)
        acc_ref[...] = jnp.zeros(acc_ref.shape, _F32)

    wgv_o_ref[:, 0:FF_TILE] = wg_ref[...].astype(_BF16)
    wgv_o_ref[:, FF_TILE:] = wv_ref[...].astype(_BF16)
    wdown_o_ref[...] = (0.5 * wdown_ref[...]).astype(_BF16)
    hb = hb_ref[...]
    gv = _dot(hb, wgv_o_ref[...])
    g, v = gv[:, 0:FF_TILE], gv[:, FF_TILE:]
    hist = [conv_ref[:, j, :] for j in range(CONV_HIST)]
    new_hist = hist[1:] + [g[0:Ms]]
    for j in range(CONV_HIST):
        conv_o_ref[:, j, :] = new_hist[j]
    g_ext_s = jnp.concatenate(hist + [g[0:Ms]], axis=0)
    g_ext_m = jnp.concatenate([jnp.zeros((CONV_HIST * Bm, FF_TILE), _F32), g[Ms:]], axis=0)
    mconv_o_ref[...] = g_ext_m[Mm:Mm + CONV_HIST * Bm, :]
    conv = jnp.concatenate([_conv_tile(g_ext_s, Bs, Ms, convw_ref[...], convb_ref[...]),
                            _conv_tile(g_ext_m, Bm, Mm, convw_ref[...], convb_ref[...])], axis=0)
    acc_ref[...] += _dot(_gelu2_mul(conv, v).astype(_BF16), wdown_o_ref[...])

    @pl.when(jt == FF_TILES - 1)
    def _():
        xo = x_ref[...] + acc_ref[...]
        if final_norm:
            xo = _rmsnorm(xo, nfg_ref[...])
        xo_ref[...] = xo


def _ffn_small(x, p, l, prev, *, Bs, Bm, Tm, final_norm):
    kern = functools.partial(_ffn_small_kernel, Bs=Bs, Bm=Bm, Tm=Tm, final_norm=final_norm,
                             n_alias=len(prev))
    sds = jax.ShapeDtypeStruct
    tile_of = lambda a, idx: pl.BlockSpec((None,) + idx[0], idx[1])
    n_rows = x.shape[0]
    in_specs = [
        _whole(x.shape),
        pl.BlockSpec((None, Bs, CONV_HIST, FF_TILE), lambda j: (l, 0, 0, j)),
        _layer(p["n2g"], l),
        pl.BlockSpec((None, CONV_W, FF_TILE), lambda j: (l, 0, j)),
        pl.BlockSpec((None, 1, FF_TILE), lambda j: (l, 0, j)),
        _whole(p["nfg"].shape),
        pl.BlockSpec((None, D_MODEL, FF_TILE), lambda j: (l, 0, j)),
        pl.BlockSpec((None, D_MODEL, FF_TILE), lambda j: (l, 0, FF_TILES + j)),
        pl.BlockSpec((None, FF_TILE, D_MODEL), lambda j: (l, j, 0)),
    ] + [pl.BlockSpec(memory_space=pl.ANY)] * len(prev)
    out_shape = [sds(x.shape, _F32), sds(p["state_conv"].shape, _F32),
                 sds((CONV_HIST * Bm, D_FF), _F32), sds((D_MODEL, 2 * D_FF), _BF16),
                 sds((D_FF, D_MODEL), _BF16)]
    out_specs = [
        _whole(x.shape),
        pl.BlockSpec((None, Bs, CONV_HIST, FF_TILE), lambda j: (l, 0, 0, j)),
        pl.BlockSpec((CONV_HIST * Bm, FF_TILE), lambda j: (0, j)),
        pl.BlockSpec((D_MODEL, 2 * FF_TILE), lambda j: (0, j)),
        pl.BlockSpec((FF_TILE, D_MODEL), lambda j: (j, 0)),
    ]
    n_in = 9
    return pl.pallas_call(
        kern,
        grid=(FF_TILES,),
        in_specs=in_specs,
        out_specs=out_specs,
        out_shape=out_shape,
        scratch_shapes=[pltpu.VMEM((n_rows, D_MODEL), _BF16), pltpu.VMEM((n_rows, D_MODEL), _F32)],
        input_output_aliases={n_in + k: 1 + k for k in range(len(prev))},
        compiler_params=_compiler_params(),
        name="convffn_small",
    )(x, p["state_conv"], p["n2g"], p["conv_w"], p["conv_b"], p["nfg"], p["w_up"], p["w_up"],
      p["w_down"], *prev)


def kernel(x_prompt, x_sample, state_pool, state_ssm_re, state_ssm_im, state_conv, meta_tokens,
           norm1_g, w_in, pool_w, pool_scale, ssm_A_re, ssm_A_im, ssm_log_dt, ssm_B_re, ssm_B_im,
           ssm_C_re, ssm_C_im, ssm_D, w_glu, w_out, norm2_g, w_up, conv_w, conv_b, w_down, norm_f_g):
    depth = w_in.shape[0]
    bp, seq, _ = x_prompt.shape
    bs = x_sample.shape[0]
    assert seq % PROMPT_CHUNK_STEPS == 0
    ab_re, ab_im, bbc, cre, cim = _discretise(ssm_A_re, ssm_A_im, ssm_log_dt, ssm_B_re, ssm_B_im,
                                              ssm_C_re, ssm_C_im)
    row = lambda a: a[:, None, :]
    p = dict(state_pool=state_pool, state_conv=state_conv,
             state_re=state_ssm_re.reshape(depth, bs, SSM_FLAT),
             state_im=state_ssm_im.reshape(depth, bs, SSM_FLAT),
             n1g=row(norm1_g), pool_scale=row(pool_scale), a_re=ab_re, a_im=ab_im, bbc=bbc,
             cre=cre, cim=cim, dskip=row(ssm_D), w_in=w_in, pool_w=pool_w, w_glu=w_glu,
             w_out=w_out, n2g=row(norm2_g), conv_w=conv_w, conv_b=row(conv_b), w_up=w_up,
             w_down=w_down, nfg=norm_f_g[None])

    xs = jnp.concatenate([x_sample.reshape(bs, D_MODEL), jnp.repeat(meta_tokens, bp, axis=0)], axis=0)
    small = dict(Bs=bs, Bm=bp, Tm=N_META)
    mix_prev, ffn_prev, meta_state, mix_w, ffn_w = [], [], [], [], []
    for l in range(depth):
        xs, pool_s, re_s, im_s, mp, mr, mi, *wb = _mixer_small(xs, p, l, mix_prev, **small)
        mix_prev = [pool_s, re_s, im_s]
        mix_w.append(wb)
        xs, conv_s, mc, *wb = _ffn_small(xs, p, l, ffn_prev, final_norm=(l == depth - 1), **small)
        ffn_prev = [conv_s]
        ffn_w.append(wb)
        meta_state.append(((mp, mr, mi), mc))
    y_sample = xs[0:bs].reshape(bs, 1, D_MODEL)

    xp = x_prompt
    pool_p, re_p, im_p, conv_p = [], [], [], []
    main = dict(B=bp, T=PROMPT_CHUNK_STEPS)
    for l in range(depth):
        mix_state, mc = meta_state[l]
        xp, pp, hr, hi = _mixer_main(xp, mix_state, p, l, mix_w[l], start_pos=N_META,
                                     batch_major_in=(l == 0), **main)
        last = l == depth - 1
        xp, cv = _ffn_main(xp, mc, p, l, ffn_w[l], final_norm=last, batch_major_out=last, **main)
        pool_p.append(pp)
        re_p.append(hr)
        im_p.append(hi)
        conv_p.append(cv)
    st = jnp.stack
    pool_p = st(pool_p).reshape(depth, POOL_KEEP, bp, POOL_WIDTH)[:, 1:].transpose(0, 2, 1, 3)
    conv_p = st(conv_p).reshape(depth, CONV_HIST, bp, D_FF).transpose(0, 2, 1, 3)
    group = lambda h: h.reshape(h.shape[:-1] + (SSM_GROUPS, SSM_STATE))
    return (xp, y_sample, pool_p, group(st(re_p)), group(st(im_p)), conv_p,
            pool_s, group(re_s), group(im_s), conv_s)
```

```python
import functools

import jax
import jax.numpy as jnp
from jax import lax
from jax.experimental import pallas as pl
from jax.experimental.pallas import tpu as pltpu

D_MODEL = 1024
N_META = 16
POOL_WIDTH = 512
POOL_WINDOWS = (2, 4, 8, 16)
POOL_GROUPS = len(POOL_WINDOWS)
POOL_GROUP_IN = 128
POOL_GROUP_OUT = 256
POOL_HIST = 15
POOL_KEEP = 16
SSM_WIDTH = 512
SSM_GROUP_CH = 16
SSM_GROUPS = 32
SSM_STATE = 64
SSM_FLAT = SSM_GROUPS * SSM_STATE
SSM_TILE = 512
SSM_TILES = SSM_FLAT // SSM_TILE
SSM_TILE_GROUPS = SSM_TILE // SSM_STATE
SSM_TILE_CH = SSM_TILE_GROUPS * SSM_GROUP_CH
IN_COLS = POOL_WIDTH + SSM_WIDTH + 2 * D_MODEL
D_FF = 2816
FF_TILE = 256
FF_TILES = D_FF // FF_TILE
FF_LOOKAHEAD = 2
FF_SLOTS = FF_LOOKAHEAD + 1
CONV_W = 3
CONV_HIST = CONV_W - 1
RMS_EPS = 1e-6
PAST_LEN = 16384

V7X_LANES = 128
D_SLABS = D_MODEL // V7X_LANES
V7X_VMEM_LIMIT_BYTES = 56 * 1024 * 1024
MIXER_CHUNK_STEPS = 64
FFN_CHUNK_STEPS = 64

_F32 = jnp.float32
_BF16 = jnp.bfloat16


def _rmsnorm(x, g):
    return x * lax.rsqrt(jnp.mean(x * x, axis=-1, keepdims=True) + RMS_EPS) * g


def _dot(a, b):
    return jnp.dot(a, b, preferred_element_type=_F32)


def _whole(shape):
    nd = len(shape)
    return pl.BlockSpec(shape, lambda *_: (0,) * nd, pipeline_mode=pl.Buffered(1))


def _layer(arr, l):
    nd = arr.ndim - 1
    return pl.BlockSpec((None,) + arr.shape[1:], lambda *_: (l,) + (0,) * nd,
                        pipeline_mode=pl.Buffered(1))


def _compiler_params():
    return pltpu.CompilerParams(dimension_semantics=("arbitrary",),
                                vmem_limit_bytes=V7X_VMEM_LIMIT_BYTES)


def _disc_kernel(are_ref, aim_ref, ldt_ref, bre_ref, bim_ref, ctre_ref, ctim_ref,
                 abre_ref, abim_ref, bbc_ref, cre_ref, cim_ref):
    lam_re = are_ref[...]
    lam_im = aim_ref[...]
    dt = jnp.exp(ldt_ref[...])
    mag = jnp.exp(lam_re * dt)
    ab_re = mag * jnp.cos(lam_im * dt)
    ab_im = mag * jnp.sin(lam_im * dt)
    den = lam_re * lam_re + lam_im * lam_im
    nr = ab_re - 1.0
    ni = ab_im
    q_re = (nr * lam_re + ni * lam_im) / den
    q_im = (ni * lam_re - nr * lam_im) / den
    b_re = bre_ref[...]
    b_im = bim_ref[...]
    abre_ref[...] = ab_re
    abim_ref[...] = ab_im
    bb_re = (q_re * b_re - q_im * b_im).astype(_BF16)
    bb_im = (q_re * b_im + q_im * b_re).astype(_BF16)
    ct_re = ctre_ref[...].astype(_BF16)
    ct_im = ctim_ref[...].astype(_BF16)
    bbc_ref[...] = jnp.zeros(bbc_ref.shape, _BF16)
    cre_ref[...] = jnp.zeros(cre_ref.shape, _BF16)
    cim_ref[...] = jnp.zeros(cim_ref.shape, _BF16)
    for i in range(SSM_TILES):
        for gl in range(SSM_TILE_GROUPS):
            g = i * SSM_TILE_GROUPS + gl
            ch = slice(gl * SSM_GROUP_CH, (gl + 1) * SSM_GROUP_CH)
            st = slice(gl * SSM_STATE, (gl + 1) * SSM_STATE)
            st_im = slice(SSM_TILE + gl * SSM_STATE, SSM_TILE + (gl + 1) * SSM_STATE)
            src_st = slice(g * SSM_STATE, (g + 1) * SSM_STATE)
            src_ch = slice(g * SSM_GROUP_CH, (g + 1) * SSM_GROUP_CH)
            bbc_ref[i, ch, st] = bb_re[:, src_st]
            bbc_ref[i, ch, st_im] = bb_im[:, src_st]
            cre_ref[i, st, ch] = ct_re[:, src_ch]
            cim_ref[i, st, ch] = ct_im[:, src_ch]


def _discretise(a_re, a_im, log_dt, b_re, b_im, c_re, c_im):
    depth = a_re.shape[0]
    flat = lambda a: a.reshape(depth, 1, SSM_FLAT)
    ldt = jnp.broadcast_to(log_dt[:, :, None], (depth, SSM_GROUPS, SSM_STATE))
    bt = lambda b: b.transpose(0, 3, 1, 2).reshape(depth, SSM_GROUP_CH, SSM_FLAT)
    ct = lambda c: c.transpose(0, 3, 1, 2).reshape(depth, SSM_STATE, SSM_WIDTH)
    ins = [flat(a_re), flat(a_im), flat(ldt), bt(b_re), bt(b_im), ct(c_re), ct(c_im)]
    out_shape = [jax.ShapeDtypeStruct((depth, 1, SSM_FLAT), _F32)] * 2 + [
        jax.ShapeDtypeStruct((depth, SSM_TILES, SSM_TILE_CH, 2 * SSM_TILE), _BF16),
        jax.ShapeDtypeStruct((depth, SSM_TILES, SSM_TILE, SSM_TILE_CH), _BF16),
        jax.ShapeDtypeStruct((depth, SSM_TILES, SSM_TILE, SSM_TILE_CH), _BF16)]
    per_layer = lambda s: pl.BlockSpec((None,) + s.shape[1:],
                                       lambda l: (l,) + (0,) * (len(s.shape) - 1))
    return pl.pallas_call(
        _disc_kernel,
        grid=(depth,),
        in_specs=[per_layer(a) for a in ins],
        out_specs=[per_layer(s) for s in out_shape],
        out_shape=out_shape,
        name="s5_discretise",
    )(*ins)


def _pool_branch(ext, n_hist, B, T, pos0, poolw, pscale):
    M = B * T
    base = n_hist * B
    pos = pos0 + lax.broadcasted_iota(jnp.int32, (M, 1), 0) // B
    tiles = []
    for g, w in enumerate(POOL_WINDOWS):
        ls = slice(g * POOL_GROUP_IN, (g + 1) * POOL_GROUP_IN)
        tok = ext[base:base + M, ls]
        win_sum = tok
        for k in range(1, w):
            win_sum = win_sum + ext[base - k * B:base - k * B + M, ls]
        cnt = jnp.minimum(w, pos + 1).astype(_F32)
        d = win_sum / cnt - tok
        tiles.append(_dot(d.astype(_BF16), poolw(g)))
    return jnp.concatenate(tiles, axis=-1) * pscale


def _ssm_inputs(u_ssm_b, bbc_ref):
    return [_dot(u_ssm_b[:, i * SSM_TILE_CH:(i + 1) * SSM_TILE_CH], bbc_ref[i])
            for i in range(SSM_TILES)]


def _ssm_tile(i, bu, h_re, h_im, B, T, are_ref, aim_ref, cre_ref, cim_ref):
    sl = slice(i * SSM_TILE, (i + 1) * SSM_TILE)
    a_re = jnp.broadcast_to(are_ref[:, sl], (B, SSM_TILE))
    a_im = jnp.broadcast_to(aim_ref[:, sl], (B, SSM_TILE))
    hs_re, hs_im = [], []
    for t in range(T):
        rows = slice(t * B, (t + 1) * B)
        n_re = a_re * h_re - a_im * h_im + bu[rows, 0:SSM_TILE]
        n_im = a_re * h_im + a_im * h_re + bu[rows, SSM_TILE:2 * SSM_TILE]
        h_re, h_im = n_re, n_im
        hs_re.append(h_re)
        hs_im.append(h_im)
    y = (_dot(jnp.concatenate(hs_re, axis=0).astype(_BF16), cre_ref[i])
         - _dot(jnp.concatenate(hs_im, axis=0).astype(_BF16), cim_ref[i]))
    return y, h_re, h_im


def _glu(y_tiles, u_ssm, dskip, wglu):
    y_s = jnp.concatenate(y_tiles, axis=-1) + dskip * u_ssm
    zz = _dot(jax.nn.gelu(y_s).astype(_BF16), wglu)
    return zz[:, 0:D_MODEL] * jax.nn.sigmoid(zz[:, D_MODEL:])


def _mixer_main_kernel(x_ref, hist_ref, h0re_ref, h0im_ref, n1g_ref, pscale_ref, are_ref, aim_ref,
                       bbc_ref, cre_ref, cim_ref, dskip_ref, win_ref, poolw_ref, wglu_ref, wout_ref,
                       xo_ref, poolst_ref, hre_o_ref, him_o_ref,
                       pool_ref, hst_ref, *scratch_refs, B, T, start_pos, n_chunks,
                       batch_major_in):
    M = B * T
    c = pl.program_id(0)

    @pl.when(c == 0)
    def _():
        pool_ref[...] = hist_ref[...]
        hst_ref[0] = h0re_ref[...]
        hst_ref[1] = h0im_ref[...]

    bu_refs, slab_refs = scratch_refs[:SSM_TILES], scratch_refs[SSM_TILES:]
    if batch_major_in:
        (xs_ref,) = slab_refs
        for b in range(B):
            for j in range(D_SLABS):
                xs_ref[j, pl.ds(b, T, stride=B), :] = x_ref[b, :, j * V7X_LANES:(j + 1) * V7X_LANES]
        x = jnp.concatenate([xs_ref[j] for j in range(D_SLABS)], axis=-1)
    else:
        x = x_ref[...]
    hb = _rmsnorm(x, n1g_ref[...]).astype(_BF16)

    u = _dot(hb, win_ref[:, 0:POOL_WIDTH + SSM_WIDTH])
    u_ssm = u[:, POOL_WIDTH:]
    u_ssm_b = u_ssm.astype(_BF16)
    ext = jnp.concatenate([pool_ref[...], u[:, 0:POOL_WIDTH]], axis=0)
    pool_ref[...] = ext[M:M + POOL_KEEP * B, :]

    gate_cols = 2 * D_MODEL // SSM_TILES
    staged_rows = pl.ds(pl.multiple_of(jnp.minimum(c, 0), B), M)
    gates = []
    for i in range(SSM_TILES):
        bu_refs[i][staged_rows, :] = _dot(u_ssm_b[:, i * SSM_TILE_CH:(i + 1) * SSM_TILE_CH],
                                          bbc_ref[i])
        lo = POOL_WIDTH + SSM_WIDTH + i * gate_cols
        gates.append(jax.nn.sigmoid(_dot(hb, win_ref[:, lo:lo + gate_cols])))
    gate_pool = jnp.concatenate(gates[0:SSM_TILES // 2], axis=-1)
    gate_ssm = jnp.concatenate(gates[SSM_TILES // 2:], axis=-1)
    y_pool = _pool_branch(ext, POOL_KEEP, B, T, start_pos + c * T, lambda g: poolw_ref[g],
                          pscale_ref[...])
    merge_pool = gate_pool * y_pool

    def tile(i):
        sl = slice(i * SSM_TILE, (i + 1) * SSM_TILE)
        y, h_re, h_im = _ssm_tile(i, bu_refs[i], hst_ref[0, :, sl], hst_ref[1, :, sl], B, T,
                                  are_ref, aim_ref, cre_ref, cim_ref)
        hst_ref[0, :, sl] = h_re
        hst_ref[1, :, sl] = h_im
        return y

    y_tiles = [tile(i) for i in range(SSM_TILES)]
    merge = merge_pool + gate_ssm * _glu(y_tiles, u_ssm, dskip_ref[...], wglu_ref[...])
    xo_ref[...] = x + _dot(merge.astype(_BF16), wout_ref[...])

    @pl.when(c == n_chunks - 1)
    def _():
        poolst_ref[...] = pool_ref[...]
        hre_o_ref[...] = hst_ref[0]
        him_o_ref[...] = hst_ref[1]


def _mixer_main(x, meta_state, p, l, wb, *, B, T, start_pos, batch_major_in):
    M = B * T
    if batch_major_in:
        n_chunks = x.shape[1] // T
        x_spec = pl.BlockSpec((B, T, D_MODEL), lambda c: (0, c, 0))
        slabs = [pltpu.VMEM((D_SLABS, M, V7X_LANES), _F32)]
    else:
        n_chunks = x.shape[0] // M
        x_spec = pl.BlockSpec((M, D_MODEL), lambda c: (c, 0))
        slabs = []
    keep = POOL_KEEP * B
    kern = functools.partial(_mixer_main_kernel, B=B, T=T, start_pos=start_pos,
                             n_chunks=n_chunks, batch_major_in=batch_major_in)
    stacked = [p["n1g"], p["pool_scale"], p["a_re"], p["a_im"], p["bbc"], p["cre"], p["cim"],
               p["dskip"]]
    return pl.pallas_call(
        kern,
        grid=(n_chunks,),
        in_specs=[x_spec] + [_whole(a.shape) for a in meta_state]
        + [_layer(a, l) for a in stacked] + [_whole(a.shape) for a in wb],
        out_specs=[pl.BlockSpec((M, D_MODEL), lambda c: (c, 0)), _whole((keep, POOL_WIDTH)),
                   _whole((B, SSM_FLAT)), _whole((B, SSM_FLAT))],
        out_shape=[jax.ShapeDtypeStruct((n_chunks * M, D_MODEL), _F32),
                   jax.ShapeDtypeStruct((keep, POOL_WIDTH), _F32),
                   jax.ShapeDtypeStruct((B, SSM_FLAT), _F32),
                   jax.ShapeDtypeStruct((B, SSM_FLAT), _F32)],
        scratch_shapes=[pltpu.VMEM((keep, POOL_WIDTH), _F32),
                        pltpu.VMEM((2, B, SSM_FLAT), _F32),
                        ] + [pltpu.VMEM((M, 2 * SSM_TILE), _F32)] * SSM_TILES + slabs,
        compiler_params=_compiler_params(),
        name="mixer_main",
    )(x, *meta_state, *stacked, *wb)


def _mixer_small_kernel(x_ref, pool_ref, sre_ref, sim_ref, n1g_ref, pscale_ref, are_ref, aim_ref,
                        bbc_ref, cre_ref, cim_ref, dskip_ref, win_ref, poolw_ref, wglu_ref,
                        wout_ref, *refs, Bs, Bm, Tm, n_alias):
    (xo_ref, pool_o_ref, sre_o_ref, sim_o_ref, mpool_o_ref, mre_o_ref, mim_o_ref,
     win_o_ref, poolw_o_ref, wglu_o_ref, wout_o_ref) = refs[n_alias:]
    Ms, Mm = Bs, Bm * Tm
    win_o_ref[...] = win_ref[...].astype(_BF16)
    poolw_o_ref[...] = poolw_ref[...].astype(_BF16)
    wglu_o_ref[...] = wglu_ref[...].astype(_BF16)
    wout_o_ref[...] = wout_ref[...].astype(_BF16)

    x = x_ref[...]
    hb = _rmsnorm(x, n1g_ref[...]).astype(_BF16)
    proj = _dot(hb, win_o_ref[...])
    u_pool, u_ssm = proj[:, 0:POOL_WIDTH], proj[:, POOL_WIDTH:POOL_WIDTH + SSM_WIDTH]
    gate_pool = jax.nn.sigmoid(proj[:, POOL_WIDTH + SSM_WIDTH:POOL_WIDTH + SSM_WIDTH + D_MODEL])
    gate_ssm = jax.nn.sigmoid(proj[:, POOL_WIDTH + SSM_WIDTH + D_MODEL:])
    bus = _ssm_inputs(u_ssm.astype(_BF16), bbc_ref)
    poolw = lambda g: poolw_o_ref[g]

    hist = [pool_ref[j] for j in range(POOL_HIST)]
    new_hist = hist[1:] + [u_pool[0:Ms]]
    for j in range(POOL_HIST):
        pool_o_ref[j] = new_hist[j]
    y_pool_s = _pool_branch(jnp.concatenate(hist + [u_pool[0:Ms]], axis=0), POOL_HIST, Bs, 1,
                            PAST_LEN, poolw, pscale_ref[...])
    ext_m = jnp.concatenate([jnp.zeros((POOL_KEEP * Bm, POOL_WIDTH), _F32), u_pool[Ms:]], axis=0)
    y_pool_m = _pool_branch(ext_m, POOL_KEEP, Bm, Tm, 0, poolw, pscale_ref[...])
    mpool_o_ref[...] = ext_m[Mm:Mm + POOL_KEEP * Bm, :]
    y_pool = jnp.concatenate([y_pool_s, y_pool_m], axis=0)

    y_tiles = []
    for i in range(SSM_TILES):
        sl = slice(i * SSM_TILE, (i + 1) * SSM_TILE)
        ys, h_re, h_im = _ssm_tile(i, bus[i][0:Ms], sre_ref[:, sl], sim_ref[:, sl], Bs, 1,
                                   are_ref, aim_ref, cre_ref, cim_ref)
        sre_o_ref[:, sl] = h_re
        sim_o_ref[:, sl] = h_im
        zero = jnp.zeros((Bm, SSM_TILE), _F32)
        ym, h_re, h_im = _ssm_tile(i, bus[i][Ms:], zero, zero, Bm, Tm,
                                   are_ref, aim_ref, cre_ref, cim_ref)
        mre_o_ref[:, sl] = h_re
        mim_o_ref[:, sl] = h_im
        y_tiles.append(jnp.concatenate([ys, ym], axis=0))
    merge = gate_pool * y_pool + gate_ssm * _glu(y_tiles, u_ssm, dskip_ref[...], wglu_o_ref[...])
    xo_ref[...] = x + _dot(merge.astype(_BF16), wout_o_ref[...])


def _mixer_small(x, p, l, prev, *, Bs, Bm, Tm):
    kern = functools.partial(_mixer_small_kernel, Bs=Bs, Bm=Bm, Tm=Tm, n_alias=len(prev))
    stacked = [p["state_pool"], p["state_re"], p["state_im"], p["n1g"], p["pool_scale"], p["a_re"],
               p["a_im"], p["bbc"], p["cre"], p["cim"], p["dskip"], p["w_in"], p["pool_w"],
               p["w_glu"], p["w_out"]]
    sds = jax.ShapeDtypeStruct
    state_shapes = [sds(p["state_pool"].shape, _F32), sds(p["state_re"].shape, _F32),
                    sds(p["state_im"].shape, _F32)]
    plain_shapes = [sds((POOL_KEEP * Bm, POOL_WIDTH), _F32), sds((Bm, SSM_FLAT), _F32),
                    sds((Bm, SSM_FLAT), _F32), sds((D_MODEL, IN_COLS), _BF16),
                    sds((POOL_GROUPS, POOL_GROUP_IN, POOL_GROUP_OUT), _BF16),
                    sds((SSM_WIDTH, 2 * D_MODEL), _BF16), sds((D_MODEL, D_MODEL), _BF16)]
    n_in = 1 + len(stacked)
    return pl.pallas_call(
        kern,
        grid=(1,),
        in_specs=[_whole(x.shape)] + [_layer(a, l) for a in stacked]
        + [pl.BlockSpec(memory_space=pl.ANY)] * len(prev),
        out_specs=[_whole(x.shape)] + [_layer(s, l) for s in state_shapes]
        + [_whole(s.shape) for s in plain_shapes],
        out_shape=[sds(x.shape, _F32)] + state_shapes + plain_shapes,
        input_output_aliases={n_in + k: 1 + k for k in range(len(prev))},
        compiler_params=_compiler_params(),
        name="mixer_small",
    )(x, *stacked, *prev)


GELU_C1 = 0.7978845608028654
GELU_C2 = GELU_C1 * 0.044715


def _gelu2_mul(x, v):
    t = jnp.tanh(x * (GELU_C1 + GELU_C2 * (x * x)))
    return (x * v) * (1.0 + t)


def _conv_tile(g_ext, B, M, convw, convb):
    conv = convb
    for k in range(CONV_W):
        conv = conv + convw[k:k + 1, :] * g_ext[k * B:k * B + M, :]
    return conv


def _ffn_main_kernel(x_ref, chist_ref, n2g_ref, convw_ref, convb_ref, nfg_ref, wgv_ref,
                     wdown_ref, xo_ref, convst_ref,
                     carry_ref, *scratch_refs, B, T, n_chunks, final_norm, batch_major_out):
    M = B * T
    keep = CONV_HIST * B
    c = pl.program_id(0)

    @pl.when(c == 0)
    def _():
        carry_ref[...] = chist_ref[...]

    x = x_ref[...]
    hb = _rmsnorm(x, n2g_ref[...]).astype(_BF16)

    staged_rows = pl.ds(pl.multiple_of(keep + jnp.minimum(c, 0), B), M)

    gv_refs, slab_refs = scratch_refs[:FF_SLOTS], scratch_refs[FF_SLOTS:]

    def up(j):
        gv_refs[j % FF_SLOTS][staged_rows, :] = _dot(
            hb, wgv_ref[:, 2 * j * FF_TILE:2 * (j + 1) * FF_TILE])

    acc = None
    for j in range(FF_LOOKAHEAD):
        up(j)
    for j in range(FF_TILES):
        fs = slice(j * FF_TILE, (j + 1) * FF_TILE)
        if j + FF_LOOKAHEAD < FF_TILES:
            up(j + FF_LOOKAHEAD)
        slot = gv_refs[j % FF_SLOTS]
        slot[0:keep, 0:FF_TILE] = carry_ref[:, fs]
        conv = _conv_tile(slot.at[:, 0:FF_TILE], B, M, convw_ref[:, fs], convb_ref[:, fs])
        carry_ref[:, fs] = slot[M:M + keep, 0:FF_TILE]
        v = slot[keep:keep + M, FF_TILE:]
        contrib = _dot(_gelu2_mul(conv, v).astype(_BF16), wdown_ref[fs, :])
        acc = contrib if acc is None else acc + contrib
    xo = x + acc
    if final_norm:
        xo = _rmsnorm(xo, nfg_ref[...])
    if batch_major_out:
        (ys_ref,) = slab_refs
        for j in range(D_SLABS):
            ys_ref[j] = xo[:, j * V7X_LANES:(j + 1) * V7X_LANES]
        for b in range(B):
            for j in range(D_SLABS):
                xo_ref[b, :, j * V7X_LANES:(j + 1) * V7X_LANES] = ys_ref[j, pl.ds(b, T, stride=B), :]
    else:
        xo_ref[...] = xo

    @pl.when(c == n_chunks - 1)
    def _():
        convst_ref[...] = carry_ref[...]


def _ffn_main(x, chist, p, l, wb, *, B, T, final_norm, batch_major_out):
    M = B * T
    n_chunks = x.shape[0] // M
    keep = CONV_HIST * B
    kern = functools.partial(_ffn_main_kernel, B=B, T=T, n_chunks=n_chunks,
                             final_norm=final_norm, batch_major_out=batch_major_out)
    stacked = [p["n2g"], p["conv_w"], p["conv_b"]]
    if batch_major_out:
        o_spec = pl.BlockSpec((B, T, D_MODEL), lambda c: (0, c, 0))
        o_shape = jax.ShapeDtypeStruct((B, n_chunks * T, D_MODEL), _F32)
        slabs = [pltpu.VMEM((D_SLABS, M, V7X_LANES), _F32)]
    else:
        o_spec = pl.BlockSpec((M, D_MODEL), lambda c: (c, 0))
        o_shape = jax.ShapeDtypeStruct((n_chunks * M, D_MODEL), _F32)
        slabs = []
    return pl.pallas_call(
        kern,
        grid=(n_chunks,),
        in_specs=[pl.BlockSpec((M, D_MODEL), lambda c: (c, 0)), _whole(chist.shape)]
        + [_layer(a, l) for a in stacked] + [_whole(p["nfg"].shape)]
        + [_whole(a.shape) for a in wb],
        out_specs=[o_spec, _whole((keep, D_FF))],
        out_shape=[o_shape, jax.ShapeDtypeStruct((keep, D_FF), _F32)],
        scratch_shapes=[pltpu.VMEM((keep, D_FF), _F32),
                        ] + [pltpu.VMEM((keep + M, 2 * FF_TILE), _F32)] * FF_SLOTS + slabs,
        compiler_params=_compiler_params(),
        name="convffn_main",
    )(x, chist, *stacked, p["nfg"], *wb)


def _ffn_small_kernel(x_ref, conv_ref, n2g_ref, convw_ref, convb_ref, nfg_ref, wg_ref, wv_ref,
                      wdown_ref, *refs, Bs, Bm, Tm, final_norm, n_alias):
    (xo_ref, conv_o_ref, mconv_o_ref, wgv_o_ref, wdown_o_ref,
     hb_ref, acc_ref) = refs[n_alias:]
    Ms, Mm = Bs, Bm * Tm
    jt = pl.program_id(0)

    @pl.when(jt == 0)
    def _():
        hb_ref[...] = _rmsnorm(x_ref[...], n2g_ref[...]).astype(_BF16)
        acc_ref[...] = jnp.zeros(acc_ref.shape, _F32)

    wgv_o_ref[:, 0:FF_TILE] = wg_ref[...].astype(_BF16)
    wgv_o_ref[:, FF_TILE:] = wv_ref[...].astype(_BF16)
    wdown_o_ref[...] = (0.5 * wdown_ref[...]).astype(_BF16)
    hb = hb_ref[...]
    gv = _dot(hb, wgv_o_ref[...])
    g, v = gv[:, 0:FF_TILE], gv[:, FF_TILE:]
    hist = [conv_ref[:, j, :] for j in range(CONV_HIST)]
    new_hist = hist[1:] + [g[0:Ms]]
    for j in range(CONV_HIST):
        conv_o_ref[:, j, :] = new_hist[j]
    g_ext_s = jnp.concatenate(hist + [g[0:Ms]], axis=0)
    g_ext_m = jnp.concatenate([jnp.zeros((CONV_HIST * Bm, FF_TILE), _F32), g[Ms:]], axis=0)
    mconv_o_ref[...] = g_ext_m[Mm:Mm + CONV_HIST * Bm, :]
    conv = jnp.concatenate([_conv_tile(g_ext_s, Bs, Ms, convw_ref[...], convb_ref[...]),
                            _conv_tile(g_ext_m, Bm, Mm, convw_ref[...], convb_ref[...])], axis=0)
    acc_ref[...] += _dot(_gelu2_mul(conv, v).astype(_BF16), wdown_o_ref[...])

    @pl.when(jt == FF_TILES - 1)
    def _():
        xo = x_ref[...] + acc_ref[...]
        if final_norm:
            xo = _rmsnorm(xo, nfg_ref[...])
        xo_ref[...] = xo


def _ffn_small(x, p, l, prev, *, Bs, Bm, Tm, final_norm):
    kern = functools.partial(_ffn_small_kernel, Bs=Bs, Bm=Bm, Tm=Tm, final_norm=final_norm,
                             n_alias=len(prev))
    sds = jax.ShapeDtypeStruct
    tile_of = lambda a, idx: pl.BlockSpec((None,) + idx[0], idx[1])
    n_rows = x.shape[0]
    in_specs = [
        _whole(x.shape),
        pl.BlockSpec((None, Bs, CONV_HIST, FF_TILE), lambda j: (l, 0, 0, j)),
        _layer(p["n2g"], l),
        pl.BlockSpec((None, CONV_W, FF_TILE), lambda j: (l, 0, j)),
        pl.BlockSpec((None, 1, FF_TILE), lambda j: (l, 0, j)),
        _whole(p["nfg"].shape),
        pl.BlockSpec((None, D_MODEL, FF_TILE), lambda j: (l, 0, j)),
        pl.BlockSpec((None, D_MODEL, FF_TILE), lambda j: (l, 0, FF_TILES + j)),
        pl.BlockSpec((None, FF_TILE, D_MODEL), lambda j: (l, j, 0)),
    ] + [pl.BlockSpec(memory_space=pl.ANY)] * len(prev)
    out_shape = [sds(x.shape, _F32), sds(p["state_conv"].shape, _F32),
                 sds((CONV_HIST * Bm, D_FF), _F32), sds((D_MODEL, 2 * D_FF), _BF16),
                 sds((D_FF, D_MODEL), _BF16)]
    out_specs = [
        _whole(x.shape),
        pl.BlockSpec((None, Bs, CONV_HIST, FF_TILE), lambda j: (l, 0, 0, j)),
        pl.BlockSpec((CONV_HIST * Bm, FF_TILE), lambda j: (0, j)),
        pl.BlockSpec((D_MODEL, 2 * FF_TILE), lambda j: (0, j)),
        pl.BlockSpec((FF_TILE, D_MODEL), lambda j: (j, 0)),
    ]
    n_in = 9
    return pl.pallas_call(
        kern,
        grid=(FF_TILES,),
        in_specs=in_specs,
        out_specs=out_specs,
        out_shape=out_shape,
        scratch_shapes=[pltpu.VMEM((n_rows, D_MODEL), _BF16), pltpu.VMEM((n_rows, D_MODEL), _F32)],
        input_output_aliases={n_in + k: 1 + k for k in range(len(prev))},
        compiler_params=_compiler_params(),
        name="convffn_small",
    )(x, p["state_conv"], p["n2g"], p["conv_w"], p["conv_b"], p["nfg"], p["w_up"], p["w_up"],
      p["w_down"], *prev)


def kernel(x_prompt, x_sample, state_pool, state_ssm_re, state_ssm_im, state_conv, meta_tokens,
           norm1_g, w_in, pool_w, pool_scale, ssm_A_re, ssm_A_im, ssm_log_dt, ssm_B_re, ssm_B_im,
           ssm_C_re, ssm_C_im, ssm_D, w_glu, w_out, norm2_g, w_up, conv_w, conv_b, w_down, norm_f_g):
    depth = w_in.shape[0]
    bp, seq, _ = x_prompt.shape
    bs = x_sample.shape[0]
    assert seq % MIXER_CHUNK_STEPS == 0 and seq % FFN_CHUNK_STEPS == 0
    ab_re, ab_im, bbc, cre, cim = _discretise(ssm_A_re, ssm_A_im, ssm_log_dt, ssm_B_re, ssm_B_im,
                                              ssm_C_re, ssm_C_im)
    row = lambda a: a[:, None, :]
    p = dict(state_pool=state_pool.transpose(0, 2, 1, 3), state_conv=state_conv,
             state_re=state_ssm_re.reshape(depth, bs, SSM_FLAT),
             state_im=state_ssm_im.reshape(depth, bs, SSM_FLAT),
             n1g=row(norm1_g), pool_scale=row(pool_scale), a_re=ab_re, a_im=ab_im, bbc=bbc,
             cre=cre, cim=cim, dskip=row(ssm_D), w_in=w_in, pool_w=pool_w, w_glu=w_glu,
             w_out=w_out, n2g=row(norm2_g), conv_w=conv_w, conv_b=row(conv_b), w_up=w_up,
             w_down=w_down, nfg=norm_f_g[None])

    xs = jnp.concatenate([x_sample.reshape(bs, D_MODEL), jnp.repeat(meta_tokens, bp, axis=0)], axis=0)
    small = dict(Bs=bs, Bm=bp, Tm=N_META)
    mix_prev, ffn_prev, meta_state, mix_w, ffn_w = [], [], [], [], []
    for l in range(depth):
        xs, pool_s, re_s, im_s, mp, mr, mi, *wb = _mixer_small(xs, p, l, mix_prev, **small)
        mix_prev = [pool_s, re_s, im_s]
        mix_w.append(wb)
        xs, conv_s, mc, *wb = _ffn_small(xs, p, l, ffn_prev, final_norm=(l == depth - 1), **small)
        ffn_prev = [conv_s]
        ffn_w.append(wb)
        meta_state.append(((mp, mr, mi), mc))
    y_sample = xs[0:bs].reshape(bs, 1, D_MODEL)

    xp = x_prompt
    pool_p, re_p, im_p, conv_p = [], [], [], []
    for l in range(depth):
        mix_state, mc = meta_state[l]
        xp, pp, hr, hi = _mixer_main(xp, mix_state, p, l, mix_w[l], start_pos=N_META,
                                     batch_major_in=(l == 0), B=bp, T=MIXER_CHUNK_STEPS)
        last = l == depth - 1
        xp, cv = _ffn_main(xp, mc, p, l, ffn_w[l], final_norm=last, batch_major_out=last, B=bp,
                           T=FFN_CHUNK_STEPS)
        pool_p.append(pp)
        re_p.append(hr)
        im_p.append(hi)
        conv_p.append(cv)
    st = jnp.stack
    pool_p = st(pool_p).reshape(depth, POOL_KEEP, bp, POOL_WIDTH)[:, 1:].transpose(0, 2, 1, 3)
    conv_p = st(conv_p).reshape(depth, CONV_HIST, bp, D_FF).transpose(0, 2, 1, 3)
    group = lambda h: h.reshape(h.shape[:-1] + (SSM_GROUPS, SSM_STATE))
    return (xp, y_sample, pool_p, group(st(re_p)), group(st(im_p)), conv_p,
            pool_s.transpose(0, 2, 1, 3), group(re_s), group(im_s), conv_s)
```

```python
import functools

import jax
import jax.numpy as jnp
from jax import lax
from jax.experimental import pallas as pl
from jax.experimental.pallas import tpu as pltpu

D_MODEL = 1024
N_META = 16
POOL_WIDTH = 512
POOL_WINDOWS = (2, 4, 8, 16)
POOL_GROUPS = len(POOL_WINDOWS)
POOL_GROUP_IN = 128
POOL_GROUP_OUT = 256
POOL_HIST = 15
POOL_KEEP = 16
SSM_WIDTH = 512
SSM_GROUP_CH = 16
SSM_GROUPS = 32
SSM_STATE = 64
SSM_FLAT = SSM_GROUPS * SSM_STATE
SSM_TILE = 512
SSM_TILES = SSM_FLAT // SSM_TILE
SSM_TILE_GROUPS = SSM_TILE // SSM_STATE
SSM_TILE_CH = SSM_TILE_GROUPS * SSM_GROUP_CH
IN_COLS = POOL_WIDTH + SSM_WIDTH + 2 * D_MODEL
D_FF = 2816
FF_TILE = 256
FF_TILES = D_FF // FF_TILE
FF_LOOKAHEAD = 2
FF_SLOTS = FF_LOOKAHEAD + 1
CONV_W = 3
CONV_HIST = CONV_W - 1
RMS_EPS = 1e-6
PAST_LEN = 16384

V7X_LANES = 128
D_SLABS = D_MODEL // V7X_LANES
V7X_VMEM_LIMIT_BYTES = 56 * 1024 * 1024
MIXER_CHUNK_STEPS = 64
MIXER_SUB_BLOCKS = 1
FFN_CHUNK_STEPS = 64
FFN_SUB_BLOCKS = 2

_F32 = jnp.float32
_BF16 = jnp.bfloat16


def _rmsnorm(x, g):
    return x * lax.rsqrt(jnp.mean(x * x, axis=-1, keepdims=True) + RMS_EPS) * g


def _dot(a, b):
    return jnp.dot(a, b, preferred_element_type=_F32)


def _whole(shape):
    nd = len(shape)
    return pl.BlockSpec(shape, lambda *_: (0,) * nd, pipeline_mode=pl.Buffered(1))


def _layer(arr, l):
    nd = arr.ndim - 1
    return pl.BlockSpec((None,) + arr.shape[1:], lambda *_: (l,) + (0,) * nd,
                        pipeline_mode=pl.Buffered(1))


def _compiler_params():
    return pltpu.CompilerParams(dimension_semantics=("arbitrary",),
                                vmem_limit_bytes=V7X_VMEM_LIMIT_BYTES)


def _disc_kernel(are_ref, aim_ref, ldt_ref, bre_ref, bim_ref, ctre_ref, ctim_ref,
                 abre_ref, abim_ref, bbc_ref, cre_ref, cim_ref):
    lam_re = are_ref[...]
    lam_im = aim_ref[...]
    dt = jnp.exp(ldt_ref[...])
    mag = jnp.exp(lam_re * dt)
    ab_re = mag * jnp.cos(lam_im * dt)
    ab_im = mag * jnp.sin(lam_im * dt)
    den = lam_re * lam_re + lam_im * lam_im
    nr = ab_re - 1.0
    ni = ab_im
    q_re = (nr * lam_re + ni * lam_im) / den
    q_im = (ni * lam_re - nr * lam_im) / den
    b_re = bre_ref[...]
    b_im = bim_ref[...]
    abre_ref[...] = ab_re
    abim_ref[...] = ab_im
    bb_re = (q_re * b_re - q_im * b_im).astype(_BF16)
    bb_im = (q_re * b_im + q_im * b_re).astype(_BF16)
    ct_re = ctre_ref[...].astype(_BF16)
    ct_im = ctim_ref[...].astype(_BF16)
    bbc_ref[...] = jnp.zeros(bbc_ref.shape, _BF16)
    cre_ref[...] = jnp.zeros(cre_ref.shape, _BF16)
    cim_ref[...] = jnp.zeros(cim_ref.shape, _BF16)
    for i in range(SSM_TILES):
        for gl in range(SSM_TILE_GROUPS):
            g = i * SSM_TILE_GROUPS + gl
            ch = slice(gl * SSM_GROUP_CH, (gl + 1) * SSM_GROUP_CH)
            st = slice(gl * SSM_STATE, (gl + 1) * SSM_STATE)
            st_im = slice(SSM_TILE + gl * SSM_STATE, SSM_TILE + (gl + 1) * SSM_STATE)
            src_st = slice(g * SSM_STATE, (g + 1) * SSM_STATE)
            src_ch = slice(g * SSM_GROUP_CH, (g + 1) * SSM_GROUP_CH)
            bbc_ref[i, ch, st] = bb_re[:, src_st]
            bbc_ref[i, ch, st_im] = bb_im[:, src_st]
            cre_ref[i, st, ch] = ct_re[:, src_ch]
            cim_ref[i, st, ch] = ct_im[:, src_ch]


def _discretise(a_re, a_im, log_dt, b_re, b_im, c_re, c_im):
    depth = a_re.shape[0]
    flat = lambda a: a.reshape(depth, 1, SSM_FLAT)
    ldt = jnp.broadcast_to(log_dt[:, :, None], (depth, SSM_GROUPS, SSM_STATE))
    bt = lambda b: b.transpose(0, 3, 1, 2).reshape(depth, SSM_GROUP_CH, SSM_FLAT)
    ct = lambda c: c.transpose(0, 3, 1, 2).reshape(depth, SSM_STATE, SSM_WIDTH)
    ins = [flat(a_re), flat(a_im), flat(ldt), bt(b_re), bt(b_im), ct(c_re), ct(c_im)]
    out_shape = [jax.ShapeDtypeStruct((depth, 1, SSM_FLAT), _F32)] * 2 + [
        jax.ShapeDtypeStruct((depth, SSM_TILES, SSM_TILE_CH, 2 * SSM_TILE), _BF16),
        jax.ShapeDtypeStruct((depth, SSM_TILES, SSM_TILE, SSM_TILE_CH), _BF16),
        jax.ShapeDtypeStruct((depth, SSM_TILES, SSM_TILE, SSM_TILE_CH), _BF16)]
    per_layer = lambda s: pl.BlockSpec((None,) + s.shape[1:],
                                       lambda l: (l,) + (0,) * (len(s.shape) - 1))
    return pl.pallas_call(
        _disc_kernel,
        grid=(depth,),
        in_specs=[per_layer(a) for a in ins],
        out_specs=[per_layer(s) for s in out_shape],
        out_shape=out_shape,
        name="s5_discretise",
    )(*ins)


def _pool_branch(ext, n_hist, B, T, pos0, poolw, pscale):
    M = B * T
    base = n_hist * B
    pos = pos0 + lax.broadcasted_iota(jnp.int32, (M, 1), 0) // B
    tiles = []
    for g, w in enumerate(POOL_WINDOWS):
        ls = slice(g * POOL_GROUP_IN, (g + 1) * POOL_GROUP_IN)
        tok = ext[base:base + M, ls]
        win_sum = tok
        for k in range(1, w):
            win_sum = win_sum + ext[base - k * B:base - k * B + M, ls]
        cnt = jnp.minimum(w, pos + 1).astype(_F32)
        d = win_sum / cnt - tok
        tiles.append(_dot(d.astype(_BF16), poolw(g)))
    return jnp.concatenate(tiles, axis=-1) * pscale


def _ssm_inputs(u_ssm_b, bbc_ref):
    return [_dot(u_ssm_b[:, i * SSM_TILE_CH:(i + 1) * SSM_TILE_CH], bbc_ref[i])
            for i in range(SSM_TILES)]


def _ssm_tile(i, bu, h_re, h_im, B, T, are_ref, aim_ref, cre_ref, cim_ref):
    sl = slice(i * SSM_TILE, (i + 1) * SSM_TILE)
    a_re = jnp.broadcast_to(are_ref[:, sl], (B, SSM_TILE))
    a_im = jnp.broadcast_to(aim_ref[:, sl], (B, SSM_TILE))
    hs_re, hs_im = [], []
    for t in range(T):
        rows = slice(t * B, (t + 1) * B)
        n_re = a_re * h_re - a_im * h_im + bu[rows, 0:SSM_TILE]
        n_im = a_re * h_im + a_im * h_re + bu[rows, SSM_TILE:2 * SSM_TILE]
        h_re, h_im = n_re, n_im
        hs_re.append(h_re)
        hs_im.append(h_im)
    y = (_dot(jnp.concatenate(hs_re, axis=0).astype(_BF16), cre_ref[i])
         - _dot(jnp.concatenate(hs_im, axis=0).astype(_BF16), cim_ref[i]))
    return y, h_re, h_im


def _glu(y_tiles, u_ssm, dskip, wglu):
    y_s = jnp.concatenate(y_tiles, axis=-1) + dskip * u_ssm
    zz = _dot(jax.nn.gelu(y_s).astype(_BF16), wglu)
    return zz[:, 0:D_MODEL] * jax.nn.sigmoid(zz[:, D_MODEL:])


def _mixer_main_kernel(x_ref, hist_ref, h0re_ref, h0im_ref, n1g_ref, pscale_ref, are_ref, aim_ref,
                       bbc_ref, cre_ref, cim_ref, dskip_ref, win_ref, poolw_ref, wglu_ref, wout_ref,
                       xo_ref, poolst_ref, hre_o_ref, him_o_ref,
                       pool_ref, hst_ref, *scratch_refs, B, T, start_pos, n_chunks,
                       batch_major_in):
    M = B * T
    c = pl.program_id(0)

    @pl.when(c == 0)
    def _():
        pool_ref[...] = hist_ref[...]
        hst_ref[0] = h0re_ref[...]
        hst_ref[1] = h0im_ref[...]

    Ts = T // MIXER_SUB_BLOCKS
    Ms = B * Ts
    bu_refs, slab_refs = scratch_refs[:SSM_TILES], scratch_refs[SSM_TILES:]
    staged_rows = pl.ds(pl.multiple_of(jnp.minimum(c, 0), B), Ms)
    gate_cols = 2 * D_MODEL // SSM_TILES

    for s in range(MIXER_SUB_BLOCKS):
        if batch_major_in:
            xs_ref = slab_refs[s]
            for b in range(B):
                for q in range(D_SLABS):
                    xs_ref[q, pl.ds(b, Ts, stride=B), :] = (
                        x_ref[b, s * Ts:(s + 1) * Ts, q * V7X_LANES:(q + 1) * V7X_LANES])
            x = jnp.concatenate([xs_ref[q] for q in range(D_SLABS)], axis=-1)
        else:
            x = x_ref[s * Ms:(s + 1) * Ms, :]
        hb = _rmsnorm(x, n1g_ref[...]).astype(_BF16)

        u = _dot(hb, win_ref[:, 0:POOL_WIDTH + SSM_WIDTH])
        u_ssm = u[:, POOL_WIDTH:]
        u_ssm_b = u_ssm.astype(_BF16)
        ext = jnp.concatenate([pool_ref[...], u[:, 0:POOL_WIDTH]], axis=0)
        pool_ref[...] = ext[Ms:Ms + POOL_KEEP * B, :]

        gates = []
        for i in range(SSM_TILES):
            bu_refs[i][staged_rows, :] = _dot(u_ssm_b[:, i * SSM_TILE_CH:(i + 1) * SSM_TILE_CH],
                                              bbc_ref[i])
            lo = POOL_WIDTH + SSM_WIDTH + i * gate_cols
            gates.append(jax.nn.sigmoid(_dot(hb, win_ref[:, lo:lo + gate_cols])))
        gate_pool = jnp.concatenate(gates[0:SSM_TILES // 2], axis=-1)
        gate_ssm = jnp.concatenate(gates[SSM_TILES // 2:], axis=-1)
        y_pool = _pool_branch(ext, POOL_KEEP, B, Ts, start_pos + c * T + s * Ts,
                              lambda g: poolw_ref[g], pscale_ref[...])
        merge_pool = gate_pool * y_pool

        y_tiles = []
        for i in range(SSM_TILES):
            sl = slice(i * SSM_TILE, (i + 1) * SSM_TILE)
            y, h_re, h_im = _ssm_tile(i, bu_refs[i], hst_ref[0, :, sl], hst_ref[1, :, sl], B, Ts,
                                      are_ref, aim_ref, cre_ref, cim_ref)
            hst_ref[0, :, sl] = h_re
            hst_ref[1, :, sl] = h_im
            y_tiles.append(y)
        merge = merge_pool + gate_ssm * _glu(y_tiles, u_ssm, dskip_ref[...], wglu_ref[...])
        xo_ref[s * Ms:(s + 1) * Ms, :] = x + _dot(merge.astype(_BF16), wout_ref[...])

    @pl.when(c == n_chunks - 1)
    def _():
        poolst_ref[...] = pool_ref[...]
        hre_o_ref[...] = hst_ref[0]
        him_o_ref[...] = hst_ref[1]


def _mixer_main(x, meta_state, p, l, wb, *, B, T, start_pos, batch_major_in):
    M = B * T
    if batch_major_in:
        n_chunks = x.shape[1] // T
        x_spec = pl.BlockSpec((B, T, D_MODEL), lambda c: (0, c, 0))
        slabs = [pltpu.VMEM((D_SLABS, M // MIXER_SUB_BLOCKS, V7X_LANES), _F32)] * MIXER_SUB_BLOCKS
    else:
        n_chunks = x.shape[0] // M
        x_spec = pl.BlockSpec((M, D_MODEL), lambda c: (c, 0))
        slabs = []
    keep = POOL_KEEP * B
    kern = functools.partial(_mixer_main_kernel, B=B, T=T, start_pos=start_pos,
                             n_chunks=n_chunks, batch_major_in=batch_major_in)
    stacked = [p["n1g"], p["pool_scale"], p["a_re"], p["a_im"], p["bbc"], p["cre"], p["cim"],
               p["dskip"]]
    return pl.pallas_call(
        kern,
        grid=(n_chunks,),
        in_specs=[x_spec] + [_whole(a.shape) for a in meta_state]
        + [_layer(a, l) for a in stacked] + [_whole(a.shape) for a in wb],
        out_specs=[pl.BlockSpec((M, D_MODEL), lambda c: (c, 0)), _whole((keep, POOL_WIDTH)),
                   _whole((B, SSM_FLAT)), _whole((B, SSM_FLAT))],
        out_shape=[jax.ShapeDtypeStruct((n_chunks * M, D_MODEL), _F32),
                   jax.ShapeDtypeStruct((keep, POOL_WIDTH), _F32),
                   jax.ShapeDtypeStruct((B, SSM_FLAT), _F32),
                   jax.ShapeDtypeStruct((B, SSM_FLAT), _F32)],
        scratch_shapes=[pltpu.VMEM((keep, POOL_WIDTH), _F32),
                        pltpu.VMEM((2, B, SSM_FLAT), _F32),
                        ] + [pltpu.VMEM((M // MIXER_SUB_BLOCKS, 2 * SSM_TILE), _F32)] * SSM_TILES
        + slabs,
        compiler_params=_compiler_params(),
        name="mixer_main",
    )(x, *meta_state, *stacked, *wb)


def _mixer_small_kernel(x_ref, pool_ref, sre_ref, sim_ref, n1g_ref, pscale_ref, are_ref, aim_ref,
                        bbc_ref, cre_ref, cim_ref, dskip_ref, win_ref, poolw_ref, wglu_ref,
                        wout_ref, *refs, Bs, Bm, Tm, n_alias):
    (xo_ref, pool_o_ref, sre_o_ref, sim_o_ref, mpool_o_ref, mre_o_ref, mim_o_ref,
     win_o_ref, poolw_o_ref, wglu_o_ref, wout_o_ref) = refs[n_alias:]
    Ms, Mm = Bs, Bm * Tm
    win_o_ref[...] = win_ref[...].astype(_BF16)
    poolw_o_ref[...] = poolw_ref[...].astype(_BF16)
    wglu_o_ref[...] = wglu_ref[...].astype(_BF16)
    wout_o_ref[...] = wout_ref[...].astype(_BF16)

    x = x_ref[...]
    hb = _rmsnorm(x, n1g_ref[...]).astype(_BF16)
    proj = _dot(hb, win_o_ref[...])
    u_pool, u_ssm = proj[:, 0:POOL_WIDTH], proj[:, POOL_WIDTH:POOL_WIDTH + SSM_WIDTH]
    gate_pool = jax.nn.sigmoid(proj[:, POOL_WIDTH + SSM_WIDTH:POOL_WIDTH + SSM_WIDTH + D_MODEL])
    gate_ssm = jax.nn.sigmoid(proj[:, POOL_WIDTH + SSM_WIDTH + D_MODEL:])
    bus = _ssm_inputs(u_ssm.astype(_BF16), bbc_ref)
    poolw = lambda g: poolw_o_ref[g]

    hist = [pool_ref[j] for j in range(POOL_HIST)]
    new_hist = hist[1:] + [u_pool[0:Ms]]
    for j in range(POOL_HIST):
        pool_o_ref[j] = new_hist[j]
    y_pool_s = _pool_branch(jnp.concatenate(hist + [u_pool[0:Ms]], axis=0), POOL_HIST, Bs, 1,
                            PAST_LEN, poolw, pscale_ref[...])
    ext_m = jnp.concatenate([jnp.zeros((POOL_KEEP * Bm, POOL_WIDTH), _F32), u_pool[Ms:]], axis=0)
    y_pool_m = _pool_branch(ext_m, POOL_KEEP, Bm, Tm, 0, poolw, pscale_ref[...])
    mpool_o_ref[...] = ext_m[Mm:Mm + POOL_KEEP * Bm, :]
    y_pool = jnp.concatenate([y_pool_s, y_pool_m], axis=0)

    y_tiles = []
    for i in range(SSM_TILES):
        sl = slice(i * SSM_TILE, (i + 1) * SSM_TILE)
        ys, h_re, h_im = _ssm_tile(i, bus[i][0:Ms], sre_ref[:, sl], sim_ref[:, sl], Bs, 1,
                                   are_ref, aim_ref, cre_ref, cim_ref)
        sre_o_ref[:, sl] = h_re
        sim_o_ref[:, sl] = h_im
        zero = jnp.zeros((Bm, SSM_TILE), _F32)
        ym, h_re, h_im = _ssm_tile(i, bus[i][Ms:], zero, zero, Bm, Tm,
                                   are_ref, aim_ref, cre_ref, cim_ref)
        mre_o_ref[:, sl] = h_re
        mim_o_ref[:, sl] = h_im
        y_tiles.append(jnp.concatenate([ys, ym], axis=0))
    merge = gate_pool * y_pool + gate_ssm * _glu(y_tiles, u_ssm, dskip_ref[...], wglu_o_ref[...])
    xo_ref[...] = x + _dot(merge.astype(_BF16), wout_o_ref[...])


def _mixer_small(x, p, l, prev, *, Bs, Bm, Tm):
    kern = functools.partial(_mixer_small_kernel, Bs=Bs, Bm=Bm, Tm=Tm, n_alias=len(prev))
    stacked = [p["state_pool"], p["state_re"], p["state_im"], p["n1g"], p["pool_scale"], p["a_re"],
               p["a_im"], p["bbc"], p["cre"], p["cim"], p["dskip"], p["w_in"], p["pool_w"],
               p["w_glu"], p["w_out"]]
    sds = jax.ShapeDtypeStruct
    state_shapes = [sds(p["state_pool"].shape, _F32), sds(p["state_re"].shape, _F32),
                    sds(p["state_im"].shape, _F32)]
    plain_shapes = [sds((POOL_KEEP * Bm, POOL_WIDTH), _F32), sds((Bm, SSM_FLAT), _F32),
                    sds((Bm, SSM_FLAT), _F32), sds((D_MODEL, IN_COLS), _BF16),
                    sds((POOL_GROUPS, POOL_GROUP_IN, POOL_GROUP_OUT), _BF16),
                    sds((SSM_WIDTH, 2 * D_MODEL), _BF16), sds((D_MODEL, D_MODEL), _BF16)]
    n_in = 1 + len(stacked)
    return pl.pallas_call(
        kern,
        grid=(1,),
        in_specs=[_whole(x.shape)] + [_layer(a, l) for a in stacked]
        + [pl.BlockSpec(memory_space=pl.ANY)] * len(prev),
        out_specs=[_whole(x.shape)] + [_layer(s, l) for s in state_shapes]
        + [_whole(s.shape) for s in plain_shapes],
        out_shape=[sds(x.shape, _F32)] + state_shapes + plain_shapes,
        input_output_aliases={n_in + k: 1 + k for k in range(len(prev))},
        compiler_params=_compiler_params(),
        name="mixer_small",
    )(x, *stacked, *prev)


GELU_C1 = 0.7978845608028654
GELU_C2 = GELU_C1 * 0.044715


def _gelu2_mul(x, v):
    t = jnp.tanh(x * (GELU_C1 + GELU_C2 * (x * x)))
    return (x * v) * (1.0 + t)


def _conv_tile(g_ext, B, M, convw, convb):
    conv = convb
    for k in range(CONV_W):
        conv = conv + convw[k:k + 1, :] * g_ext[k * B:k * B + M, :]
    return conv


def _ffn_main_kernel(x_ref, chist_ref, n2g_ref, convw_ref, convb_ref, nfg_ref, wgv_ref,
                     wdown_ref, xo_ref, convst_ref,
                     carry_ref, *scratch_refs, B, T, n_chunks, final_norm, batch_major_out):
    M = B * T
    keep = CONV_HIST * B
    c = pl.program_id(0)

    @pl.when(c == 0)
    def _():
        carry_ref[...] = chist_ref[...]

    Ts = T // FFN_SUB_BLOCKS
    Ms = B * Ts
    gv_refs, slab_refs = scratch_refs[:FF_SLOTS], scratch_refs[FF_SLOTS:]
    xs = [x_ref[s * Ms:(s + 1) * Ms, :] for s in range(FFN_SUB_BLOCKS)]
    hbs = [_rmsnorm(x, n2g_ref[...]).astype(_BF16) for x in xs]

    staged_rows = pl.ds(pl.multiple_of(keep + jnp.minimum(c, 0), B), Ms)
    n_tiles = FFN_SUB_BLOCKS * FF_TILES

    def up(k):
        s, j = divmod(k, FF_TILES)
        gv_refs[k % FF_SLOTS][staged_rows, :] = _dot(
            hbs[s], wgv_ref[:, 2 * j * FF_TILE:2 * (j + 1) * FF_TILE])

    for k in range(FF_LOOKAHEAD):
        up(k)
    acc = None
    for k in range(n_tiles):
        s, j = divmod(k, FF_TILES)
        fs = slice(j * FF_TILE, (j + 1) * FF_TILE)
        if k + FF_LOOKAHEAD < n_tiles:
            up(k + FF_LOOKAHEAD)
        slot = gv_refs[k % FF_SLOTS]
        slot[0:keep, 0:FF_TILE] = carry_ref[:, fs]
        conv = _conv_tile(slot.at[:, 0:FF_TILE], B, Ms, convw_ref[:, fs], convb_ref[:, fs])
        carry_ref[:, fs] = slot[Ms:Ms + keep, 0:FF_TILE]
        v = slot[keep:keep + Ms, FF_TILE:]
        contrib = _dot(_gelu2_mul(conv, v).astype(_BF16), wdown_ref[fs, :])
        acc = contrib if j == 0 else acc + contrib
        if j == FF_TILES - 1:
            xo = xs[s] + acc
            if final_norm:
                xo = _rmsnorm(xo, nfg_ref[...])
            if batch_major_out:
                ys_ref = slab_refs[s]
                for q in range(D_SLABS):
                    ys_ref[q] = xo[:, q * V7X_LANES:(q + 1) * V7X_LANES]
                for b in range(B):
                    for q in range(D_SLABS):
                        xo_ref[b, s * Ts:(s + 1) * Ts, q * V7X_LANES:(q + 1) * V7X_LANES] = (
                            ys_ref[q, pl.ds(b, Ts, stride=B), :])
            else:
                xo_ref[s * Ms:(s + 1) * Ms, :] = xo

    @pl.when(c == n_chunks - 1)
    def _():
        convst_ref[...] = carry_ref[...]


def _ffn_main(x, chist, p, l, wb, *, B, T, final_norm, batch_major_out):
    M = B * T
    n_chunks = x.shape[0] // M
    keep = CONV_HIST * B
    kern = functools.partial(_ffn_main_kernel, B=B, T=T, n_chunks=n_chunks,
                             final_norm=final_norm, batch_major_out=batch_major_out)
    stacked = [p["n2g"], p["conv_w"], p["conv_b"]]
    if batch_major_out:
        o_spec = pl.BlockSpec((B, T, D_MODEL), lambda c: (0, c, 0))
        o_shape = jax.ShapeDtypeStruct((B, n_chunks * T, D_MODEL), _F32)
        slabs = [pltpu.VMEM((D_SLABS, M // FFN_SUB_BLOCKS, V7X_LANES), _F32)] * FFN_SUB_BLOCKS
    else:
        o_spec = pl.BlockSpec((M, D_MODEL), lambda c: (c, 0))
        o_shape = jax.ShapeDtypeStruct((n_chunks * M, D_MODEL), _F32)
        slabs = []
    return pl.pallas_call(
        kern,
        grid=(n_chunks,),
        in_specs=[pl.BlockSpec((M, D_MODEL), lambda c: (c, 0)), _whole(chist.shape)]
        + [_layer(a, l) for a in stacked] + [_whole(p["nfg"].shape)]
        + [_whole(a.shape) for a in wb],
        out_specs=[o_spec, _whole((keep, D_FF))],
        out_shape=[o_shape, jax.ShapeDtypeStruct((keep, D_FF), _F32)],
        scratch_shapes=[pltpu.VMEM((keep, D_FF), _F32),
                        ] + [pltpu.VMEM((keep + M // FFN_SUB_BLOCKS, 2 * FF_TILE), _F32)] * FF_SLOTS
        + slabs,
        compiler_params=_compiler_params(),
        name="convffn_main",
    )(x, chist, *stacked, p["nfg"], *wb)


def _ffn_small_kernel(x_ref, conv_ref, n2g_ref, convw_ref, convb_ref, nfg_ref, wg_ref, wv_ref,
                      wdown_ref, *refs, Bs, Bm, Tm, final_norm, n_alias):
    (xo_ref, conv_o_ref, mconv_o_ref, wgv_o_ref, wdown_o_ref,
     hb_ref, acc_ref) = refs[n_alias:]
    Ms, Mm = Bs, Bm * Tm
    jt = pl.program_id(0)

    @pl.when(jt == 0)
    def _():
        hb_ref[...] = _rmsnorm(x_ref[...], n2g_ref[...]).astype(_BF16)
        acc_ref[...] = jnp.zeros(acc_ref.shape, _F32)

    wgv_o_ref[:, 0:FF_TILE] = wg_ref[...].astype(_BF16)
    wgv_o_ref[:, FF_TILE:] = wv_ref[...].astype(_BF16)
    wdown_o_ref[...] = (0.5 * wdown_ref[...]).astype(_BF16)
    hb = hb_ref[...]
    gv = _dot(hb, wgv_o_ref[...])
    g, v = gv[:, 0:FF_TILE], gv[:, FF_TILE:]
    hist = [conv_ref[:, j, :] for j in range(CONV_HIST)]
    new_hist = hist[1:] + [g[0:Ms]]
    for j in range(CONV_HIST):
        conv_o_ref[:, j, :] = new_hist[j]
    g_ext_s = jnp.concatenate(hist + [g[0:Ms]], axis=0)
    g_ext_m = jnp.concatenate([jnp.zeros((CONV_HIST * Bm, FF_TILE), _F32), g[Ms:]], axis=0)
    mconv_o_ref[...] = g_ext_m[Mm:Mm + CONV_HIST * Bm, :]
    conv = jnp.concatenate([_conv_tile(g_ext_s, Bs, Ms, convw_ref[...], convb_ref[...]),
                            _conv_tile(g_ext_m, Bm, Mm, convw_ref[...], convb_ref[...])], axis=0)
    acc_ref[...] += _dot(_gelu2_mul(conv, v).astype(_BF16), wdown_o_ref[...])

    @pl.when(jt == FF_TILES - 1)
    def _():
        xo = x_ref[...] + acc_ref[...]
        if final_norm:
            xo = _rmsnorm(xo, nfg_ref[...])
        xo_ref[...] = xo


def _ffn_small(x, p, l, prev, *, Bs, Bm, Tm, final_norm):
    kern = functools.partial(_ffn_small_kernel, Bs=Bs, Bm=Bm, Tm=Tm, final_norm=final_norm,
                             n_alias=len(prev))
    sds = jax.ShapeDtypeStruct
    tile_of = lambda a, idx: pl.BlockSpec((None,) + idx[0], idx[1])
    n_rows = x.shape[0]
    in_specs = [
        _whole(x.shape),
        pl.BlockSpec((None, Bs, CONV_HIST, FF_TILE), lambda j: (l, 0, 0, j)),
        _layer(p["n2g"], l),
        pl.BlockSpec((None, CONV_W, FF_TILE), lambda j: (l, 0, j)),
        pl.BlockSpec((None, 1, FF_TILE), lambda j: (l, 0, j)),
        _whole(p["nfg"].shape),
        pl.BlockSpec((None, D_MODEL, FF_TILE), lambda j: (l, 0, j)),
        pl.BlockSpec((None, D_MODEL, FF_TILE), lambda j: (l, 0, FF_TILES + j)),
        pl.BlockSpec((None, FF_TILE, D_MODEL), lambda j: (l, j, 0)),
    ] + [pl.BlockSpec(memory_space=pl.ANY)] * len(prev)
    out_shape = [sds(x.shape, _F32), sds(p["state_conv"].shape, _F32),
                 sds((CONV_HIST * Bm, D_FF), _F32), sds((D_MODEL, 2 * D_FF), _BF16),
                 sds((D_FF, D_MODEL), _BF16)]
    out_specs = [
        _whole(x.shape),
        pl.BlockSpec((None, Bs, CONV_HIST, FF_TILE), lambda j: (l, 0, 0, j)),
        pl.BlockSpec((CONV_HIST * Bm, FF_TILE), lambda j: (0, j)),
        pl.BlockSpec((D_MODEL, 2 * FF_TILE), lambda j: (0, j)),
        pl.BlockSpec((FF_TILE, D_MODEL), lambda j: (j, 0)),
    ]
    n_in = 9
    return pl.pallas_call(
        kern,
        grid=(FF_TILES,),
        in_specs=in_specs,
        out_specs=out_specs,
        out_shape=out_shape,
        scratch_shapes=[pltpu.VMEM((n_rows, D_MODEL), _BF16), pltpu.VMEM((n_rows, D_MODEL), _F32)],
        input_output_aliases={n_in + k: 1 + k for k in range(len(prev))},
        compiler_params=_compiler_params(),
        name="convffn_small",
    )(x, p["state_conv"], p["n2g"], p["conv_w"], p["conv_b"], p["nfg"], p["w_up"], p["w_up"],
      p["w_down"], *prev)


def kernel(x_prompt, x_sample, state_pool, state_ssm_re, state_ssm_im, state_conv, meta_tokens,
           norm1_g, w_in, pool_w, pool_scale, ssm_A_re, ssm_A_im, ssm_log_dt, ssm_B_re, ssm_B_im,
           ssm_C_re, ssm_C_im, ssm_D, w_glu, w_out, norm2_g, w_up, conv_w, conv_b, w_down, norm_f_g):
    depth = w_in.shape[0]
    bp, seq, _ = x_prompt.shape
    bs = x_sample.shape[0]
    assert seq % MIXER_CHUNK_STEPS == 0 and seq % FFN_CHUNK_STEPS == 0
    ab_re, ab_im, bbc, cre, cim = _discretise(ssm_A_re, ssm_A_im, ssm_log_dt, ssm_B_re, ssm_B_im,
                                              ssm_C_re, ssm_C_im)
    row = lambda a: a[:, None, :]
    p = dict(state_pool=state_pool.transpose(0, 2, 1, 3), state_conv=state_conv,
             state_re=state_ssm_re.reshape(depth, bs, SSM_FLAT),
             state_im=state_ssm_im.reshape(depth, bs, SSM_FLAT),
             n1g=row(norm1_g), pool_scale=row(pool_scale), a_re=ab_re, a_im=ab_im, bbc=bbc,
             cre=cre, cim=cim, dskip=row(ssm_D), w_in=w_in, pool_w=pool_w, w_glu=w_glu,
             w_out=w_out, n2g=row(norm2_g), conv_w=conv_w, conv_b=row(conv_b), w_up=w_up,
             w_down=w_down, nfg=norm_f_g[None])

    xs = jnp.concatenate([x_sample.reshape(bs, D_MODEL), jnp.repeat(meta_tokens, bp, axis=0)], axis=0)
    small = dict(Bs=bs, Bm=bp, Tm=N_META)
    mix_prev, ffn_prev, meta_state, mix_w, ffn_w = [], [], [], [], []
    for l in range(depth):
        xs, pool_s, re_s, im_s, mp, mr, mi, *wb = _mixer_small(xs, p, l, mix_prev, **small)
        mix_prev = [pool_s, re_s, im_s]
        mix_w.append(wb)
        xs, conv_s, mc, *wb = _ffn_small(xs, p, l, ffn_prev, final_norm=(l == depth - 1), **small)
        ffn_prev = [conv_s]
        ffn_w.append(wb)
        meta_state.append(((mp, mr, mi), mc))
    y_sample = xs[0:bs].reshape(bs, 1, D_MODEL)

    xp = x_prompt
    pool_p, re_p, im_p, conv_p = [], [], [], []
    for l in range(depth):
        mix_state, mc = meta_state[l]
        xp, pp, hr, hi = _mixer_main(xp, mix_state, p, l, mix_w[l], start_pos=N_META,
                                     batch_major_in=(l == 0), B=bp, T=MIXER_CHUNK_STEPS)
        last = l == depth - 1
        xp, cv = _ffn_main(xp, mc, p, l, ffn_w[l], final_norm=last, batch_major_out=last, B=bp,
                           T=FFN_CHUNK_STEPS)
        pool_p.append(pp)
        re_p.append(hr)
        im_p.append(hi)
        conv_p.append(cv)
    st = jnp.stack
    pool_p = st(pool_p).reshape(depth, POOL_KEEP, bp, POOL_WIDTH)[:, 1:].transpose(0, 2, 1, 3)
    conv_p = st(conv_p).reshape(depth, CONV_HIST, bp, D_FF).transpose(0, 2, 1, 3)
    group = lambda h: h.reshape(h.shape[:-1] + (SSM_GROUPS, SSM_STATE))
    return (xp, y_sample, pool_p, group(st(re_p)), group(st(im_p)), conv_p,
            pool_s.transpose(0, 2, 1, 3), group(re_s), group(im_s), conv_s)
```

```python
import functools

import jax
import jax.numpy as jnp
from jax import lax
from jax.experimental import pallas as pl
from jax.experimental.pallas import tpu as pltpu

D_MODEL = 1024
N_META = 16
POOL_WIDTH = 512
POOL_WINDOWS = (2, 4, 8, 16)
POOL_GROUPS = len(POOL_WINDOWS)
POOL_GROUP_IN = 128
POOL_GROUP_OUT = 256
POOL_HIST = 15
POOL_KEEP = 16
SSM_WIDTH = 512
SSM_GROUP_CH = 16
SSM_GROUPS = 32
SSM_STATE = 64
SSM_FLAT = SSM_GROUPS * SSM_STATE
SSM_TILE = 512
SSM_TILES = SSM_FLAT // SSM_TILE
SSM_TILE_GROUPS = SSM_TILE // SSM_STATE
SSM_TILE_CH = SSM_TILE_GROUPS * SSM_GROUP_CH
IN_COLS = POOL_WIDTH + SSM_WIDTH + 2 * D_MODEL
D_FF = 2816
FF_TILE = 256
FF_TILES = D_FF // FF_TILE
FF_LOOKAHEAD = 2
FF_SLOTS = FF_LOOKAHEAD + 1
CONV_W = 3
CONV_HIST = CONV_W - 1
RMS_EPS = 1e-6
PAST_LEN = 16384

V7X_LANES = 128
V7X_SUBLANES = 8
D_SLABS = D_MODEL // V7X_LANES
V7X_VMEM_LIMIT_BYTES = 56 * 1024 * 1024
MIXER_CHUNK_STEPS = 64
MIXER_SUB_BLOCKS = 1
FFN_CHUNK_STEPS = 128
FFN_SUB_BLOCKS = 4

_F32 = jnp.float32
_BF16 = jnp.bfloat16


def _rmsnorm(x, g):
    return x * lax.rsqrt(jnp.mean(x * x, axis=-1, keepdims=True) + RMS_EPS) * g


def _dot(a, b):
    return jnp.dot(a, b, preferred_element_type=_F32)


def _whole(shape):
    nd = len(shape)
    return pl.BlockSpec(shape, lambda *_: (0,) * nd, pipeline_mode=pl.Buffered(1))


def _layer(arr, l):
    nd = arr.ndim - 1
    return pl.BlockSpec((None,) + arr.shape[1:], lambda *_: (l,) + (0,) * nd,
                        pipeline_mode=pl.Buffered(1))


def _compiler_params():
    return pltpu.CompilerParams(dimension_semantics=("arbitrary",),
                                vmem_limit_bytes=V7X_VMEM_LIMIT_BYTES)


def _disc_kernel(are_ref, aim_ref, ldt_ref, bre_ref, bim_ref, ctre_ref, ctim_ref,
                 abre_ref, abim_ref, bbc_ref, cre_ref, cim_ref):
    lam_re = are_ref[...]
    lam_im = aim_ref[...]
    dt = jnp.exp(ldt_ref[...])
    mag = jnp.exp(lam_re * dt)
    ab_re = mag * jnp.cos(lam_im * dt)
    ab_im = mag * jnp.sin(lam_im * dt)
    den = lam_re * lam_re + lam_im * lam_im
    nr = ab_re - 1.0
    ni = ab_im
    q_re = (nr * lam_re + ni * lam_im) / den
    q_im = (ni * lam_re - nr * lam_im) / den
    b_re = bre_ref[...]
    b_im = bim_ref[...]
    abre_ref[...] = ab_re
    abim_ref[...] = ab_im
    bb_re = (q_re * b_re - q_im * b_im).astype(_BF16)
    bb_im = (q_re * b_im + q_im * b_re).astype(_BF16)
    ct_re = ctre_ref[...].astype(_BF16)
    ct_im = ctim_ref[...].astype(_BF16)
    bbc_ref[...] = jnp.zeros(bbc_ref.shape, _BF16)
    cre_ref[...] = jnp.zeros(cre_ref.shape, _BF16)
    cim_ref[...] = jnp.zeros(cim_ref.shape, _BF16)
    for i in range(SSM_TILES):
        for gl in range(SSM_TILE_GROUPS):
            g = i * SSM_TILE_GROUPS + gl
            ch = slice(gl * SSM_GROUP_CH, (gl + 1) * SSM_GROUP_CH)
            st = slice(gl * SSM_STATE, (gl + 1) * SSM_STATE)
            st_im = slice(SSM_TILE + gl * SSM_STATE, SSM_TILE + (gl + 1) * SSM_STATE)
            src_st = slice(g * SSM_STATE, (g + 1) * SSM_STATE)
            src_ch = slice(g * SSM_GROUP_CH, (g + 1) * SSM_GROUP_CH)
            bbc_ref[i, ch, st] = bb_re[:, src_st]
            bbc_ref[i, ch, st_im] = bb_im[:, src_st]
            cre_ref[i, st, ch] = ct_re[:, src_ch]
            cim_ref[i, st, ch] = ct_im[:, src_ch]


def _discretise(a_re, a_im, log_dt, b_re, b_im, c_re, c_im):
    depth = a_re.shape[0]
    flat = lambda a: a.reshape(depth, 1, SSM_FLAT)
    ldt = jnp.broadcast_to(log_dt[:, :, None], (depth, SSM_GROUPS, SSM_STATE))
    bt = lambda b: b.transpose(0, 3, 1, 2).reshape(depth, SSM_GROUP_CH, SSM_FLAT)
    ct = lambda c: c.transpose(0, 3, 1, 2).reshape(depth, SSM_STATE, SSM_WIDTH)
    ins = [flat(a_re), flat(a_im), flat(ldt), bt(b_re), bt(b_im), ct(c_re), ct(c_im)]
    out_shape = [jax.ShapeDtypeStruct((depth, 1, SSM_FLAT), _F32)] * 2 + [
        jax.ShapeDtypeStruct((depth, SSM_TILES, SSM_TILE_CH, 2 * SSM_TILE), _BF16),
        jax.ShapeDtypeStruct((depth, SSM_TILES, SSM_TILE, SSM_TILE_CH), _BF16),
        jax.ShapeDtypeStruct((depth, SSM_TILES, SSM_TILE, SSM_TILE_CH), _BF16)]
    per_layer = lambda s: pl.BlockSpec((None,) + s.shape[1:],
                                       lambda l: (l,) + (0,) * (len(s.shape) - 1))
    return pl.pallas_call(
        _disc_kernel,
        grid=(depth,),
        in_specs=[per_layer(a) for a in ins],
        out_specs=[per_layer(s) for s in out_shape],
        out_shape=out_shape,
        name="s5_discretise",
    )(*ins)


def _pool_branch(ext, n_hist, B, T, pos0, poolw, pscale):
    M = B * T
    base = n_hist * B
    pos = pos0 + lax.broadcasted_iota(jnp.int32, (M, 1), 0) // B
    tiles = []
    for g, w in enumerate(POOL_WINDOWS):
        ls = slice(g * POOL_GROUP_IN, (g + 1) * POOL_GROUP_IN)
        tok = ext[base:base + M, ls]
        win_sum = tok
        for k in range(1, w):
            win_sum = win_sum + ext[base - k * B:base - k * B + M, ls]
        cnt = jnp.minimum(w, pos + 1).astype(_F32)
        d = win_sum / cnt - tok
        tiles.append(_dot(d.astype(_BF16), poolw(g)))
    return jnp.concatenate(tiles, axis=-1) * pscale


def _ssm_inputs(u_ssm_b, bbc_ref):
    return [_dot(u_ssm_b[:, i * SSM_TILE_CH:(i + 1) * SSM_TILE_CH], bbc_ref[i])
            for i in range(SSM_TILES)]


def _ssm_tile(i, bu, h_re, h_im, B, T, are_ref, aim_ref, cre_ref, cim_ref):
    sl = slice(i * SSM_TILE, (i + 1) * SSM_TILE)
    a_re = jnp.broadcast_to(are_ref[:, sl], (B, SSM_TILE))
    a_im = jnp.broadcast_to(aim_ref[:, sl], (B, SSM_TILE))
    hs_re, hs_im = [], []
    for t in range(T):
        rows = slice(t * B, (t + 1) * B)
        n_re = a_re * h_re - a_im * h_im + bu[rows, 0:SSM_TILE]
        n_im = a_re * h_im + a_im * h_re + bu[rows, SSM_TILE:2 * SSM_TILE]
        h_re, h_im = n_re, n_im
        hs_re.append(h_re)
        hs_im.append(h_im)
    y = (_dot(jnp.concatenate(hs_re, axis=0).astype(_BF16), cre_ref[i])
         - _dot(jnp.concatenate(hs_im, axis=0).astype(_BF16), cim_ref[i]))
    return y, h_re, h_im


def _glu(y_tiles, u_ssm, dskip, wglu):
    y_s = jnp.concatenate(y_tiles, axis=-1) + dskip * u_ssm
    zz = _dot(jax.nn.gelu(y_s).astype(_BF16), wglu)
    return zz[:, 0:D_MODEL] * jax.nn.sigmoid(zz[:, D_MODEL:])


def _mixer_main_kernel(x_ref, hist_ref, h0re_ref, h0im_ref, n1g_ref, pscale_ref, are_ref, aim_ref,
                       bbc_ref, cre_ref, cim_ref, dskip_ref, win_ref, poolw_ref, wglu_ref, wout_ref,
                       xo_ref, poolst_ref, hre_o_ref, him_o_ref,
                       pool_ref, hst_ref, *scratch_refs, B, T, start_pos, n_chunks,
                       batch_major_in):
    M = B * T
    c = pl.program_id(0)

    @pl.when(c == 0)
    def _():
        pool_ref[...] = hist_ref[...]
        hst_ref[0] = h0re_ref[...]
        hst_ref[1] = h0im_ref[...]

    Ts = T // MIXER_SUB_BLOCKS
    Ms = B * Ts
    bu_refs, slab_refs = scratch_refs[:SSM_TILES], scratch_refs[SSM_TILES:]
    staged_rows = pl.ds(pl.multiple_of(jnp.minimum(c, 0), B), Ms)
    gate_cols = 2 * D_MODEL // SSM_TILES

    for s in range(MIXER_SUB_BLOCKS):
        if batch_major_in:
            xs_ref = slab_refs[s]
            for t0 in range(0, Ts, V7X_SUBLANES):
                for b in range(B):
                    for q in range(D_SLABS):
                        xs_ref[q, pl.ds(t0 * B + b, V7X_SUBLANES, stride=B), :] = x_ref[
                            b, s * Ts + t0:s * Ts + t0 + V7X_SUBLANES,
                            q * V7X_LANES:(q + 1) * V7X_LANES]
            x = jnp.concatenate([xs_ref[q] for q in range(D_SLABS)], axis=-1)
        else:
            x = x_ref[s * Ms:(s + 1) * Ms, :]
        hb = _rmsnorm(x, n1g_ref[...]).astype(_BF16)

        u_ssm = _dot(hb, win_ref[:, POOL_WIDTH:POOL_WIDTH + SSM_WIDTH])
        u_ssm_b = u_ssm.astype(_BF16)
        def bbar(i):
            bu_refs[i][staged_rows, :] = _dot(u_ssm_b[:, i * SSM_TILE_CH:(i + 1) * SSM_TILE_CH],
                                              bbc_ref[i])

        def gate(i):
            lo = POOL_WIDTH + SSM_WIDTH + i * gate_cols
            return jax.nn.sigmoid(_dot(hb, win_ref[:, lo:lo + gate_cols]))

        def scan_c(i):
            sl = slice(i * SSM_TILE, (i + 1) * SSM_TILE)
            y, h_re, h_im = _ssm_tile(i, bu_refs[i], hst_ref[0, :, sl], hst_ref[1, :, sl], B, Ts,
                                      are_ref, aim_ref, cre_ref, cim_ref)
            hst_ref[0, :, sl] = h_re
            hst_ref[1, :, sl] = h_im
            return y

        gates, y_tiles = [], []
        bbar(0)
        u_pool = _dot(hb, win_ref[:, 0:POOL_WIDTH])
        bbar(1)
        gates.append(gate(0))
        bbar(2)
        gates.append(gate(1))
        bbar(3)
        y_tiles.append(scan_c(0))
        gates.append(gate(2))
        y_tiles.append(scan_c(1))
        gates.append(gate(3))
        y_tiles.append(scan_c(2))
        ext = jnp.concatenate([pool_ref[...], u_pool], axis=0)
        pool_ref[...] = ext[Ms:Ms + POOL_KEEP * B, :]
        y_pool = _pool_branch(ext, POOL_KEEP, B, Ts, start_pos + c * T + s * Ts,
                              lambda g: poolw_ref[g], pscale_ref[...])
        y_tiles.append(scan_c(3))
        gate_pool = jnp.concatenate(gates[0:SSM_TILES // 2], axis=-1)
        gate_ssm = jnp.concatenate(gates[SSM_TILES // 2:], axis=-1)
        merge_pool = gate_pool * y_pool
        merge = merge_pool + gate_ssm * _glu(y_tiles, u_ssm, dskip_ref[...], wglu_ref[...])
        xo_ref[s * Ms:(s + 1) * Ms, :] = x + _dot(merge.astype(_BF16), wout_ref[...])

    @pl.when(c == n_chunks - 1)
    def _():
        poolst_ref[...] = pool_ref[...]
        hre_o_ref[...] = hst_ref[0]
        him_o_ref[...] = hst_ref[1]


def _mixer_main(x, meta_state, p, l, wb, *, B, T, start_pos, batch_major_in):
    M = B * T
    if batch_major_in:
        n_chunks = x.shape[1] // T
        x_spec = pl.BlockSpec((B, T, D_MODEL), lambda c: (0, c, 0))
        slabs = [pltpu.VMEM((D_SLABS, M // MIXER_SUB_BLOCKS, V7X_LANES), _F32)] * MIXER_SUB_BLOCKS
    else:
        n_chunks = x.shape[0] // M
        x_spec = pl.BlockSpec((M, D_MODEL), lambda c: (c, 0))
        slabs = []
    keep = POOL_KEEP * B
    kern = functools.partial(_mixer_main_kernel, B=B, T=T, start_pos=start_pos,
                             n_chunks=n_chunks, batch_major_in=batch_major_in)
    stacked = [p["n1g"], p["pool_scale"], p["a_re"], p["a_im"], p["bbc"], p["cre"], p["cim"],
               p["dskip"]]
    return pl.pallas_call(
        kern,
        grid=(n_chunks,),
        in_specs=[x_spec] + [_whole(a.shape) for a in meta_state]
        + [_layer(a, l) for a in stacked] + [_whole(a.shape) for a in wb],
        out_specs=[pl.BlockSpec((M, D_MODEL), lambda c: (c, 0)), _whole((keep, POOL_WIDTH)),
                   _whole((B, SSM_FLAT)), _whole((B, SSM_FLAT))],
        out_shape=[jax.ShapeDtypeStruct((n_chunks * M, D_MODEL), _F32),
                   jax.ShapeDtypeStruct((keep, POOL_WIDTH), _F32),
                   jax.ShapeDtypeStruct((B, SSM_FLAT), _F32),
                   jax.ShapeDtypeStruct((B, SSM_FLAT), _F32)],
        scratch_shapes=[pltpu.VMEM((keep, POOL_WIDTH), _F32),
                        pltpu.VMEM((2, B, SSM_FLAT), _F32),
                        ] + [pltpu.VMEM((M // MIXER_SUB_BLOCKS, 2 * SSM_TILE), _F32)] * SSM_TILES
        + slabs,
        compiler_params=_compiler_params(),
        name="mixer_main",
    )(x, *meta_state, *stacked, *wb)


def _mixer_small_kernel(x_ref, pool_ref, sre_ref, sim_ref, n1g_ref, pscale_ref, are_ref, aim_ref,
                        bbc_ref, cre_ref, cim_ref, dskip_ref, win_ref, poolw_ref, wglu_ref,
                        wout_ref, *refs, Bs, Bm, Tm, n_alias):
    (xo_ref, pool_o_ref, sre_o_ref, sim_o_ref, mpool_o_ref, mre_o_ref, mim_o_ref,
     win_o_ref, poolw_o_ref, wglu_o_ref, wout_o_ref) = refs[n_alias:]
    Ms, Mm = Bs, Bm * Tm
    win_o_ref[...] = win_ref[...].astype(_BF16)
    poolw_o_ref[...] = poolw_ref[...].astype(_BF16)
    wglu_o_ref[...] = wglu_ref[...].astype(_BF16)
    wout_o_ref[...] = wout_ref[...].astype(_BF16)

    x = x_ref[...]
    hb = _rmsnorm(x, n1g_ref[...]).astype(_BF16)
    proj = _dot(hb, win_o_ref[...])
    u_pool, u_ssm = proj[:, 0:POOL_WIDTH], proj[:, POOL_WIDTH:POOL_WIDTH + SSM_WIDTH]
    gate_pool = jax.nn.sigmoid(proj[:, POOL_WIDTH + SSM_WIDTH:POOL_WIDTH + SSM_WIDTH + D_MODEL])
    gate_ssm = jax.nn.sigmoid(proj[:, POOL_WIDTH + SSM_WIDTH + D_MODEL:])
    bus = _ssm_inputs(u_ssm.astype(_BF16), bbc_ref)
    poolw = lambda g: poolw_o_ref[g]

    hist = [pool_ref[j] for j in range(POOL_HIST)]
    new_hist = hist[1:] + [u_pool[0:Ms]]
    for j in range(POOL_HIST):
        pool_o_ref[j] = new_hist[j]
    y_pool_s = _pool_branch(jnp.concatenate(hist + [u_pool[0:Ms]], axis=0), POOL_HIST, Bs, 1,
                            PAST_LEN, poolw, pscale_ref[...])
    ext_m = jnp.concatenate([jnp.zeros((POOL_KEEP * Bm, POOL_WIDTH), _F32), u_pool[Ms:]], axis=0)
    y_pool_m = _pool_branch(ext_m, POOL_KEEP, Bm, Tm, 0, poolw, pscale_ref[...])
    mpool_o_ref[...] = ext_m[Mm:Mm + POOL_KEEP * Bm, :]
    y_pool = jnp.concatenate([y_pool_s, y_pool_m], axis=0)

    y_tiles = []
    for i in range(SSM_TILES):
        sl = slice(i * SSM_TILE, (i + 1) * SSM_TILE)
        ys, h_re, h_im = _ssm_tile(i, bus[i][0:Ms], sre_ref[:, sl], sim_ref[:, sl], Bs, 1,
                                   are_ref, aim_ref, cre_ref, cim_ref)
        sre_o_ref[:, sl] = h_re
        sim_o_ref[:, sl] = h_im
        zero = jnp.zeros((Bm, SSM_TILE), _F32)
        ym, h_re, h_im = _ssm_tile(i, bus[i][Ms:], zero, zero, Bm, Tm,
                                   are_ref, aim_ref, cre_ref, cim_ref)
        mre_o_ref[:, sl] = h_re
        mim_o_ref[:, sl] = h_im
        y_tiles.append(jnp.concatenate([ys, ym], axis=0))
    merge = gate_pool * y_pool + gate_ssm * _glu(y_tiles, u_ssm, dskip_ref[...], wglu_o_ref[...])
    xo_ref[...] = x + _dot(merge.astype(_BF16), wout_o_ref[...])


def _mixer_small(x, p, l, prev, *, Bs, Bm, Tm):
    kern = functools.partial(_mixer_small_kernel, Bs=Bs, Bm=Bm, Tm=Tm, n_alias=len(prev))
    stacked = [p["state_pool"], p["state_re"], p["state_im"], p["n1g"], p["pool_scale"], p["a_re"],
               p["a_im"], p["bbc"], p["cre"], p["cim"], p["dskip"], p["w_in"], p["pool_w"],
               p["w_glu"], p["w_out"]]
    sds = jax.ShapeDtypeStruct
    state_shapes = [sds(p["state_pool"].shape, _F32), sds(p["state_re"].shape, _F32),
                    sds(p["state_im"].shape, _F32)]
    plain_shapes = [sds((POOL_KEEP * Bm, POOL_WIDTH), _F32), sds((Bm, SSM_FLAT), _F32),
                    sds((Bm, SSM_FLAT), _F32), sds((D_MODEL, IN_COLS), _BF16),
                    sds((POOL_GROUPS, POOL_GROUP_IN, POOL_GROUP_OUT), _BF16),
                    sds((SSM_WIDTH, 2 * D_MODEL), _BF16), sds((D_MODEL, D_MODEL), _BF16)]
    n_in = 1 + len(stacked)
    return pl.pallas_call(
        kern,
        grid=(1,),
        in_specs=[_whole(x.shape)] + [_layer(a, l) for a in stacked]
        + [pl.BlockSpec(memory_space=pl.ANY)] * len(prev),
        out_specs=[_whole(x.shape)] + [_layer(s, l) for s in state_shapes]
        + [_whole(s.shape) for s in plain_shapes],
        out_shape=[sds(x.shape, _F32)] + state_shapes + plain_shapes,
        input_output_aliases={n_in + k: 1 + k for k in range(len(prev))},
        compiler_params=_compiler_params(),
        name="mixer_small",
    )(x, *stacked, *prev)


GELU_C1 = 0.7978845608028654
GELU_C2 = GELU_C1 * 0.044715


def _gelu2_mul(x, v):
    t = jnp.tanh(x * (GELU_C1 + GELU_C2 * (x * x)))
    return (x * v) * (1.0 + t)


def _conv_tile(g_ext, B, M, convw, convb):
    conv = convb
    for k in range(CONV_W):
        conv = conv + convw[k:k + 1, :] * g_ext[k * B:k * B + M, :]
    return conv


def _ffn_main_kernel(x_ref, chist_ref, n2g_ref, convw_ref, convb_ref, nfg_ref, wgv_ref,
                     wdown_ref, xo_ref, convst_ref,
                     carry_ref, *scratch_refs, B, T, n_chunks, final_norm, batch_major_out):
    M = B * T
    keep = CONV_HIST * B
    c = pl.program_id(0)

    @pl.when(c == 0)
    def _():
        carry_ref[...] = chist_ref[...]

    Ts = T // FFN_SUB_BLOCKS
    Ms = B * Ts
    gv_refs, slab_refs = scratch_refs[:FF_SLOTS], scratch_refs[FF_SLOTS:]
    xs = [x_ref[s * Ms:(s + 1) * Ms, :] for s in range(FFN_SUB_BLOCKS)]
    hbs = [_rmsnorm(x, n2g_ref[...]).astype(_BF16) for x in xs]

    staged_rows = pl.ds(pl.multiple_of(keep + jnp.minimum(c, 0), B), Ms)
    n_tiles = FFN_SUB_BLOCKS * FF_TILES

    def up(k):
        s, j = divmod(k, FF_TILES)
        gv_refs[k % FF_SLOTS][staged_rows, :] = _dot(
            hbs[s], wgv_ref[:, 2 * j * FF_TILE:2 * (j + 1) * FF_TILE])

    for k in range(FF_LOOKAHEAD):
        up(k)
    acc = None
    for k in range(n_tiles):
        s, j = divmod(k, FF_TILES)
        fs = slice(j * FF_TILE, (j + 1) * FF_TILE)
        if k + FF_LOOKAHEAD < n_tiles:
            up(k + FF_LOOKAHEAD)
        slot = gv_refs[k % FF_SLOTS]
        slot[0:keep, 0:FF_TILE] = carry_ref[:, fs]
        conv = _conv_tile(slot.at[:, 0:FF_TILE], B, Ms, convw_ref[:, fs], convb_ref[:, fs])
        carry_ref[:, fs] = slot[Ms:Ms + keep, 0:FF_TILE]
        v = slot[keep:keep + Ms, FF_TILE:]
        contrib = _dot(_gelu2_mul(conv, v).astype(_BF16), wdown_ref[fs, :])
        acc = contrib if j == 0 else acc + contrib
        if j == FF_TILES - 1:
            xo = xs[s] + acc
            if final_norm:
                xo = _rmsnorm(xo, nfg_ref[...])
            if batch_major_out:
                ys_ref = slab_refs[s]
                for q in range(D_SLABS):
                    ys_ref[q] = xo[:, q * V7X_LANES:(q + 1) * V7X_LANES]
                for b in range(B):
                    for q in range(D_SLABS):
                        xo_ref[b, s * Ts:(s + 1) * Ts, q * V7X_LANES:(q + 1) * V7X_LANES] = (
                            ys_ref[q, pl.ds(b, Ts, stride=B), :])
            else:
                xo_ref[s * Ms:(s + 1) * Ms, :] = xo

    @pl.when(c == n_chunks - 1)
    def _():
        convst_ref[...] = carry_ref[...]


def _ffn_main(x, chist, p, l, wb, *, B, T, final_norm, batch_major_out):
    M = B * T
    n_chunks = x.shape[0] // M
    keep = CONV_HIST * B
    kern = functools.partial(_ffn_main_kernel, B=B, T=T, n_chunks=n_chunks,
                             final_norm=final_norm, batch_major_out=batch_major_out)
    stacked = [p["n2g"], p["conv_w"], p["conv_b"]]
    if batch_major_out:
        o_spec = pl.BlockSpec((B, T, D_MODEL), lambda c: (0, c, 0))
        o_shape = jax.ShapeDtypeStruct((B, n_chunks * T, D_MODEL), _F32)
        slabs = [pltpu.VMEM((D_SLABS, M // FFN_SUB_BLOCKS, V7X_LANES), _F32)] * FFN_SUB_BLOCKS
    else:
        o_spec = pl.BlockSpec((M, D_MODEL), lambda c: (c, 0))
        o_shape = jax.ShapeDtypeStruct((n_chunks * M, D_MODEL), _F32)
        slabs = []
    return pl.pallas_call(
        kern,
        grid=(n_chunks,),
        in_specs=[pl.BlockSpec((M, D_MODEL), lambda c: (c, 0)), _whole(chist.shape)]
        + [_layer(a, l) for a in stacked] + [_whole(p["nfg"].shape)]
        + [_whole(a.shape) for a in wb],
        out_specs=[o_spec, _whole((keep, D_FF))],
        out_shape=[o_shape, jax.ShapeDtypeStruct((keep, D_FF), _F32)],
        scratch_shapes=[pltpu.VMEM((keep, D_FF), _F32),
                        ] + [pltpu.VMEM((keep + M // FFN_SUB_BLOCKS, 2 * FF_TILE), _F32)] * FF_SLOTS
        + slabs,
        compiler_params=_compiler_params(),
        name="convffn_main",
    )(x, chist, *stacked, p["nfg"], *wb)


def _ffn_small_kernel(x_ref, conv_ref, n2g_ref, convw_ref, convb_ref, nfg_ref, wg_ref, wv_ref,
                      wdown_ref, *refs, Bs, Bm, Tm, final_norm, n_alias):
    (xo_ref, conv_o_ref, mconv_o_ref, wgv_o_ref, wdown_o_ref,
     hb_ref, acc_ref) = refs[n_alias:]
    Ms, Mm = Bs, Bm * Tm
    jt = pl.program_id(0)

    @pl.when(jt == 0)
    def _():
        hb_ref[...] = _rmsnorm(x_ref[...], n2g_ref[...]).astype(_BF16)
        acc_ref[...] = jnp.zeros(acc_ref.shape, _F32)

    wgv_o_ref[:, 0:FF_TILE] = wg_ref[...].astype(_BF16)
    wgv_o_ref[:, FF_TILE:] = wv_ref[...].astype(_BF16)
    wdown_o_ref[...] = (0.5 * wdown_ref[...]).astype(_BF16)
    hb = hb_ref[...]
    gv = _dot(hb, wgv_o_ref[...])
    g, v = gv[:, 0:FF_TILE], gv[:, FF_TILE:]
    hist = [conv_ref[:, j, :] for j in range(CONV_HIST)]
    new_hist = hist[1:] + [g[0:Ms]]
    for j in range(CONV_HIST):
        conv_o_ref[:, j, :] = new_hist[j]
    g_ext_s = jnp.concatenate(hist + [g[0:Ms]], axis=0)
    g_ext_m = jnp.concatenate([jnp.zeros((CONV_HIST * Bm, FF_TILE), _F32), g[Ms:]], axis=0)
    mconv_o_ref[...] = g_ext_m[Mm:Mm + CONV_HIST * Bm, :]
    conv = jnp.concatenate([_conv_tile(g_ext_s, Bs, Ms, convw_ref[...], convb_ref[...]),
                            _conv_tile(g_ext_m, Bm, Mm, convw_ref[...], convb_ref[...])], axis=0)
    acc_ref[...] += _dot(_gelu2_mul(conv, v).astype(_BF16), wdown_o_ref[...])

    @pl.when(jt == FF_TILES - 1)
    def _():
        xo = x_ref[...] + acc_ref[...]
        if final_norm:
            xo = _rmsnorm(xo, nfg_ref[...])
        xo_ref[...] = xo


def _ffn_small(x, p, l, prev, *, Bs, Bm, Tm, final_norm):
    kern = functools.partial(_ffn_small_kernel, Bs=Bs, Bm=Bm, Tm=Tm, final_norm=final_norm,
                             n_alias=len(prev))
    sds = jax.ShapeDtypeStruct
    tile_of = lambda a, idx: pl.BlockSpec((None,) + idx[0], idx[1])
    n_rows = x.shape[0]
    in_specs = [
        _whole(x.shape),
        pl.BlockSpec((None, Bs, CONV_HIST, FF_TILE), lambda j: (l, 0, 0, j)),
        _layer(p["n2g"], l),
        pl.BlockSpec((None, CONV_W, FF_TILE), lambda j: (l, 0, j)),
        pl.BlockSpec((None, 1, FF_TILE), lambda j: (l, 0, j)),
        _whole(p["nfg"].shape),
        pl.BlockSpec((None, D_MODEL, FF_TILE), lambda j: (l, 0, j)),
        pl.BlockSpec((None, D_MODEL, FF_TILE), lambda j: (l, 0, FF_TILES + j)),
        pl.BlockSpec((None, FF_TILE, D_MODEL), lambda j: (l, j, 0)),
    ] + [pl.BlockSpec(memory_space=pl.ANY)] * len(prev)
    out_shape = [sds(x.shape, _F32), sds(p["state_conv"].shape, _F32),
                 sds((CONV_HIST * Bm, D_FF), _F32), sds((D_MODEL, 2 * D_FF), _BF16),
                 sds((D_FF, D_MODEL), _BF16)]
    out_specs = [
        _whole(x.shape),
        pl.BlockSpec((None, Bs, CONV_HIST, FF_TILE), lambda j: (l, 0, 0, j)),
        pl.BlockSpec((CONV_HIST * Bm, FF_TILE), lambda j: (0, j)),
        pl.BlockSpec((D_MODEL, 2 * FF_TILE), lambda j: (0, j)),
        pl.BlockSpec((FF_TILE, D_MODEL), lambda j: (j, 0)),
    ]
    n_in = 9
    return pl.pallas_call(
        kern,
        grid=(FF_TILES,),
        in_specs=in_specs,
        out_specs=out_specs,
        out_shape=out_shape,
        scratch_shapes=[pltpu.VMEM((n_rows, D_MODEL), _BF16), pltpu.VMEM((n_rows, D_MODEL), _F32)],
        input_output_aliases={n_in + k: 1 + k for k in range(len(prev))},
        compiler_params=_compiler_params(),
        name="convffn_small",
    )(x, p["state_conv"], p["n2g"], p["conv_w"], p["conv_b"], p["nfg"], p["w_up"], p["w_up"],
      p["w_down"], *prev)


def kernel(x_prompt, x_sample, state_pool, state_ssm_re, state_ssm_im, state_conv, meta_tokens,
           norm1_g, w_in, pool_w, pool_scale, ssm_A_re, ssm_A_im, ssm_log_dt, ssm_B_re, ssm_B_im,
           ssm_C_re, ssm_C_im, ssm_D, w_glu, w_out, norm2_g, w_up, conv_w, conv_b, w_down, norm_f_g):
    depth = w_in.shape[0]
    bp, seq, _ = x_prompt.shape
    bs = x_sample.shape[0]
    assert seq % MIXER_CHUNK_STEPS == 0 and seq % FFN_CHUNK_STEPS == 0
    ab_re, ab_im, bbc, cre, cim = _discretise(ssm_A_re, ssm_A_im, ssm_log_dt, ssm_B_re, ssm_B_im,
                                              ssm_C_re, ssm_C_im)
    row = lambda a: a[:, None, :]
    p = dict(state_pool=state_pool.transpose(0, 2, 1, 3), state_conv=state_conv,
             state_re=state_ssm_re.reshape(depth, bs, SSM_FLAT),
             state_im=state_ssm_im.reshape(depth, bs, SSM_FLAT),
             n1g=row(norm1_g), pool_scale=row(pool_scale), a_re=ab_re, a_im=ab_im, bbc=bbc,
             cre=cre, cim=cim, dskip=row(ssm_D), w_in=w_in, pool_w=pool_w, w_glu=w_glu,
             w_out=w_out, n2g=row(norm2_g), conv_w=conv_w, conv_b=row(conv_b), w_up=w_up,
             w_down=w_down, nfg=norm_f_g[None])

    xs = jnp.concatenate([x_sample.reshape(bs, D_MODEL), jnp.repeat(meta_tokens, bp, axis=0)], axis=0)
    small = dict(Bs=bs, Bm=bp, Tm=N_META)
    mix_prev, ffn_prev, meta_state, mix_w, ffn_w = [], [], [], [], []
    for l in range(depth):
        xs, pool_s, re_s, im_s, mp, mr, mi, *wb = _mixer_small(xs, p, l, mix_prev, **small)
        mix_prev = [pool_s, re_s, im_s]
        mix_w.append(wb)
        xs, conv_s, mc, *wb = _ffn_small(xs, p, l, ffn_prev, final_norm=(l == depth - 1), **small)
        ffn_prev = [conv_s]
        ffn_w.append(wb)
        meta_state.append(((mp, mr, mi), mc))
    y_sample = xs[0:bs].reshape(bs, 1, D_MODEL)

    xp = x_prompt
    pool_p, re_p, im_p, conv_p = [], [], [], []
    for l in range(depth):
        mix_state, mc = meta_state[l]
        xp, pp, hr, hi = _mixer_main(xp, mix_state, p, l, mix_w[l], start_pos=N_META,
                                     batch_major_in=(l == 0), B=bp, T=MIXER_CHUNK_STEPS)
        last = l == depth - 1
        xp, cv = _ffn_main(xp, mc, p, l, ffn_w[l], final_norm=last, batch_major_out=last, B=bp,
                           T=FFN_CHUNK_STEPS)
        pool_p.append(pp)
        re_p.append(hr)
        im_p.append(hi)
        conv_p.append(cv)
    st = jnp.stack
    pool_p = st(pool_p).reshape(depth, POOL_KEEP, bp, POOL_WIDTH)[:, 1:].transpose(0, 2, 1, 3)
    conv_p = st(conv_p).reshape(depth, CONV_HIST, bp, D_FF).transpose(0, 2, 1, 3)
    group = lambda h: h.reshape(h.shape[:-1] + (SSM_GROUPS, SSM_STATE))
    return (xp, y_sample, pool_p, group(st(re_p)), group(st(im_p)), conv_p,
            pool_s.transpose(0, 2, 1, 3), group(re_s), group(im_s), conv_s)
```

```python
import functools

import jax
import jax.numpy as jnp
from jax import lax
from jax.experimental import pallas as pl
from jax.experimental.pallas import tpu as pltpu

D_MODEL = 1024
N_META = 16
POOL_WIDTH = 512
POOL_WINDOWS = (2, 4, 8, 16)
POOL_GROUPS = len(POOL_WINDOWS)
POOL_GROUP_IN = 128
POOL_GROUP_OUT = 256
POOL_HIST = 15
POOL_KEEP = 16
SSM_WIDTH = 512
SSM_GROUP_CH = 16
SSM_GROUPS = 32
SSM_STATE = 64
SSM_FLAT = SSM_GROUPS * SSM_STATE
SSM_TILE = 512
SSM_TILES = SSM_FLAT // SSM_TILE
SSM_TILE_GROUPS = SSM_TILE // SSM_STATE
SSM_TILE_CH = SSM_TILE_GROUPS * SSM_GROUP_CH
IN_COLS = POOL_WIDTH + SSM_WIDTH + 2 * D_MODEL
D_FF = 2816
FF_TILE = 256
FF_TILES = D_FF // FF_TILE
FF_LOOKAHEAD = 2
FF_SLOTS = FF_LOOKAHEAD + 1
CONV_W = 3
CONV_HIST = CONV_W - 1
RMS_EPS = 1e-6
PAST_LEN = 16384

V7X_LANES = 128
V7X_SUBLANES = 8
D_SLABS = D_MODEL // V7X_LANES
V7X_VMEM_LIMIT_BYTES = 56 * 1024 * 1024
MIXER_CHUNK_STEPS = 64
MIXER_SUB_BLOCKS = 1
FFN_CHUNK_STEPS = 32
FFN_SUB_BLOCKS = 1

_F32 = jnp.float32
_BF16 = jnp.bfloat16


def _rmsnorm(x, g):
    return x * lax.rsqrt(jnp.mean(x * x, axis=-1, keepdims=True) + RMS_EPS) * g


def _dot(a, b):
    return jnp.dot(a, b, preferred_element_type=_F32)


def _whole(shape):
    nd = len(shape)
    return pl.BlockSpec(shape, lambda *_: (0,) * nd, pipeline_mode=pl.Buffered(1))


def _layer(arr, l):
    nd = arr.ndim - 1
    return pl.BlockSpec((None,) + arr.shape[1:], lambda *_: (l,) + (0,) * nd,
                        pipeline_mode=pl.Buffered(1))


def _compiler_params():
    return pltpu.CompilerParams(dimension_semantics=("arbitrary",),
                                vmem_limit_bytes=V7X_VMEM_LIMIT_BYTES)


def _disc_kernel(are_ref, aim_ref, ldt_ref, bre_ref, bim_ref, ctre_ref, ctim_ref,
                 abre_ref, abim_ref, bbc_ref, cre_ref, cim_ref):
    lam_re = are_ref[...]
    lam_im = aim_ref[...]
    dt = jnp.exp(ldt_ref[...])
    mag = jnp.exp(lam_re * dt)
    ab_re = mag * jnp.cos(lam_im * dt)
    ab_im = mag * jnp.sin(lam_im * dt)
    den = lam_re * lam_re + lam_im * lam_im
    nr = ab_re - 1.0
    ni = ab_im
    q_re = (nr * lam_re + ni * lam_im) / den
    q_im = (ni * lam_re - nr * lam_im) / den
    b_re = bre_ref[...]
    b_im = bim_ref[...]
    abre_ref[...] = ab_re
    abim_ref[...] = ab_im
    bb_re = (q_re * b_re - q_im * b_im).astype(_BF16)
    bb_im = (q_re * b_im + q_im * b_re).astype(_BF16)
    ct_re = ctre_ref[...].astype(_BF16)
    ct_im = ctim_ref[...].astype(_BF16)
    bbc_ref[...] = jnp.zeros(bbc_ref.shape, _BF16)
    cre_ref[...] = jnp.zeros(cre_ref.shape, _BF16)
    cim_ref[...] = jnp.zeros(cim_ref.shape, _BF16)
    for i in range(SSM_TILES):
        for gl in range(SSM_TILE_GROUPS):
            g = i * SSM_TILE_GROUPS + gl
            ch = slice(gl * SSM_GROUP_CH, (gl + 1) * SSM_GROUP_CH)
            st = slice(gl * SSM_STATE, (gl + 1) * SSM_STATE)
            st_im = slice(SSM_TILE + gl * SSM_STATE, SSM_TILE + (gl + 1) * SSM_STATE)
            src_st = slice(g * SSM_STATE, (g + 1) * SSM_STATE)
            src_ch = slice(g * SSM_GROUP_CH, (g + 1) * SSM_GROUP_CH)
            bbc_ref[i, ch, st] = bb_re[:, src_st]
            bbc_ref[i, ch, st_im] = bb_im[:, src_st]
            cre_ref[i, st, ch] = ct_re[:, src_ch]
            cim_ref[i, st, ch] = ct_im[:, src_ch]


def _discretise(a_re, a_im, log_dt, b_re, b_im, c_re, c_im):
    depth = a_re.shape[0]
    flat = lambda a: a.reshape(depth, 1, SSM_FLAT)
    ldt = jnp.broadcast_to(log_dt[:, :, None], (depth, SSM_GROUPS, SSM_STATE))
    bt = lambda b: b.transpose(0, 3, 1, 2).reshape(depth, SSM_GROUP_CH, SSM_FLAT)
    ct = lambda c: c.transpose(0, 3, 1, 2).reshape(depth, SSM_STATE, SSM_WIDTH)
    ins = [flat(a_re), flat(a_im), flat(ldt), bt(b_re), bt(b_im), ct(c_re), ct(c_im)]
    out_shape = [jax.ShapeDtypeStruct((depth, 1, SSM_FLAT), _F32)] * 2 + [
        jax.ShapeDtypeStruct((depth, SSM_TILES, SSM_TILE_CH, 2 * SSM_TILE), _BF16),
        jax.ShapeDtypeStruct((depth, SSM_TILES, SSM_TILE, SSM_TILE_CH), _BF16),
        jax.ShapeDtypeStruct((depth, SSM_TILES, SSM_TILE, SSM_TILE_CH), _BF16)]
    per_layer = lambda s: pl.BlockSpec((None,) + s.shape[1:],
                                       lambda l: (l,) + (0,) * (len(s.shape) - 1))
    return pl.pallas_call(
        _disc_kernel,
        grid=(depth,),
        in_specs=[per_layer(a) for a in ins],
        out_specs=[per_layer(s) for s in out_shape],
        out_shape=out_shape,
        name="s5_discretise",
    )(*ins)


def _pool_branch(ext, n_hist, B, T, pos0, poolw, pscale):
    M = B * T
    base = n_hist * B
    pos = pos0 + lax.broadcasted_iota(jnp.int32, (M, 1), 0) // B
    tiles = []
    for g, w in enumerate(POOL_WINDOWS):
        ls = slice(g * POOL_GROUP_IN, (g + 1) * POOL_GROUP_IN)
        tok = ext[base:base + M, ls]
        win_sum = tok
        for k in range(1, w):
            win_sum = win_sum + ext[base - k * B:base - k * B + M, ls]
        cnt = jnp.minimum(w, pos + 1).astype(_F32)
        d = win_sum / cnt - tok
        tiles.append(_dot(d.astype(_BF16), poolw(g)))
    return jnp.concatenate(tiles, axis=-1) * pscale


def _ssm_inputs(u_ssm_b, bbc_ref):
    return [_dot(u_ssm_b[:, i * SSM_TILE_CH:(i + 1) * SSM_TILE_CH], bbc_ref[i])
            for i in range(SSM_TILES)]


def _ssm_tile(i, bu, h_re, h_im, B, T, are_ref, aim_ref, cre_ref, cim_ref):
    sl = slice(i * SSM_TILE, (i + 1) * SSM_TILE)
    a_re = jnp.broadcast_to(are_ref[:, sl], (B, SSM_TILE))
    a_im = jnp.broadcast_to(aim_ref[:, sl], (B, SSM_TILE))
    hs_re, hs_im = [], []
    for t in range(T):
        rows = slice(t * B, (t + 1) * B)
        n_re = a_re * h_re - a_im * h_im + bu[rows, 0:SSM_TILE]
        n_im = a_re * h_im + a_im * h_re + bu[rows, SSM_TILE:2 * SSM_TILE]
        h_re, h_im = n_re, n_im
        hs_re.append(h_re)
        hs_im.append(h_im)
    y = (_dot(jnp.concatenate(hs_re, axis=0).astype(_BF16), cre_ref[i])
         - _dot(jnp.concatenate(hs_im, axis=0).astype(_BF16), cim_ref[i]))
    return y, h_re, h_im


def _glu(y_tiles, u_ssm, dskip, wglu):
    y_s = jnp.concatenate(y_tiles, axis=-1) + dskip * u_ssm
    zz = _dot(jax.nn.gelu(y_s).astype(_BF16), wglu)
    return zz[:, 0:D_MODEL] * jax.nn.sigmoid(zz[:, D_MODEL:])


def _mixer_main_kernel(x_ref, hist_ref, h0re_ref, h0im_ref, n1g_ref, pscale_ref, are_ref, aim_ref,
                       bbc_ref, cre_ref, cim_ref, dskip_ref, win_ref, poolw_ref, wglu_ref, wout_ref,
                       xo_ref, poolst_ref, hre_o_ref, him_o_ref,
                       pool_ref, hst_ref, *scratch_refs, B, T, start_pos, n_chunks,
                       batch_major_in):
    M = B * T
    c = pl.program_id(0)

    @pl.when(c == 0)
    def _():
        pool_ref[...] = hist_ref[...]
        hst_ref[0] = h0re_ref[...]
        hst_ref[1] = h0im_ref[...]

    Ts = T // MIXER_SUB_BLOCKS
    Ms = B * Ts
    bu_refs, slab_refs = scratch_refs[:SSM_TILES], scratch_refs[SSM_TILES:]
    staged_rows = pl.ds(pl.multiple_of(jnp.minimum(c, 0), B), Ms)
    gate_cols = 2 * D_MODEL // SSM_TILES

    for s in range(MIXER_SUB_BLOCKS):
        if batch_major_in:
            xs_ref = slab_refs[s]
            for t0 in range(0, Ts, V7X_SUBLANES):
                for b in range(B):
                    for q in range(D_SLABS):
                        xs_ref[q, pl.ds(t0 * B + b, V7X_SUBLANES, stride=B), :] = x_ref[
                            b, s * Ts + t0:s * Ts + t0 + V7X_SUBLANES,
                            q * V7X_LANES:(q + 1) * V7X_LANES]
            x = jnp.concatenate([xs_ref[q] for q in range(D_SLABS)], axis=-1)
        else:
            x = x_ref[s * Ms:(s + 1) * Ms, :]
        hb = _rmsnorm(x, n1g_ref[...]).astype(_BF16)

        u_ssm = _dot(hb, win_ref[:, POOL_WIDTH:POOL_WIDTH + SSM_WIDTH])
        u_ssm_b = u_ssm.astype(_BF16)
        def bbar(i):
            bu_refs[i][staged_rows, :] = _dot(u_ssm_b[:, i * SSM_TILE_CH:(i + 1) * SSM_TILE_CH],
                                              bbc_ref[i])

        def gate(i):
            lo = POOL_WIDTH + SSM_WIDTH + i * gate_cols
            return jax.nn.sigmoid(_dot(hb, win_ref[:, lo:lo + gate_cols]))

        def scan_c(i):
            sl = slice(i * SSM_TILE, (i + 1) * SSM_TILE)
            y, h_re, h_im = _ssm_tile(i, bu_refs[i], hst_ref[0, :, sl], hst_ref[1, :, sl], B, Ts,
                                      are_ref, aim_ref, cre_ref, cim_ref)
            hst_ref[0, :, sl] = h_re
            hst_ref[1, :, sl] = h_im
            return y

        gates, y_tiles = [], []
        bbar(0)
        u_pool = _dot(hb, win_ref[:, 0:POOL_WIDTH])
        bbar(1)
        gates.append(gate(0))
        bbar(2)
        gates.append(gate(1))
        bbar(3)
        y_tiles.append(scan_c(0))
        gates.append(gate(2))
        y_tiles.append(scan_c(1))
        gates.append(gate(3))
        y_tiles.append(scan_c(2))
        ext = jnp.concatenate([pool_ref[...], u_pool], axis=0)
        pool_ref[...] = ext[Ms:Ms + POOL_KEEP * B, :]
        y_pool = _pool_branch(ext, POOL_KEEP, B, Ts, start_pos + c * T + s * Ts,
                              lambda g: poolw_ref[g], pscale_ref[...])
        y_tiles.append(scan_c(3))
        gate_pool = jnp.concatenate(gates[0:SSM_TILES // 2], axis=-1)
        gate_ssm = jnp.concatenate(gates[SSM_TILES // 2:], axis=-1)
        merge_pool = gate_pool * y_pool
        merge = merge_pool + gate_ssm * _glu(y_tiles, u_ssm, dskip_ref[...], wglu_ref[...])
        xo_ref[s * Ms:(s + 1) * Ms, :] = x + _dot(merge.astype(_BF16), wout_ref[...])

    @pl.when(c == n_chunks - 1)
    def _():
        poolst_ref[...] = pool_ref[...]
        hre_o_ref[...] = hst_ref[0]
        him_o_ref[...] = hst_ref[1]


def _mixer_main(x, meta_state, p, l, wb, *, B, T, start_pos, batch_major_in):
    M = B * T
    if batch_major_in:
        n_chunks = x.shape[1] // T
        x_spec = pl.BlockSpec((B, T, D_MODEL), lambda c: (0, c, 0))
        slabs = [pltpu.VMEM((D_SLABS, M // MIXER_SUB_BLOCKS, V7X_LANES), _F32)] * MIXER_SUB_BLOCKS
    else:
        n_chunks = x.shape[0] // M
        x_spec = pl.BlockSpec((M, D_MODEL), lambda c: (c, 0))
        slabs = []
    keep = POOL_KEEP * B
    kern = functools.partial(_mixer_main_kernel, B=B, T=T, start_pos=start_pos,
                             n_chunks=n_chunks, batch_major_in=batch_major_in)
    stacked = [p["n1g"], p["pool_scale"], p["a_re"], p["a_im"], p["bbc"], p["cre"], p["cim"],
               p["dskip"]]
    return pl.pallas_call(
        kern,
        grid=(n_chunks,),
        in_specs=[x_spec] + [_whole(a.shape) for a in meta_state]
        + [_layer(a, l) for a in stacked] + [_whole(a.shape) for a in wb],
        out_specs=[pl.BlockSpec((M, D_MODEL), lambda c: (c, 0)), _whole((keep, POOL_WIDTH)),
                   _whole((B, SSM_FLAT)), _whole((B, SSM_FLAT))],
        out_shape=[jax.ShapeDtypeStruct((n_chunks * M, D_MODEL), _F32),
                   jax.ShapeDtypeStruct((keep, POOL_WIDTH), _F32),
                   jax.ShapeDtypeStruct((B, SSM_FLAT), _F32),
                   jax.ShapeDtypeStruct((B, SSM_FLAT), _F32)],
        scratch_shapes=[pltpu.VMEM((keep, POOL_WIDTH), _F32),
                        pltpu.VMEM((2, B, SSM_FLAT), _F32),
                        ] + [pltpu.VMEM((M // MIXER_SUB_BLOCKS, 2 * SSM_TILE), _F32)] * SSM_TILES
        + slabs,
        compiler_params=_compiler_params(),
        name="mixer_main",
    )(x, *meta_state, *stacked, *wb)


def _mixer_small_kernel(x_ref, pool_ref, sre_ref, sim_ref, n1g_ref, pscale_ref, are_ref, aim_ref,
                        bbc_ref, cre_ref, cim_ref, dskip_ref, win_ref, poolw_ref, wglu_ref,
                        wout_ref, *refs, Bs, Bm, Tm, n_alias):
    (xo_ref, pool_o_ref, sre_o_ref, sim_o_ref, mpool_o_ref, mre_o_ref, mim_o_ref,
     win_o_ref, poolw_o_ref, wglu_o_ref, wout_o_ref) = refs[n_alias:]
    Ms, Mm = Bs, Bm * Tm
    win_o_ref[...] = win_ref[...].astype(_BF16)
    poolw_o_ref[...] = poolw_ref[...].astype(_BF16)
    wglu_o_ref[...] = wglu_ref[...].astype(_BF16)
    wout_o_ref[...] = wout_ref[...].astype(_BF16)

    x = x_ref[...]
    hb = _rmsnorm(x, n1g_ref[...]).astype(_BF16)
    proj = _dot(hb, win_o_ref[...])
    u_pool, u_ssm = proj[:, 0:POOL_WIDTH], proj[:, POOL_WIDTH:POOL_WIDTH + SSM_WIDTH]
    gate_pool = jax.nn.sigmoid(proj[:, POOL_WIDTH + SSM_WIDTH:POOL_WIDTH + SSM_WIDTH + D_MODEL])
    gate_ssm = jax.nn.sigmoid(proj[:, POOL_WIDTH + SSM_WIDTH + D_MODEL:])
    bus = _ssm_inputs(u_ssm.astype(_BF16), bbc_ref)
    poolw = lambda g: poolw_o_ref[g]

    hist = [pool_ref[j] for j in range(POOL_HIST)]
    new_hist = hist[1:] + [u_pool[0:Ms]]
    for j in range(POOL_HIST):
        pool_o_ref[j] = new_hist[j]
    y_pool_s = _pool_branch(jnp.concatenate(hist + [u_pool[0:Ms]], axis=0), POOL_HIST, Bs, 1,
                            PAST_LEN, poolw, pscale_ref[...])
    ext_m = jnp.concatenate([jnp.zeros((POOL_KEEP * Bm, POOL_WIDTH), _F32), u_pool[Ms:]], axis=0)
    y_pool_m = _pool_branch(ext_m, POOL_KEEP, Bm, Tm, 0, poolw, pscale_ref[...])
    mpool_o_ref[...] = ext_m[Mm:Mm + POOL_KEEP * Bm, :]
    y_pool = jnp.concatenate([y_pool_s, y_pool_m], axis=0)

    y_tiles = []
    for i in range(SSM_TILES):
        sl = slice(i * SSM_TILE, (i + 1) * SSM_TILE)
        ys, h_re, h_im = _ssm_tile(i, bus[i][0:Ms], sre_ref[:, sl], sim_ref[:, sl], Bs, 1,
                                   are_ref, aim_ref, cre_ref, cim_ref)
        sre_o_ref[:, sl] = h_re
        sim_o_ref[:, sl] = h_im
        zero = jnp.zeros((Bm, SSM_TILE), _F32)
        ym, h_re, h_im = _ssm_tile(i, bus[i][Ms:], zero, zero, Bm, Tm,
                                   are_ref, aim_ref, cre_ref, cim_ref)
        mre_o_ref[:, sl] = h_re
        mim_o_ref[:, sl] = h_im
        y_tiles.append(jnp.concatenate([ys, ym], axis=0))
    merge = gate_pool * y_pool + gate_ssm * _glu(y_tiles, u_ssm, dskip_ref[...], wglu_o_ref[...])
    xo_ref[...] = x + _dot(merge.astype(_BF16), wout_o_ref[...])


def _mixer_small(x, p, l, prev, *, Bs, Bm, Tm):
    kern = functools.partial(_mixer_small_kernel, Bs=Bs, Bm=Bm, Tm=Tm, n_alias=len(prev))
    stacked = [p["state_pool"], p["state_re"], p["state_im"], p["n1g"], p["pool_scale"], p["a_re"],
               p["a_im"], p["bbc"], p["cre"], p["cim"], p["dskip"], p["w_in"], p["pool_w"],
               p["w_glu"], p["w_out"]]
    sds = jax.ShapeDtypeStruct
    state_shapes = [sds(p["state_pool"].shape, _F32), sds(p["state_re"].shape, _F32),
                    sds(p["state_im"].shape, _F32)]
    plain_shapes = [sds((POOL_KEEP * Bm, POOL_WIDTH), _F32), sds((Bm, SSM_FLAT), _F32),
                    sds((Bm, SSM_FLAT), _F32), sds((D_MODEL, IN_COLS), _BF16),
                    sds((POOL_GROUPS, POOL_GROUP_IN, POOL_GROUP_OUT), _BF16),
                    sds((SSM_WIDTH, 2 * D_MODEL), _BF16), sds((D_MODEL, D_MODEL), _BF16)]
    n_in = 1 + len(stacked)
    return pl.pallas_call(
        kern,
        grid=(1,),
        in_specs=[_whole(x.shape)] + [_layer(a, l) for a in stacked]
        + [pl.BlockSpec(memory_space=pl.ANY)] * len(prev),
        out_specs=[_whole(x.shape)] + [_layer(s, l) for s in state_shapes]
        + [_whole(s.shape) for s in plain_shapes],
        out_shape=[sds(x.shape, _F32)] + state_shapes + plain_shapes,
        input_output_aliases={n_in + k: 1 + k for k in range(len(prev))},
        compiler_params=_compiler_params(),
        name="mixer_small",
    )(x, *stacked, *prev)


GELU_C1 = 0.7978845608028654
GELU_C2 = GELU_C1 * 0.044715


def _gelu2_mul(x, v):
    t = jnp.tanh(x * (GELU_C1 + GELU_C2 * (x * x)))
    return (x * v) * (1.0 + t)


def _conv_tile(g_ext, B, M, convw, convb):
    conv = convb
    for k in range(CONV_W):
        conv = conv + convw[k:k + 1, :] * g_ext[k * B:k * B + M, :]
    return conv


def _ffn_main_kernel(x_ref, chist_ref, n2g_ref, convw_ref, convb_ref, nfg_ref, wgv_ref,
                     wdown_ref, xo_ref, convst_ref,
                     carry_ref, *scratch_refs, B, T, n_chunks, final_norm, batch_major_out):
    M = B * T
    keep = CONV_HIST * B
    c = pl.program_id(0)

    @pl.when(c == 0)
    def _():
        carry_ref[...] = chist_ref[...]

    Ts = T // FFN_SUB_BLOCKS
    Ms = B * Ts
    gv_refs, slab_refs = scratch_refs[:FF_SLOTS], scratch_refs[FF_SLOTS:]
    xs = [x_ref[s * Ms:(s + 1) * Ms, :] for s in range(FFN_SUB_BLOCKS)]
    hbs = [_rmsnorm(x, n2g_ref[...]).astype(_BF16) for x in xs]

    staged_rows = pl.ds(pl.multiple_of(keep + jnp.minimum(c, 0), B), Ms)
    n_tiles = FFN_SUB_BLOCKS * FF_TILES

    def up(k):
        s, j = divmod(k, FF_TILES)
        gv_refs[k % FF_SLOTS][staged_rows, :] = _dot(
            hbs[s], wgv_ref[:, 2 * j * FF_TILE:2 * (j + 1) * FF_TILE])

    for k in range(FF_LOOKAHEAD):
        up(k)
    acc = None
    for k in range(n_tiles):
        s, j = divmod(k, FF_TILES)
        fs = slice(j * FF_TILE, (j + 1) * FF_TILE)
        if k + FF_LOOKAHEAD < n_tiles:
            up(k + FF_LOOKAHEAD)
        slot = gv_refs[k % FF_SLOTS]
        slot[0:keep, 0:FF_TILE] = carry_ref[:, fs]
        conv = _conv_tile(slot.at[:, 0:FF_TILE], B, Ms, convw_ref[:, fs], convb_ref[:, fs])
        carry_ref[:, fs] = slot[Ms:Ms + keep, 0:FF_TILE]
        v = slot[keep:keep + Ms, FF_TILE:]
        contrib = _dot(_gelu2_mul(conv, v).astype(_BF16), wdown_ref[fs, :])
        acc = contrib if j == 0 else acc + contrib
        if j == FF_TILES - 1:
            xo = xs[s] + acc
            if final_norm:
                xo = _rmsnorm(xo, nfg_ref[...])
            if batch_major_out:
                ys_ref = slab_refs[s]
                for q in range(D_SLABS):
                    ys_ref[q] = xo[:, q * V7X_LANES:(q + 1) * V7X_LANES]
                for b in range(B):
                    for q in range(D_SLABS):
                        xo_ref[b, s * Ts:(s + 1) * Ts, q * V7X_LANES:(q + 1) * V7X_LANES] = (
                            ys_ref[q, pl.ds(b, Ts, stride=B), :])
            else:
                xo_ref[s * Ms:(s + 1) * Ms, :] = xo

    @pl.when(c == n_chunks - 1)
    def _():
        convst_ref[...] = carry_ref[...]


def _ffn_main(x, chist, p, l, wb, *, B, T, final_norm, batch_major_out):
    M = B * T
    n_chunks = x.shape[0] // M
    keep = CONV_HIST * B
    kern = functools.partial(_ffn_main_kernel, B=B, T=T, n_chunks=n_chunks,
                             final_norm=final_norm, batch_major_out=batch_major_out)
    stacked = [p["n2g"], p["conv_w"], p["conv_b"]]
    if batch_major_out:
        o_spec = pl.BlockSpec((B, T, D_MODEL), lambda c: (0, c, 0))
        o_shape = jax.ShapeDtypeStruct((B, n_chunks * T, D_MODEL), _F32)
        slabs = [pltpu.VMEM((D_SLABS, M // FFN_SUB_BLOCKS, V7X_LANES), _F32)] * FFN_SUB_BLOCKS
    else:
        o_spec = pl.BlockSpec((M, D_MODEL), lambda c: (c, 0))
        o_shape = jax.ShapeDtypeStruct((n_chunks * M, D_MODEL), _F32)
        slabs = []
    return pl.pallas_call(
        kern,
        grid=(n_chunks,),
        in_specs=[pl.BlockSpec((M, D_MODEL), lambda c: (c, 0)), _whole(chist.shape)]
        + [_layer(a, l) for a in stacked] + [_whole(p["nfg"].shape)]
        + [_whole(a.shape) for a in wb],
        out_specs=[o_spec, _whole((keep, D_FF))],
        out_shape=[o_shape, jax.ShapeDtypeStruct((keep, D_FF), _F32)],
        scratch_shapes=[pltpu.VMEM((keep, D_FF), _F32),
                        ] + [pltpu.VMEM((keep + M // FFN_SUB_BLOCKS, 2 * FF_TILE), _F32)] * FF_SLOTS
        + slabs,
        compiler_params=_compiler_params(),
        name="convffn_main",
    )(x, chist, *stacked, p["nfg"], *wb)


def _ffn_small_kernel(x_ref, conv_ref, n2g_ref, convw_ref, convb_ref, nfg_ref, wg_ref, wv_ref,
                      wdown_ref, *refs, Bs, Bm, Tm, final_norm, n_alias):
    (xo_ref, conv_o_ref, mconv_o_ref, wgv_o_ref, wdown_o_ref,
     hb_ref, acc_ref) = refs[n_alias:]
    Ms, Mm = Bs, Bm * Tm
    jt = pl.program_id(0)

    @pl.when(jt == 0)
    def _():
        hb_ref[...] = _rmsnorm(x_ref[...], n2g_ref[...]).astype(_BF16)
        acc_ref[...] = jnp.zeros(acc_ref.shape, _F32)

    wgv_o_ref[:, 0:FF_TILE] = wg_ref[...].astype(_BF16)
    wgv_o_ref[:, FF_TILE:] = wv_ref[...].astype(_BF16)
    wdown_o_ref[...] = (0.5 * wdown_ref[...]).astype(_BF16)
    hb = hb_ref[...]
    gv = _dot(hb, wgv_o_ref[...])
    g, v = gv[:, 0:FF_TILE], gv[:, FF_TILE:]
    hist = [conv_ref[:, j, :] for j in range(CONV_HIST)]
    new_hist = hist[1:] + [g[0:Ms]]
    for j in range(CONV_HIST):
        conv_o_ref[:, j, :] = new_hist[j]
    g_ext_s = jnp.concatenate(hist + [g[0:Ms]], axis=0)
    g_ext_m = jnp.concatenate([jnp.zeros((CONV_HIST * Bm, FF_TILE), _F32), g[Ms:]], axis=0)
    mconv_o_ref[...] = g_ext_m[Mm:Mm + CONV_HIST * Bm, :]
    conv = jnp.concatenate([_conv_tile(g_ext_s, Bs, Ms, convw_ref[...], convb_ref[...]),
                            _conv_tile(g_ext_m, Bm, Mm, convw_ref[...], convb_ref[...])], axis=0)
    acc_ref[...] += _dot(_gelu2_mul(conv, v).astype(_BF16), wdown_o_ref[...])

    @pl.when(jt == FF_TILES - 1)
    def _():
        xo = x_ref[...] + acc_ref[...]
        if final_norm:
            xo = _rmsnorm(xo, nfg_ref[...])
        xo_ref[...] = xo


def _ffn_small(x, p, l, prev, *, Bs, Bm, Tm, final_norm):
    kern = functools.partial(_ffn_small_kernel, Bs=Bs, Bm=Bm, Tm=Tm, final_norm=final_norm,
                             n_alias=len(prev))
    sds = jax.ShapeDtypeStruct
    tile_of = lambda a, idx: pl.BlockSpec((None,) + idx[0], idx[1])
    n_rows = x.shape[0]
    in_specs = [
        _whole(x.shape),
        pl.BlockSpec((None, Bs, CONV_HIST, FF_TILE), lambda j: (l, 0, 0, j)),
        _layer(p["n2g"], l),
        pl.BlockSpec((None, CONV_W, FF_TILE), lambda j: (l, 0, j)),
        pl.BlockSpec((None, 1, FF_TILE), lambda j: (l, 0, j)),
        _whole(p["nfg"].shape),
        pl.BlockSpec((None, D_MODEL, FF_TILE), lambda j: (l, 0, j)),
        pl.BlockSpec((None, D_MODEL, FF_TILE), lambda j: (l, 0, FF_TILES + j)),
        pl.BlockSpec((None, FF_TILE, D_MODEL), lambda j: (l, j, 0)),
    ] + [pl.BlockSpec(memory_space=pl.ANY)] * len(prev)
    out_shape = [sds(x.shape, _F32), sds(p["state_conv"].shape, _F32),
                 sds((CONV_HIST * Bm, D_FF), _F32), sds((D_MODEL, 2 * D_FF), _BF16),
                 sds((D_FF, D_MODEL), _BF16)]
    out_specs = [
        _whole(x.shape),
        pl.BlockSpec((None, Bs, CONV_HIST, FF_TILE), lambda j: (l, 0, 0, j)),
        pl.BlockSpec((CONV_HIST * Bm, FF_TILE), lambda j: (0, j)),
        pl.BlockSpec((D_MODEL, 2 * FF_TILE), lambda j: (0, j)),
        pl.BlockSpec((FF_TILE, D_MODEL), lambda j: (j, 0)),
    ]
    n_in = 9
    return pl.pallas_call(
        kern,
        grid=(FF_TILES,),
        in_specs=in_specs,
        out_specs=out_specs,
        out_shape=out_shape,
        scratch_shapes=[pltpu.VMEM((n_rows, D_MODEL), _BF16), pltpu.VMEM((n_rows, D_MODEL), _F32)],
        input_output_aliases={n_in + k: 1 + k for k in range(len(prev))},
        compiler_params=_compiler_params(),
        name="convffn_small",
    )(x, p["state_conv"], p["n2g"], p["conv_w"], p["conv_b"], p["nfg"], p["w_up"], p["w_up"],
      p["w_down"], *prev)


def kernel(x_prompt, x_sample, state_pool, state_ssm_re, state_ssm_im, state_conv, meta_tokens,
           norm1_g, w_in, pool_w, pool_scale, ssm_A_re, ssm_A_im, ssm_log_dt, ssm_B_re, ssm_B_im,
           ssm_C_re, ssm_C_im, ssm_D, w_glu, w_out, norm2_g, w_up, conv_w, conv_b, w_down, norm_f_g):
    depth = w_in.shape[0]
    bp, seq, _ = x_prompt.shape
    bs = x_sample.shape[0]
    assert seq % MIXER_CHUNK_STEPS == 0 and seq % FFN_CHUNK_STEPS == 0
    ab_re, ab_im, bbc, cre, cim = _discretise(ssm_A_re, ssm_A_im, ssm_log_dt, ssm_B_re, ssm_B_im,
                                              ssm_C_re, ssm_C_im)
    row = lambda a: a[:, None, :]
    p = dict(state_pool=state_pool.transpose(0, 2, 1, 3), state_conv=state_conv,
             state_re=state_ssm_re.reshape(depth, bs, SSM_FLAT),
             state_im=state_ssm_im.reshape(depth, bs, SSM_FLAT),
             n1g=row(norm1_g), pool_scale=row(pool_scale), a_re=ab_re, a_im=ab_im, bbc=bbc,
             cre=cre, cim=cim, dskip=row(ssm_D), w_in=w_in, pool_w=pool_w, w_glu=w_glu,
             w_out=w_out, n2g=row(norm2_g), conv_w=conv_w, conv_b=row(conv_b), w_up=w_up,
             w_down=w_down, nfg=norm_f_g[None])

    xs = jnp.concatenate([x_sample.reshape(bs, D_MODEL), jnp.repeat(meta_tokens, bp, axis=0)], axis=0)
    small = dict(Bs=bs, Bm=bp, Tm=N_META)
    mix_prev, ffn_prev, meta_state, mix_w, ffn_w = [], [], [], [], []
    for l in range(depth):
        xs, pool_s, re_s, im_s, mp, mr, mi, *wb = _mixer_small(xs, p, l, mix_prev, **small)
        mix_prev = [pool_s, re_s, im_s]
        mix_w.append(wb)
        xs, conv_s, mc, *wb = _ffn_small(xs, p, l, ffn_prev, final_norm=(l == depth - 1), **small)
        ffn_prev = [conv_s]
        ffn_w.append(wb)
        meta_state.append(((mp, mr, mi), mc))
    y_sample = xs[0:bs].reshape(bs, 1, D_MODEL)

    xp = x_prompt
    pool_p, re_p, im_p, conv_p = [], [], [], []
    for l in range(depth):
        mix_state, mc = meta_state[l]
        xp, pp, hr, hi = _mixer_main(xp, mix_state, p, l, mix_w[l], start_pos=N_META,
                                     batch_major_in=(l == 0), B=bp, T=MIXER_CHUNK_STEPS)
        last = l == depth - 1
        xp, cv = _ffn_main(xp, mc, p, l, ffn_w[l], final_norm=last, batch_major_out=last, B=bp,
                           T=FFN_CHUNK_STEPS)
        pool_p.append(pp)
        re_p.append(hr)
        im_p.append(hi)
        conv_p.append(cv)
    st = jnp.stack
    pool_p = st(pool_p).reshape(depth, POOL_KEEP, bp, POOL_WIDTH)[:, 1:].transpose(0, 2, 1, 3)
    conv_p = st(conv_p).reshape(depth, CONV_HIST, bp, D_FF).transpose(0, 2, 1, 3)
    group = lambda h: h.reshape(h.shape[:-1] + (SSM_GROUPS, SSM_STATE))
    return (xp, y_sample, pool_p, group(st(re_p)), group(st(im_p)), conv_p,
            pool_s.transpose(0, 2, 1, 3), group(re_s), group(im_s), conv_s)
```

```python
import functools

import jax
import jax.numpy as jnp
from jax import lax
from jax.experimental import pallas as pl
from jax.experimental.pallas import tpu as pltpu

D_MODEL = 1024
N_META = 16
POOL_WIDTH = 512
POOL_WINDOWS = (2, 4, 8, 16)
POOL_GROUPS = len(POOL_WINDOWS)
POOL_GROUP_IN = 128
POOL_GROUP_OUT = 256
POOL_HIST = 15
POOL_KEEP = 16
SSM_WIDTH = 512
SSM_GROUP_CH = 16
SSM_GROUPS = 32
SSM_STATE = 64
SSM_FLAT = SSM_GROUPS * SSM_STATE
SSM_TILE = 512
SSM_TILES = SSM_FLAT // SSM_TILE
SSM_TILE_GROUPS = SSM_TILE // SSM_STATE
SSM_TILE_CH = SSM_TILE_GROUPS * SSM_GROUP_CH
IN_COLS = POOL_WIDTH + SSM_WIDTH + 2 * D_MODEL
D_FF = 2816
FF_TILE = 256
FF_TILES = D_FF // FF_TILE
FF_LOOKAHEAD = 2
FF_SLOTS = FF_LOOKAHEAD + 1
CONV_W = 3
CONV_HIST = CONV_W - 1
RMS_EPS = 1e-6
PAST_LEN = 16384

V7X_LANES = 128
V7X_SUBLANES = 8
D_SLABS = D_MODEL // V7X_LANES
V7X_VMEM_LIMIT_BYTES = 56 * 1024 * 1024
MIXER_CHUNK_STEPS = 64
MIXER_SUB_BLOCKS = 1
FFN_CHUNK_STEPS = 64
FFN_SUB_BLOCKS = 2

_F32 = jnp.float32
_BF16 = jnp.bfloat16


def _rmsnorm(x, g):
    return x * lax.rsqrt(jnp.mean(x * x, axis=-1, keepdims=True) + RMS_EPS) * g


def _dot(a, b):
    return jnp.dot(a, b, preferred_element_type=_F32)


def _whole(shape):
    nd = len(shape)
    return pl.BlockSpec(shape, lambda *_: (0,) * nd, pipeline_mode=pl.Buffered(1))


def _layer(arr, l):
    nd = arr.ndim - 1
    return pl.BlockSpec((None,) + arr.shape[1:], lambda *_: (l,) + (0,) * nd,
                        pipeline_mode=pl.Buffered(1))


def _compiler_params():
    return pltpu.CompilerParams(dimension_semantics=("arbitrary",),
                                vmem_limit_bytes=V7X_VMEM_LIMIT_BYTES)


def _disc_kernel(are_ref, aim_ref, ldt_ref, bre_ref, bim_ref, ctre_ref, ctim_ref,
                 abre_ref, abim_ref, bbc_ref, cre_ref, cim_ref):
    lam_re = are_ref[...]
    lam_im = aim_ref[...]
    dt = jnp.exp(ldt_ref[...])
    mag = jnp.exp(lam_re * dt)
    ab_re = mag * jnp.cos(lam_im * dt)
    ab_im = mag * jnp.sin(lam_im * dt)
    den = lam_re * lam_re + lam_im * lam_im
    nr = ab_re - 1.0
    ni = ab_im
    q_re = (nr * lam_re + ni * lam_im) / den
    q_im = (ni * lam_re - nr * lam_im) / den
    b_re = bre_ref[...]
    b_im = bim_ref[...]
    abre_ref[...] = ab_re
    abim_ref[...] = ab_im
    bb_re = (q_re * b_re - q_im * b_im).astype(_BF16)
    bb_im = (q_re * b_im + q_im * b_re).astype(_BF16)
    ct_re = ctre_ref[...].astype(_BF16)
    ct_im = ctim_ref[...].astype(_BF16)
    bbc_ref[...] = jnp.zeros(bbc_ref.shape, _BF16)
    cre_ref[...] = jnp.zeros(cre_ref.shape, _BF16)
    cim_ref[...] = jnp.zeros(cim_ref.shape, _BF16)
    for i in range(SSM_TILES):
        for gl in range(SSM_TILE_GROUPS):
            g = i * SSM_TILE_GROUPS + gl
            ch = slice(gl * SSM_GROUP_CH, (gl + 1) * SSM_GROUP_CH)
            st = slice(gl * SSM_STATE, (gl + 1) * SSM_STATE)
            st_im = slice(SSM_TILE + gl * SSM_STATE, SSM_TILE + (gl + 1) * SSM_STATE)
            src_st = slice(g * SSM_STATE, (g + 1) * SSM_STATE)
            src_ch = slice(g * SSM_GROUP_CH, (g + 1) * SSM_GROUP_CH)
            bbc_ref[i, ch, st] = bb_re[:, src_st]
            bbc_ref[i, ch, st_im] = bb_im[:, src_st]
            cre_ref[i, st, ch] = ct_re[:, src_ch]
            cim_ref[i, st, ch] = ct_im[:, src_ch]


def _discretise(a_re, a_im, log_dt, b_re, b_im, c_re, c_im):
    depth = a_re.shape[0]
    flat = lambda a: a.reshape(depth, 1, SSM_FLAT)
    ldt = jnp.broadcast_to(log_dt[:, :, None], (depth, SSM_GROUPS, SSM_STATE))
    bt = lambda b: b.transpose(0, 3, 1, 2).reshape(depth, SSM_GROUP_CH, SSM_FLAT)
    ct = lambda c: c.transpose(0, 3, 1, 2).reshape(depth, SSM_STATE, SSM_WIDTH)
    ins = [flat(a_re), flat(a_im), flat(ldt), bt(b_re), bt(b_im), ct(c_re), ct(c_im)]
    out_shape = [jax.ShapeDtypeStruct((depth, 1, SSM_FLAT), _F32)] * 2 + [
        jax.ShapeDtypeStruct((depth, SSM_TILES, SSM_TILE_CH, 2 * SSM_TILE), _BF16),
        jax.ShapeDtypeStruct((depth, SSM_TILES, SSM_TILE, SSM_TILE_CH), _BF16),
        jax.ShapeDtypeStruct((depth, SSM_TILES, SSM_TILE, SSM_TILE_CH), _BF16)]
    per_layer = lambda s: pl.BlockSpec((None,) + s.shape[1:],
                                       lambda l: (l,) + (0,) * (len(s.shape) - 1))
    return pl.pallas_call(
        _disc_kernel,
        grid=(depth,),
        in_specs=[per_layer(a) for a in ins],
        out_specs=[per_layer(s) for s in out_shape],
        out_shape=out_shape,
        name="s5_discretise",
    )(*ins)


def _pool_branch(ext, n_hist, B, T, pos0, poolw, pscale):
    M = B * T
    base = n_hist * B
    pos = pos0 + lax.broadcasted_iota(jnp.int32, (M, 1), 0) // B
    tiles = []
    for g, w in enumerate(POOL_WINDOWS):
        ls = slice(g * POOL_GROUP_IN, (g + 1) * POOL_GROUP_IN)
        tok = ext[base:base + M, ls]
        win_sum = tok
        for k in range(1, w):
            win_sum = win_sum + ext[base - k * B:base - k * B + M, ls]
        cnt = jnp.minimum(w, pos + 1).astype(_F32)
        d = win_sum / cnt - tok
        tiles.append(_dot(d.astype(_BF16), poolw(g)))
    return jnp.concatenate(tiles, axis=-1) * pscale


def _ssm_inputs(u_ssm_b, bbc_ref):
    return [_dot(u_ssm_b[:, i * SSM_TILE_CH:(i + 1) * SSM_TILE_CH], bbc_ref[i])
            for i in range(SSM_TILES)]


def _ssm_tile(i, bu, h_re, h_im, B, T, are_ref, aim_ref, cre_ref, cim_ref):
    sl = slice(i * SSM_TILE, (i + 1) * SSM_TILE)
    a_re = jnp.broadcast_to(are_ref[:, sl], (B, SSM_TILE))
    a_im = jnp.broadcast_to(aim_ref[:, sl], (B, SSM_TILE))
    hs_re, hs_im = [], []
    for t in range(T):
        rows = slice(t * B, (t + 1) * B)
        n_re = a_re * h_re - a_im * h_im + bu[rows, 0:SSM_TILE]
        n_im = a_re * h_im + a_im * h_re + bu[rows, SSM_TILE:2 * SSM_TILE]
        h_re, h_im = n_re, n_im
        hs_re.append(h_re)
        hs_im.append(h_im)
    y = (_dot(jnp.concatenate(hs_re, axis=0).astype(_BF16), cre_ref[i])
         - _dot(jnp.concatenate(hs_im, axis=0).astype(_BF16), cim_ref[i]))
    return y, h_re, h_im


def _glu(y_tiles, u_ssm, dskip, wglu):
    y_s = jnp.concatenate(y_tiles, axis=-1) + dskip * u_ssm
    zz = _dot(jax.nn.gelu(y_s).astype(_BF16), wglu)
    return zz[:, 0:D_MODEL] * jax.nn.sigmoid(zz[:, D_MODEL:])


def _mixer_main_kernel(x_ref, hist_ref, h0re_ref, h0im_ref, n1g_ref, pscale_ref, are_ref, aim_ref,
                       bbc_ref, cre_ref, cim_ref, dskip_ref, win_ref, poolw_ref, wglu_ref, wout_ref,
                       xo_ref, poolst_ref, hre_o_ref, him_o_ref,
                       pool_ref, hst_ref, *scratch_refs, B, T, start_pos, n_chunks,
                       batch_major_in):
    M = B * T
    c = pl.program_id(0)

    @pl.when(c == 0)
    def _():
        pool_ref[...] = hist_ref[...]
        hst_ref[0] = h0re_ref[...]
        hst_ref[1] = h0im_ref[...]

    Ts = T // MIXER_SUB_BLOCKS
    Ms = B * Ts
    bu_refs, slab_refs = scratch_refs[:SSM_TILES], scratch_refs[SSM_TILES:]
    staged_rows = pl.ds(pl.multiple_of(jnp.minimum(c, 0), B), Ms)
    gate_cols = 2 * D_MODEL // SSM_TILES

    for s in range(MIXER_SUB_BLOCKS):
        if batch_major_in:
            xs_ref = slab_refs[s]
            for t0 in range(0, Ts, V7X_SUBLANES):
                for b in range(B):
                    for q in range(D_SLABS):
                        xs_ref[q, pl.ds(t0 * B + b, V7X_SUBLANES, stride=B), :] = x_ref[
                            b, s * Ts + t0:s * Ts + t0 + V7X_SUBLANES,
                            q * V7X_LANES:(q + 1) * V7X_LANES]
            x = jnp.concatenate([xs_ref[q] for q in range(D_SLABS)], axis=-1)
        else:
            x = x_ref[s * Ms:(s + 1) * Ms, :]
        hb = _rmsnorm(x, n1g_ref[...]).astype(_BF16)

        u_ssm = _dot(hb, win_ref[:, POOL_WIDTH:POOL_WIDTH + SSM_WIDTH])
        u_ssm_b = u_ssm.astype(_BF16)
        def bbar(i):
            bu_refs[i][staged_rows, :] = _dot(u_ssm_b[:, i * SSM_TILE_CH:(i + 1) * SSM_TILE_CH],
                                              bbc_ref[i])

        def gate(i):
            lo = POOL_WIDTH + SSM_WIDTH + i * gate_cols
            return jax.nn.sigmoid(_dot(hb, win_ref[:, lo:lo + gate_cols]))

        def scan_c(i):
            sl = slice(i * SSM_TILE, (i + 1) * SSM_TILE)
            y, h_re, h_im = _ssm_tile(i, bu_refs[i], hst_ref[0, :, sl], hst_ref[1, :, sl], B, Ts,
                                      are_ref, aim_ref, cre_ref, cim_ref)
            hst_ref[0, :, sl] = h_re
            hst_ref[1, :, sl] = h_im
            return y

        gates, y_tiles = [], []
        bbar(0)
        u_pool = _dot(hb, win_ref[:, 0:POOL_WIDTH])
        bbar(1)
        gates.append(gate(0))
        bbar(2)
        gates.append(gate(1))
        bbar(3)
        y_tiles.append(scan_c(0))
        gates.append(gate(2))
        y_tiles.append(scan_c(1))
        gates.append(gate(3))
        y_tiles.append(scan_c(2))
        ext = jnp.concatenate([pool_ref[...], u_pool], axis=0)
        pool_ref[...] = ext[Ms:Ms + POOL_KEEP * B, :]
        y_pool = _pool_branch(ext, POOL_KEEP, B, Ts, start_pos + c * T + s * Ts,
                              lambda g: poolw_ref[g], pscale_ref[...])
        y_tiles.append(scan_c(3))
        gate_pool = jnp.concatenate(gates[0:SSM_TILES // 2], axis=-1)
        gate_ssm = jnp.concatenate(gates[SSM_TILES // 2:], axis=-1)
        merge_pool = gate_pool * y_pool
        merge = merge_pool + gate_ssm * _glu(y_tiles, u_ssm, dskip_ref[...], wglu_ref[...])
        xo_ref[s * Ms:(s + 1) * Ms, :] = x + _dot(merge.astype(_BF16), wout_ref[...])

    @pl.when(c == n_chunks - 1)
    def _():
        poolst_ref[...] = pool_ref[...]
        hre_o_ref[...] = hst_ref[0]
        him_o_ref[...] = hst_ref[1]


def _mixer_main(x, meta_state, p, l, wb, *, B, T, start_pos, batch_major_in):
    M = B * T
    if batch_major_in:
        n_chunks = x.shape[1] // T
        x_spec = pl.BlockSpec((B, T, D_MODEL), lambda c: (0, c, 0))
        slabs = [pltpu.VMEM((D_SLABS, M // MIXER_SUB_BLOCKS, V7X_LANES), _F32)] * MIXER_SUB_BLOCKS
    else:
        n_chunks = x.shape[0] // M
        x_spec = pl.BlockSpec((M, D_MODEL), lambda c: (c, 0))
        slabs = []
    keep = POOL_KEEP * B
    kern = functools.partial(_mixer_main_kernel, B=B, T=T, start_pos=start_pos,
                             n_chunks=n_chunks, batch_major_in=batch_major_in)
    stacked = [p["n1g"], p["pool_scale"], p["a_re"], p["a_im"], p["bbc"], p["cre"], p["cim"],
               p["dskip"]]
    return pl.pallas_call(
        kern,
        grid=(n_chunks,),
        in_specs=[x_spec] + [_whole(a.shape) for a in meta_state]
        + [_layer(a, l) for a in stacked] + [_whole(a.shape) for a in wb],
        out_specs=[pl.BlockSpec((M, D_MODEL), lambda c: (c, 0)), _whole((keep, POOL_WIDTH)),
                   _whole((B, SSM_FLAT)), _whole((B, SSM_FLAT))],
        out_shape=[jax.ShapeDtypeStruct((n_chunks * M, D_MODEL), _F32),
                   jax.ShapeDtypeStruct((keep, POOL_WIDTH), _F32),
                   jax.ShapeDtypeStruct((B, SSM_FLAT), _F32),
                   jax.ShapeDtypeStruct((B, SSM_FLAT), _F32)],
        scratch_shapes=[pltpu.VMEM((keep, POOL_WIDTH), _F32),
                        pltpu.VMEM((2, B, SSM_FLAT), _F32),
                        ] + [pltpu.VMEM((M // MIXER_SUB_BLOCKS, 2 * SSM_TILE), _F32)] * SSM_TILES
        + slabs,
        compiler_params=_compiler_params(),
        name="mixer_main",
    )(x, *meta_state, *stacked, *wb)


def _mixer_small_kernel(x_ref, pool_ref, sre_ref, sim_ref, n1g_ref, pscale_ref, are_ref, aim_ref,
                        bbc_ref, cre_ref, cim_ref, dskip_ref, win_ref, poolw_ref, wglu_ref,
                        wout_ref, *refs, Bs, Bm, Tm, n_alias):
    (xo_ref, pool_o_ref, sre_o_ref, sim_o_ref, mpool_o_ref, mre_o_ref, mim_o_ref,
     win_o_ref, poolw_o_ref, wglu_o_ref, wout_o_ref) = refs[n_alias:]
    Ms, Mm = Bs, Bm * Tm
    win_o_ref[...] = win_ref[...].astype(_BF16)
    poolw_o_ref[...] = poolw_ref[...].astype(_BF16)
    wglu_o_ref[...] = wglu_ref[...].astype(_BF16)
    wout_o_ref[...] = wout_ref[...].astype(_BF16)

    x = x_ref[...]
    hb = _rmsnorm(x, n1g_ref[...]).astype(_BF16)
    proj = _dot(hb, win_o_ref[...])
    u_pool, u_ssm = proj[:, 0:POOL_WIDTH], proj[:, POOL_WIDTH:POOL_WIDTH + SSM_WIDTH]
    gate_pool = jax.nn.sigmoid(proj[:, POOL_WIDTH + SSM_WIDTH:POOL_WIDTH + SSM_WIDTH + D_MODEL])
    gate_ssm = jax.nn.sigmoid(proj[:, POOL_WIDTH + SSM_WIDTH + D_MODEL:])
    bus = _ssm_inputs(u_ssm.astype(_BF16), bbc_ref)
    poolw = lambda g: poolw_o_ref[g]

    hist = [pool_ref[j] for j in range(POOL_HIST)]
    new_hist = hist[1:] + [u_pool[0:Ms]]
    for j in range(POOL_HIST):
        pool_o_ref[j] = new_hist[j]
    y_pool_s = _pool_branch(jnp.concatenate(hist + [u_pool[0:Ms]], axis=0), POOL_HIST, Bs, 1,
                            PAST_LEN, poolw, pscale_ref[...])
    ext_m = jnp.concatenate([jnp.zeros((POOL_KEEP * Bm, POOL_WIDTH), _F32), u_pool[Ms:]], axis=0)
    y_pool_m = _pool_branch(ext_m, POOL_KEEP, Bm, Tm, 0, poolw, pscale_ref[...])
    mpool_o_ref[...] = ext_m[Mm:Mm + POOL_KEEP * Bm, :]
    y_pool = jnp.concatenate([y_pool_s, y_pool_m], axis=0)

    y_tiles = []
    for i in range(SSM_TILES):
        sl = slice(i * SSM_TILE, (i + 1) * SSM_TILE)
        ys, h_re, h_im = _ssm_tile(i, bus[i][0:Ms], sre_ref[:, sl], sim_ref[:, sl], Bs, 1,
                                   are_ref, aim_ref, cre_ref, cim_ref)
        sre_o_ref[:, sl] = h_re
        sim_o_ref[:, sl] = h_im
        zero = jnp.zeros((Bm, SSM_TILE), _F32)
        ym, h_re, h_im = _ssm_tile(i, bus[i][Ms:], zero, zero, Bm, Tm,
                                   are_ref, aim_ref, cre_ref, cim_ref)
        mre_o_ref[:, sl] = h_re
        mim_o_ref[:, sl] = h_im
        y_tiles.append(jnp.concatenate([ys, ym], axis=0))
    merge = gate_pool * y_pool + gate_ssm * _glu(y_tiles, u_ssm, dskip_ref[...], wglu_o_ref[...])
    xo_ref[...] = x + _dot(merge.astype(_BF16), wout_o_ref[...])


def _mixer_small(x, p, l, prev, *, Bs, Bm, Tm):
    kern = functools.partial(_mixer_small_kernel, Bs=Bs, Bm=Bm, Tm=Tm, n_alias=len(prev))
    stacked = [p["state_pool"], p["state_re"], p["state_im"], p["n1g"], p["pool_scale"], p["a_re"],
               p["a_im"], p["bbc"], p["cre"], p["cim"], p["dskip"], p["w_in"], p["pool_w"],
               p["w_glu"], p["w_out"]]
    sds = jax.ShapeDtypeStruct
    state_shapes = [sds(p["state_pool"].shape, _F32), sds(p["state_re"].shape, _F32),
                    sds(p["state_im"].shape, _F32)]
    plain_shapes = [sds((POOL_KEEP * Bm, POOL_WIDTH), _F32), sds((Bm, SSM_FLAT), _F32),
                    sds((Bm, SSM_FLAT), _F32), sds((D_MODEL, IN_COLS), _BF16),
                    sds((POOL_GROUPS, POOL_GROUP_IN, POOL_GROUP_OUT), _BF16),
                    sds((SSM_WIDTH, 2 * D_MODEL), _BF16), sds((D_MODEL, D_MODEL), _BF16)]
    n_in = 1 + len(stacked)
    return pl.pallas_call(
        kern,
        grid=(1,),
        in_specs=[_whole(x.shape)] + [_layer(a, l) for a in stacked]
        + [pl.BlockSpec(memory_space=pl.ANY)] * len(prev),
        out_specs=[_whole(x.shape)] + [_layer(s, l) for s in state_shapes]
        + [_whole(s.shape) for s in plain_shapes],
        out_shape=[sds(x.shape, _F32)] + state_shapes + plain_shapes,
        input_output_aliases={n_in + k: 1 + k for k in range(len(prev))},
        compiler_params=_compiler_params(),
        name="mixer_small",
    )(x, *stacked, *prev)


GELU_C1 = 0.7978845608028654
GELU_C2 = GELU_C1 * 0.044715


def _gelu2_mul(x, v):
    t = jnp.tanh(x * (GELU_C1 + GELU_C2 * (x * x)))
    return (x * v) * (1.0 + t)


def _conv_tile(g_ext, B, M, convw, convb):
    conv = convb
    for k in range(CONV_W):
        conv = conv + convw[k:k + 1, :] * g_ext[k * B:k * B + M, :]
    return conv


def _ffn_main_kernel(x_ref, chist_ref, n2g_ref, convw_ref, convb_ref, nfg_ref, wgv_ref,
                     wdown_ref, xo_ref, convst_ref,
                     carry_ref, *scratch_refs, B, T, n_chunks, final_norm, batch_major_out):
    M = B * T
    keep = CONV_HIST * B
    c = pl.program_id(0)

    @pl.when(c == 0)
    def _():
        carry_ref[...] = chist_ref[...]

    Ts = T // FFN_SUB_BLOCKS
    Ms = B * Ts
    gv_refs, slab_refs = scratch_refs[:FF_SLOTS], scratch_refs[FF_SLOTS:]
    xs = [x_ref[s * Ms:(s + 1) * Ms, :] for s in range(FFN_SUB_BLOCKS)]
    hbs = [_rmsnorm(x, n2g_ref[...]).astype(_BF16) for x in xs]

    staged_rows = pl.ds(pl.multiple_of(keep + jnp.minimum(c, 0), B), Ms)
    n_tiles = FFN_SUB_BLOCKS * FF_TILES

    def up(k):
        s, j = divmod(k, FF_TILES)
        gv_refs[k % FF_SLOTS][staged_rows, :] = _dot(
            hbs[s], wgv_ref[:, 2 * j * FF_TILE:2 * (j + 1) * FF_TILE])

    for k in range(FF_LOOKAHEAD):
        up(k)
    acc = None
    for k in range(n_tiles):
        s, j = divmod(k, FF_TILES)
        fs = slice(j * FF_TILE, (j + 1) * FF_TILE)
        if k + FF_LOOKAHEAD < n_tiles:
            up(k + FF_LOOKAHEAD)
        slot = gv_refs[k % FF_SLOTS]
        slot[0:keep, 0:FF_TILE] = carry_ref[:, fs]
        conv = _conv_tile(slot.at[:, 0:FF_TILE], B, Ms, convw_ref[:, fs], convb_ref[:, fs])
        carry_ref[:, fs] = slot[Ms:Ms + keep, 0:FF_TILE]
        v = slot[keep:keep + Ms, FF_TILE:]
        contrib = _dot(_gelu2_mul(conv, v).astype(_BF16), wdown_ref[fs, :])
        acc = contrib if j == 0 else acc + contrib
        if j == FF_TILES - 1:
            xo = xs[s] + acc
            if final_norm:
                xo = _rmsnorm(xo, nfg_ref[...])
            if batch_major_out:
                ys_ref = slab_refs[s]
                for q in range(D_SLABS):
                    ys_ref[q] = xo[:, q * V7X_LANES:(q + 1) * V7X_LANES]
                for b in range(B):
                    for q in range(D_SLABS):
                        xo_ref[b, s * Ts:(s + 1) * Ts, q * V7X_LANES:(q + 1) * V7X_LANES] = (
                            ys_ref[q, pl.ds(b, Ts, stride=B), :])
            else:
                xo_ref[s * Ms:(s + 1) * Ms, :] = xo

    @pl.when(c == n_chunks - 1)
    def _():
        convst_ref[...] = carry_ref[...]


def _ffn_main(x, chist, p, l, wb, *, B, T, final_norm, batch_major_out):
    M = B * T
    n_chunks = x.shape[0] // M
    keep = CONV_HIST * B
    kern = functools.partial(_ffn_main_kernel, B=B, T=T, n_chunks=n_chunks,
                             final_norm=final_norm, batch_major_out=batch_major_out)
    stacked = [p["n2g"], p["conv_w"], p["conv_b"]]
    if batch_major_out:
        o_spec = pl.BlockSpec((B, T, D_MODEL), lambda c: (0, c, 0))
        o_shape = jax.ShapeDtypeStruct((B, n_chunks * T, D_MODEL), _F32)
        slabs = [pltpu.VMEM((D_SLABS, M // FFN_SUB_BLOCKS, V7X_LANES), _F32)] * FFN_SUB_BLOCKS
    else:
        o_spec = pl.BlockSpec((M, D_MODEL), lambda c: (c, 0))
        o_shape = jax.ShapeDtypeStruct((n_chunks * M, D_MODEL), _F32)
        slabs = []
    return pl.pallas_call(
        kern,
        grid=(n_chunks,),
        in_specs=[pl.BlockSpec((M, D_MODEL), lambda c: (c, 0)), _whole(chist.shape)]
        + [_layer(a, l) for a in stacked] + [_whole(p["nfg"].shape)]
        + [_whole(a.shape) for a in wb],
        out_specs=[o_spec, _whole((keep, D_FF))],
        out_shape=[o_shape, jax.ShapeDtypeStruct((keep, D_FF), _F32)],
        scratch_shapes=[pltpu.VMEM((keep, D_FF), _F32),
                        ] + [pltpu.VMEM((keep + M // FFN_SUB_BLOCKS, 2 * FF_TILE), _F32)] * FF_SLOTS
        + slabs,
        compiler_params=_compiler_params(),
        name="convffn_main",
    )(x, chist, *stacked, p["nfg"], *wb)


def _ffn_small_kernel(x_ref, conv_ref, n2g_ref, convw_ref, convb_ref, nfg_ref, wg_ref, wv_ref,
                      wdown_ref, *refs, Bs, Bm, Tm, final_norm, n_alias):
    (xo_ref, conv_o_ref, mconv_o_ref, wgv_o_ref, wdown_o_ref,
     hb_ref, acc_ref) = refs[n_alias:]
    Ms, Mm = Bs, Bm * Tm
    jt = pl.program_id(0)

    @pl.when(jt == 0)
    def _():
        hb_ref[...] = _rmsnorm(x_ref[...], n2g_ref[...]).astype(_BF16)
        acc_ref[...] = jnp.zeros(acc_ref.shape, _F32)

    wgv_o_ref[:, 0:FF_TILE] = wg_ref[...].astype(_BF16)
    wgv_o_ref[:, FF_TILE:] = wv_ref[...].astype(_BF16)
    wdown_o_ref[...] = (0.5 * wdown_ref[...]).astype(_BF16)
    hb = hb_ref[...]
    gv = _dot(hb, wgv_o_ref[...])
    g, v = gv[:, 0:FF_TILE], gv[:, FF_TILE:]
    hist = [conv_ref[:, j, :] for j in range(CONV_HIST)]
    new_hist = hist[1:] + [g[0:Ms]]
    for j in range(CONV_HIST):
        conv_o_ref[:, j, :] = new_hist[j]
    g_ext_s = jnp.concatenate(hist + [g[0:Ms]], axis=0)
    g_ext_m = jnp.concatenate([jnp.zeros((CONV_HIST * Bm, FF_TILE), _F32), g[Ms:]], axis=0)
    mconv_o_ref[...] = g_ext_m[Mm:Mm + CONV_HIST * Bm, :]
    conv = jnp.concatenate([_conv_tile(g_ext_s, Bs, Ms, convw_ref[...], convb_ref[...]),
                            _conv_tile(g_ext_m, Bm, Mm, convw_ref[...], convb_ref[...])], axis=0)
    acc_ref[...] += _dot(_gelu2_mul(conv, v).astype(_BF16), wdown_o_ref[...])

    @pl.when(jt == FF_TILES - 1)
    def _():
        xo = x_ref[...] + acc_ref[...]
        if final_norm:
            xo = _rmsnorm(xo, nfg_ref[...])
        xo_ref[...] = xo


def _ffn_small(x, p, l, prev, *, Bs, Bm, Tm, final_norm):
    kern = functools.partial(_ffn_small_kernel, Bs=Bs, Bm=Bm, Tm=Tm, final_norm=final_norm,
                             n_alias=len(prev))
    sds = jax.ShapeDtypeStruct
    tile_of = lambda a, idx: pl.BlockSpec((None,) + idx[0], idx[1])
    n_rows = x.shape[0]
    in_specs = [
        _whole(x.shape),
        pl.BlockSpec((None, Bs, CONV_HIST, FF_TILE), lambda j: (l, 0, 0, j)),
        _layer(p["n2g"], l),
        pl.BlockSpec((None, CONV_W, FF_TILE), lambda j: (l, 0, j)),
        pl.BlockSpec((None, 1, FF_TILE), lambda j: (l, 0, j)),
        _whole(p["nfg"].shape),
        pl.BlockSpec((None, D_MODEL, FF_TILE), lambda j: (l, 0, j)),
        pl.BlockSpec((None, D_MODEL, FF_TILE), lambda j: (l, 0, FF_TILES + j)),
        pl.BlockSpec((None, FF_TILE, D_MODEL), lambda j: (l, j, 0)),
    ] + [pl.BlockSpec(memory_space=pl.ANY)] * len(prev)
    out_shape = [sds(x.shape, _F32), sds(p["state_conv"].shape, _F32),
                 sds((CONV_HIST * Bm, D_FF), _F32), sds((D_MODEL, 2 * D_FF), _BF16),
                 sds((D_FF, D_MODEL), _BF16)]
    out_specs = [
        _whole(x.shape),
        pl.BlockSpec((None, Bs, CONV_HIST, FF_TILE), lambda j: (l, 0, 0, j)),
        pl.BlockSpec((CONV_HIST * Bm, FF_TILE), lambda j: (0, j)),
        pl.BlockSpec((D_MODEL, 2 * FF_TILE), lambda j: (0, j)),
        pl.BlockSpec((FF_TILE, D_MODEL), lambda j: (j, 0)),
    ]
    n_in = 9
    return pl.pallas_call(
        kern,
        grid=(FF_TILES,),
        in_specs=in_specs,
        out_specs=out_specs,
        out_shape=out_shape,
        scratch_shapes=[pltpu.VMEM((n_rows, D_MODEL), _BF16), pltpu.VMEM((n_rows, D_MODEL), _F32)],
        input_output_aliases={n_in + k: 1 + k for k in range(len(prev))},
        compiler_params=_compiler_params(),
        name="convffn_small",
    )(x, p["state_conv"], p["n2g"], p["conv_w"], p["conv_b"], p["nfg"], p["w_up"], p["w_up"],
      p["w_down"], *prev)


def kernel(x_prompt, x_sample, state_pool, state_ssm_re, state_ssm_im, state_conv, meta_tokens,
           norm1_g, w_in, pool_w, pool_scale, ssm_A_re, ssm_A_im, ssm_log_dt, ssm_B_re, ssm_B_im,
           ssm_C_re, ssm_C_im, ssm_D, w_glu, w_out, norm2_g, w_up, conv_w, conv_b, w_down, norm_f_g):
    depth = w_in.shape[0]
    bp, seq, _ = x_prompt.shape
    bs = x_sample.shape[0]
    assert seq % MIXER_CHUNK_STEPS == 0 and seq % FFN_CHUNK_STEPS == 0
    ab_re, ab_im, bbc, cre, cim = _discretise(ssm_A_re, ssm_A_im, ssm_log_dt, ssm_B_re, ssm_B_im,
                                              ssm_C_re, ssm_C_im)
    row = lambda a: a[:, None, :]
    p = dict(state_pool=state_pool.transpose(0, 2, 1, 3), state_conv=state_conv,
             state_re=state_ssm_re.reshape(depth, bs, SSM_FLAT),
             state_im=state_ssm_im.reshape(depth, bs, SSM_FLAT),
             n1g=row(norm1_g), pool_scale=row(pool_scale), a_re=ab_re, a_im=ab_im, bbc=bbc,
             cre=cre, cim=cim, dskip=row(ssm_D), w_in=w_in, pool_w=pool_w, w_glu=w_glu,
             w_out=w_out, n2g=row(norm2_g), conv_w=conv_w, conv_b=row(conv_b), w_up=w_up,
             w_down=w_down, nfg=norm_f_g[None])

    xs = jnp.concatenate([x_sample.reshape(bs, D_MODEL), jnp.repeat(meta_tokens, bp, axis=0)], axis=0)
    small = dict(Bs=bs, Bm=bp, Tm=N_META)
    mix_prev, ffn_prev, meta_state, mix_w, ffn_w = [], [], [], [], []
    for l in range(depth):
        xs, pool_s, re_s, im_s, mp, mr, mi, *wb = _mixer_small(xs, p, l, mix_prev, **small)
        mix_prev = [pool_s, re_s, im_s]
        mix_w.append(wb)
        xs, conv_s, mc, *wb = _ffn_small(xs, p, l, ffn_prev, final_norm=(l == depth - 1), **small)
        ffn_prev = [conv_s]
        ffn_w.append(wb)
        meta_state.append(((mp, mr, mi), mc))
    y_sample = xs[0:bs].reshape(bs, 1, D_MODEL)

    xp = x_prompt
    pool_p, re_p, im_p, conv_p = [], [], [], []
    for l in range(depth):
        mix_state, mc = meta_state[l]
        xp, pp, hr, hi = _mixer_main(xp, mix_state, p, l, mix_w[l], start_pos=N_META,
                                     batch_major_in=(l == 0), B=bp, T=MIXER_CHUNK_STEPS)
        last = l == depth - 1
        xp, cv = _ffn_main(xp, mc, p, l, ffn_w[l], final_norm=last, batch_major_out=last, B=bp,
                           T=FFN_CHUNK_STEPS)
        pool_p.append(pp)
        re_p.append(hr)
        im_p.append(hi)
        conv_p.append(cv)
    st = jnp.stack
    pool_p = st(pool_p).reshape(depth, POOL_KEEP, bp, POOL_WIDTH)[:, 1:].transpose(0, 2, 1, 3)
    conv_p = st(conv_p).reshape(depth, CONV_HIST, bp, D_FF).transpose(0, 2, 1, 3)
    group = lambda h: h.reshape(h.shape[:-1] + (SSM_GROUPS, SSM_STATE))
    return (xp, y_sample, pool_p, group(st(re_p)), group(st(im_p)), conv_p,
            pool_s.transpose(0, 2, 1, 3), group(re_s), group(im_s), conv_s)
```

```python
import functools

import jax
import jax.numpy as jnp
from jax import lax
from jax.experimental import pallas as pl
from jax.experimental.pallas import tpu as pltpu

D_MODEL = 1024
N_META = 16
POOL_WIDTH = 512
POOL_WINDOWS = (2, 4, 8, 16)
POOL_GROUPS = len(POOL_WINDOWS)
POOL_GROUP_IN = 128
POOL_GROUP_OUT = 256
POOL_HIST = 15
POOL_KEEP = 16
SSM_WIDTH = 512
SSM_GROUP_CH = 16
SSM_GROUPS = 32
SSM_STATE = 64
SSM_FLAT = SSM_GROUPS * SSM_STATE
SSM_TILE = 512
SSM_TILES = SSM_FLAT // SSM_TILE
SSM_TILE_GROUPS = SSM_TILE // SSM_STATE
SSM_TILE_CH = SSM_TILE_GROUPS * SSM_GROUP_CH
IN_COLS = POOL_WIDTH + SSM_WIDTH + 2 * D_MODEL
D_FF = 2816
FF_TILE = 256
FF_TILES = D_FF // FF_TILE
FF_LOOKAHEAD = 2
FF_SLOTS = FF_LOOKAHEAD + 1
CONV_W = 3
CONV_HIST = CONV_W - 1
RMS_EPS = 1e-6
PAST_LEN = 16384

V7X_LANES = 128
V7X_SUBLANES = 8
D_SLABS = D_MODEL // V7X_LANES
V7X_VMEM_LIMIT_BYTES = 56 * 1024 * 1024
MIXER_CHUNK_STEPS = 64
MIXER_SUB_BLOCKS = 1
FFN_CHUNK_STEPS = 64
FFN_SUB_BLOCKS = 2

_F32 = jnp.float32
_BF16 = jnp.bfloat16


def _rmsnorm(x, g):
    return x * lax.rsqrt(jnp.mean(x * x, axis=-1, keepdims=True) + RMS_EPS) * g


def _dot(a, b):
    return jnp.dot(a, b, preferred_element_type=_F32)


def _whole(shape):
    nd = len(shape)
    return pl.BlockSpec(shape, lambda *_: (0,) * nd, pipeline_mode=pl.Buffered(1))


def _layer(arr, l):
    nd = arr.ndim - 1
    return pl.BlockSpec((None,) + arr.shape[1:], lambda *_: (l,) + (0,) * nd,
                        pipeline_mode=pl.Buffered(1))


def _compiler_params():
    return pltpu.CompilerParams(dimension_semantics=("arbitrary",),
                                vmem_limit_bytes=V7X_VMEM_LIMIT_BYTES)


def _disc_kernel(are_ref, aim_ref, ldt_ref, bre_ref, bim_ref, ctre_ref, ctim_ref,
                 abre_ref, abim_ref, bbc_ref, cre_ref, cim_ref):
    lam_re = are_ref[...]
    lam_im = aim_ref[...]
    dt = jnp.exp(ldt_ref[...])
    mag = jnp.exp(lam_re * dt)
    ab_re = mag * jnp.cos(lam_im * dt)
    ab_im = mag * jnp.sin(lam_im * dt)
    den = lam_re * lam_re + lam_im * lam_im
    nr = ab_re - 1.0
    ni = ab_im
    q_re = (nr * lam_re + ni * lam_im) / den
    q_im = (ni * lam_re - nr * lam_im) / den
    b_re = bre_ref[...]
    b_im = bim_ref[...]
    abre_ref[...] = ab_re
    abim_ref[...] = ab_im
    bb_re = (q_re * b_re - q_im * b_im).astype(_BF16)
    bb_im = (q_re * b_im + q_im * b_re).astype(_BF16)
    ct_re = ctre_ref[...].astype(_BF16)
    ct_im = ctim_ref[...].astype(_BF16)
    bbc_ref[...] = jnp.zeros(bbc_ref.shape, _BF16)
    cre_ref[...] = jnp.zeros(cre_ref.shape, _BF16)
    cim_ref[...] = jnp.zeros(cim_ref.shape, _BF16)
    for i in range(SSM_TILES):
        for gl in range(SSM_TILE_GROUPS):
            g = i * SSM_TILE_GROUPS + gl
            ch = slice(gl * SSM_GROUP_CH, (gl + 1) * SSM_GROUP_CH)
            st = slice(gl * SSM_STATE, (gl + 1) * SSM_STATE)
            st_im = slice(SSM_TILE + gl * SSM_STATE, SSM_TILE + (gl + 1) * SSM_STATE)
            src_st = slice(g * SSM_STATE, (g + 1) * SSM_STATE)
            src_ch = slice(g * SSM_GROUP_CH, (g + 1) * SSM_GROUP_CH)
            bbc_ref[i, ch, st] = bb_re[:, src_st]
            bbc_ref[i, ch, st_im] = bb_im[:, src_st]
            cre_ref[i, st, ch] = ct_re[:, src_ch]
            cim_ref[i, st, ch] = ct_im[:, src_ch]


def _discretise(a_re, a_im, log_dt, b_re, b_im, c_re, c_im):
    depth = a_re.shape[0]
    flat = lambda a: a.reshape(depth, 1, SSM_FLAT)
    ldt = jnp.broadcast_to(log_dt[:, :, None], (depth, SSM_GROUPS, SSM_STATE))
    bt = lambda b: b.transpose(0, 3, 1, 2).reshape(depth, SSM_GROUP_CH, SSM_FLAT)
    ct = lambda c: c.transpose(0, 3, 1, 2).reshape(depth, SSM_STATE, SSM_WIDTH)
    ins = [flat(a_re), flat(a_im), flat(ldt), bt(b_re), bt(b_im), ct(c_re), ct(c_im)]
    out_shape = [jax.ShapeDtypeStruct((depth, 1, SSM_FLAT), _F32)] * 2 + [
        jax.ShapeDtypeStruct((depth, SSM_TILES, SSM_TILE_CH, 2 * SSM_TILE), _BF16),
        jax.ShapeDtypeStruct((depth, SSM_TILES, SSM_TILE, SSM_TILE_CH), _BF16),
        jax.ShapeDtypeStruct((depth, SSM_TILES, SSM_TILE, SSM_TILE_CH), _BF16)]
    per_layer = lambda s: pl.BlockSpec((None,) + s.shape[1:],
                                       lambda l: (l,) + (0,) * (len(s.shape) - 1))
    return pl.pallas_call(
        _disc_kernel,
        grid=(depth,),
        in_specs=[per_layer(a) for a in ins],
        out_specs=[per_layer(s) for s in out_shape],
        out_shape=out_shape,
        name="s5_discretise",
    )(*ins)


def _pool_branch(ext, n_hist, B, T, pos0, poolw, pscale):
    M = B * T
    base = n_hist * B
    pos = pos0 + lax.broadcasted_iota(jnp.int32, (M, 1), 0) // B
    tiles = []
    for g, w in enumerate(POOL_WINDOWS):
        ls = slice(g * POOL_GROUP_IN, (g + 1) * POOL_GROUP_IN)
        tok = ext[base:base + M, ls]
        win_sum = tok
        for k in range(1, w):
            win_sum = win_sum + ext[base - k * B:base - k * B + M, ls]
        cnt = jnp.minimum(w, pos + 1).astype(_F32)
        d = win_sum / cnt - tok
        tiles.append(_dot(d.astype(_BF16), poolw(g)))
    return jnp.concatenate(tiles, axis=-1) * pscale


def _ssm_inputs(u_ssm_b, bbc_ref):
    return [_dot(u_ssm_b[:, i * SSM_TILE_CH:(i + 1) * SSM_TILE_CH], bbc_ref[i])
            for i in range(SSM_TILES)]


def _ssm_tile(i, bu, h_re, h_im, B, T, are_ref, aim_ref, cre_ref, cim_ref):
    sl = slice(i * SSM_TILE, (i + 1) * SSM_TILE)
    a_re = jnp.broadcast_to(are_ref[:, sl], (B, SSM_TILE))
    a_im = jnp.broadcast_to(aim_ref[:, sl], (B, SSM_TILE))
    hs_re, hs_im = [], []
    for t in range(T):
        rows = slice(t * B, (t + 1) * B)
        n_re = a_re * h_re - a_im * h_im + bu[rows, 0:SSM_TILE]
        n_im = a_re * h_im + a_im * h_re + bu[rows, SSM_TILE:2 * SSM_TILE]
        h_re, h_im = n_re, n_im
        hs_re.append(h_re)
        hs_im.append(h_im)
    y = (_dot(jnp.concatenate(hs_re, axis=0).astype(_BF16), cre_ref[i])
         - _dot(jnp.concatenate(hs_im, axis=0).astype(_BF16), cim_ref[i]))
    return y, h_re, h_im


def _glu(y_tiles, u_ssm, dskip, wglu):
    y_s = jnp.concatenate(y_tiles, axis=-1) + dskip * u_ssm
    zz = _dot(jax.nn.gelu(y_s).astype(_BF16), wglu)
    return zz[:, 0:D_MODEL] * jax.nn.sigmoid(zz[:, D_MODEL:])


def _mixer_main_kernel(x_ref, hist_ref, h0re_ref, h0im_ref, n1g_ref, pscale_ref, are_ref, aim_ref,
                       bbc_ref, cre_ref, cim_ref, dskip_ref, win_ref, poolw_ref, wglu_ref, wout_ref,
                       xo_ref, poolst_ref, hre_o_ref, him_o_ref,
                       pool_ref, hst_ref, *scratch_refs, B, T, start_pos, n_chunks,
                       batch_major_in):
    M = B * T
    c = pl.program_id(0)

    @pl.when(c == 0)
    def _():
        pool_ref[...] = hist_ref[...]
        hst_ref[0] = h0re_ref[...]
        hst_ref[1] = h0im_ref[...]

    Ts = T // MIXER_SUB_BLOCKS
    Ms = B * Ts
    bu_refs, dma_refs = scratch_refs[:SSM_TILES], scratch_refs[SSM_TILES:]
    staged_rows = pl.ds(pl.multiple_of(jnp.minimum(c, 0), B), Ms)
    gate_cols = 2 * D_MODEL // SSM_TILES

    if batch_major_in:
        xt_ref, x_sem = dma_refs
        slot = lax.rem(c, 2)

        def x_copy(chunk, buf, b):
            return pltpu.make_async_copy(x_ref.at[b, pl.ds(chunk * T, T), :],
                                         xt_ref.at[buf, :, b, :], x_sem.at[buf, b])

        @pl.when(c == 0)
        def _():
            for b in range(B):
                x_copy(0, 0, b).start()

        @pl.when(c + 1 < n_chunks)
        def _():
            for b in range(B):
                x_copy(c + 1, 1 - slot, b).start()

        for b in range(B):
            x_copy(c, slot, b).wait()

    for s in range(MIXER_SUB_BLOCKS):
        if batch_major_in:
            x = xt_ref[slot, s * Ts:(s + 1) * Ts].reshape(Ms, D_MODEL)
        else:
            x = x_ref[s * Ms:(s + 1) * Ms, :]
        hb = _rmsnorm(x, n1g_ref[...]).astype(_BF16)

        u_ssm = _dot(hb, win_ref[:, POOL_WIDTH:POOL_WIDTH + SSM_WIDTH])
        u_ssm_b = u_ssm.astype(_BF16)
        def bbar(i):
            bu_refs[i][staged_rows, :] = _dot(u_ssm_b[:, i * SSM_TILE_CH:(i + 1) * SSM_TILE_CH],
                                              bbc_ref[i])

        def gate(i):
            lo = POOL_WIDTH + SSM_WIDTH + i * gate_cols
            return jax.nn.sigmoid(_dot(hb, win_ref[:, lo:lo + gate_cols]))

        def scan_c(i):
            sl = slice(i * SSM_TILE, (i + 1) * SSM_TILE)
            y, h_re, h_im = _ssm_tile(i, bu_refs[i], hst_ref[0, :, sl], hst_ref[1, :, sl], B, Ts,
                                      are_ref, aim_ref, cre_ref, cim_ref)
            hst_ref[0, :, sl] = h_re
            hst_ref[1, :, sl] = h_im
            return y

        gates, y_tiles = [], []
        bbar(0)
        u_pool = _dot(hb, win_ref[:, 0:POOL_WIDTH])
        bbar(1)
        gates.append(gate(0))
        bbar(2)
        gates.append(gate(1))
        bbar(3)
        y_tiles.append(scan_c(0))
        gates.append(gate(2))
        y_tiles.append(scan_c(1))
        gates.append(gate(3))
        y_tiles.append(scan_c(2))
        ext = jnp.concatenate([pool_ref[...], u_pool], axis=0)
        pool_ref[...] = ext[Ms:Ms + POOL_KEEP * B, :]
        y_pool = _pool_branch(ext, POOL_KEEP, B, Ts, start_pos + c * T + s * Ts,
                              lambda g: poolw_ref[g], pscale_ref[...])
        y_tiles.append(scan_c(3))
        gate_pool = jnp.concatenate(gates[0:SSM_TILES // 2], axis=-1)
        gate_ssm = jnp.concatenate(gates[SSM_TILES // 2:], axis=-1)
        merge_pool = gate_pool * y_pool
        merge = merge_pool + gate_ssm * _glu(y_tiles, u_ssm, dskip_ref[...], wglu_ref[...])
        xo_ref[s * Ms:(s + 1) * Ms, :] = x + _dot(merge.astype(_BF16), wout_ref[...])

    @pl.when(c == n_chunks - 1)
    def _():
        poolst_ref[...] = pool_ref[...]
        hre_o_ref[...] = hst_ref[0]
        him_o_ref[...] = hst_ref[1]


def _mixer_main(x, meta_state, p, l, wb, *, B, T, start_pos, batch_major_in):
    M = B * T
    if batch_major_in:
        n_chunks = x.shape[1] // T
        x_spec = pl.BlockSpec(memory_space=pl.ANY)
        slabs = [pltpu.VMEM((2, T, B, D_MODEL), _F32), pltpu.SemaphoreType.DMA((2, B))]
    else:
        n_chunks = x.shape[0] // M
        x_spec = pl.BlockSpec((M, D_MODEL), lambda c: (c, 0))
        slabs = []
    keep = POOL_KEEP * B
    kern = functools.partial(_mixer_main_kernel, B=B, T=T, start_pos=start_pos,
                             n_chunks=n_chunks, batch_major_in=batch_major_in)
    stacked = [p["n1g"], p["pool_scale"], p["a_re"], p["a_im"], p["bbc"], p["cre"], p["cim"],
               p["dskip"]]
    return pl.pallas_call(
        kern,
        grid=(n_chunks,),
        in_specs=[x_spec] + [_whole(a.shape) for a in meta_state]
        + [_layer(a, l) for a in stacked] + [_whole(a.shape) for a in wb],
        out_specs=[pl.BlockSpec((M, D_MODEL), lambda c: (c, 0)), _whole((keep, POOL_WIDTH)),
                   _whole((B, SSM_FLAT)), _whole((B, SSM_FLAT))],
        out_shape=[jax.ShapeDtypeStruct((n_chunks * M, D_MODEL), _F32),
                   jax.ShapeDtypeStruct((keep, POOL_WIDTH), _F32),
                   jax.ShapeDtypeStruct((B, SSM_FLAT), _F32),
                   jax.ShapeDtypeStruct((B, SSM_FLAT), _F32)],
        scratch_shapes=[pltpu.VMEM((keep, POOL_WIDTH), _F32),
                        pltpu.VMEM((2, B, SSM_FLAT), _F32),
                        ] + [pltpu.VMEM((M // MIXER_SUB_BLOCKS, 2 * SSM_TILE), _F32)] * SSM_TILES
        + slabs,
        compiler_params=_compiler_params(),
        name="mixer_main",
    )(x, *meta_state, *stacked, *wb)


def _mixer_small_kernel(x_ref, pool_ref, sre_ref, sim_ref, n1g_ref, pscale_ref, are_ref, aim_ref,
                        bbc_ref, cre_ref, cim_ref, dskip_ref, win_ref, poolw_ref, wglu_ref,
                        wout_ref, *refs, Bs, Bm, Tm, n_alias):
    (xo_ref, pool_o_ref, sre_o_ref, sim_o_ref, mpool_o_ref, mre_o_ref, mim_o_ref,
     win_o_ref, poolw_o_ref, wglu_o_ref, wout_o_ref) = refs[n_alias:]
    Ms, Mm = Bs, Bm * Tm
    win_o_ref[...] = win_ref[...].astype(_BF16)
    poolw_o_ref[...] = poolw_ref[...].astype(_BF16)
    wglu_o_ref[...] = wglu_ref[...].astype(_BF16)
    wout_o_ref[...] = wout_ref[...].astype(_BF16)

    x = x_ref[...]
    hb = _rmsnorm(x, n1g_ref[...]).astype(_BF16)
    proj = _dot(hb, win_o_ref[...])
    u_pool, u_ssm = proj[:, 0:POOL_WIDTH], proj[:, POOL_WIDTH:POOL_WIDTH + SSM_WIDTH]
    gate_pool = jax.nn.sigmoid(proj[:, POOL_WIDTH + SSM_WIDTH:POOL_WIDTH + SSM_WIDTH + D_MODEL])
    gate_ssm = jax.nn.sigmoid(proj[:, POOL_WIDTH + SSM_WIDTH + D_MODEL:])
    bus = _ssm_inputs(u_ssm.astype(_BF16), bbc_ref)
    poolw = lambda g: poolw_o_ref[g]

    hist = [pool_ref[j] for j in range(POOL_HIST)]
    new_hist = hist[1:] + [u_pool[0:Ms]]
    for j in range(POOL_HIST):
        pool_o_ref[j] = new_hist[j]
    y_pool_s = _pool_branch(jnp.concatenate(hist + [u_pool[0:Ms]], axis=0), POOL_HIST, Bs, 1,
                            PAST_LEN, poolw, pscale_ref[...])
    ext_m = jnp.concatenate([jnp.zeros((POOL_KEEP * Bm, POOL_WIDTH), _F32), u_pool[Ms:]], axis=0)
    y_pool_m = _pool_branch(ext_m, POOL_KEEP, Bm, Tm, 0, poolw, pscale_ref[...])
    mpool_o_ref[...] = ext_m[Mm:Mm + POOL_KEEP * Bm, :]
    y_pool = jnp.concatenate([y_pool_s, y_pool_m], axis=0)

    y_tiles = []
    for i in range(SSM_TILES):
        sl = slice(i * SSM_TILE, (i + 1) * SSM_TILE)
        ys, h_re, h_im = _ssm_tile(i, bus[i][0:Ms], sre_ref[:, sl], sim_ref[:, sl], Bs, 1,
                                   are_ref, aim_ref, cre_ref, cim_ref)
        sre_o_ref[:, sl] = h_re
        sim_o_ref[:, sl] = h_im
        zero = jnp.zeros((Bm, SSM_TILE), _F32)
        ym, h_re, h_im = _ssm_tile(i, bus[i][Ms:], zero, zero, Bm, Tm,
                                   are_ref, aim_ref, cre_ref, cim_ref)
        mre_o_ref[:, sl] = h_re
        mim_o_ref[:, sl] = h_im
        y_tiles.append(jnp.concatenate([ys, ym], axis=0))
    merge = gate_pool * y_pool + gate_ssm * _glu(y_tiles, u_ssm, dskip_ref[...], wglu_o_ref[...])
    xo_ref[...] = x + _dot(merge.astype(_BF16), wout_o_ref[...])


def _mixer_small(x, p, l, prev, *, Bs, Bm, Tm):
    kern = functools.partial(_mixer_small_kernel, Bs=Bs, Bm=Bm, Tm=Tm, n_alias=len(prev))
    stacked = [p["state_pool"], p["state_re"], p["state_im"], p["n1g"], p["pool_scale"], p["a_re"],
               p["a_im"], p["bbc"], p["cre"], p["cim"], p["dskip"], p["w_in"], p["pool_w"],
               p["w_glu"], p["w_out"]]
    sds = jax.ShapeDtypeStruct
    state_shapes = [sds(p["state_pool"].shape, _F32), sds(p["state_re"].shape, _F32),
                    sds(p["state_im"].shape, _F32)]
    plain_shapes = [sds((POOL_KEEP * Bm, POOL_WIDTH), _F32), sds((Bm, SSM_FLAT), _F32),
                    sds((Bm, SSM_FLAT), _F32), sds((D_MODEL, IN_COLS), _BF16),
                    sds((POOL_GROUPS, POOL_GROUP_IN, POOL_GROUP_OUT), _BF16),
                    sds((SSM_WIDTH, 2 * D_MODEL), _BF16), sds((D_MODEL, D_MODEL), _BF16)]
    n_in = 1 + len(stacked)
    return pl.pallas_call(
        kern,
        grid=(1,),
        in_specs=[_whole(x.shape)] + [_layer(a, l) for a in stacked]
        + [pl.BlockSpec(memory_space=pl.ANY)] * len(prev),
        out_specs=[_whole(x.shape)] + [_layer(s, l) for s in state_shapes]
        + [_whole(s.shape) for s in plain_shapes],
        out_shape=[sds(x.shape, _F32)] + state_shapes + plain_shapes,
        input_output_aliases={n_in + k: 1 + k for k in range(len(prev))},
        compiler_params=_compiler_params(),
        name="mixer_small",
    )(x, *stacked, *prev)


GELU_C1 = 0.7978845608028654
GELU_C2 = GELU_C1 * 0.044715


def _gelu2_mul(x, v):
    t = jnp.tanh(x * (GELU_C1 + GELU_C2 * (x * x)))
    return (x * v) * (1.0 + t)


def _conv_tile(g_ext, B, M, convw, convb):
    conv = convb
    for k in range(CONV_W):
        conv = conv + convw[k:k + 1, :] * g_ext[k * B:k * B + M, :]
    return conv


def _ffn_main_kernel(x_ref, chist_ref, n2g_ref, convw_ref, convb_ref, nfg_ref, wgv_ref,
                     wdown_ref, xo_ref, convst_ref,
                     carry_ref, *scratch_refs, B, T, n_chunks, final_norm, batch_major_out):
    M = B * T
    keep = CONV_HIST * B
    c = pl.program_id(0)

    @pl.when(c == 0)
    def _():
        carry_ref[...] = chist_ref[...]

    Ts = T // FFN_SUB_BLOCKS
    Ms = B * Ts
    gv_refs, dma_refs = scratch_refs[:FF_SLOTS], scratch_refs[FF_SLOTS:]
    if batch_major_out:
        yt_ref, y_sem = dma_refs
        slot_o = lax.rem(c, 2)

        def y_copy(chunk, buf, b):
            return pltpu.make_async_copy(yt_ref.at[buf, :, b, :],
                                         xo_ref.at[b, pl.ds(chunk * T, T), :], y_sem.at[buf, b])

        @pl.when(c >= 2)
        def _():
            for b in range(B):
                y_copy(c - 2, slot_o, b).wait()

    xs = [x_ref[s * Ms:(s + 1) * Ms, :] for s in range(FFN_SUB_BLOCKS)]
    hbs = [_rmsnorm(x, n2g_ref[...]).astype(_BF16) for x in xs]

    staged_rows = pl.ds(pl.multiple_of(keep + jnp.minimum(c, 0), B), Ms)
    n_tiles = FFN_SUB_BLOCKS * FF_TILES

    def up(k):
        s, j = divmod(k, FF_TILES)
        gv_refs[k % FF_SLOTS][staged_rows, :] = _dot(
            hbs[s], wgv_ref[:, 2 * j * FF_TILE:2 * (j + 1) * FF_TILE])

    for k in range(FF_LOOKAHEAD):
        up(k)
    acc = None
    for k in range(n_tiles):
        s, j = divmod(k, FF_TILES)
        fs = slice(j * FF_TILE, (j + 1) * FF_TILE)
        if k + FF_LOOKAHEAD < n_tiles:
            up(k + FF_LOOKAHEAD)
        slot = gv_refs[k % FF_SLOTS]
        slot[0:keep, 0:FF_TILE] = carry_ref[:, fs]
        conv = _conv_tile(slot.at[:, 0:FF_TILE], B, Ms, convw_ref[:, fs], convb_ref[:, fs])
        carry_ref[:, fs] = slot[Ms:Ms + keep, 0:FF_TILE]
        v = slot[keep:keep + Ms, FF_TILE:]
        contrib = _dot(_gelu2_mul(conv, v).astype(_BF16), wdown_ref[fs, :])
        acc = contrib if j == 0 else acc + contrib
        if j == FF_TILES - 1:
            xo = xs[s] + acc
            if final_norm:
                xo = _rmsnorm(xo, nfg_ref[...])
            if batch_major_out:
                yt_ref[slot_o, s * Ts:(s + 1) * Ts] = xo.reshape(Ts, B, D_MODEL)
            else:
                xo_ref[s * Ms:(s + 1) * Ms, :] = xo

    if batch_major_out:
        for b in range(B):
            y_copy(c, slot_o, b).start()

    @pl.when(c == n_chunks - 1)
    def _():
        convst_ref[...] = carry_ref[...]
        if batch_major_out:
            for b in range(B):
                y_copy(c - 1, 1 - slot_o, b).wait()
                y_copy(c, slot_o, b).wait()


def _ffn_main(x, chist, p, l, wb, *, B, T, final_norm, batch_major_out):
    M = B * T
    n_chunks = x.shape[0] // M
    keep = CONV_HIST * B
    kern = functools.partial(_ffn_main_kernel, B=B, T=T, n_chunks=n_chunks,
                             final_norm=final_norm, batch_major_out=batch_major_out)
    stacked = [p["n2g"], p["conv_w"], p["conv_b"]]
    if batch_major_out:
        assert n_chunks >= 2
        o_spec = pl.BlockSpec(memory_space=pl.ANY)
        o_shape = jax.ShapeDtypeStruct((B, n_chunks * T, D_MODEL), _F32)
        slabs = [pltpu.VMEM((2, T, B, D_MODEL), _F32), pltpu.SemaphoreType.DMA((2, B))]
    else:
        o_spec = pl.BlockSpec((M, D_MODEL), lambda c: (c, 0))
        o_shape = jax.ShapeDtypeStruct((n_chunks * M, D_MODEL), _F32)
        slabs = []
    return pl.pallas_call(
        kern,
        grid=(n_chunks,),
        in_specs=[pl.BlockSpec((M, D_MODEL), lambda c: (c, 0)), _whole(chist.shape)]
        + [_layer(a, l) for a in stacked] + [_whole(p["nfg"].shape)]
        + [_whole(a.shape) for a in wb],
        out_specs=[o_spec, _whole((keep, D_FF))],
        out_shape=[o_shape, jax.ShapeDtypeStruct((keep, D_FF), _F32)],
        scratch_shapes=[pltpu.VMEM((keep, D_FF), _F32),
                        ] + [pltpu.VMEM((keep + M // FFN_SUB_BLOCKS, 2 * FF_TILE), _F32)] * FF_SLOTS
        + slabs,
        compiler_params=_compiler_params(),
        name="convffn_main",
    )(x, chist, *stacked, p["nfg"], *wb)


def _ffn_small_kernel(x_ref, conv_ref, n2g_ref, convw_ref, convb_ref, nfg_ref, wg_ref, wv_ref,
                      wdown_ref, *refs, Bs, Bm, Tm, final_norm, n_alias):
    (xo_ref, conv_o_ref, mconv_o_ref, wgv_o_ref, wdown_o_ref,
     hb_ref, acc_ref) = refs[n_alias:]
    Ms, Mm = Bs, Bm * Tm
    jt = pl.program_id(0)

    @pl.when(jt == 0)
    def _():
        hb_ref[...] = _rmsnorm(x_ref[...], n2g_ref[...]).astype(_BF16)
        acc_ref[...] = jnp.zeros(acc_ref.shape, _F32)

    wgv_o_ref[:, 0:FF_TILE] = wg_ref[...].astype(_BF16)
    wgv_o_ref[:, FF_TILE:] = wv_ref[...].astype(_BF16)
    wdown_o_ref[...] = (0.5 * wdown_ref[...]).astype(_BF16)
    hb = hb_ref[...]
    gv = _dot(hb, wgv_o_ref[...])
    g, v = gv[:, 0:FF_TILE], gv[:, FF_TILE:]
    hist = [conv_ref[:, j, :] for j in range(CONV_HIST)]
    new_hist = hist[1:] + [g[0:Ms]]
    for j in range(CONV_HIST):
        conv_o_ref[:, j, :] = new_hist[j]
    g_ext_s = jnp.concatenate(hist + [g[0:Ms]], axis=0)
    g_ext_m = jnp.concatenate([jnp.zeros((CONV_HIST * Bm, FF_TILE), _F32), g[Ms:]], axis=0)
    mconv_o_ref[...] = g_ext_m[Mm:Mm + CONV_HIST * Bm, :]
    conv = jnp.concatenate([_conv_tile(g_ext_s, Bs, Ms, convw_ref[...], convb_ref[...]),
                            _conv_tile(g_ext_m, Bm, Mm, convw_ref[...], convb_ref[...])], axis=0)
    acc_ref[...] += _dot(_gelu2_mul(conv, v).astype(_BF16), wdown_o_ref[...])

    @pl.when(jt == FF_TILES - 1)
    def _():
        xo = x_ref[...] + acc_ref[...]
        if final_norm:
            xo = _rmsnorm(xo, nfg_ref[...])
        xo_ref[...] = xo


def _ffn_small(x, p, l, prev, *, Bs, Bm, Tm, final_norm):
    kern = functools.partial(_ffn_small_kernel, Bs=Bs, Bm=Bm, Tm=Tm, final_norm=final_norm,
                             n_alias=len(prev))
    sds = jax.ShapeDtypeStruct
    tile_of = lambda a, idx: pl.BlockSpec((None,) + idx[0], idx[1])
    n_rows = x.shape[0]
    in_specs = [
        _whole(x.shape),
        pl.BlockSpec((None, Bs, CONV_HIST, FF_TILE), lambda j: (l, 0, 0, j)),
        _layer(p["n2g"], l),
        pl.BlockSpec((None, CONV_W, FF_TILE), lambda j: (l, 0, j)),
        pl.BlockSpec((None, 1, FF_TILE), lambda j: (l, 0, j)),
        _whole(p["nfg"].shape),
        pl.BlockSpec((None, D_MODEL, FF_TILE), lambda j: (l, 0, j)),
        pl.BlockSpec((None, D_MODEL, FF_TILE), lambda j: (l, 0, FF_TILES + j)),
        pl.BlockSpec((None, FF_TILE, D_MODEL), lambda j: (l, j, 0)),
    ] + [pl.BlockSpec(memory_space=pl.ANY)] * len(prev)
    out_shape = [sds(x.shape, _F32), sds(p["state_conv"].shape, _F32),
                 sds((CONV_HIST * Bm, D_FF), _F32), sds((D_MODEL, 2 * D_FF), _BF16),
                 sds((D_FF, D_MODEL), _BF16)]
    out_specs = [
        _whole(x.shape),
        pl.BlockSpec((None, Bs, CONV_HIST, FF_TILE), lambda j: (l, 0, 0, j)),
        pl.BlockSpec((CONV_HIST * Bm, FF_TILE), lambda j: (0, j)),
        pl.BlockSpec((D_MODEL, 2 * FF_TILE), lambda j: (0, j)),
        pl.BlockSpec((FF_TILE, D_MODEL), lambda j: (j, 0)),
    ]
    n_in = 9
    return pl.pallas_call(
        kern,
        grid=(FF_TILES,),
        in_specs=in_specs,
        out_specs=out_specs,
        out_shape=out_shape,
        scratch_shapes=[pltpu.VMEM((n_rows, D_MODEL), _BF16), pltpu.VMEM((n_rows, D_MODEL), _F32)],
        input_output_aliases={n_in + k: 1 + k for k in range(len(prev))},
        compiler_params=_compiler_params(),
        name="convffn_small",
    )(x, p["state_conv"], p["n2g"], p["conv_w"], p["conv_b"], p["nfg"], p["w_up"], p["w_up"],
      p["w_down"], *prev)


def kernel(x_prompt, x_sample, state_pool, state_ssm_re, state_ssm_im, state_conv, meta_tokens,
           norm1_g, w_in, pool_w, pool_scale, ssm_A_re, ssm_A_im, ssm_log_dt, ssm_B_re, ssm_B_im,
           ssm_C_re, ssm_C_im, ssm_D, w_glu, w_out, norm2_g, w_up, conv_w, conv_b, w_down, norm_f_g):
    depth = w_in.shape[0]
    bp, seq, _ = x_prompt.shape
    bs = x_sample.shape[0]
    assert seq % MIXER_CHUNK_STEPS == 0 and seq % FFN_CHUNK_STEPS == 0
    ab_re, ab_im, bbc, cre, cim = _discretise(ssm_A_re, ssm_A_im, ssm_log_dt, ssm_B_re, ssm_B_im,
                                              ssm_C_re, ssm_C_im)
    row = lambda a: a[:, None, :]
    p = dict(state_pool=state_pool.transpose(0, 2, 1, 3), state_conv=state_conv,
             state_re=state_ssm_re.reshape(depth, bs, SSM_FLAT),
             state_im=state_ssm_im.reshape(depth, bs, SSM_FLAT),
             n1g=row(norm1_g), pool_scale=row(pool_scale), a_re=ab_re, a_im=ab_im, bbc=bbc,
             cre=cre, cim=cim, dskip=row(ssm_D), w_in=w_in, pool_w=pool_w, w_glu=w_glu,
             w_out=w_out, n2g=row(norm2_g), conv_w=conv_w, conv_b=row(conv_b), w_up=w_up,
             w_down=w_down, nfg=norm_f_g[None])

    xs = jnp.concatenate([x_sample.reshape(bs, D_MODEL), jnp.repeat(meta_tokens, bp, axis=0)], axis=0)
    small = dict(Bs=bs, Bm=bp, Tm=N_META)
    mix_prev, ffn_prev, meta_state, mix_w, ffn_w = [], [], [], [], []
    for l in range(depth):
        xs, pool_s, re_s, im_s, mp, mr, mi, *wb = _mixer_small(xs, p, l, mix_prev, **small)
        mix_prev = [pool_s, re_s, im_s]
        mix_w.append(wb)
        xs, conv_s, mc, *wb = _ffn_small(xs, p, l, ffn_prev, final_norm=(l == depth - 1), **small)
        ffn_prev = [conv_s]
        ffn_w.append(wb)
        meta_state.append(((mp, mr, mi), mc))
    y_sample = xs[0:bs].reshape(bs, 1, D_MODEL)

    xp = x_prompt
    pool_p, re_p, im_p, conv_p = [], [], [], []
    for l in range(depth):
        mix_state, mc = meta_state[l]
        xp, pp, hr, hi = _mixer_main(xp, mix_state, p, l, mix_w[l], start_pos=N_META,
                                     batch_major_in=(l == 0), B=bp, T=MIXER_CHUNK_STEPS)
        last = l == depth - 1
        xp, cv = _ffn_main(xp, mc, p, l, ffn_w[l], final_norm=last, batch_major_out=last, B=bp,
                           T=FFN_CHUNK_STEPS)
        pool_p.append(pp)
        re_p.append(hr)
        im_p.append(hi)
        conv_p.append(cv)
    st = jnp.stack
    pool_p = st(pool_p).reshape(depth, POOL_KEEP, bp, POOL_WIDTH)[:, 1:].transpose(0, 2, 1, 3)
    conv_p = st(conv_p).reshape(depth, CONV_HIST, bp, D_FF).transpose(0, 2, 1, 3)
    group = lambda h: h.reshape(h.shape[:-1] + (SSM_GROUPS, SSM_STATE))
    return (xp, y_sample, pool_p, group(st(re_p)), group(st(im_p)), conv_p,
            pool_s.transpose(0, 2, 1, 3), group(re_s), group(im_s), conv_s)
```

```python
import functools

import jax
import jax.numpy as jnp
from jax import lax
from jax.experimental import pallas as pl
from jax.experimental.pallas import tpu as pltpu

D_MODEL = 1024
N_META = 16
POOL_WIDTH = 512
POOL_WINDOWS = (2, 4, 8, 16)
POOL_GROUPS = len(POOL_WINDOWS)
POOL_GROUP_IN = 128
POOL_GROUP_OUT = 256
POOL_HIST = 15
POOL_KEEP = 16
SSM_WIDTH = 512
SSM_GROUP_CH = 16
SSM_GROUPS = 32
SSM_STATE = 64
SSM_FLAT = SSM_GROUPS * SSM_STATE
SSM_TILE = 512
SSM_TILES = SSM_FLAT // SSM_TILE
SSM_TILE_GROUPS = SSM_TILE // SSM_STATE
SSM_TILE_CH = SSM_TILE_GROUPS * SSM_GROUP_CH
IN_COLS = POOL_WIDTH + SSM_WIDTH + 2 * D_MODEL
D_FF = 2816
FF_TILE = 256
FF_TILES = D_FF // FF_TILE
FF_LOOKAHEAD = 3
FF_SLOTS = FF_LOOKAHEAD + 1
CONV_W = 3
CONV_HIST = CONV_W - 1
RMS_EPS = 1e-6
PAST_LEN = 16384

V7X_LANES = 128
V7X_SUBLANES = 8
D_SLABS = D_MODEL // V7X_LANES
V7X_VMEM_LIMIT_BYTES = 56 * 1024 * 1024
MIXER_CHUNK_STEPS = 64
MIXER_SUB_BLOCKS = 1
FFN_CHUNK_STEPS = 64
FFN_SUB_BLOCKS = 2

_F32 = jnp.float32
_BF16 = jnp.bfloat16


def _rmsnorm(x, g):
    return x * lax.rsqrt(jnp.mean(x * x, axis=-1, keepdims=True) + RMS_EPS) * g


def _dot(a, b):
    return jnp.dot(a, b, preferred_element_type=_F32)


def _whole(shape):
    nd = len(shape)
    return pl.BlockSpec(shape, lambda *_: (0,) * nd, pipeline_mode=pl.Buffered(1))


def _layer(arr, l):
    nd = arr.ndim - 1
    return pl.BlockSpec((None,) + arr.shape[1:], lambda *_: (l,) + (0,) * nd,
                        pipeline_mode=pl.Buffered(1))


def _compiler_params():
    return pltpu.CompilerParams(dimension_semantics=("arbitrary",),
                                vmem_limit_bytes=V7X_VMEM_LIMIT_BYTES)


def _disc_kernel(are_ref, aim_ref, ldt_ref, bre_ref, bim_ref, ctre_ref, ctim_ref,
                 abre_ref, abim_ref, bbc_ref, cre_ref, cim_ref):
    lam_re = are_ref[...]
    lam_im = aim_ref[...]
    dt = jnp.exp(ldt_ref[...])
    mag = jnp.exp(lam_re * dt)
    ab_re = mag * jnp.cos(lam_im * dt)
    ab_im = mag * jnp.sin(lam_im * dt)
    den = lam_re * lam_re + lam_im * lam_im
    nr = ab_re - 1.0
    ni = ab_im
    q_re = (nr * lam_re + ni * lam_im) / den
    q_im = (ni * lam_re - nr * lam_im) / den
    b_re = bre_ref[...]
    b_im = bim_ref[...]
    abre_ref[...] = ab_re
    abim_ref[...] = ab_im
    bb_re = (q_re * b_re - q_im * b_im).astype(_BF16)
    bb_im = (q_re * b_im + q_im * b_re).astype(_BF16)
    ct_re = ctre_ref[...].astype(_BF16)
    ct_im = ctim_ref[...].astype(_BF16)
    bbc_ref[...] = jnp.zeros(bbc_ref.shape, _BF16)
    cre_ref[...] = jnp.zeros(cre_ref.shape, _BF16)
    cim_ref[...] = jnp.zeros(cim_ref.shape, _BF16)
    for i in range(SSM_TILES):
        for gl in range(SSM_TILE_GROUPS):
            g = i * SSM_TILE_GROUPS + gl
            ch = slice(gl * SSM_GROUP_CH, (gl + 1) * SSM_GROUP_CH)
            st = slice(gl * SSM_STATE, (gl + 1) * SSM_STATE)
            st_im = slice(SSM_TILE + gl * SSM_STATE, SSM_TILE + (gl + 1) * SSM_STATE)
            src_st = slice(g * SSM_STATE, (g + 1) * SSM_STATE)
            src_ch = slice(g * SSM_GROUP_CH, (g + 1) * SSM_GROUP_CH)
            bbc_ref[i, ch, st] = bb_re[:, src_st]
            bbc_ref[i, ch, st_im] = bb_im[:, src_st]
            cre_ref[i, st, ch] = ct_re[:, src_ch]
            cim_ref[i, st, ch] = ct_im[:, src_ch]


def _discretise(a_re, a_im, log_dt, b_re, b_im, c_re, c_im):
    depth = a_re.shape[0]
    flat = lambda a: a.reshape(depth, 1, SSM_FLAT)
    ldt = jnp.broadcast_to(log_dt[:, :, None], (depth, SSM_GROUPS, SSM_STATE))
    bt = lambda b: b.transpose(0, 3, 1, 2).reshape(depth, SSM_GROUP_CH, SSM_FLAT)
    ct = lambda c: c.transpose(0, 3, 1, 2).reshape(depth, SSM_STATE, SSM_WIDTH)
    ins = [flat(a_re), flat(a_im), flat(ldt), bt(b_re), bt(b_im), ct(c_re), ct(c_im)]
    out_shape = [jax.ShapeDtypeStruct((depth, 1, SSM_FLAT), _F32)] * 2 + [
        jax.ShapeDtypeStruct((depth, SSM_TILES, SSM_TILE_CH, 2 * SSM_TILE), _BF16),
        jax.ShapeDtypeStruct((depth, SSM_TILES, SSM_TILE, SSM_TILE_CH), _BF16),
        jax.ShapeDtypeStruct((depth, SSM_TILES, SSM_TILE, SSM_TILE_CH), _BF16)]
    per_layer = lambda s: pl.BlockSpec((None,) + s.shape[1:],
                                       lambda l: (l,) + (0,) * (len(s.shape) - 1))
    return pl.pallas_call(
        _disc_kernel,
        grid=(depth,),
        in_specs=[per_layer(a) for a in ins],
        out_specs=[per_layer(s) for s in out_shape],
        out_shape=out_shape,
        name="s5_discretise",
    )(*ins)


def _pool_branch(ext, n_hist, B, T, pos0, poolw, pscale):
    M = B * T
    base = n_hist * B
    pos = pos0 + lax.broadcasted_iota(jnp.int32, (M, 1), 0) // B
    tiles = []
    for g, w in enumerate(POOL_WINDOWS):
        ls = slice(g * POOL_GROUP_IN, (g + 1) * POOL_GROUP_IN)
        tok = ext[base:base + M, ls]
        win_sum = tok
        for k in range(1, w):
            win_sum = win_sum + ext[base - k * B:base - k * B + M, ls]
        cnt = jnp.minimum(w, pos + 1).astype(_F32)
        d = win_sum / cnt - tok
        tiles.append(_dot(d.astype(_BF16), poolw(g)))
    return jnp.concatenate(tiles, axis=-1) * pscale


def _ssm_inputs(u_ssm_b, bbc_ref):
    return [_dot(u_ssm_b[:, i * SSM_TILE_CH:(i + 1) * SSM_TILE_CH], bbc_ref[i])
            for i in range(SSM_TILES)]


def _ssm_tile(i, bu, h_re, h_im, B, T, are_ref, aim_ref, cre_ref, cim_ref):
    sl = slice(i * SSM_TILE, (i + 1) * SSM_TILE)
    a_re = jnp.broadcast_to(are_ref[:, sl], (B, SSM_TILE))
    a_im = jnp.broadcast_to(aim_ref[:, sl], (B, SSM_TILE))
    hs_re, hs_im = [], []
    for t in range(T):
        rows = slice(t * B, (t + 1) * B)
        n_re = a_re * h_re - a_im * h_im + bu[rows, 0:SSM_TILE]
        n_im = a_re * h_im + a_im * h_re + bu[rows, SSM_TILE:2 * SSM_TILE]
        h_re, h_im = n_re, n_im
        hs_re.append(h_re)
        hs_im.append(h_im)
    y = (_dot(jnp.concatenate(hs_re, axis=0).astype(_BF16), cre_ref[i])
         - _dot(jnp.concatenate(hs_im, axis=0).astype(_BF16), cim_ref[i]))
    return y, h_re, h_im


GLU_TILE = 256


def _glu_weight(wglu_f32):
    tiles = []
    for k in range(D_MODEL // GLU_TILE):
        tiles.append(wglu_f32[:, k * GLU_TILE:(k + 1) * GLU_TILE])
        tiles.append(wglu_f32[:, D_MODEL + k * GLU_TILE:D_MODEL + (k + 1) * GLU_TILE])
    return jnp.concatenate(tiles, axis=-1).astype(_BF16)


def _glu(y_tiles, u_ssm, dskip, wglu):
    y_s = jnp.concatenate(y_tiles, axis=-1) + dskip * u_ssm
    zz = _dot(jax.nn.gelu(y_s).astype(_BF16), wglu)
    outs = []
    for k in range(D_MODEL // GLU_TILE):
        lo = 2 * k * GLU_TILE
        outs.append(zz[:, lo:lo + GLU_TILE] * jax.nn.sigmoid(zz[:, lo + GLU_TILE:lo + 2 * GLU_TILE]))
    return jnp.concatenate(outs, axis=-1)


def _mixer_main_kernel(x_ref, hist_ref, h0re_ref, h0im_ref, n1g_ref, pscale_ref, are_ref, aim_ref,
                       bbc_ref, cre_ref, cim_ref, dskip_ref, win_ref, poolw_ref, wglu_ref, wout_ref,
                       xo_ref, poolst_ref, hre_o_ref, him_o_ref,
                       pool_ref, hst_ref, *scratch_refs, B, T, start_pos, n_chunks,
                       batch_major_in):
    M = B * T
    c = pl.program_id(0)

    @pl.when(c == 0)
    def _():
        pool_ref[...] = hist_ref[...]
        hst_ref[0] = h0re_ref[...]
        hst_ref[1] = h0im_ref[...]

    Ts = T // MIXER_SUB_BLOCKS
    Ms = B * Ts
    bu_refs, dma_refs = scratch_refs[:SSM_TILES], scratch_refs[SSM_TILES:]
    staged_rows = pl.ds(pl.multiple_of(jnp.minimum(c, 0), B), Ms)
    gate_cols = 2 * D_MODEL // SSM_TILES

    if batch_major_in:
        xt_ref, x_sem = dma_refs
        slot = lax.rem(c, 2)

        def x_copy(chunk, buf, b):
            return pltpu.make_async_copy(x_ref.at[b, pl.ds(chunk * T, T), :],
                                         xt_ref.at[buf, :, b, :], x_sem.at[buf, b])

        @pl.when(c == 0)
        def _():
            for b in range(B):
                x_copy(0, 0, b).start()

        @pl.when(c + 1 < n_chunks)
        def _():
            for b in range(B):
                x_copy(c + 1, 1 - slot, b).start()

        for b in range(B):
            x_copy(c, slot, b).wait()

    for s in range(MIXER_SUB_BLOCKS):
        if batch_major_in:
            x = xt_ref[slot, s * Ts:(s + 1) * Ts].reshape(Ms, D_MODEL)
        else:
            x = x_ref[s * Ms:(s + 1) * Ms, :]
        hb = _rmsnorm(x, n1g_ref[...]).astype(_BF16)

        u_ssm = _dot(hb, win_ref[:, POOL_WIDTH:POOL_WIDTH + SSM_WIDTH])
        u_ssm_b = u_ssm.astype(_BF16)
        def bbar(i):
            bu_refs[i][staged_rows, :] = _dot(u_ssm_b[:, i * SSM_TILE_CH:(i + 1) * SSM_TILE_CH],
                                              bbc_ref[i])

        def gate(i):
            lo = POOL_WIDTH + SSM_WIDTH + i * gate_cols
            return jax.nn.sigmoid(_dot(hb, win_ref[:, lo:lo + gate_cols]))

        def scan_c(i):
            sl = slice(i * SSM_TILE, (i + 1) * SSM_TILE)
            y, h_re, h_im = _ssm_tile(i, bu_refs[i], hst_ref[0, :, sl], hst_ref[1, :, sl], B, Ts,
                                      are_ref, aim_ref, cre_ref, cim_ref)
            hst_ref[0, :, sl] = h_re
            hst_ref[1, :, sl] = h_im
            return y

        gates, y_tiles = [], []
        bbar(0)
        u_pool = _dot(hb, win_ref[:, 0:POOL_WIDTH])
        bbar(1)
        gates.append(gate(0))
        bbar(2)
        gates.append(gate(1))
        bbar(3)
        y_tiles.append(scan_c(0))
        gates.append(gate(2))
        y_tiles.append(scan_c(1))
        gates.append(gate(3))
        y_tiles.append(scan_c(2))
        ext = jnp.concatenate([pool_ref[...], u_pool], axis=0)
        pool_ref[...] = ext[Ms:Ms + POOL_KEEP * B, :]
        y_pool = _pool_branch(ext, POOL_KEEP, B, Ts, start_pos + c * T + s * Ts,
                              lambda g: poolw_ref[g], pscale_ref[...])
        y_tiles.append(scan_c(3))
        gate_pool = jnp.concatenate(gates[0:SSM_TILES // 2], axis=-1)
        gate_ssm = jnp.concatenate(gates[SSM_TILES // 2:], axis=-1)
        merge_pool = gate_pool * y_pool
        merge = merge_pool + gate_ssm * _glu(y_tiles, u_ssm, dskip_ref[...], wglu_ref[...])
        xo_ref[s * Ms:(s + 1) * Ms, :] = x + _dot(merge.astype(_BF16), wout_ref[...])

    @pl.when(c == n_chunks - 1)
    def _():
        poolst_ref[...] = pool_ref[...]
        hre_o_ref[...] = hst_ref[0]
        him_o_ref[...] = hst_ref[1]


def _mixer_main(x, meta_state, p, l, wb, *, B, T, start_pos, batch_major_in):
    M = B * T
    if batch_major_in:
        n_chunks = x.shape[1] // T
        x_spec = pl.BlockSpec(memory_space=pl.ANY)
        slabs = [pltpu.VMEM((2, T, B, D_MODEL), _F32), pltpu.SemaphoreType.DMA((2, B))]
    else:
        n_chunks = x.shape[0] // M
        x_spec = pl.BlockSpec((M, D_MODEL), lambda c: (c, 0))
        slabs = []
    keep = POOL_KEEP * B
    kern = functools.partial(_mixer_main_kernel, B=B, T=T, start_pos=start_pos,
                             n_chunks=n_chunks, batch_major_in=batch_major_in)
    stacked = [p["n1g"], p["pool_scale"], p["a_re"], p["a_im"], p["bbc"], p["cre"], p["cim"],
               p["dskip"]]
    return pl.pallas_call(
        kern,
        grid=(n_chunks,),
        in_specs=[x_spec] + [_whole(a.shape) for a in meta_state]
        + [_layer(a, l) for a in stacked] + [_whole(a.shape) for a in wb],
        out_specs=[pl.BlockSpec((M, D_MODEL), lambda c: (c, 0)), _whole((keep, POOL_WIDTH)),
                   _whole((B, SSM_FLAT)), _whole((B, SSM_FLAT))],
        out_shape=[jax.ShapeDtypeStruct((n_chunks * M, D_MODEL), _F32),
                   jax.ShapeDtypeStruct((keep, POOL_WIDTH), _F32),
                   jax.ShapeDtypeStruct((B, SSM_FLAT), _F32),
                   jax.ShapeDtypeStruct((B, SSM_FLAT), _F32)],
        scratch_shapes=[pltpu.VMEM((keep, POOL_WIDTH), _F32),
                        pltpu.VMEM((2, B, SSM_FLAT), _F32),
                        ] + [pltpu.VMEM((M // MIXER_SUB_BLOCKS, 2 * SSM_TILE), _F32)] * SSM_TILES
        + slabs,
        compiler_params=_compiler_params(),
        name="mixer_main",
    )(x, *meta_state, *stacked, *wb)


def _mixer_small_kernel(x_ref, pool_ref, sre_ref, sim_ref, n1g_ref, pscale_ref, are_ref, aim_ref,
                        bbc_ref, cre_ref, cim_ref, dskip_ref, win_ref, poolw_ref, wglu_ref,
                        wout_ref, *refs, Bs, Bm, Tm, n_alias):
    (xo_ref, pool_o_ref, sre_o_ref, sim_o_ref, mpool_o_ref, mre_o_ref, mim_o_ref,
     win_o_ref, poolw_o_ref, wglu_o_ref, wout_o_ref) = refs[n_alias:]
    Ms, Mm = Bs, Bm * Tm
    win_o_ref[...] = win_ref[...].astype(_BF16)
    poolw_o_ref[...] = poolw_ref[...].astype(_BF16)
    wglu_o_ref[...] = _glu_weight(wglu_ref[...])
    wout_o_ref[...] = wout_ref[...].astype(_BF16)

    x = x_ref[...]
    hb = _rmsnorm(x, n1g_ref[...]).astype(_BF16)
    proj = _dot(hb, win_o_ref[...])
    u_pool, u_ssm = proj[:, 0:POOL_WIDTH], proj[:, POOL_WIDTH:POOL_WIDTH + SSM_WIDTH]
    gate_pool = jax.nn.sigmoid(proj[:, POOL_WIDTH + SSM_WIDTH:POOL_WIDTH + SSM_WIDTH + D_MODEL])
    gate_ssm = jax.nn.sigmoid(proj[:, POOL_WIDTH + SSM_WIDTH + D_MODEL:])
    bus = _ssm_inputs(u_ssm.astype(_BF16), bbc_ref)
    poolw = lambda g: poolw_o_ref[g]

    hist = [pool_ref[j] for j in range(POOL_HIST)]
    new_hist = hist[1:] + [u_pool[0:Ms]]
    for j in range(POOL_HIST):
        pool_o_ref[j] = new_hist[j]
    y_pool_s = _pool_branch(jnp.concatenate(hist + [u_pool[0:Ms]], axis=0), POOL_HIST, Bs, 1,
                            PAST_LEN, poolw, pscale_ref[...])
    ext_m = jnp.concatenate([jnp.zeros((POOL_KEEP * Bm, POOL_WIDTH), _F32), u_pool[Ms:]], axis=0)
    y_pool_m = _pool_branch(ext_m, POOL_KEEP, Bm, Tm, 0, poolw, pscale_ref[...])
    mpool_o_ref[...] = ext_m[Mm:Mm + POOL_KEEP * Bm, :]
    y_pool = jnp.concatenate([y_pool_s, y_pool_m], axis=0)

    y_tiles = []
    for i in range(SSM_TILES):
        sl = slice(i * SSM_TILE, (i + 1) * SSM_TILE)
        ys, h_re, h_im = _ssm_tile(i, bus[i][0:Ms], sre_ref[:, sl], sim_ref[:, sl], Bs, 1,
                                   are_ref, aim_ref, cre_ref, cim_ref)
        sre_o_ref[:, sl] = h_re
        sim_o_ref[:, sl] = h_im
        zero = jnp.zeros((Bm, SSM_TILE), _F32)
        ym, h_re, h_im = _ssm_tile(i, bus[i][Ms:], zero, zero, Bm, Tm,
                                   are_ref, aim_ref, cre_ref, cim_ref)
        mre_o_ref[:, sl] = h_re
        mim_o_ref[:, sl] = h_im
        y_tiles.append(jnp.concatenate([ys, ym], axis=0))
    merge = gate_pool * y_pool + gate_ssm * _glu(y_tiles, u_ssm, dskip_ref[...], wglu_o_ref[...])
    xo_ref[...] = x + _dot(merge.astype(_BF16), wout_o_ref[...])


def _mixer_small(x, p, l, prev, *, Bs, Bm, Tm):
    kern = functools.partial(_mixer_small_kernel, Bs=Bs, Bm=Bm, Tm=Tm, n_alias=len(prev))
    stacked = [p["state_pool"], p["state_re"], p["state_im"], p["n1g"], p["pool_scale"], p["a_re"],
               p["a_im"], p["bbc"], p["cre"], p["cim"], p["dskip"], p["w_in"], p["pool_w"],
               p["w_glu"], p["w_out"]]
    sds = jax.ShapeDtypeStruct
    state_shapes = [sds(p["state_pool"].shape, _F32), sds(p["state_re"].shape, _F32),
                    sds(p["state_im"].shape, _F32)]
    plain_shapes = [sds((POOL_KEEP * Bm, POOL_WIDTH), _F32), sds((Bm, SSM_FLAT), _F32),
                    sds((Bm, SSM_FLAT), _F32), sds((D_MODEL, IN_COLS), _BF16),
                    sds((POOL_GROUPS, POOL_GROUP_IN, POOL_GROUP_OUT), _BF16),
                    sds((SSM_WIDTH, 2 * D_MODEL), _BF16), sds((D_MODEL, D_MODEL), _BF16)]
    n_in = 1 + len(stacked)
    return pl.pallas_call(
        kern,
        grid=(1,),
        in_specs=[_whole(x.shape)] + [_layer(a, l) for a in stacked]
        + [pl.BlockSpec(memory_space=pl.ANY)] * len(prev),
        out_specs=[_whole(x.shape)] + [_layer(s, l) for s in state_shapes]
        + [_whole(s.shape) for s in plain_shapes],
        out_shape=[sds(x.shape, _F32)] + state_shapes + plain_shapes,
        input_output_aliases={n_in + k: 1 + k for k in range(len(prev))},
        compiler_params=_compiler_params(),
        name="mixer_small",
    )(x, *stacked, *prev)


GELU_C1 = 0.7978845608028654
GELU_C2 = GELU_C1 * 0.044715


def _gelu2_mul(x, v):
    t = jnp.tanh(x * (GELU_C1 + GELU_C2 * (x * x)))
    return (x * v) * (1.0 + t)


def _conv_tile(g_ext, B, M, convw, convb):
    conv = convb
    for k in range(CONV_W):
        conv = conv + convw[k:k + 1, :] * g_ext[k * B:k * B + M, :]
    return conv


def _ffn_main_kernel(x_ref, chist_ref, n2g_ref, convw_ref, convb_ref, nfg_ref, wgv_ref,
                     wdown_ref, xo_ref, convst_ref,
                     carry_ref, *scratch_refs, B, T, n_chunks, final_norm, batch_major_out):
    M = B * T
    keep = CONV_HIST * B
    c = pl.program_id(0)

    @pl.when(c == 0)
    def _():
        carry_ref[...] = chist_ref[...]

    Ts = T // FFN_SUB_BLOCKS
    Ms = B * Ts
    gv_refs, dma_refs = scratch_refs[:FF_SLOTS], scratch_refs[FF_SLOTS:]
    if batch_major_out:
        yt_ref, y_sem = dma_refs
        slot_o = lax.rem(c, 2)

        def y_copy(chunk, buf, b):
            return pltpu.make_async_copy(yt_ref.at[buf, :, b, :],
                                         xo_ref.at[b, pl.ds(chunk * T, T), :], y_sem.at[buf, b])

        @pl.when(c >= 2)
        def _():
            for b in range(B):
                y_copy(c - 2, slot_o, b).wait()

    xs = [x_ref[s * Ms:(s + 1) * Ms, :] for s in range(FFN_SUB_BLOCKS)]
    hbs = [_rmsnorm(x, n2g_ref[...]).astype(_BF16) for x in xs]

    staged_rows = pl.ds(pl.multiple_of(keep + jnp.minimum(c, 0), B), Ms)
    n_tiles = FFN_SUB_BLOCKS * FF_TILES

    def up(k):
        s, j = divmod(k, FF_TILES)
        gv_refs[k % FF_SLOTS][staged_rows, :] = _dot(
            hbs[s], wgv_ref[:, 2 * j * FF_TILE:2 * (j + 1) * FF_TILE])

    for k in range(FF_LOOKAHEAD):
        up(k)
    acc = None
    for k in range(n_tiles):
        s, j = divmod(k, FF_TILES)
        fs = slice(j * FF_TILE, (j + 1) * FF_TILE)
        if k + FF_LOOKAHEAD < n_tiles:
            up(k + FF_LOOKAHEAD)
        slot = gv_refs[k % FF_SLOTS]
        slot[0:keep, 0:FF_TILE] = carry_ref[:, fs]
        conv = _conv_tile(slot.at[:, 0:FF_TILE], B, Ms, convw_ref[:, fs], convb_ref[:, fs])
        carry_ref[:, fs] = slot[Ms:Ms + keep, 0:FF_TILE]
        v = slot[keep:keep + Ms, FF_TILE:]
        contrib = _dot(_gelu2_mul(conv, v).astype(_BF16), wdown_ref[fs, :])
        acc = contrib if j == 0 else acc + contrib
        if j == FF_TILES - 1:
            xo = xs[s] + acc
            if final_norm:
                xo = _rmsnorm(xo, nfg_ref[...])
            if batch_major_out:
                yt_ref[slot_o, s * Ts:(s + 1) * Ts] = xo.reshape(Ts, B, D_MODEL)
            else:
                xo_ref[s * Ms:(s + 1) * Ms, :] = xo

    if batch_major_out:
        for b in range(B):
            y_copy(c, slot_o, b).start()

    @pl.when(c == n_chunks - 1)
    def _():
        convst_ref[...] = carry_ref[...]
        if batch_major_out:
            for b in range(B):
                y_copy(c - 1, 1 - slot_o, b).wait()
                y_copy(c, slot_o, b).wait()


def _ffn_main(x, chist, p, l, wb, *, B, T, final_norm, batch_major_out):
    M = B * T
    n_chunks = x.shape[0] // M
    keep = CONV_HIST * B
    kern = functools.partial(_ffn_main_kernel, B=B, T=T, n_chunks=n_chunks,
                             final_norm=final_norm, batch_major_out=batch_major_out)
    stacked = [p["n2g"], p["conv_w"], p["conv_b"]]
    if batch_major_out:
        assert n_chunks >= 2
        o_spec = pl.BlockSpec(memory_space=pl.ANY)
        o_shape = jax.ShapeDtypeStruct((B, n_chunks * T, D_MODEL), _F32)
        slabs = [pltpu.VMEM((2, T, B, D_MODEL), _F32), pltpu.SemaphoreType.DMA((2, B))]
    else:
        o_spec = pl.BlockSpec((M, D_MODEL), lambda c: (c, 0))
        o_shape = jax.ShapeDtypeStruct((n_chunks * M, D_MODEL), _F32)
        slabs = []
    return pl.pallas_call(
        kern,
        grid=(n_chunks,),
        in_specs=[pl.BlockSpec((M, D_MODEL), lambda c: (c, 0)), _whole(chist.shape)]
        + [_layer(a, l) for a in stacked] + [_whole(p["nfg"].shape)]
        + [_whole(a.shape) for a in wb],
        out_specs=[o_spec, _whole((keep, D_FF))],
        out_shape=[o_shape, jax.ShapeDtypeStruct((keep, D_FF), _F32)],
        scratch_shapes=[pltpu.VMEM((keep, D_FF), _F32),
                        ] + [pltpu.VMEM((keep + M // FFN_SUB_BLOCKS, 2 * FF_TILE), _F32)] * FF_SLOTS
        + slabs,
        compiler_params=_compiler_params(),
        name="convffn_main",
    )(x, chist, *stacked, p["nfg"], *wb)


def _ffn_small_kernel(x_ref, conv_ref, n2g_ref, convw_ref, convb_ref, nfg_ref, wg_ref, wv_ref,
                      wdown_ref, *refs, Bs, Bm, Tm, final_norm, n_alias):
    (xo_ref, conv_o_ref, mconv_o_ref, wgv_o_ref, wdown_o_ref,
     hb_ref, acc_ref) = refs[n_alias:]
    Ms, Mm = Bs, Bm * Tm
    jt = pl.program_id(0)

    @pl.when(jt == 0)
    def _():
        hb_ref[...] = _rmsnorm(x_ref[...], n2g_ref[...]).astype(_BF16)
        acc_ref[...] = jnp.zeros(acc_ref.shape, _F32)

    wgv_o_ref[:, 0:FF_TILE] = wg_ref[...].astype(_BF16)
    wgv_o_ref[:, FF_TILE:] = wv_ref[...].astype(_BF16)
    wdown_o_ref[...] = (0.5 * wdown_ref[...]).astype(_BF16)
    hb = hb_ref[...]
    gv = _dot(hb, wgv_o_ref[...])
    g, v = gv[:, 0:FF_TILE], gv[:, FF_TILE:]
    hist = [conv_ref[:, j, :] for j in range(CONV_HIST)]
    new_hist = hist[1:] + [g[0:Ms]]
    for j in range(CONV_HIST):
        conv_o_ref[:, j, :] = new_hist[j]
    g_ext_s = jnp.concatenate(hist + [g[0:Ms]], axis=0)
    g_ext_m = jnp.concatenate([jnp.zeros((CONV_HIST * Bm, FF_TILE), _F32), g[Ms:]], axis=0)
    mconv_o_ref[...] = g_ext_m[Mm:Mm + CONV_HIST * Bm, :]
    conv = jnp.concatenate([_conv_tile(g_ext_s, Bs, Ms, convw_ref[...], convb_ref[...]),
                            _conv_tile(g_ext_m, Bm, Mm, convw_ref[...], convb_ref[...])], axis=0)
    acc_ref[...] += _dot(_gelu2_mul(conv, v).astype(_BF16), wdown_o_ref[...])

    @pl.when(jt == FF_TILES - 1)
    def _():
        xo = x_ref[...] + acc_ref[...]
        if final_norm:
            xo = _rmsnorm(xo, nfg_ref[...])
        xo_ref[...] = xo


def _ffn_small(x, p, l, prev, *, Bs, Bm, Tm, final_norm):
    kern = functools.partial(_ffn_small_kernel, Bs=Bs, Bm=Bm, Tm=Tm, final_norm=final_norm,
                             n_alias=len(prev))
    sds = jax.ShapeDtypeStruct
    tile_of = lambda a, idx: pl.BlockSpec((None,) + idx[0], idx[1])
    n_rows = x.shape[0]
    in_specs = [
        _whole(x.shape),
        pl.BlockSpec((None, Bs, CONV_HIST, FF_TILE), lambda j: (l, 0, 0, j)),
        _layer(p["n2g"], l),
        pl.BlockSpec((None, CONV_W, FF_TILE), lambda j: (l, 0, j)),
        pl.BlockSpec((None, 1, FF_TILE), lambda j: (l, 0, j)),
        _whole(p["nfg"].shape),
        pl.BlockSpec((None, D_MODEL, FF_TILE), lambda j: (l, 0, j)),
        pl.BlockSpec((None, D_MODEL, FF_TILE), lambda j: (l, 0, FF_TILES + j)),
        pl.BlockSpec((None, FF_TILE, D_MODEL), lambda j: (l, j, 0)),
    ] + [pl.BlockSpec(memory_space=pl.ANY)] * len(prev)
    out_shape = [sds(x.shape, _F32), sds(p["state_conv"].shape, _F32),
                 sds((CONV_HIST * Bm, D_FF), _F32), sds((D_MODEL, 2 * D_FF), _BF16),
                 sds((D_FF, D_MODEL), _BF16)]
    out_specs = [
        _whole(x.shape),
        pl.BlockSpec((None, Bs, CONV_HIST, FF_TILE), lambda j: (l, 0, 0, j)),
        pl.BlockSpec((CONV_HIST * Bm, FF_TILE), lambda j: (0, j)),
        pl.BlockSpec((D_MODEL, 2 * FF_TILE), lambda j: (0, j)),
        pl.BlockSpec((FF_TILE, D_MODEL), lambda j: (j, 0)),
    ]
    n_in = 9
    return pl.pallas_call(
        kern,
        grid=(FF_TILES,),
        in_specs=in_specs,
        out_specs=out_specs,
        out_shape=out_shape,
        scratch_shapes=[pltpu.VMEM((n_rows, D_MODEL), _BF16), pltpu.VMEM((n_rows, D_MODEL), _F32)],
        input_output_aliases={n_in + k: 1 + k for k in range(len(prev))},
        compiler_params=_compiler_params(),
        name="convffn_small",
    )(x, p["state_conv"], p["n2g"], p["conv_w"], p["conv_b"], p["nfg"], p["w_up"], p["w_up"],
      p["w_down"], *prev)


def kernel(x_prompt, x_sample, state_pool, state_ssm_re, state_ssm_im, state_conv, meta_tokens,
           norm1_g, w_in, pool_w, pool_scale, ssm_A_re, ssm_A_im, ssm_log_dt, ssm_B_re, ssm_B_im,
           ssm_C_re, ssm_C_im, ssm_D, w_glu, w_out, norm2_g, w_up, conv_w, conv_b, w_down, norm_f_g):
    depth = w_in.shape[0]
    bp, seq, _ = x_prompt.shape
    bs = x_sample.shape[0]
    assert seq % MIXER_CHUNK_STEPS == 0 and seq % FFN_CHUNK_STEPS == 0
    ab_re, ab_im, bbc, cre, cim = _discretise(ssm_A_re, ssm_A_im, ssm_log_dt, ssm_B_re, ssm_B_im,
                                              ssm_C_re, ssm_C_im)
    row = lambda a: a[:, None, :]
    p = dict(state_pool=state_pool.transpose(0, 2, 1, 3), state_conv=state_conv,
             state_re=state_ssm_re.reshape(depth, bs, SSM_FLAT),
             state_im=state_ssm_im.reshape(depth, bs, SSM_FLAT),
             n1g=row(norm1_g), pool_scale=row(pool_scale), a_re=ab_re, a_im=ab_im, bbc=bbc,
             cre=cre, cim=cim, dskip=row(ssm_D), w_in=w_in, pool_w=pool_w, w_glu=w_glu,
             w_out=w_out, n2g=row(norm2_g), conv_w=conv_w, conv_b=row(conv_b), w_up=w_up,
             w_down=w_down, nfg=norm_f_g[None])

    xs = jnp.concatenate([x_sample.reshape(bs, D_MODEL), jnp.repeat(meta_tokens, bp, axis=0)], axis=0)
    small = dict(Bs=bs, Bm=bp, Tm=N_META)
    mix_prev, ffn_prev, meta_state, mix_w, ffn_w = [], [], [], [], []
    for l in range(depth):
        xs, pool_s, re_s, im_s, mp, mr, mi, *wb = _mixer_small(xs, p, l, mix_prev, **small)
        mix_prev = [pool_s, re_s, im_s]
        mix_w.append(wb)
        xs, conv_s, mc, *wb = _ffn_small(xs, p, l, ffn_prev, final_norm=(l == depth - 1), **small)
        ffn_prev = [conv_s]
        ffn_w.append(wb)
        meta_state.append(((mp, mr, mi), mc))
    y_sample = xs[0:bs].reshape(bs, 1, D_MODEL)

    xp = x_prompt
    pool_p, re_p, im_p, conv_p = [], [], [], []
    for l in range(depth):
        mix_state, mc = meta_state[l]
        xp, pp, hr, hi = _mixer_main(xp, mix_state, p, l, mix_w[l], start_pos=N_META,
                                     batch_major_in=(l == 0), B=bp, T=MIXER_CHUNK_STEPS)
        last = l == depth - 1
        xp, cv = _ffn_main(xp, mc, p, l, ffn_w[l], final_norm=last, batch_major_out=last, B=bp,
                           T=FFN_CHUNK_STEPS)
        pool_p.append(pp)
        re_p.append(hr)
        im_p.append(hi)
        conv_p.append(cv)
    st = jnp.stack
    pool_p = st(pool_p).reshape(depth, POOL_KEEP, bp, POOL_WIDTH)[:, 1:].transpose(0, 2, 1, 3)
    conv_p = st(conv_p).reshape(depth, CONV_HIST, bp, D_FF).transpose(0, 2, 1, 3)
    group = lambda h: h.reshape(h.shape[:-1] + (SSM_GROUPS, SSM_STATE))
    return (xp, y_sample, pool_p, group(st(re_p)), group(st(im_p)), conv_p,
            pool_s.transpose(0, 2, 1, 3), group(re_s), group(im_s), conv_s)
```

```python
import functools

import jax
import jax.numpy as jnp
from jax import lax
from jax.experimental import pallas as pl
from jax.experimental.pallas import tpu as pltpu

D_MODEL = 1024
N_META = 16
POOL_WIDTH = 512
POOL_WINDOWS = (2, 4, 8, 16)
POOL_GROUPS = len(POOL_WINDOWS)
POOL_GROUP_IN = 128
POOL_GROUP_OUT = 256
POOL_HIST = 15
POOL_KEEP = 16
SSM_WIDTH = 512
SSM_GROUP_CH = 16
SSM_GROUPS = 32
SSM_STATE = 64
SSM_FLAT = SSM_GROUPS * SSM_STATE
SSM_TILE = 512
SSM_TILES = SSM_FLAT // SSM_TILE
SSM_TILE_GROUPS = SSM_TILE // SSM_STATE
SSM_TILE_CH = SSM_TILE_GROUPS * SSM_GROUP_CH
IN_COLS = POOL_WIDTH + SSM_WIDTH + 2 * D_MODEL
D_FF = 2816
FF_TILE = 256
FF_TILES = D_FF // FF_TILE
FF_LOOKAHEAD = 3
FF_SLOTS = FF_LOOKAHEAD + 1
CONV_W = 3
CONV_HIST = CONV_W - 1
RMS_EPS = 1e-6
PAST_LEN = 16384

V7X_VMEM_LIMIT_BYTES = 56 * 1024 * 1024
MIXER_CHUNK_STEPS = 64
MIXER_SUB_BLOCKS = 1
FFN_CHUNK_STEPS = 64
FFN_SUB_BLOCKS = 2

_F32 = jnp.float32
_BF16 = jnp.bfloat16


def _rmsnorm(x, g):
    return x * lax.rsqrt(jnp.mean(x * x, axis=-1, keepdims=True) + RMS_EPS) * g


def _dot(a, b):
    return jnp.dot(a, b, preferred_element_type=_F32)


def _whole(shape):
    nd = len(shape)
    return pl.BlockSpec(shape, lambda *_: (0,) * nd, pipeline_mode=pl.Buffered(1))


def _layer(arr, l):
    nd = arr.ndim - 1
    return pl.BlockSpec((None,) + arr.shape[1:], lambda *_: (l,) + (0,) * nd,
                        pipeline_mode=pl.Buffered(1))


def _compiler_params():
    return pltpu.CompilerParams(dimension_semantics=("arbitrary",),
                                vmem_limit_bytes=V7X_VMEM_LIMIT_BYTES)


def _disc_kernel(are_ref, aim_ref, ldt_ref, bre_ref, bim_ref, ctre_ref, ctim_ref,
                 abre_ref, abim_ref, bbc_ref, cre_ref, cim_ref):
    lam_re = are_ref[...]
    lam_im = aim_ref[...]
    dt = jnp.exp(ldt_ref[...])
    mag = jnp.exp(lam_re * dt)
    ab_re = mag * jnp.cos(lam_im * dt)
    ab_im = mag * jnp.sin(lam_im * dt)
    den = lam_re * lam_re + lam_im * lam_im
    nr = ab_re - 1.0
    ni = ab_im
    q_re = (nr * lam_re + ni * lam_im) / den
    q_im = (ni * lam_re - nr * lam_im) / den
    b_re = bre_ref[...]
    b_im = bim_ref[...]
    abre_ref[...] = ab_re
    abim_ref[...] = ab_im
    bb_re = (q_re * b_re - q_im * b_im).astype(_BF16)
    bb_im = (q_re * b_im + q_im * b_re).astype(_BF16)
    ct_re = ctre_ref[...].astype(_BF16)
    ct_im = ctim_ref[...].astype(_BF16)
    bbc_ref[...] = jnp.zeros(bbc_ref.shape, _BF16)
    cre_ref[...] = jnp.zeros(cre_ref.shape, _BF16)
    cim_ref[...] = jnp.zeros(cim_ref.shape, _BF16)
    for i in range(SSM_TILES):
        for gl in range(SSM_TILE_GROUPS):
            g = i * SSM_TILE_GROUPS + gl
            ch = slice(gl * SSM_GROUP_CH, (gl + 1) * SSM_GROUP_CH)
            st = slice(gl * SSM_STATE, (gl + 1) * SSM_STATE)
            st_im = slice(SSM_TILE + gl * SSM_STATE, SSM_TILE + (gl + 1) * SSM_STATE)
            src_st = slice(g * SSM_STATE, (g + 1) * SSM_STATE)
            src_ch = slice(g * SSM_GROUP_CH, (g + 1) * SSM_GROUP_CH)
            bbc_ref[i, ch, st] = bb_re[:, src_st]
            bbc_ref[i, ch, st_im] = bb_im[:, src_st]
            cre_ref[i, st, ch] = ct_re[:, src_ch]
            cim_ref[i, st, ch] = ct_im[:, src_ch]


def _discretise(a_re, a_im, log_dt, b_re, b_im, c_re, c_im):
    depth = a_re.shape[0]
    flat = lambda a: a.reshape(depth, 1, SSM_FLAT)
    ldt = jnp.broadcast_to(log_dt[:, :, None], (depth, SSM_GROUPS, SSM_STATE))
    bt = lambda b: b.transpose(0, 3, 1, 2).reshape(depth, SSM_GROUP_CH, SSM_FLAT)
    ct = lambda c: c.transpose(0, 3, 1, 2).reshape(depth, SSM_STATE, SSM_WIDTH)
    ins = [flat(a_re), flat(a_im), flat(ldt), bt(b_re), bt(b_im), ct(c_re), ct(c_im)]
    out_shape = [jax.ShapeDtypeStruct((depth, 1, SSM_FLAT), _F32)] * 2 + [
        jax.ShapeDtypeStruct((depth, SSM_TILES, SSM_TILE_CH, 2 * SSM_TILE), _BF16),
        jax.ShapeDtypeStruct((depth, SSM_TILES, SSM_TILE, SSM_TILE_CH), _BF16),
        jax.ShapeDtypeStruct((depth, SSM_TILES, SSM_TILE, SSM_TILE_CH), _BF16)]
    per_layer = lambda s: pl.BlockSpec((None,) + s.shape[1:],
                                       lambda l: (l,) + (0,) * (len(s.shape) - 1))
    return pl.pallas_call(
        _disc_kernel,
        grid=(depth,),
        in_specs=[per_layer(a) for a in ins],
        out_specs=[per_layer(s) for s in out_shape],
        out_shape=out_shape,
        name="s5_discretise",
    )(*ins)


def _pool_branch(ext, n_hist, B, T, pos0, poolw, pscale):
    M = B * T
    base = n_hist * B
    pos = pos0 + lax.broadcasted_iota(jnp.int32, (M, 1), 0) // B
    tiles = []
    for g, w in enumerate(POOL_WINDOWS):
        ls = slice(g * POOL_GROUP_IN, (g + 1) * POOL_GROUP_IN)
        tok = ext[base:base + M, ls]
        win_sum = tok
        for k in range(1, w):
            win_sum = win_sum + ext[base - k * B:base - k * B + M, ls]
        cnt = jnp.minimum(w, pos + 1).astype(_F32)
        d = win_sum / cnt - tok
        tiles.append(_dot(d.astype(_BF16), poolw(g)))
    return jnp.concatenate(tiles, axis=-1) * pscale


def _ssm_inputs(u_ssm_b, bbc_ref):
    return [_dot(u_ssm_b[:, i * SSM_TILE_CH:(i + 1) * SSM_TILE_CH], bbc_ref[i])
            for i in range(SSM_TILES)]


def _ssm_tile(i, bu, h_re, h_im, B, T, are_ref, aim_ref, cre_ref, cim_ref):
    sl = slice(i * SSM_TILE, (i + 1) * SSM_TILE)
    a_re = jnp.broadcast_to(are_ref[:, sl], (B, SSM_TILE))
    a_im = jnp.broadcast_to(aim_ref[:, sl], (B, SSM_TILE))
    hs_re, hs_im = [], []
    for t in range(T):
        rows = slice(t * B, (t + 1) * B)
        n_re = a_re * h_re - a_im * h_im + bu[rows, 0:SSM_TILE]
        n_im = a_re * h_im + a_im * h_re + bu[rows, SSM_TILE:2 * SSM_TILE]
        h_re, h_im = n_re, n_im
        hs_re.append(h_re)
        hs_im.append(h_im)
    y = (_dot(jnp.concatenate(hs_re, axis=0).astype(_BF16), cre_ref[i])
         - _dot(jnp.concatenate(hs_im, axis=0).astype(_BF16), cim_ref[i]))
    return y, h_re, h_im


GLU_TILE = 256


def _glu_weight(wglu_f32):
    tiles = []
    for k in range(D_MODEL // GLU_TILE):
        tiles.append(wglu_f32[:, k * GLU_TILE:(k + 1) * GLU_TILE])
        tiles.append(wglu_f32[:, D_MODEL + k * GLU_TILE:D_MODEL + (k + 1) * GLU_TILE])
    return jnp.concatenate(tiles, axis=-1).astype(_BF16)


def _glu(y_tiles, u_ssm, dskip, wglu):
    y_s = jnp.concatenate(y_tiles, axis=-1) + dskip * u_ssm
    zz = _dot(jax.nn.gelu(y_s).astype(_BF16), wglu)
    outs = []
    for k in range(D_MODEL // GLU_TILE):
        lo = 2 * k * GLU_TILE
        outs.append(zz[:, lo:lo + GLU_TILE] * jax.nn.sigmoid(zz[:, lo + GLU_TILE:lo + 2 * GLU_TILE]))
    return jnp.concatenate(outs, axis=-1)


def _mixer_main_kernel(x_ref, hist_ref, h0re_ref, h0im_ref, n1g_ref, pscale_ref, are_ref, aim_ref,
                       bbc_ref, cre_ref, cim_ref, dskip_ref, win_ref, poolw_ref, wglu_ref, wout_ref,
                       xo_ref, poolst_ref, hre_o_ref, him_o_ref,
                       pool_ref, hst_ref, *scratch_refs, B, T, start_pos, n_chunks,
                       batch_major_in):
    M = B * T
    c = pl.program_id(0)

    @pl.when(c == 0)
    def _():
        pool_ref[...] = hist_ref[...]
        hst_ref[0] = h0re_ref[...]
        hst_ref[1] = h0im_ref[...]

    Ts = T // MIXER_SUB_BLOCKS
    Ms = B * Ts
    bu_refs, dma_refs = scratch_refs[:SSM_TILES], scratch_refs[SSM_TILES:]
    staged_rows = pl.ds(pl.multiple_of(jnp.minimum(c, 0), B), Ms)
    gate_cols = 2 * D_MODEL // SSM_TILES

    if batch_major_in:
        xt_ref, x_sem = dma_refs
        slot = lax.rem(c, 2)

        def x_copy(chunk, buf, b):
            return pltpu.make_async_copy(x_ref.at[b, pl.ds(chunk * T, T), :],
                                         xt_ref.at[buf, :, b, :], x_sem.at[buf, b])

        @pl.when(c == 0)
        def _():
            for b in range(B):
                x_copy(0, 0, b).start()

        @pl.when(c + 1 < n_chunks)
        def _():
            for b in range(B):
                x_copy(c + 1, 1 - slot, b).start()

        for b in range(B):
            x_copy(c, slot, b).wait()

    for s in range(MIXER_SUB_BLOCKS):
        if batch_major_in:
            x = xt_ref[slot, s * Ts:(s + 1) * Ts].reshape(Ms, D_MODEL)
        else:
            x = x_ref[s * Ms:(s + 1) * Ms, :]
        hb = (x * n1g_ref[...]).astype(_BF16)
        r = lax.rsqrt(jnp.mean(x * x, axis=-1, keepdims=True) + RMS_EPS)

        u_ssm = r * _dot(hb, win_ref[:, POOL_WIDTH:POOL_WIDTH + SSM_WIDTH])
        u_ssm_b = u_ssm.astype(_BF16)
        def bbar(i):
            bu_refs[i][staged_rows, :] = _dot(u_ssm_b[:, i * SSM_TILE_CH:(i + 1) * SSM_TILE_CH],
                                              bbc_ref[i])

        def gate(i):
            lo = POOL_WIDTH + SSM_WIDTH + i * gate_cols
            return jax.nn.sigmoid(r * _dot(hb, win_ref[:, lo:lo + gate_cols]))

        def scan_c(i):
            sl = slice(i * SSM_TILE, (i + 1) * SSM_TILE)
            y, h_re, h_im = _ssm_tile(i, bu_refs[i], hst_ref[0, :, sl], hst_ref[1, :, sl], B, Ts,
                                      are_ref, aim_ref, cre_ref, cim_ref)
            hst_ref[0, :, sl] = h_re
            hst_ref[1, :, sl] = h_im
            return y

        gates, y_tiles = [], []
        bbar(0)
        u_pool = r * _dot(hb, win_ref[:, 0:POOL_WIDTH])
        bbar(1)
        gates.append(gate(0))
        bbar(2)
        gates.append(gate(1))
        bbar(3)
        y_tiles.append(scan_c(0))
        gates.append(gate(2))
        y_tiles.append(scan_c(1))
        gates.append(gate(3))
        y_tiles.append(scan_c(2))
        ext = jnp.concatenate([pool_ref[...], u_pool], axis=0)
        pool_ref[...] = ext[Ms:Ms + POOL_KEEP * B, :]
        y_pool = _pool_branch(ext, POOL_KEEP, B, Ts, start_pos + c * T + s * Ts,
                              lambda g: poolw_ref[g], pscale_ref[...])
        y_tiles.append(scan_c(3))
        gate_pool = jnp.concatenate(gates[0:SSM_TILES // 2], axis=-1)
        gate_ssm = jnp.concatenate(gates[SSM_TILES // 2:], axis=-1)
        merge_pool = gate_pool * y_pool
        merge = merge_pool + gate_ssm * _glu(y_tiles, u_ssm, dskip_ref[...], wglu_ref[...])
        xo_ref[s * Ms:(s + 1) * Ms, :] = x + _dot(merge.astype(_BF16), wout_ref[...])

    @pl.when(c == n_chunks - 1)
    def _():
        poolst_ref[...] = pool_ref[...]
        hre_o_ref[...] = hst_ref[0]
        him_o_ref[...] = hst_ref[1]


def _mixer_main(x, meta_state, p, l, wb, *, B, T, start_pos, batch_major_in):
    M = B * T
    if batch_major_in:
        n_chunks = x.shape[1] // T
        x_spec = pl.BlockSpec(memory_space=pl.ANY)
        dma_scratch = [pltpu.VMEM((2, T, B, D_MODEL), _F32), pltpu.SemaphoreType.DMA((2, B))]
    else:
        n_chunks = x.shape[0] // M
        x_spec = pl.BlockSpec((M, D_MODEL), lambda c: (c, 0))
        dma_scratch = []
    keep = POOL_KEEP * B
    kern = functools.partial(_mixer_main_kernel, B=B, T=T, start_pos=start_pos,
                             n_chunks=n_chunks, batch_major_in=batch_major_in)
    stacked = [p["n1g"], p["pool_scale"], p["a_re"], p["a_im"], p["bbc"], p["cre"], p["cim"],
               p["dskip"]]
    return pl.pallas_call(
        kern,
        grid=(n_chunks,),
        in_specs=[x_spec] + [_whole(a.shape) for a in meta_state]
        + [_layer(a, l) for a in stacked] + [_whole(a.shape) for a in wb],
        out_specs=[pl.BlockSpec((M, D_MODEL), lambda c: (c, 0)), _whole((keep, POOL_WIDTH)),
                   _whole((B, SSM_FLAT)), _whole((B, SSM_FLAT))],
        out_shape=[jax.ShapeDtypeStruct((n_chunks * M, D_MODEL), _F32),
                   jax.ShapeDtypeStruct((keep, POOL_WIDTH), _F32),
                   jax.ShapeDtypeStruct((B, SSM_FLAT), _F32),
                   jax.ShapeDtypeStruct((B, SSM_FLAT), _F32)],
        scratch_shapes=[pltpu.VMEM((keep, POOL_WIDTH), _F32),
                        pltpu.VMEM((2, B, SSM_FLAT), _F32),
                        ] + [pltpu.VMEM((M // MIXER_SUB_BLOCKS, 2 * SSM_TILE), _F32)] * SSM_TILES
        + dma_scratch,
        compiler_params=_compiler_params(),
        name="mixer_main",
    )(x, *meta_state, *stacked, *wb)


def _mixer_small_kernel(x_ref, pool_ref, sre_ref, sim_ref, n1g_ref, pscale_ref, are_ref, aim_ref,
                        bbc_ref, cre_ref, cim_ref, dskip_ref, win_ref, poolw_ref, wglu_ref,
                        wout_ref, *refs, Bs, Bm, Tm, n_alias):
    (xo_ref, pool_o_ref, sre_o_ref, sim_o_ref, mpool_o_ref, mre_o_ref, mim_o_ref,
     win_o_ref, poolw_o_ref, wglu_o_ref, wout_o_ref) = refs[n_alias:]
    Ms, Mm = Bs, Bm * Tm
    win_o_ref[...] = win_ref[...].astype(_BF16)
    poolw_o_ref[...] = poolw_ref[...].astype(_BF16)
    wglu_o_ref[...] = _glu_weight(wglu_ref[...])
    wout_o_ref[...] = wout_ref[...].astype(_BF16)

    x = x_ref[...]
    hb = _rmsnorm(x, n1g_ref[...]).astype(_BF16)
    proj = _dot(hb, win_o_ref[...])
    u_pool, u_ssm = proj[:, 0:POOL_WIDTH], proj[:, POOL_WIDTH:POOL_WIDTH + SSM_WIDTH]
    gate_pool = jax.nn.sigmoid(proj[:, POOL_WIDTH + SSM_WIDTH:POOL_WIDTH + SSM_WIDTH + D_MODEL])
    gate_ssm = jax.nn.sigmoid(proj[:, POOL_WIDTH + SSM_WIDTH + D_MODEL:])
    bus = _ssm_inputs(u_ssm.astype(_BF16), bbc_ref)
    poolw = lambda g: poolw_o_ref[g]

    hist = [pool_ref[j] for j in range(POOL_HIST)]
    new_hist = hist[1:] + [u_pool[0:Ms]]
    for j in range(POOL_HIST):
        pool_o_ref[j] = new_hist[j]
    y_pool_s = _pool_branch(jnp.concatenate(hist + [u_pool[0:Ms]], axis=0), POOL_HIST, Bs, 1,
                            PAST_LEN, poolw, pscale_ref[...])
    ext_m = jnp.concatenate([jnp.zeros((POOL_KEEP * Bm, POOL_WIDTH), _F32), u_pool[Ms:]], axis=0)
    y_pool_m = _pool_branch(ext_m, POOL_KEEP, Bm, Tm, 0, poolw, pscale_ref[...])
    mpool_o_ref[...] = ext_m[Mm:Mm + POOL_KEEP * Bm, :]
    y_pool = jnp.concatenate([y_pool_s, y_pool_m], axis=0)

    y_tiles = []
    for i in range(SSM_TILES):
        sl = slice(i * SSM_TILE, (i + 1) * SSM_TILE)
        ys, h_re, h_im = _ssm_tile(i, bus[i][0:Ms], sre_ref[sl, :].T, sim_ref[sl, :].T, Bs, 1,
                                   are_ref, aim_ref, cre_ref, cim_ref)
        sre_o_ref[sl, :] = h_re.T
        sim_o_ref[sl, :] = h_im.T
        zero = jnp.zeros((Bm, SSM_TILE), _F32)
        ym, h_re, h_im = _ssm_tile(i, bus[i][Ms:], zero, zero, Bm, Tm,
                                   are_ref, aim_ref, cre_ref, cim_ref)
        mre_o_ref[:, sl] = h_re
        mim_o_ref[:, sl] = h_im
        y_tiles.append(jnp.concatenate([ys, ym], axis=0))
    merge = gate_pool * y_pool + gate_ssm * _glu(y_tiles, u_ssm, dskip_ref[...], wglu_o_ref[...])
    xo_ref[...] = x + _dot(merge.astype(_BF16), wout_o_ref[...])


def _mixer_small(x, p, l, prev, *, Bs, Bm, Tm):
    kern = functools.partial(_mixer_small_kernel, Bs=Bs, Bm=Bm, Tm=Tm, n_alias=len(prev))
    stacked = [p["state_pool"], p["state_re"], p["state_im"], p["n1g"], p["pool_scale"], p["a_re"],
               p["a_im"], p["bbc"], p["cre"], p["cim"], p["dskip"], p["w_in"], p["pool_w"],
               p["w_glu"], p["w_out"]]
    sds = jax.ShapeDtypeStruct
    state_shapes = [sds(p["state_pool"].shape, _F32), sds(p["state_re"].shape, _F32),
                    sds(p["state_im"].shape, _F32)]
    plain_shapes = [sds((POOL_KEEP * Bm, POOL_WIDTH), _F32), sds((Bm, SSM_FLAT), _F32),
                    sds((Bm, SSM_FLAT), _F32), sds((D_MODEL, IN_COLS), _BF16),
                    sds((POOL_GROUPS, POOL_GROUP_IN, POOL_GROUP_OUT), _BF16),
                    sds((SSM_WIDTH, 2 * D_MODEL), _BF16), sds((D_MODEL, D_MODEL), _BF16)]
    n_in = 1 + len(stacked)
    return pl.pallas_call(
        kern,
        grid=(1,),
        in_specs=[_whole(x.shape)] + [_layer(a, l) for a in stacked]
        + [pl.BlockSpec(memory_space=pl.ANY)] * len(prev),
        out_specs=[_whole(x.shape)] + [_layer(s, l) for s in state_shapes]
        + [_whole(s.shape) for s in plain_shapes],
        out_shape=[sds(x.shape, _F32)] + state_shapes + plain_shapes,
        input_output_aliases={n_in + k: 1 + k for k in range(len(prev))},
        compiler_params=_compiler_params(),
        name="mixer_small",
    )(x, *stacked, *prev)


GELU_C1 = 0.7978845608028654
GELU_C2 = GELU_C1 * 0.044715


def _gelu2_mul(x, v):
    t = jnp.tanh(x * (GELU_C1 + GELU_C2 * (x * x)))
    return (x * v) * (1.0 + t)


def _conv_tile(g_ext, B, M, convw, convb):
    conv = convb
    for k in range(CONV_W):
        conv = conv + convw[k:k + 1, :] * g_ext[k * B:k * B + M, :]
    return conv


def _ffn_main_kernel(x_ref, chist_ref, n2g_ref, convw_ref, convb_ref, nfg_ref, wgv_ref,
                     wdown_ref, xo_ref, convst_ref,
                     carry_ref, *scratch_refs, B, T, n_chunks, final_norm, batch_major_out):
    M = B * T
    keep = CONV_HIST * B
    c = pl.program_id(0)

    @pl.when(c == 0)
    def _():
        carry_ref[...] = chist_ref[...]

    Ts = T // FFN_SUB_BLOCKS
    Ms = B * Ts
    gv_refs, dma_refs = scratch_refs[:FF_SLOTS], scratch_refs[FF_SLOTS:]
    if batch_major_out:
        yt_ref, y_sem = dma_refs
        slot_o = lax.rem(c, 2)

        def y_copy(chunk, buf, b):
            return pltpu.make_async_copy(yt_ref.at[buf, :, b, :],
                                         xo_ref.at[b, pl.ds(chunk * T, T), :], y_sem.at[buf, b])

        @pl.when(c >= 2)
        def _():
            for b in range(B):
                y_copy(c - 2, slot_o, b).wait()

    xs = [x_ref[s * Ms:(s + 1) * Ms, :] for s in range(FFN_SUB_BLOCKS)]
    hbs = [_rmsnorm(x, n2g_ref[...]).astype(_BF16) for x in xs]

    staged_rows = pl.ds(pl.multiple_of(keep + jnp.minimum(c, 0), B), Ms)
    n_tiles = FFN_SUB_BLOCKS * FF_TILES

    def up(k):
        s, j = divmod(k, FF_TILES)
        gv_refs[k % FF_SLOTS][staged_rows, :] = _dot(
            hbs[s], wgv_ref[:, 2 * j * FF_TILE:2 * (j + 1) * FF_TILE])

    for k in range(FF_LOOKAHEAD):
        up(k)
    acc = None
    for k in range(n_tiles):
        s, j = divmod(k, FF_TILES)
        fs = slice(j * FF_TILE, (j + 1) * FF_TILE)
        if k + FF_LOOKAHEAD < n_tiles:
            up(k + FF_LOOKAHEAD)
        slot = gv_refs[k % FF_SLOTS]
        slot[0:keep, 0:FF_TILE] = carry_ref[:, fs]
        conv = _conv_tile(slot.at[:, 0:FF_TILE], B, Ms, convw_ref[:, fs], convb_ref[:, fs])
        carry_ref[:, fs] = slot[Ms:Ms + keep, 0:FF_TILE]
        v = slot[keep:keep + Ms, FF_TILE:]
        contrib = _dot(_gelu2_mul(conv, v).astype(_BF16), wdown_ref[fs, :])
        acc = contrib if j == 0 else acc + contrib
        if j == FF_TILES - 1:
            xo = xs[s] + acc
            if final_norm:
                xo = _rmsnorm(xo, nfg_ref[...])
            if batch_major_out:
                yt_ref[slot_o, s * Ts:(s + 1) * Ts] = xo.reshape(Ts, B, D_MODEL)
            else:
                xo_ref[s * Ms:(s + 1) * Ms, :] = xo

    if batch_major_out:
        for b in range(B):
            y_copy(c, slot_o, b).start()

    @pl.when(c == n_chunks - 1)
    def _():
        convst_ref[...] = carry_ref[...]
        if batch_major_out:
            for b in range(B):
                y_copy(c - 1, 1 - slot_o, b).wait()
                y_copy(c, slot_o, b).wait()


def _ffn_main(x, chist, p, l, wb, *, B, T, final_norm, batch_major_out):
    M = B * T
    n_chunks = x.shape[0] // M
    keep = CONV_HIST * B
    kern = functools.partial(_ffn_main_kernel, B=B, T=T, n_chunks=n_chunks,
                             final_norm=final_norm, batch_major_out=batch_major_out)
    stacked = [p["n2g"], p["conv_w"], p["conv_b"]]
    if batch_major_out:
        assert n_chunks >= 2
        o_spec = pl.BlockSpec(memory_space=pl.ANY)
        o_shape = jax.ShapeDtypeStruct((B, n_chunks * T, D_MODEL), _F32)
        dma_scratch = [pltpu.VMEM((2, T, B, D_MODEL), _F32), pltpu.SemaphoreType.DMA((2, B))]
    else:
        o_spec = pl.BlockSpec((M, D_MODEL), lambda c: (c, 0))
        o_shape = jax.ShapeDtypeStruct((n_chunks * M, D_MODEL), _F32)
        dma_scratch = []
    return pl.pallas_call(
        kern,
        grid=(n_chunks,),
        in_specs=[pl.BlockSpec((M, D_MODEL), lambda c: (c, 0)), _whole(chist.shape)]
        + [_layer(a, l) for a in stacked] + [_whole(p["nfg"].shape)]
        + [_whole(a.shape) for a in wb],
        out_specs=[o_spec, _whole((keep, D_FF))],
        out_shape=[o_shape, jax.ShapeDtypeStruct((keep, D_FF), _F32)],
        scratch_shapes=[pltpu.VMEM((keep, D_FF), _F32),
                        ] + [pltpu.VMEM((keep + M // FFN_SUB_BLOCKS, 2 * FF_TILE), _F32)] * FF_SLOTS
        + dma_scratch,
        compiler_params=_compiler_params(),
        name="convffn_main",
    )(x, chist, *stacked, p["nfg"], *wb)


def _ffn_small_kernel(x_ref, conv_ref, n2g_ref, convw_ref, convb_ref, nfg_ref, wg_ref, wv_ref,
                      wdown_ref, *refs, Bs, Bm, Tm, final_norm, n_alias):
    (xo_ref, conv_o_ref, mconv_o_ref, wgv_o_ref, wdown_o_ref,
     hb_ref, acc_ref) = refs[n_alias:]
    Ms, Mm = Bs, Bm * Tm
    jt = pl.program_id(0)

    @pl.when(jt == 0)
    def _():
        hb_ref[...] = _rmsnorm(x_ref[...], n2g_ref[...]).astype(_BF16)
        acc_ref[...] = jnp.zeros(acc_ref.shape, _F32)

    wgv_o_ref[:, 0:FF_TILE] = wg_ref[...].astype(_BF16)
    wgv_o_ref[:, FF_TILE:] = wv_ref[...].astype(_BF16)
    wdown_o_ref[...] = (0.5 * wdown_ref[...]).astype(_BF16)
    hb = hb_ref[...]
    gv = _dot(hb, wgv_o_ref[...])
    g, v = gv[:, 0:FF_TILE], gv[:, FF_TILE:]
    hist = [conv_ref[:, j, :] for j in range(CONV_HIST)]
    new_hist = hist[1:] + [g[0:Ms]]
    for j in range(CONV_HIST):
        conv_o_ref[:, j, :] = new_hist[j]
    g_ext_s = jnp.concatenate(hist + [g[0:Ms]], axis=0)
    g_ext_m = jnp.concatenate([jnp.zeros((CONV_HIST * Bm, FF_TILE), _F32), g[Ms:]], axis=0)
    mconv_o_ref[...] = g_ext_m[Mm:Mm + CONV_HIST * Bm, :]
    conv = jnp.concatenate([_conv_tile(g_ext_s, Bs, Ms, convw_ref[...], convb_ref[...]),
                            _conv_tile(g_ext_m, Bm, Mm, convw_ref[...], convb_ref[...])], axis=0)
    acc_ref[...] += _dot(_gelu2_mul(conv, v).astype(_BF16), wdown_o_ref[...])

    @pl.when(jt == FF_TILES - 1)
    def _():
        xo = x_ref[...] + acc_ref[...]
        if final_norm:
            xo = _rmsnorm(xo, nfg_ref[...])
        xo_ref[...] = xo


def _ffn_small(x, p, l, prev, *, Bs, Bm, Tm, final_norm):
    kern = functools.partial(_ffn_small_kernel, Bs=Bs, Bm=Bm, Tm=Tm, final_norm=final_norm,
                             n_alias=len(prev))
    sds = jax.ShapeDtypeStruct
    tile_of = lambda a, idx: pl.BlockSpec((None,) + idx[0], idx[1])
    n_rows = x.shape[0]
    in_specs = [
        _whole(x.shape),
        pl.BlockSpec((None, Bs, CONV_HIST, FF_TILE), lambda j: (l, 0, 0, j)),
        _layer(p["n2g"], l),
        pl.BlockSpec((None, CONV_W, FF_TILE), lambda j: (l, 0, j)),
        pl.BlockSpec((None, 1, FF_TILE), lambda j: (l, 0, j)),
        _whole(p["nfg"].shape),
        pl.BlockSpec((None, D_MODEL, FF_TILE), lambda j: (l, 0, j)),
        pl.BlockSpec((None, D_MODEL, FF_TILE), lambda j: (l, 0, FF_TILES + j)),
        pl.BlockSpec((None, FF_TILE, D_MODEL), lambda j: (l, j, 0)),
    ] + [pl.BlockSpec(memory_space=pl.ANY)] * len(prev)
    out_shape = [sds(x.shape, _F32), sds(p["state_conv"].shape, _F32),
                 sds((CONV_HIST * Bm, D_FF), _F32), sds((D_MODEL, 2 * D_FF), _BF16),
                 sds((D_FF, D_MODEL), _BF16)]
    out_specs = [
        _whole(x.shape),
        pl.BlockSpec((None, Bs, CONV_HIST, FF_TILE), lambda j: (l, 0, 0, j)),
        pl.BlockSpec((CONV_HIST * Bm, FF_TILE), lambda j: (0, j)),
        pl.BlockSpec((D_MODEL, 2 * FF_TILE), lambda j: (0, j)),
        pl.BlockSpec((FF_TILE, D_MODEL), lambda j: (j, 0)),
    ]
    n_in = 9
    return pl.pallas_call(
        kern,
        grid=(FF_TILES,),
        in_specs=in_specs,
        out_specs=out_specs,
        out_shape=out_shape,
        scratch_shapes=[pltpu.VMEM((n_rows, D_MODEL), _BF16), pltpu.VMEM((n_rows, D_MODEL), _F32)],
        input_output_aliases={n_in + k: 1 + k for k in range(len(prev))},
        compiler_params=_compiler_params(),
        name="convffn_small",
    )(x, p["state_conv"], p["n2g"], p["conv_w"], p["conv_b"], p["nfg"], p["w_up"], p["w_up"],
      p["w_down"], *prev)


def kernel(x_prompt, x_sample, state_pool, state_ssm_re, state_ssm_im, state_conv, meta_tokens,
           norm1_g, w_in, pool_w, pool_scale, ssm_A_re, ssm_A_im, ssm_log_dt, ssm_B_re, ssm_B_im,
           ssm_C_re, ssm_C_im, ssm_D, w_glu, w_out, norm2_g, w_up, conv_w, conv_b, w_down, norm_f_g):
    depth = w_in.shape[0]
    bp, seq, _ = x_prompt.shape
    bs = x_sample.shape[0]
    assert seq % MIXER_CHUNK_STEPS == 0 and seq % FFN_CHUNK_STEPS == 0
    ab_re, ab_im, bbc, cre, cim = _discretise(ssm_A_re, ssm_A_im, ssm_log_dt, ssm_B_re, ssm_B_im,
                                              ssm_C_re, ssm_C_im)
    row = lambda a: a[:, None, :]
    p = dict(state_pool=state_pool.transpose(0, 2, 1, 3), state_conv=state_conv,
             state_re=state_ssm_re.transpose(0, 2, 3, 1).reshape(depth, SSM_FLAT, bs),
             state_im=state_ssm_im.transpose(0, 2, 3, 1).reshape(depth, SSM_FLAT, bs),
             n1g=row(norm1_g), pool_scale=row(pool_scale), a_re=ab_re, a_im=ab_im, bbc=bbc,
             cre=cre, cim=cim, dskip=row(ssm_D), w_in=w_in, pool_w=pool_w, w_glu=w_glu,
             w_out=w_out, n2g=row(norm2_g), conv_w=conv_w, conv_b=row(conv_b), w_up=w_up,
             w_down=w_down, nfg=norm_f_g[None])

    xs = jnp.concatenate([x_sample.reshape(bs, D_MODEL), jnp.repeat(meta_tokens, bp, axis=0)], axis=0)
    small = dict(Bs=bs, Bm=bp, Tm=N_META)
    mix_prev, ffn_prev, meta_state, mix_w, ffn_w = [], [], [], [], []
    for l in range(depth):
        xs, pool_s, re_s, im_s, mp, mr, mi, *wb = _mixer_small(xs, p, l, mix_prev, **small)
        mix_prev = [pool_s, re_s, im_s]
        mix_w.append(wb)
        xs, conv_s, mc, *wb = _ffn_small(xs, p, l, ffn_prev, final_norm=(l == depth - 1), **small)
        ffn_prev = [conv_s]
        ffn_w.append(wb)
        meta_state.append(((mp, mr, mi), mc))
    y_sample = xs[0:bs].reshape(bs, 1, D_MODEL)

    xp = x_prompt
    pool_p, re_p, im_p, conv_p = [], [], [], []
    for l in range(depth):
        mix_state, mc = meta_state[l]
        xp, pp, hr, hi = _mixer_main(xp, mix_state, p, l, mix_w[l], start_pos=N_META,
                                     batch_major_in=(l == 0), B=bp, T=MIXER_CHUNK_STEPS)
        last = l == depth - 1
        xp, cv = _ffn_main(xp, mc, p, l, ffn_w[l], final_norm=last, batch_major_out=last, B=bp,
                           T=FFN_CHUNK_STEPS)
        pool_p.append(pp)
        re_p.append(hr)
        im_p.append(hi)
        conv_p.append(cv)
    st = jnp.stack
    pool_p = st(pool_p).reshape(depth, POOL_KEEP, bp, POOL_WIDTH)[:, 1:].transpose(0, 2, 1, 3)
    conv_p = st(conv_p).reshape(depth, CONV_HIST, bp, D_FF).transpose(0, 2, 1, 3)
    group = lambda h: h.reshape(h.shape[:-1] + (SSM_GROUPS, SSM_STATE))
    sample_ssm = lambda h: h.reshape(depth, SSM_GROUPS, SSM_STATE, bs).transpose(0, 3, 1, 2)
    return (xp, y_sample, pool_p, group(st(re_p)), group(st(im_p)), conv_p,
            pool_s.transpose(0, 2, 1, 3), sample_ssm(re_s), sample_ssm(im_s), conv_s)
```

```python
import functools

import jax
import jax.numpy as jnp
from jax import lax
from jax.experimental import pallas as pl
from jax.experimental.pallas import tpu as pltpu

D_MODEL = 1024
N_META = 16
POOL_WIDTH = 512
POOL_WINDOWS = (2, 4, 8, 16)
POOL_GROUPS = len(POOL_WINDOWS)
POOL_GROUP_IN = 128
POOL_GROUP_OUT = 256
POOL_HIST = 15
POOL_KEEP = 16
SSM_WIDTH = 512
SSM_GROUP_CH = 16
SSM_GROUPS = 32
SSM_STATE = 64
SSM_FLAT = SSM_GROUPS * SSM_STATE
SSM_TILE = 512
SSM_TILES = SSM_FLAT // SSM_TILE
SSM_TILE_GROUPS = SSM_TILE // SSM_STATE
SSM_TILE_CH = SSM_TILE_GROUPS * SSM_GROUP_CH
IN_COLS = POOL_WIDTH + SSM_WIDTH + 2 * D_MODEL
D_FF = 2816
FF_TILE = 256
FF_TILES = D_FF // FF_TILE
FF_LOOKAHEAD = 3
FF_SLOTS = FF_LOOKAHEAD + 1
CONV_W = 3
CONV_HIST = CONV_W - 1
RMS_EPS = 1e-6
PAST_LEN = 16384

V7X_VMEM_LIMIT_BYTES = 56 * 1024 * 1024
MIXER_CHUNK_STEPS = 64
MIXER_SUB_BLOCKS = 1
FFN_CHUNK_STEPS = 64
FFN_SUB_BLOCKS = 2

_F32 = jnp.float32
_BF16 = jnp.bfloat16


def _rmsnorm(x, g):
    return x * lax.rsqrt(jnp.mean(x * x, axis=-1, keepdims=True) + RMS_EPS) * g


def _dot(a, b):
    return jnp.dot(a, b, preferred_element_type=_F32)


def _whole(shape):
    nd = len(shape)
    return pl.BlockSpec(shape, lambda *_: (0,) * nd, pipeline_mode=pl.Buffered(1))


def _layer(arr, l):
    nd = arr.ndim - 1
    return pl.BlockSpec((None,) + arr.shape[1:], lambda *_: (l,) + (0,) * nd,
                        pipeline_mode=pl.Buffered(1))


def _compiler_params():
    return pltpu.CompilerParams(dimension_semantics=("arbitrary",),
                                vmem_limit_bytes=V7X_VMEM_LIMIT_BYTES)


def _disc_kernel(are_ref, aim_ref, ldt_ref, bre_ref, bim_ref, ctre_ref, ctim_ref,
                 abre_ref, abim_ref, bbc_ref, cre_ref, cim_ref):
    lam_re = are_ref[...]
    lam_im = aim_ref[...]
    dt = jnp.exp(ldt_ref[...])
    mag = jnp.exp(lam_re * dt)
    ab_re = mag * jnp.cos(lam_im * dt)
    ab_im = mag * jnp.sin(lam_im * dt)
    den = lam_re * lam_re + lam_im * lam_im
    nr = ab_re - 1.0
    ni = ab_im
    q_re = (nr * lam_re + ni * lam_im) / den
    q_im = (ni * lam_re - nr * lam_im) / den
    b_re = bre_ref[...]
    b_im = bim_ref[...]
    abre_ref[...] = ab_re
    abim_ref[...] = ab_im
    bb_re = (q_re * b_re - q_im * b_im).astype(_BF16)
    bb_im = (q_re * b_im + q_im * b_re).astype(_BF16)
    ct_re = ctre_ref[...].astype(_BF16)
    ct_im = ctim_ref[...].astype(_BF16)
    bbc_ref[...] = jnp.zeros(bbc_ref.shape, _BF16)
    cre_ref[...] = jnp.zeros(cre_ref.shape, _BF16)
    cim_ref[...] = jnp.zeros(cim_ref.shape, _BF16)
    for i in range(SSM_TILES):
        for gl in range(SSM_TILE_GROUPS):
            g = i * SSM_TILE_GROUPS + gl
            ch = slice(gl * SSM_GROUP_CH, (gl + 1) * SSM_GROUP_CH)
            st = slice(gl * SSM_STATE, (gl + 1) * SSM_STATE)
            st_im = slice(SSM_TILE + gl * SSM_STATE, SSM_TILE + (gl + 1) * SSM_STATE)
            src_st = slice(g * SSM_STATE, (g + 1) * SSM_STATE)
            src_ch = slice(g * SSM_GROUP_CH, (g + 1) * SSM_GROUP_CH)
            bbc_ref[i, ch, st] = bb_re[:, src_st]
            bbc_ref[i, ch, st_im] = bb_im[:, src_st]
            cre_ref[i, st, ch] = ct_re[:, src_ch]
            cim_ref[i, st, ch] = ct_im[:, src_ch]


def _discretise(a_re, a_im, log_dt, b_re, b_im, c_re, c_im):
    depth = a_re.shape[0]
    flat = lambda a: a.reshape(depth, 1, SSM_FLAT)
    ldt = jnp.broadcast_to(log_dt[:, :, None], (depth, SSM_GROUPS, SSM_STATE))
    bt = lambda b: b.transpose(0, 3, 1, 2).reshape(depth, SSM_GROUP_CH, SSM_FLAT)
    ct = lambda c: c.transpose(0, 3, 1, 2).reshape(depth, SSM_STATE, SSM_WIDTH)
    ins = [flat(a_re), flat(a_im), flat(ldt), bt(b_re), bt(b_im), ct(c_re), ct(c_im)]
    out_shape = [jax.ShapeDtypeStruct((depth, 1, SSM_FLAT), _F32)] * 2 + [
        jax.ShapeDtypeStruct((depth, SSM_TILES, SSM_TILE_CH, 2 * SSM_TILE), _BF16),
        jax.ShapeDtypeStruct((depth, SSM_TILES, SSM_TILE, SSM_TILE_CH), _BF16),
        jax.ShapeDtypeStruct((depth, SSM_TILES, SSM_TILE, SSM_TILE_CH), _BF16)]
    per_layer = lambda s: pl.BlockSpec((None,) + s.shape[1:],
                                       lambda l: (l,) + (0,) * (len(s.shape) - 1))
    return pl.pallas_call(
        _disc_kernel,
        grid=(depth,),
        in_specs=[per_layer(a) for a in ins],
        out_specs=[per_layer(s) for s in out_shape],
        out_shape=out_shape,
        name="s5_discretise",
    )(*ins)


def _pool_branch(ext, n_hist, B, T, pos0, poolw, pscale):
    M = B * T
    base = n_hist * B
    pos = pos0 + lax.broadcasted_iota(jnp.int32, (M, 1), 0) // B
    tiles = []
    for g, w in enumerate(POOL_WINDOWS):
        ls = slice(g * POOL_GROUP_IN, (g + 1) * POOL_GROUP_IN)
        tok = ext[base:base + M, ls]
        win_sum = tok
        for k in range(1, w):
            win_sum = win_sum + ext[base - k * B:base - k * B + M, ls]
        cnt = jnp.minimum(w, pos + 1).astype(_F32)
        d = win_sum / cnt - tok
        tiles.append(_dot(d.astype(_BF16), poolw(g)))
    return jnp.concatenate(tiles, axis=-1) * pscale


def _ssm_inputs(u_ssm_b, bbc_ref):
    return [_dot(u_ssm_b[:, i * SSM_TILE_CH:(i + 1) * SSM_TILE_CH], bbc_ref[i])
            for i in range(SSM_TILES)]


def _ssm_tile(i, bu, h_re, h_im, B, T, are_ref, aim_ref, cre_ref, cim_ref):
    sl = slice(i * SSM_TILE, (i + 1) * SSM_TILE)
    a_re = jnp.broadcast_to(are_ref[:, sl], (B, SSM_TILE))
    a_im = jnp.broadcast_to(aim_ref[:, sl], (B, SSM_TILE))
    hs_re, hs_im = [], []
    for t in range(T):
        rows = slice(t * B, (t + 1) * B)
        n_re = a_re * h_re - a_im * h_im + bu[rows, 0:SSM_TILE]
        n_im = a_re * h_im + a_im * h_re + bu[rows, SSM_TILE:2 * SSM_TILE]
        h_re, h_im = n_re, n_im
        hs_re.append(h_re)
        hs_im.append(h_im)
    y = (_dot(jnp.concatenate(hs_re, axis=0).astype(_BF16), cre_ref[i])
         - _dot(jnp.concatenate(hs_im, axis=0).astype(_BF16), cim_ref[i]))
    return y, h_re, h_im


GLU_TILE = 256


def _glu_weight(wglu_f32):
    tiles = []
    for k in range(D_MODEL // GLU_TILE):
        tiles.append(wglu_f32[:, k * GLU_TILE:(k + 1) * GLU_TILE])
        tiles.append(wglu_f32[:, D_MODEL + k * GLU_TILE:D_MODEL + (k + 1) * GLU_TILE])
    return jnp.concatenate(tiles, axis=-1).astype(_BF16)


def _glu(y_tiles, u_ssm, dskip, wglu):
    y_s = jnp.concatenate(y_tiles, axis=-1) + dskip * u_ssm
    zz = _dot(jax.nn.gelu(y_s).astype(_BF16), wglu)
    outs = []
    for k in range(D_MODEL // GLU_TILE):
        lo = 2 * k * GLU_TILE
        outs.append(zz[:, lo:lo + GLU_TILE] * jax.nn.sigmoid(zz[:, lo + GLU_TILE:lo + 2 * GLU_TILE]))
    return jnp.concatenate(outs, axis=-1)


def _mixer_main_kernel(x_ref, hist_ref, h0re_ref, h0im_ref, n1g_ref, pscale_ref, are_ref, aim_ref,
                       bbc_ref, cre_ref, cim_ref, dskip_ref, win_ref, poolw_ref, wglu_ref, wout_ref,
                       xo_ref, poolst_ref, hre_o_ref, him_o_ref,
                       pool_ref, hst_ref, *scratch_refs, B, T, start_pos, n_chunks,
                       batch_major_in):
    M = B * T
    c = pl.program_id(0)

    @pl.when(c == 0)
    def _():
        pool_ref[...] = hist_ref[...]
        hst_ref[0] = h0re_ref[...]
        hst_ref[1] = h0im_ref[...]

    Ts = T // MIXER_SUB_BLOCKS
    Ms = B * Ts
    bu_refs, dma_refs = scratch_refs[:SSM_TILES], scratch_refs[SSM_TILES:]
    staged_rows = pl.ds(pl.multiple_of(jnp.minimum(c, 0), B), Ms)
    gate_cols = 2 * D_MODEL // SSM_TILES

    if batch_major_in:
        xt_ref, x_sem = dma_refs
        slot = lax.rem(c, 2)

        def x_copy(chunk, buf, b):
            return pltpu.make_async_copy(x_ref.at[b, pl.ds(chunk * T, T), :],
                                         xt_ref.at[buf, :, b, :], x_sem.at[buf, b])

        @pl.when(c == 0)
        def _():
            for b in range(B):
                x_copy(0, 0, b).start()

        @pl.when(c + 1 < n_chunks)
        def _():
            for b in range(B):
                x_copy(c + 1, 1 - slot, b).start()

        for b in range(B):
            x_copy(c, slot, b).wait()

    for s in range(MIXER_SUB_BLOCKS):
        if batch_major_in:
            x = xt_ref[slot, s * Ts:(s + 1) * Ts].reshape(Ms, D_MODEL)
        else:
            x = x_ref[s * Ms:(s + 1) * Ms, :]
        hb = (x * n1g_ref[...]).astype(_BF16)
        r = lax.rsqrt(jnp.mean(x * x, axis=-1, keepdims=True) + RMS_EPS)

        u_ssm = r * _dot(hb, win_ref[:, POOL_WIDTH:POOL_WIDTH + SSM_WIDTH])
        u_ssm_b = u_ssm.astype(_BF16)
        def bbar(i):
            bu_refs[i][staged_rows, :] = _dot(u_ssm_b[:, i * SSM_TILE_CH:(i + 1) * SSM_TILE_CH],
                                              bbc_ref[i])

        def gate(i):
            lo = POOL_WIDTH + SSM_WIDTH + i * gate_cols
            return jax.nn.sigmoid(r * _dot(hb, win_ref[:, lo:lo + gate_cols]))

        def scan_c(i):
            sl = slice(i * SSM_TILE, (i + 1) * SSM_TILE)
            y, h_re, h_im = _ssm_tile(i, bu_refs[i], hst_ref[0, :, sl], hst_ref[1, :, sl], B, Ts,
                                      are_ref, aim_ref, cre_ref, cim_ref)
            hst_ref[0, :, sl] = h_re
            hst_ref[1, :, sl] = h_im
            return y

        gates, y_tiles = [], []
        u_pool = r * _dot(hb, win_ref[:, 0:POOL_WIDTH])
        bbar(0)
        gates.append(gate(0))
        bbar(1)
        gates.append(gate(1))
        bbar(2)
        bbar(3)
        y_tiles.append(scan_c(0))
        gates.append(gate(2))
        y_tiles.append(scan_c(1))
        gates.append(gate(3))
        y_tiles.append(scan_c(2))
        ext = jnp.concatenate([pool_ref[...], u_pool], axis=0)
        pool_ref[...] = ext[Ms:Ms + POOL_KEEP * B, :]
        y_pool = _pool_branch(ext, POOL_KEEP, B, Ts, start_pos + c * T + s * Ts,
                              lambda g: poolw_ref[g], pscale_ref[...])
        y_tiles.append(scan_c(3))
        gate_pool = jnp.concatenate(gates[0:SSM_TILES // 2], axis=-1)
        gate_ssm = jnp.concatenate(gates[SSM_TILES // 2:], axis=-1)
        merge_pool = gate_pool * y_pool
        merge = merge_pool + gate_ssm * _glu(y_tiles, u_ssm, dskip_ref[...], wglu_ref[...])
        xo_ref[s * Ms:(s + 1) * Ms, :] = x + _dot(merge.astype(_BF16), wout_ref[...])

    @pl.when(c == n_chunks - 1)
    def _():
        poolst_ref[...] = pool_ref[...]
        hre_o_ref[...] = hst_ref[0]
        him_o_ref[...] = hst_ref[1]


def _mixer_main(x, meta_state, p, l, wb, *, B, T, start_pos, batch_major_in):
    M = B * T
    if batch_major_in:
        n_chunks = x.shape[1] // T
        x_spec = pl.BlockSpec(memory_space=pl.ANY)
        dma_scratch = [pltpu.VMEM((2, T, B, D_MODEL), _F32), pltpu.SemaphoreType.DMA((2, B))]
    else:
        n_chunks = x.shape[0] // M
        x_spec = pl.BlockSpec((M, D_MODEL), lambda c: (c, 0))
        dma_scratch = []
    keep = POOL_KEEP * B
    kern = functools.partial(_mixer_main_kernel, B=B, T=T, start_pos=start_pos,
                             n_chunks=n_chunks, batch_major_in=batch_major_in)
    stacked = [p["n1g"], p["pool_scale"], p["a_re"], p["a_im"], p["bbc"], p["cre"], p["cim"],
               p["dskip"]]
    return pl.pallas_call(
        kern,
        grid=(n_chunks,),
        in_specs=[x_spec] + [_whole(a.shape) for a in meta_state]
        + [_layer(a, l) for a in stacked] + [_whole(a.shape) for a in wb],
        out_specs=[pl.BlockSpec((M, D_MODEL), lambda c: (c, 0)), _whole((keep, POOL_WIDTH)),
                   _whole((B, SSM_FLAT)), _whole((B, SSM_FLAT))],
        out_shape=[jax.ShapeDtypeStruct((n_chunks * M, D_MODEL), _F32),
                   jax.ShapeDtypeStruct((keep, POOL_WIDTH), _F32),
                   jax.ShapeDtypeStruct((B, SSM_FLAT), _F32),
                   jax.ShapeDtypeStruct((B, SSM_FLAT), _F32)],
        scratch_shapes=[pltpu.VMEM((keep, POOL_WIDTH), _F32),
                        pltpu.VMEM((2, B, SSM_FLAT), _F32),
                        ] + [pltpu.VMEM((M // MIXER_SUB_BLOCKS, 2 * SSM_TILE), _F32)] * SSM_TILES
        + dma_scratch,
        compiler_params=_compiler_params(),
        name="mixer_main",
    )(x, *meta_state, *stacked, *wb)


def _mixer_small_kernel(x_ref, pool_ref, sre_ref, sim_ref, n1g_ref, pscale_ref, are_ref, aim_ref,
                        bbc_ref, cre_ref, cim_ref, dskip_ref, win_ref, poolw_ref, wglu_ref,
                        wout_ref, *refs, Bs, Bm, Tm, n_alias):
    (xo_ref, pool_o_ref, sre_o_ref, sim_o_ref, mpool_o_ref, mre_o_ref, mim_o_ref,
     win_o_ref, poolw_o_ref, wglu_o_ref, wout_o_ref) = refs[n_alias:]
    Ms, Mm = Bs, Bm * Tm
    win_o_ref[...] = win_ref[...].astype(_BF16)
    poolw_o_ref[...] = poolw_ref[...].astype(_BF16)
    wglu_o_ref[...] = _glu_weight(wglu_ref[...])
    wout_o_ref[...] = wout_ref[...].astype(_BF16)

    x = x_ref[...]
    hb = _rmsnorm(x, n1g_ref[...]).astype(_BF16)
    proj = _dot(hb, win_o_ref[...])
    u_pool, u_ssm = proj[:, 0:POOL_WIDTH], proj[:, POOL_WIDTH:POOL_WIDTH + SSM_WIDTH]
    gate_pool = jax.nn.sigmoid(proj[:, POOL_WIDTH + SSM_WIDTH:POOL_WIDTH + SSM_WIDTH + D_MODEL])
    gate_ssm = jax.nn.sigmoid(proj[:, POOL_WIDTH + SSM_WIDTH + D_MODEL:])
    bus = _ssm_inputs(u_ssm.astype(_BF16), bbc_ref)
    poolw = lambda g: poolw_o_ref[g]

    hist = [pool_ref[j] for j in range(POOL_HIST)]
    new_hist = hist[1:] + [u_pool[0:Ms]]
    for j in range(POOL_HIST):
        pool_o_ref[j] = new_hist[j]
    y_pool_s = _pool_branch(jnp.concatenate(hist + [u_pool[0:Ms]], axis=0), POOL_HIST, Bs, 1,
                            PAST_LEN, poolw, pscale_ref[...])
    ext_m = jnp.concatenate([jnp.zeros((POOL_KEEP * Bm, POOL_WIDTH), _F32), u_pool[Ms:]], axis=0)
    y_pool_m = _pool_branch(ext_m, POOL_KEEP, Bm, Tm, 0, poolw, pscale_ref[...])
    mpool_o_ref[...] = ext_m[Mm:Mm + POOL_KEEP * Bm, :]
    y_pool = jnp.concatenate([y_pool_s, y_pool_m], axis=0)

    y_tiles = []
    for i in range(SSM_TILES):
        sl = slice(i * SSM_TILE, (i + 1) * SSM_TILE)
        ys, h_re, h_im = _ssm_tile(i, bus[i][0:Ms], sre_ref[sl, :].T, sim_ref[sl, :].T, Bs, 1,
                                   are_ref, aim_ref, cre_ref, cim_ref)
        sre_o_ref[sl, :] = h_re.T
        sim_o_ref[sl, :] = h_im.T
        zero = jnp.zeros((Bm, SSM_TILE), _F32)
        ym, h_re, h_im = _ssm_tile(i, bus[i][Ms:], zero, zero, Bm, Tm,
                                   are_ref, aim_ref, cre_ref, cim_ref)
        mre_o_ref[:, sl] = h_re
        mim_o_ref[:, sl] = h_im
        y_tiles.append(jnp.concatenate([ys, ym], axis=0))
    merge = gate_pool * y_pool + gate_ssm * _glu(y_tiles, u_ssm, dskip_ref[...], wglu_o_ref[...])
    xo_ref[...] = x + _dot(merge.astype(_BF16), wout_o_ref[...])


def _mixer_small(x, p, l, prev, *, Bs, Bm, Tm):
    kern = functools.partial(_mixer_small_kernel, Bs=Bs, Bm=Bm, Tm=Tm, n_alias=len(prev))
    stacked = [p["state_pool"], p["state_re"], p["state_im"], p["n1g"], p["pool_scale"], p["a_re"],
               p["a_im"], p["bbc"], p["cre"], p["cim"], p["dskip"], p["w_in"], p["pool_w"],
               p["w_glu"], p["w_out"]]
    sds = jax.ShapeDtypeStruct
    state_shapes = [sds(p["state_pool"].shape, _F32), sds(p["state_re"].shape, _F32),
                    sds(p["state_im"].shape, _F32)]
    plain_shapes = [sds((POOL_KEEP * Bm, POOL_WIDTH), _F32), sds((Bm, SSM_FLAT), _F32),
                    sds((Bm, SSM_FLAT), _F32), sds((D_MODEL, IN_COLS), _BF16),
                    sds((POOL_GROUPS, POOL_GROUP_IN, POOL_GROUP_OUT), _BF16),
                    sds((SSM_WIDTH, 2 * D_MODEL), _BF16), sds((D_MODEL, D_MODEL), _BF16)]
    n_in = 1 + len(stacked)
    return pl.pallas_call(
        kern,
        grid=(1,),
        in_specs=[_whole(x.shape)] + [_layer(a, l) for a in stacked]
        + [pl.BlockSpec(memory_space=pl.ANY)] * len(prev),
        out_specs=[_whole(x.shape)] + [_layer(s, l) for s in state_shapes]
        + [_whole(s.shape) for s in plain_shapes],
        out_shape=[sds(x.shape, _F32)] + state_shapes + plain_shapes,
        input_output_aliases={n_in + k: 1 + k for k in range(len(prev))},
        compiler_params=_compiler_params(),
        name="mixer_small",
    )(x, *stacked, *prev)


GELU_C1 = 0.7978845608028654
GELU_C2 = GELU_C1 * 0.044715


def _gelu2_mul(x, v):
    t = jnp.tanh(x * (GELU_C1 + GELU_C2 * (x * x)))
    return (x * v) * (1.0 + t)


def _conv_tile(g_ext, B, M, convw, convb):
    conv = convb
    for k in range(CONV_W):
        conv = conv + convw[k:k + 1, :] * g_ext[k * B:k * B + M, :]
    return conv


def _ffn_main_kernel(x_ref, chist_ref, n2g_ref, convw_ref, convb_ref, nfg_ref, wgv_ref,
                     wdown_ref, xo_ref, convst_ref,
                     carry_ref, *scratch_refs, B, T, n_chunks, final_norm, batch_major_out):
    M = B * T
    keep = CONV_HIST * B
    c = pl.program_id(0)

    @pl.when(c == 0)
    def _():
        carry_ref[...] = chist_ref[...]

    Ts = T // FFN_SUB_BLOCKS
    Ms = B * Ts
    gv_refs, dma_refs = scratch_refs[:FF_SLOTS], scratch_refs[FF_SLOTS:]
    if batch_major_out:
        yt_ref, y_sem = dma_refs
        slot_o = lax.rem(c, 2)

        def y_copy(chunk, buf, b):
            return pltpu.make_async_copy(yt_ref.at[buf, :, b, :],
                                         xo_ref.at[b, pl.ds(chunk * T, T), :], y_sem.at[buf, b])

        @pl.when(c >= 2)
        def _():
            for b in range(B):
                y_copy(c - 2, slot_o, b).wait()

    xs = [x_ref[s * Ms:(s + 1) * Ms, :] for s in range(FFN_SUB_BLOCKS)]
    hbs = [_rmsnorm(x, n2g_ref[...]).astype(_BF16) for x in xs]

    staged_rows = pl.ds(pl.multiple_of(keep + jnp.minimum(c, 0), B), Ms)
    n_tiles = FFN_SUB_BLOCKS * FF_TILES

    def up(k):
        s, j = divmod(k, FF_TILES)
        gv_refs[k % FF_SLOTS][staged_rows, :] = _dot(
            hbs[s], wgv_ref[:, 2 * j * FF_TILE:2 * (j + 1) * FF_TILE])

    for k in range(FF_LOOKAHEAD):
        up(k)
    acc = None
    for k in range(n_tiles):
        s, j = divmod(k, FF_TILES)
        fs = slice(j * FF_TILE, (j + 1) * FF_TILE)
        if k + FF_LOOKAHEAD < n_tiles:
            up(k + FF_LOOKAHEAD)
        slot = gv_refs[k % FF_SLOTS]
        slot[0:keep, 0:FF_TILE] = carry_ref[:, fs]
        conv = _conv_tile(slot.at[:, 0:FF_TILE], B, Ms, convw_ref[:, fs], convb_ref[:, fs])
        carry_ref[:, fs] = slot[Ms:Ms + keep, 0:FF_TILE]
        v = slot[keep:keep + Ms, FF_TILE:]
        contrib = _dot(_gelu2_mul(conv, v).astype(_BF16), wdown_ref[fs, :])
        acc = contrib if j == 0 else acc + contrib
        if j == FF_TILES - 1:
            xo = xs[s] + acc
            if final_norm:
                xo = _rmsnorm(xo, nfg_ref[...])
            if batch_major_out:
                yt_ref[slot_o, s * Ts:(s + 1) * Ts] = xo.reshape(Ts, B, D_MODEL)
            else:
                xo_ref[s * Ms:(s + 1) * Ms, :] = xo

    if batch_major_out:
        for b in range(B):
            y_copy(c, slot_o, b).start()

    @pl.when(c == n_chunks - 1)
    def _():
        convst_ref[...] = carry_ref[...]
        if batch_major_out:
            for b in range(B):
                y_copy(c - 1, 1 - slot_o, b).wait()
                y_copy(c, slot_o, b).wait()


def _ffn_main(x, chist, p, l, wb, *, B, T, final_norm, batch_major_out):
    M = B * T
    n_chunks = x.shape[0] // M
    keep = CONV_HIST * B
    kern = functools.partial(_ffn_main_kernel, B=B, T=T, n_chunks=n_chunks,
                             final_norm=final_norm, batch_major_out=batch_major_out)
    stacked = [p["n2g"], p["conv_w"], p["conv_b"]]
    if batch_major_out:
        assert n_chunks >= 2
        o_spec = pl.BlockSpec(memory_space=pl.ANY)
        o_shape = jax.ShapeDtypeStruct((B, n_chunks * T, D_MODEL), _F32)
        dma_scratch = [pltpu.VMEM((2, T, B, D_MODEL), _F32), pltpu.SemaphoreType.DMA((2, B))]
    else:
        o_spec = pl.BlockSpec((M, D_MODEL), lambda c: (c, 0))
        o_shape = jax.ShapeDtypeStruct((n_chunks * M, D_MODEL), _F32)
        dma_scratch = []
    return pl.pallas_call(
        kern,
        grid=(n_chunks,),
        in_specs=[pl.BlockSpec((M, D_MODEL), lambda c: (c, 0)), _whole(chist.shape)]
        + [_layer(a, l) for a in stacked] + [_whole(p["nfg"].shape)]
        + [_whole(a.shape) for a in wb],
        out_specs=[o_spec, _whole((keep, D_FF))],
        out_shape=[o_shape, jax.ShapeDtypeStruct((keep, D_FF), _F32)],
        scratch_shapes=[pltpu.VMEM((keep, D_FF), _F32),
                        ] + [pltpu.VMEM((keep + M // FFN_SUB_BLOCKS, 2 * FF_TILE), _F32)] * FF_SLOTS
        + dma_scratch,
        compiler_params=_compiler_params(),
        name="convffn_main",
    )(x, chist, *stacked, p["nfg"], *wb)


def _ffn_small_kernel(x_ref, conv_ref, n2g_ref, convw_ref, convb_ref, nfg_ref, wg_ref, wv_ref,
                      wdown_ref, *refs, Bs, Bm, Tm, final_norm, n_alias):
    (xo_ref, conv_o_ref, mconv_o_ref, wgv_o_ref, wdown_o_ref,
     hb_ref, acc_ref) = refs[n_alias:]
    Ms, Mm = Bs, Bm * Tm
    jt = pl.program_id(0)

    @pl.when(jt == 0)
    def _():
        hb_ref[...] = _rmsnorm(x_ref[...], n2g_ref[...]).astype(_BF16)
        acc_ref[...] = jnp.zeros(acc_ref.shape, _F32)

    wgv_o_ref[:, 0:FF_TILE] = wg_ref[...].astype(_BF16)
    wgv_o_ref[:, FF_TILE:] = wv_ref[...].astype(_BF16)
    wdown_o_ref[...] = (0.5 * wdown_ref[...]).astype(_BF16)
    hb = hb_ref[...]
    gv = _dot(hb, wgv_o_ref[...])
    g, v = gv[:, 0:FF_TILE], gv[:, FF_TILE:]
    hist = [conv_ref[:, j, :] for j in range(CONV_HIST)]
    new_hist = hist[1:] + [g[0:Ms]]
    for j in range(CONV_HIST):
        conv_o_ref[:, j, :] = new_hist[j]
    g_ext_s = jnp.concatenate(hist + [g[0:Ms]], axis=0)
    g_ext_m = jnp.concatenate([jnp.zeros((CONV_HIST * Bm, FF_TILE), _F32), g[Ms:]], axis=0)
    mconv_o_ref[...] = g_ext_m[Mm:Mm + CONV_HIST * Bm, :]
    conv = jnp.concatenate([_conv_tile(g_ext_s, Bs, Ms, convw_ref[...], convb_ref[...]),
                            _conv_tile(g_ext_m, Bm, Mm, convw_ref[...], convb_ref[...])], axis=0)
    acc_ref[...] += _dot(_gelu2_mul(conv, v).astype(_BF16), wdown_o_ref[...])

    @pl.when(jt == FF_TILES - 1)
    def _():
        xo = x_ref[...] + acc_ref[...]
        if final_norm:
            xo = _rmsnorm(xo, nfg_ref[...])
        xo_ref[...] = xo


def _ffn_small(x, p, l, prev, *, Bs, Bm, Tm, final_norm):
    kern = functools.partial(_ffn_small_kernel, Bs=Bs, Bm=Bm, Tm=Tm, final_norm=final_norm,
                             n_alias=len(prev))
    sds = jax.ShapeDtypeStruct
    tile_of = lambda a, idx: pl.BlockSpec((None,) + idx[0], idx[1])
    n_rows = x.shape[0]
    in_specs = [
        _whole(x.shape),
        pl.BlockSpec((None, Bs, CONV_HIST, FF_TILE), lambda j: (l, 0, 0, j)),
        _layer(p["n2g"], l),
        pl.BlockSpec((None, CONV_W, FF_TILE), lambda j: (l, 0, j)),
        pl.BlockSpec((None, 1, FF_TILE), lambda j: (l, 0, j)),
        _whole(p["nfg"].shape),
        pl.BlockSpec((None, D_MODEL, FF_TILE), lambda j: (l, 0, j)),
        pl.BlockSpec((None, D_MODEL, FF_TILE), lambda j: (l, 0, FF_TILES + j)),
        pl.BlockSpec((None, FF_TILE, D_MODEL), lambda j: (l, j, 0)),
    ] + [pl.BlockSpec(memory_space=pl.ANY)] * len(prev)
    out_shape = [sds(x.shape, _F32), sds(p["state_conv"].shape, _F32),
                 sds((CONV_HIST * Bm, D_FF), _F32), sds((D_MODEL, 2 * D_FF), _BF16),
                 sds((D_FF, D_MODEL), _BF16)]
    out_specs = [
        _whole(x.shape),
        pl.BlockSpec((None, Bs, CONV_HIST, FF_TILE), lambda j: (l, 0, 0, j)),
        pl.BlockSpec((CONV_HIST * Bm, FF_TILE), lambda j: (0, j)),
        pl.BlockSpec((D_MODEL, 2 * FF_TILE), lambda j: (0, j)),
        pl.BlockSpec((FF_TILE, D_MODEL), lambda j: (j, 0)),
    ]
    n_in = 9
    return pl.pallas_call(
        kern,
        grid=(FF_TILES,),
        in_specs=in_specs,
        out_specs=out_specs,
        out_shape=out_shape,
        scratch_shapes=[pltpu.VMEM((n_rows, D_MODEL), _BF16), pltpu.VMEM((n_rows, D_MODEL), _F32)],
        input_output_aliases={n_in + k: 1 + k for k in range(len(prev))},
        compiler_params=_compiler_params(),
        name="convffn_small",
    )(x, p["state_conv"], p["n2g"], p["conv_w"], p["conv_b"], p["nfg"], p["w_up"], p["w_up"],
      p["w_down"], *prev)


def kernel(x_prompt, x_sample, state_pool, state_ssm_re, state_ssm_im, state_conv, meta_tokens,
           norm1_g, w_in, pool_w, pool_scale, ssm_A_re, ssm_A_im, ssm_log_dt, ssm_B_re, ssm_B_im,
           ssm_C_re, ssm_C_im, ssm_D, w_glu, w_out, norm2_g, w_up, conv_w, conv_b, w_down, norm_f_g):
    depth = w_in.shape[0]
    bp, seq, _ = x_prompt.shape
    bs = x_sample.shape[0]
    assert seq % MIXER_CHUNK_STEPS == 0 and seq % FFN_CHUNK_STEPS == 0
    ab_re, ab_im, bbc, cre, cim = _discretise(ssm_A_re, ssm_A_im, ssm_log_dt, ssm_B_re, ssm_B_im,
                                              ssm_C_re, ssm_C_im)
    row = lambda a: a[:, None, :]
    p = dict(state_pool=state_pool.transpose(0, 2, 1, 3), state_conv=state_conv,
             state_re=state_ssm_re.transpose(0, 2, 3, 1).reshape(depth, SSM_FLAT, bs),
             state_im=state_ssm_im.transpose(0, 2, 3, 1).reshape(depth, SSM_FLAT, bs),
             n1g=row(norm1_g), pool_scale=row(pool_scale), a_re=ab_re, a_im=ab_im, bbc=bbc,
             cre=cre, cim=cim, dskip=row(ssm_D), w_in=w_in, pool_w=pool_w, w_glu=w_glu,
             w_out=w_out, n2g=row(norm2_g), conv_w=conv_w, conv_b=row(conv_b), w_up=w_up,
             w_down=w_down, nfg=norm_f_g[None])

    xs = jnp.concatenate([x_sample.reshape(bs, D_MODEL), jnp.repeat(meta_tokens, bp, axis=0)], axis=0)
    small = dict(Bs=bs, Bm=bp, Tm=N_META)
    mix_prev = [jnp.zeros(p[k].shape, _F32) for k in ("state_pool", "state_re", "state_im")]
    ffn_prev = [jnp.zeros(p["state_conv"].shape, _F32)]
    meta_state, mix_w, ffn_w = [], [], []
    for l in range(depth):
        xs, pool_s, re_s, im_s, mp, mr, mi, *wb = _mixer_small(xs, p, l, mix_prev, **small)
        mix_prev = [pool_s, re_s, im_s]
        mix_w.append(wb)
        xs, conv_s, mc, *wb = _ffn_small(xs, p, l, ffn_prev, final_norm=(l == depth - 1), **small)
        ffn_prev = [conv_s]
        ffn_w.append(wb)
        meta_state.append(((mp, mr, mi), mc))
    y_sample = xs[0:bs].reshape(bs, 1, D_MODEL)

    xp = x_prompt
    pool_p, re_p, im_p, conv_p = [], [], [], []
    for l in range(depth):
        mix_state, mc = meta_state[l]
        xp, pp, hr, hi = _mixer_main(xp, mix_state, p, l, mix_w[l], start_pos=N_META,
                                     batch_major_in=(l == 0), B=bp, T=MIXER_CHUNK_STEPS)
        last = l == depth - 1
        xp, cv = _ffn_main(xp, mc, p, l, ffn_w[l], final_norm=last, batch_major_out=last, B=bp,
                           T=FFN_CHUNK_STEPS)
        pool_p.append(pp)
        re_p.append(hr)
        im_p.append(hi)
        conv_p.append(cv)
    st = jnp.stack
    pool_p = st(pool_p).reshape(depth, POOL_KEEP, bp, POOL_WIDTH)[:, 1:].transpose(0, 2, 1, 3)
    conv_p = st(conv_p).reshape(depth, CONV_HIST, bp, D_FF).transpose(0, 2, 1, 3)
    group = lambda h: h.reshape(h.shape[:-1] + (SSM_GROUPS, SSM_STATE))
    sample_ssm = lambda h: h.reshape(depth, SSM_GROUPS, SSM_STATE, bs).transpose(0, 3, 1, 2)
    return (xp, y_sample, pool_p, group(st(re_p)), group(st(im_p)), conv_p,
            pool_s.transpose(0, 2, 1, 3), sample_ssm(re_s), sample_ssm(im_s), conv_s)
```

```python
import functools

import jax
import jax.numpy as jnp
from jax import lax
from jax.experimental import pallas as pl
from jax.experimental.pallas import tpu as pltpu

D_MODEL = 1024
N_META = 16
POOL_WIDTH = 512
POOL_WINDOWS = (2, 4, 8, 16)
POOL_GROUPS = len(POOL_WINDOWS)
POOL_GROUP_IN = 128
POOL_GROUP_OUT = 256
POOL_HIST = 15
POOL_KEEP = 16
SSM_WIDTH = 512
SSM_GROUP_CH = 16
SSM_GROUPS = 32
SSM_STATE = 64
SSM_FLAT = SSM_GROUPS * SSM_STATE
SSM_TILE = 512
SSM_TILES = SSM_FLAT // SSM_TILE
SSM_TILE_GROUPS = SSM_TILE // SSM_STATE
SSM_TILE_CH = SSM_TILE_GROUPS * SSM_GROUP_CH
IN_COLS = POOL_WIDTH + SSM_WIDTH + 2 * D_MODEL
D_FF = 2816
FF_TILE = 256
FF_TILES = D_FF // FF_TILE
FF_LOOKAHEAD = 3
FF_SLOTS = FF_LOOKAHEAD + 1
CONV_W = 3
CONV_HIST = CONV_W - 1
RMS_EPS = 1e-6
PAST_LEN = 16384

V7X_VMEM_LIMIT_BYTES = 56 * 1024 * 1024
MIXER_CHUNK_STEPS = 64
MIXER_SUB_BLOCKS = 1
FFN_CHUNK_STEPS = 64
FFN_SUB_BLOCKS = 2

_F32 = jnp.float32
_BF16 = jnp.bfloat16


def _rmsnorm(x, g):
    return x * lax.rsqrt(jnp.mean(x * x, axis=-1, keepdims=True) + RMS_EPS) * g


def _dot(a, b):
    return jnp.dot(a, b, preferred_element_type=_F32)


def _whole(shape):
    nd = len(shape)
    return pl.BlockSpec(shape, lambda *_: (0,) * nd, pipeline_mode=pl.Buffered(1))


def _layer(arr, l):
    nd = arr.ndim - 1
    return pl.BlockSpec((None,) + arr.shape[1:], lambda *_: (l,) + (0,) * nd,
                        pipeline_mode=pl.Buffered(1))


def _compiler_params():
    return pltpu.CompilerParams(dimension_semantics=("arbitrary",),
                                vmem_limit_bytes=V7X_VMEM_LIMIT_BYTES)


def _disc_kernel(are_ref, aim_ref, ldt_ref, bre_ref, bim_ref, ctre_ref, ctim_ref,
                 abre_ref, abim_ref, bbc_ref, cre_ref, cim_ref, *zero_refs):
    for z_ref in zero_refs:
        z_ref[...] = jnp.zeros(z_ref.shape, z_ref.dtype)
    lam_re = are_ref[...]
    lam_im = aim_ref[...]
    dt = jnp.exp(ldt_ref[...])
    mag = jnp.exp(lam_re * dt)
    ab_re = mag * jnp.cos(lam_im * dt)
    ab_im = mag * jnp.sin(lam_im * dt)
    den = lam_re * lam_re + lam_im * lam_im
    nr = ab_re - 1.0
    ni = ab_im
    q_re = (nr * lam_re + ni * lam_im) / den
    q_im = (ni * lam_re - nr * lam_im) / den
    b_re = bre_ref[...]
    b_im = bim_ref[...]
    abre_ref[...] = ab_re
    abim_ref[...] = ab_im
    bb_re = (q_re * b_re - q_im * b_im).astype(_BF16)
    bb_im = (q_re * b_im + q_im * b_re).astype(_BF16)
    ct_re = ctre_ref[...].astype(_BF16)
    ct_im = ctim_ref[...].astype(_BF16)
    bbc_ref[...] = jnp.zeros(bbc_ref.shape, _BF16)
    cre_ref[...] = jnp.zeros(cre_ref.shape, _BF16)
    cim_ref[...] = jnp.zeros(cim_ref.shape, _BF16)
    for i in range(SSM_TILES):
        for gl in range(SSM_TILE_GROUPS):
            g = i * SSM_TILE_GROUPS + gl
            ch = slice(gl * SSM_GROUP_CH, (gl + 1) * SSM_GROUP_CH)
            st = slice(gl * SSM_STATE, (gl + 1) * SSM_STATE)
            st_im = slice(SSM_TILE + gl * SSM_STATE, SSM_TILE + (gl + 1) * SSM_STATE)
            src_st = slice(g * SSM_STATE, (g + 1) * SSM_STATE)
            src_ch = slice(g * SSM_GROUP_CH, (g + 1) * SSM_GROUP_CH)
            bbc_ref[i, ch, st] = bb_re[:, src_st]
            bbc_ref[i, ch, st_im] = bb_im[:, src_st]
            cre_ref[i, st, ch] = ct_re[:, src_ch]
            cim_ref[i, st, ch] = ct_im[:, src_ch]


def _discretise(a_re, a_im, log_dt, b_re, b_im, c_re, c_im, zero_shapes):
    depth = a_re.shape[0]
    flat = lambda a: a.reshape(depth, 1, SSM_FLAT)
    ldt = jnp.broadcast_to(log_dt[:, :, None], (depth, SSM_GROUPS, SSM_STATE))
    bt = lambda b: b.transpose(0, 3, 1, 2).reshape(depth, SSM_GROUP_CH, SSM_FLAT)
    ct = lambda c: c.transpose(0, 3, 1, 2).reshape(depth, SSM_STATE, SSM_WIDTH)
    ins = [flat(a_re), flat(a_im), flat(ldt), bt(b_re), bt(b_im), ct(c_re), ct(c_im)]
    out_shape = [jax.ShapeDtypeStruct((depth, 1, SSM_FLAT), _F32)] * 2 + [
        jax.ShapeDtypeStruct((depth, SSM_TILES, SSM_TILE_CH, 2 * SSM_TILE), _BF16),
        jax.ShapeDtypeStruct((depth, SSM_TILES, SSM_TILE, SSM_TILE_CH), _BF16),
        jax.ShapeDtypeStruct((depth, SSM_TILES, SSM_TILE, SSM_TILE_CH), _BF16)]
    out_shape += [jax.ShapeDtypeStruct(s, _F32) for s in zero_shapes]
    per_layer = lambda s: pl.BlockSpec((None,) + s.shape[1:],
                                       lambda l: (l,) + (0,) * (len(s.shape) - 1))
    return pl.pallas_call(
        _disc_kernel,
        grid=(depth,),
        in_specs=[per_layer(a) for a in ins],
        out_specs=[per_layer(s) for s in out_shape],
        out_shape=out_shape,
        name="s5_discretise",
    )(*ins)


def _pool_branch(ext, n_hist, B, T, pos0, poolw, pscale):
    M = B * T
    base = n_hist * B
    pos = pos0 + lax.broadcasted_iota(jnp.int32, (M, 1), 0) // B
    tiles = []
    for g, w in enumerate(POOL_WINDOWS):
        ls = slice(g * POOL_GROUP_IN, (g + 1) * POOL_GROUP_IN)
        tok = ext[base:base + M, ls]
        win_sum = tok
        for k in range(1, w):
            win_sum = win_sum + ext[base - k * B:base - k * B + M, ls]
        cnt = jnp.minimum(w, pos + 1).astype(_F32)
        d = win_sum / cnt - tok
        tiles.append(_dot(d.astype(_BF16), poolw(g)))
    return jnp.concatenate(tiles, axis=-1) * pscale


def _ssm_inputs(u_ssm_b, bbc_ref):
    return [_dot(u_ssm_b[:, i * SSM_TILE_CH:(i + 1) * SSM_TILE_CH], bbc_ref[i])
            for i in range(SSM_TILES)]


def _ssm_tile(i, bu, h_re, h_im, B, T, are_ref, aim_ref, cre_ref, cim_ref):
    sl = slice(i * SSM_TILE, (i + 1) * SSM_TILE)
    a_re = jnp.broadcast_to(are_ref[:, sl], (B, SSM_TILE))
    a_im = jnp.broadcast_to(aim_ref[:, sl], (B, SSM_TILE))
    hs_re, hs_im = [], []
    for t in range(T):
        rows = slice(t * B, (t + 1) * B)
        n_re = a_re * h_re - a_im * h_im + bu[rows, 0:SSM_TILE]
        n_im = a_re * h_im + a_im * h_re + bu[rows, SSM_TILE:2 * SSM_TILE]
        h_re, h_im = n_re, n_im
        hs_re.append(h_re)
        hs_im.append(h_im)
    y = (_dot(jnp.concatenate(hs_re, axis=0).astype(_BF16), cre_ref[i])
         - _dot(jnp.concatenate(hs_im, axis=0).astype(_BF16), cim_ref[i]))
    return y, h_re, h_im


GLU_TILE = 256


def _glu_weight(wglu_f32):
    tiles = []
    for k in range(D_MODEL // GLU_TILE):
        tiles.append(wglu_f32[:, k * GLU_TILE:(k + 1) * GLU_TILE])
        tiles.append(wglu_f32[:, D_MODEL + k * GLU_TILE:D_MODEL + (k + 1) * GLU_TILE])
    return jnp.concatenate(tiles, axis=-1).astype(_BF16)


def _glu(y_tiles, u_ssm, dskip, wglu):
    y_s = jnp.concatenate(y_tiles, axis=-1) + dskip * u_ssm
    zz = _dot(jax.nn.gelu(y_s).astype(_BF16), wglu)
    outs = []
    for k in range(D_MODEL // GLU_TILE):
        lo = 2 * k * GLU_TILE
        outs.append(zz[:, lo:lo + GLU_TILE] * jax.nn.sigmoid(zz[:, lo + GLU_TILE:lo + 2 * GLU_TILE]))
    return jnp.concatenate(outs, axis=-1)


def _mixer_main_kernel(x_ref, hist_ref, h0re_ref, h0im_ref, n1g_ref, pscale_ref, are_ref, aim_ref,
                       bbc_ref, cre_ref, cim_ref, dskip_ref, win_ref, poolw_ref, wglu_ref, wout_ref,
                       xo_ref, poolst_ref, hre_o_ref, him_o_ref,
                       pool_ref, hst_ref, *scratch_refs, B, T, start_pos, n_chunks,
                       batch_major_in):
    M = B * T
    c = pl.program_id(0)

    @pl.when(c == 0)
    def _():
        pool_ref[...] = hist_ref[...]
        hst_ref[0] = h0re_ref[...]
        hst_ref[1] = h0im_ref[...]

    Ts = T // MIXER_SUB_BLOCKS
    Ms = B * Ts
    bu_refs, dma_refs = scratch_refs[:SSM_TILES], scratch_refs[SSM_TILES:]
    staged_rows = pl.ds(pl.multiple_of(jnp.minimum(c, 0), B), Ms)
    gate_cols = 2 * D_MODEL // SSM_TILES

    if batch_major_in:
        xt_ref, x_sem = dma_refs
        slot = lax.rem(c, 2)

        def x_copy(chunk, buf, b):
            return pltpu.make_async_copy(x_ref.at[b, pl.ds(chunk * T, T), :],
                                         xt_ref.at[buf, :, b, :], x_sem.at[buf, b])

        @pl.when(c == 0)
        def _():
            for b in range(B):
                x_copy(0, 0, b).start()

        @pl.when(c + 1 < n_chunks)
        def _():
            for b in range(B):
                x_copy(c + 1, 1 - slot, b).start()

        for b in range(B):
            x_copy(c, slot, b).wait()

    for s in range(MIXER_SUB_BLOCKS):
        if batch_major_in:
            x = xt_ref[slot, s * Ts:(s + 1) * Ts].reshape(Ms, D_MODEL)
        else:
            x = x_ref[s * Ms:(s + 1) * Ms, :]
        hb = (x * n1g_ref[...]).astype(_BF16)
        r = lax.rsqrt(jnp.mean(x * x, axis=-1, keepdims=True) + RMS_EPS)

        u_ssm = r * _dot(hb, win_ref[:, POOL_WIDTH:POOL_WIDTH + SSM_WIDTH])
        u_ssm_b = u_ssm.astype(_BF16)
        def bbar(i):
            bu_refs[i][staged_rows, :] = _dot(u_ssm_b[:, i * SSM_TILE_CH:(i + 1) * SSM_TILE_CH],
                                              bbc_ref[i])

        def gate(i):
            lo = POOL_WIDTH + SSM_WIDTH + i * gate_cols
            return jax.nn.sigmoid(r * _dot(hb, win_ref[:, lo:lo + gate_cols]))

        def scan_c(i):
            sl = slice(i * SSM_TILE, (i + 1) * SSM_TILE)
            y, h_re, h_im = _ssm_tile(i, bu_refs[i], hst_ref[0, :, sl], hst_ref[1, :, sl], B, Ts,
                                      are_ref, aim_ref, cre_ref, cim_ref)
            hst_ref[0, :, sl] = h_re
            hst_ref[1, :, sl] = h_im
            return y

        gates, y_tiles = [], []
        u_pool = r * _dot(hb, win_ref[:, 0:POOL_WIDTH])
        bbar(0)
        gates.append(gate(0))
        bbar(1)
        gates.append(gate(1))
        bbar(2)
        bbar(3)
        y_tiles.append(scan_c(0))
        gates.append(gate(2))
        y_tiles.append(scan_c(1))
        gates.append(gate(3))
        y_tiles.append(scan_c(2))
        ext = jnp.concatenate([pool_ref[...], u_pool], axis=0)
        pool_ref[...] = ext[Ms:Ms + POOL_KEEP * B, :]
        y_pool = _pool_branch(ext, POOL_KEEP, B, Ts, start_pos + c * T + s * Ts,
                              lambda g: poolw_ref[g], pscale_ref[...])
        y_tiles.append(scan_c(3))
        gate_pool = jnp.concatenate(gates[0:SSM_TILES // 2], axis=-1)
        gate_ssm = jnp.concatenate(gates[SSM_TILES // 2:], axis=-1)
        merge_pool = gate_pool * y_pool
        merge = merge_pool + gate_ssm * _glu(y_tiles, u_ssm, dskip_ref[...], wglu_ref[...])
        xo_ref[s * Ms:(s + 1) * Ms, :] = x + _dot(merge.astype(_BF16), wout_ref[...])

    @pl.when(c == n_chunks - 1)
    def _():
        poolst_ref[...] = pool_ref[...]
        hre_o_ref[...] = hst_ref[0]
        him_o_ref[...] = hst_ref[1]


def _mixer_main(x, meta_state, p, l, wb, *, B, T, start_pos, batch_major_in):
    M = B * T
    if batch_major_in:
        n_chunks = x.shape[1] // T
        x_spec = pl.BlockSpec(memory_space=pl.ANY)
        dma_scratch = [pltpu.VMEM((2, T, B, D_MODEL), _F32), pltpu.SemaphoreType.DMA((2, B))]
    else:
        n_chunks = x.shape[0] // M
        x_spec = pl.BlockSpec((M, D_MODEL), lambda c: (c, 0))
        dma_scratch = []
    keep = POOL_KEEP * B
    kern = functools.partial(_mixer_main_kernel, B=B, T=T, start_pos=start_pos,
                             n_chunks=n_chunks, batch_major_in=batch_major_in)
    stacked = [p["n1g"], p["pool_scale"], p["a_re"], p["a_im"], p["bbc"], p["cre"], p["cim"],
               p["dskip"]]
    return pl.pallas_call(
        kern,
        grid=(n_chunks,),
        in_specs=[x_spec] + [_whole(a.shape) for a in meta_state]
        + [_layer(a, l) for a in stacked] + [_whole(a.shape) for a in wb],
        out_specs=[pl.BlockSpec((M, D_MODEL), lambda c: (c, 0)), _whole((keep, POOL_WIDTH)),
                   _whole((B, SSM_FLAT)), _whole((B, SSM_FLAT))],
        out_shape=[jax.ShapeDtypeStruct((n_chunks * M, D_MODEL), _F32),
                   jax.ShapeDtypeStruct((keep, POOL_WIDTH), _F32),
                   jax.ShapeDtypeStruct((B, SSM_FLAT), _F32),
                   jax.ShapeDtypeStruct((B, SSM_FLAT), _F32)],
        scratch_shapes=[pltpu.VMEM((keep, POOL_WIDTH), _F32),
                        pltpu.VMEM((2, B, SSM_FLAT), _F32),
                        ] + [pltpu.VMEM((M // MIXER_SUB_BLOCKS, 2 * SSM_TILE), _F32)] * SSM_TILES
        + dma_scratch,
        compiler_params=_compiler_params(),
        name="mixer_main",
    )(x, *meta_state, *stacked, *wb)


def _mixer_small_kernel(x_ref, pool_ref, sre_ref, sim_ref, n1g_ref, pscale_ref, are_ref, aim_ref,
                        bbc_ref, cre_ref, cim_ref, dskip_ref, win_ref, poolw_ref, wglu_ref,
                        wout_ref, *refs, Bs, Bm, Tm, n_alias):
    (xo_ref, pool_o_ref, sre_o_ref, sim_o_ref, mpool_o_ref, mre_o_ref, mim_o_ref,
     win_o_ref, poolw_o_ref, wglu_o_ref, wout_o_ref) = refs[n_alias:]
    Ms, Mm = Bs, Bm * Tm
    win_o_ref[...] = win_ref[...].astype(_BF16)
    poolw_o_ref[...] = poolw_ref[...].astype(_BF16)
    wglu_o_ref[...] = _glu_weight(wglu_ref[...])
    wout_o_ref[...] = wout_ref[...].astype(_BF16)

    x = x_ref[...]
    hb = _rmsnorm(x, n1g_ref[...]).astype(_BF16)
    proj = _dot(hb, win_o_ref[...])
    u_pool, u_ssm = proj[:, 0:POOL_WIDTH], proj[:, POOL_WIDTH:POOL_WIDTH + SSM_WIDTH]
    gate_pool = jax.nn.sigmoid(proj[:, POOL_WIDTH + SSM_WIDTH:POOL_WIDTH + SSM_WIDTH + D_MODEL])
    gate_ssm = jax.nn.sigmoid(proj[:, POOL_WIDTH + SSM_WIDTH + D_MODEL:])
    bus = _ssm_inputs(u_ssm.astype(_BF16), bbc_ref)
    poolw = lambda g: poolw_o_ref[g]

    hist = [pool_ref[j] for j in range(POOL_HIST)]
    new_hist = hist[1:] + [u_pool[0:Ms]]
    for j in range(POOL_HIST):
        pool_o_ref[j] = new_hist[j]
    y_pool_s = _pool_branch(jnp.concatenate(hist + [u_pool[0:Ms]], axis=0), POOL_HIST, Bs, 1,
                            PAST_LEN, poolw, pscale_ref[...])
    ext_m = jnp.concatenate([jnp.zeros((POOL_KEEP * Bm, POOL_WIDTH), _F32), u_pool[Ms:]], axis=0)
    y_pool_m = _pool_branch(ext_m, POOL_KEEP, Bm, Tm, 0, poolw, pscale_ref[...])
    mpool_o_ref[...] = ext_m[Mm:Mm + POOL_KEEP * Bm, :]
    y_pool = jnp.concatenate([y_pool_s, y_pool_m], axis=0)

    y_tiles = []
    for i in range(SSM_TILES):
        sl = slice(i * SSM_TILE, (i + 1) * SSM_TILE)
        ys, h_re, h_im = _ssm_tile(i, bus[i][0:Ms], sre_ref[sl, :].T, sim_ref[sl, :].T, Bs, 1,
                                   are_ref, aim_ref, cre_ref, cim_ref)
        sre_o_ref[sl, :] = h_re.T
        sim_o_ref[sl, :] = h_im.T
        zero = jnp.zeros((Bm, SSM_TILE), _F32)
        ym, h_re, h_im = _ssm_tile(i, bus[i][Ms:], zero, zero, Bm, Tm,
                                   are_ref, aim_ref, cre_ref, cim_ref)
        mre_o_ref[:, sl] = h_re
        mim_o_ref[:, sl] = h_im
        y_tiles.append(jnp.concatenate([ys, ym], axis=0))
    merge = gate_pool * y_pool + gate_ssm * _glu(y_tiles, u_ssm, dskip_ref[...], wglu_o_ref[...])
    xo_ref[...] = x + _dot(merge.astype(_BF16), wout_o_ref[...])


def _mixer_small(x, p, l, prev, *, Bs, Bm, Tm):
    kern = functools.partial(_mixer_small_kernel, Bs=Bs, Bm=Bm, Tm=Tm, n_alias=len(prev))
    stacked = [p["state_pool"], p["state_re"], p["state_im"], p["n1g"], p["pool_scale"], p["a_re"],
               p["a_im"], p["bbc"], p["cre"], p["cim"], p["dskip"], p["w_in"], p["pool_w"],
               p["w_glu"], p["w_out"]]
    sds = jax.ShapeDtypeStruct
    state_shapes = [sds(p["state_pool"].shape, _F32), sds(p["state_re"].shape, _F32),
                    sds(p["state_im"].shape, _F32)]
    plain_shapes = [sds((POOL_KEEP * Bm, POOL_WIDTH), _F32), sds((Bm, SSM_FLAT), _F32),
                    sds((Bm, SSM_FLAT), _F32), sds((D_MODEL, IN_COLS), _BF16),
                    sds((POOL_GROUPS, POOL_GROUP_IN, POOL_GROUP_OUT), _BF16),
                    sds((SSM_WIDTH, 2 * D_MODEL), _BF16), sds((D_MODEL, D_MODEL), _BF16)]
    n_in = 1 + len(stacked)
    return pl.pallas_call(
        kern,
        grid=(1,),
        in_specs=[_whole(x.shape)] + [_layer(a, l) for a in stacked]
        + [pl.BlockSpec(memory_space=pl.ANY)] * len(prev),
        out_specs=[_whole(x.shape)] + [_layer(s, l) for s in state_shapes]
        + [_whole(s.shape) for s in plain_shapes],
        out_shape=[sds(x.shape, _F32)] + state_shapes + plain_shapes,
        input_output_aliases={n_in + k: 1 + k for k in range(len(prev))},
        compiler_params=_compiler_params(),
        name="mixer_small",
    )(x, *stacked, *prev)


GELU_C1 = 0.7978845608028654
GELU_C2 = GELU_C1 * 0.044715


def _gelu2_mul(x, v):
    t = jnp.tanh(x * (GELU_C1 + GELU_C2 * (x * x)))
    return (x * v) * (1.0 + t)


def _conv_tile(g_ext, B, M, convw, convb):
    conv = convb
    for k in range(CONV_W):
        conv = conv + convw[k:k + 1, :] * g_ext[k * B:k * B + M, :]
    return conv


def _ffn_main_kernel(x_ref, chist_ref, n2g_ref, convw_ref, convb_ref, nfg_ref, wgv_ref,
                     wdown_ref, xo_ref, convst_ref,
                     carry_ref, *scratch_refs, B, T, n_chunks, final_norm, batch_major_out):
    M = B * T
    keep = CONV_HIST * B
    c = pl.program_id(0)

    @pl.when(c == 0)
    def _():
        carry_ref[...] = chist_ref[...]

    Ts = T // FFN_SUB_BLOCKS
    Ms = B * Ts
    gv_refs, dma_refs = scratch_refs[:FF_SLOTS], scratch_refs[FF_SLOTS:]
    if batch_major_out:
        yt_ref, y_sem = dma_refs
        slot_o = lax.rem(c, 2)

        def y_copy(chunk, buf, b):
            return pltpu.make_async_copy(yt_ref.at[buf, :, b, :],
                                         xo_ref.at[b, pl.ds(chunk * T, T), :], y_sem.at[buf, b])

        @pl.when(c >= 2)
        def _():
            for b in range(B):
                y_copy(c - 2, slot_o, b).wait()

    xs = [x_ref[s * Ms:(s + 1) * Ms, :] for s in range(FFN_SUB_BLOCKS)]
    hbs = [_rmsnorm(x, n2g_ref[...]).astype(_BF16) for x in xs]

    staged_rows = pl.ds(pl.multiple_of(keep + jnp.minimum(c, 0), B), Ms)
    n_tiles = FFN_SUB_BLOCKS * FF_TILES

    def up(k):
        s, j = divmod(k, FF_TILES)
        gv_refs[k % FF_SLOTS][staged_rows, :] = _dot(
            hbs[s], wgv_ref[:, 2 * j * FF_TILE:2 * (j + 1) * FF_TILE])

    for k in range(FF_LOOKAHEAD):
        up(k)
    acc = None
    for k in range(n_tiles):
        s, j = divmod(k, FF_TILES)
        fs = slice(j * FF_TILE, (j + 1) * FF_TILE)
        if k + FF_LOOKAHEAD < n_tiles:
            up(k + FF_LOOKAHEAD)
        slot = gv_refs[k % FF_SLOTS]
        slot[0:keep, 0:FF_TILE] = carry_ref[:, fs]
        conv = _conv_tile(slot.at[:, 0:FF_TILE], B, Ms, convw_ref[:, fs], convb_ref[:, fs])
        carry_ref[:, fs] = slot[Ms:Ms + keep, 0:FF_TILE]
        v = slot[keep:keep + Ms, FF_TILE:]
        contrib = _dot(_gelu2_mul(conv, v).astype(_BF16), wdown_ref[fs, :])
        acc = contrib if j == 0 else acc + contrib
        if j == FF_TILES - 1:
            xo = xs[s] + acc
            if final_norm:
                xo = _rmsnorm(xo, nfg_ref[...])
            if batch_major_out:
                yt_ref[slot_o, s * Ts:(s + 1) * Ts] = xo.reshape(Ts, B, D_MODEL)
            else:
                xo_ref[s * Ms:(s + 1) * Ms, :] = xo

    if batch_major_out:
        for b in range(B):
            y_copy(c, slot_o, b).start()

    @pl.when(c == n_chunks - 1)
    def _():
        convst_ref[...] = carry_ref[...]
        if batch_major_out:
            for b in range(B):
                y_copy(c - 1, 1 - slot_o, b).wait()
                y_copy(c, slot_o, b).wait()


def _ffn_main(x, chist, p, l, wb, *, B, T, final_norm, batch_major_out):
    M = B * T
    n_chunks = x.shape[0] // M
    keep = CONV_HIST * B
    kern = functools.partial(_ffn_main_kernel, B=B, T=T, n_chunks=n_chunks,
                             final_norm=final_norm, batch_major_out=batch_major_out)
    stacked = [p["n2g"], p["conv_w"], p["conv_b"]]
    if batch_major_out:
        assert n_chunks >= 2
        o_spec = pl.BlockSpec(memory_space=pl.ANY)
        o_shape = jax.ShapeDtypeStruct((B, n_chunks * T, D_MODEL), _F32)
        dma_scratch = [pltpu.VMEM((2, T, B, D_MODEL), _F32), pltpu.SemaphoreType.DMA((2, B))]
    else:
        o_spec = pl.BlockSpec((M, D_MODEL), lambda c: (c, 0))
        o_shape = jax.ShapeDtypeStruct((n_chunks * M, D_MODEL), _F32)
        dma_scratch = []
    return pl.pallas_call(
        kern,
        grid=(n_chunks,),
        in_specs=[pl.BlockSpec((M, D_MODEL), lambda c: (c, 0)), _whole(chist.shape)]
        + [_layer(a, l) for a in stacked] + [_whole(p["nfg"].shape)]
        + [_whole(a.shape) for a in wb],
        out_specs=[o_spec, _whole((keep, D_FF))],
        out_shape=[o_shape, jax.ShapeDtypeStruct((keep, D_FF), _F32)],
        scratch_shapes=[pltpu.VMEM((keep, D_FF), _F32),
                        ] + [pltpu.VMEM((keep + M // FFN_SUB_BLOCKS, 2 * FF_TILE), _F32)] * FF_SLOTS
        + dma_scratch,
        compiler_params=_compiler_params(),
        name="convffn_main",
    )(x, chist, *stacked, p["nfg"], *wb)


def _ffn_small_kernel(x_ref, conv_ref, n2g_ref, convw_ref, convb_ref, nfg_ref, wg_ref, wv_ref,
                      wdown_ref, *refs, Bs, Bm, Tm, final_norm, n_alias):
    (xo_ref, conv_o_ref, mconv_o_ref, wgv_o_ref, wdown_o_ref,
     hb_ref, acc_ref) = refs[n_alias:]
    Ms, Mm = Bs, Bm * Tm
    jt = pl.program_id(0)

    @pl.when(jt == 0)
    def _():
        hb_ref[...] = _rmsnorm(x_ref[...], n2g_ref[...]).astype(_BF16)
        acc_ref[...] = jnp.zeros(acc_ref.shape, _F32)

    wgv_o_ref[:, 0:FF_TILE] = wg_ref[...].astype(_BF16)
    wgv_o_ref[:, FF_TILE:] = wv_ref[...].astype(_BF16)
    wdown_o_ref[...] = (0.5 * wdown_ref[...]).astype(_BF16)
    hb = hb_ref[...]
    gv = _dot(hb, wgv_o_ref[...])
    g, v = gv[:, 0:FF_TILE], gv[:, FF_TILE:]
    hist = [conv_ref[:, j, :] for j in range(CONV_HIST)]
    new_hist = hist[1:] + [g[0:Ms]]
    for j in range(CONV_HIST):
        conv_o_ref[:, j, :] = new_hist[j]
    g_ext_s = jnp.concatenate(hist + [g[0:Ms]], axis=0)
    g_ext_m = jnp.concatenate([jnp.zeros((CONV_HIST * Bm, FF_TILE), _F32), g[Ms:]], axis=0)
    mconv_o_ref[...] = g_ext_m[Mm:Mm + CONV_HIST * Bm, :]
    conv = jnp.concatenate([_conv_tile(g_ext_s, Bs, Ms, convw_ref[...], convb_ref[...]),
                            _conv_tile(g_ext_m, Bm, Mm, convw_ref[...], convb_ref[...])], axis=0)
    acc_ref[...] += _dot(_gelu2_mul(conv, v).astype(_BF16), wdown_o_ref[...])

    @pl.when(jt == FF_TILES - 1)
    def _():
        xo = x_ref[...] + acc_ref[...]
        if final_norm:
            xo = _rmsnorm(xo, nfg_ref[...])
        xo_ref[...] = xo


def _ffn_small(x, p, l, prev, *, Bs, Bm, Tm, final_norm):
    kern = functools.partial(_ffn_small_kernel, Bs=Bs, Bm=Bm, Tm=Tm, final_norm=final_norm,
                             n_alias=len(prev))
    sds = jax.ShapeDtypeStruct
    tile_of = lambda a, idx: pl.BlockSpec((None,) + idx[0], idx[1])
    n_rows = x.shape[0]
    in_specs = [
        _whole(x.shape),
        pl.BlockSpec((None, Bs, CONV_HIST, FF_TILE), lambda j: (l, 0, 0, j)),
        _layer(p["n2g"], l),
        pl.BlockSpec((None, CONV_W, FF_TILE), lambda j: (l, 0, j)),
        pl.BlockSpec((None, 1, FF_TILE), lambda j: (l, 0, j)),
        _whole(p["nfg"].shape),
        pl.BlockSpec((None, D_MODEL, FF_TILE), lambda j: (l, 0, j)),
        pl.BlockSpec((None, D_MODEL, FF_TILE), lambda j: (l, 0, FF_TILES + j)),
        pl.BlockSpec((None, FF_TILE, D_MODEL), lambda j: (l, j, 0)),
    ] + [pl.BlockSpec(memory_space=pl.ANY)] * len(prev)
    out_shape = [sds(x.shape, _F32), sds(p["state_conv"].shape, _F32),
                 sds((CONV_HIST * Bm, D_FF), _F32), sds((D_MODEL, 2 * D_FF), _BF16),
                 sds((D_FF, D_MODEL), _BF16)]
    out_specs = [
        _whole(x.shape),
        pl.BlockSpec((None, Bs, CONV_HIST, FF_TILE), lambda j: (l, 0, 0, j)),
        pl.BlockSpec((CONV_HIST * Bm, FF_TILE), lambda j: (0, j)),
        pl.BlockSpec((D_MODEL, 2 * FF_TILE), lambda j: (0, j)),
        pl.BlockSpec((FF_TILE, D_MODEL), lambda j: (j, 0)),
    ]
    n_in = 9
    return pl.pallas_call(
        kern,
        grid=(FF_TILES,),
        in_specs=in_specs,
        out_specs=out_specs,
        out_shape=out_shape,
        scratch_shapes=[pltpu.VMEM((n_rows, D_MODEL), _BF16), pltpu.VMEM((n_rows, D_MODEL), _F32)],
        input_output_aliases={n_in + k: 1 + k for k in range(len(prev))},
        compiler_params=_compiler_params(),
        name="convffn_small",
    )(x, p["state_conv"], p["n2g"], p["conv_w"], p["conv_b"], p["nfg"], p["w_up"], p["w_up"],
      p["w_down"], *prev)


def kernel(x_prompt, x_sample, state_pool, state_ssm_re, state_ssm_im, state_conv, meta_tokens,
           norm1_g, w_in, pool_w, pool_scale, ssm_A_re, ssm_A_im, ssm_log_dt, ssm_B_re, ssm_B_im,
           ssm_C_re, ssm_C_im, ssm_D, w_glu, w_out, norm2_g, w_up, conv_w, conv_b, w_down, norm_f_g):
    depth = w_in.shape[0]
    bp, seq, _ = x_prompt.shape
    bs = x_sample.shape[0]
    assert seq % MIXER_CHUNK_STEPS == 0 and seq % FFN_CHUNK_STEPS == 0
    state_shapes = [(depth, POOL_HIST, bs, POOL_WIDTH), (depth, SSM_FLAT, bs), (depth, SSM_FLAT, bs),
                    state_conv.shape]
    ab_re, ab_im, bbc, cre, cim, *new_states = _discretise(
        ssm_A_re, ssm_A_im, ssm_log_dt, ssm_B_re, ssm_B_im, ssm_C_re, ssm_C_im, state_shapes)
    row = lambda a: a[:, None, :]
    p = dict(state_pool=state_pool.transpose(0, 2, 1, 3), state_conv=state_conv,
             state_re=state_ssm_re.transpose(0, 2, 3, 1).reshape(depth, SSM_FLAT, bs),
             state_im=state_ssm_im.transpose(0, 2, 3, 1).reshape(depth, SSM_FLAT, bs),
             n1g=row(norm1_g), pool_scale=row(pool_scale), a_re=ab_re, a_im=ab_im, bbc=bbc,
             cre=cre, cim=cim, dskip=row(ssm_D), w_in=w_in, pool_w=pool_w, w_glu=w_glu,
             w_out=w_out, n2g=row(norm2_g), conv_w=conv_w, conv_b=row(conv_b), w_up=w_up,
             w_down=w_down, nfg=norm_f_g[None])

    xs = jnp.concatenate([x_sample.reshape(bs, D_MODEL), jnp.repeat(meta_tokens, bp, axis=0)], axis=0)
    small = dict(Bs=bs, Bm=bp, Tm=N_META)
    mix_prev, ffn_prev = new_states[0:3], new_states[3:4]
    meta_state, mix_w, ffn_w = [], [], []
    for l in range(depth):
        xs, pool_s, re_s, im_s, mp, mr, mi, *wb = _mixer_small(xs, p, l, mix_prev, **small)
        mix_prev = [pool_s, re_s, im_s]
        mix_w.append(wb)
        xs, conv_s, mc, *wb = _ffn_small(xs, p, l, ffn_prev, final_norm=(l == depth - 1), **small)
        ffn_prev = [conv_s]
        ffn_w.append(wb)
        meta_state.append(((mp, mr, mi), mc))
    y_sample = xs[0:bs].reshape(bs, 1, D_MODEL)

    xp = x_prompt
    pool_p, re_p, im_p, conv_p = [], [], [], []
    for l in range(depth):
        mix_state, mc = meta_state[l]
        xp, pp, hr, hi = _mixer_main(xp, mix_state, p, l, mix_w[l], start_pos=N_META,
                                     batch_major_in=(l == 0), B=bp, T=MIXER_CHUNK_STEPS)
        last = l == depth - 1
        xp, cv = _ffn_main(xp, mc, p, l, ffn_w[l], final_norm=last, batch_major_out=last, B=bp,
                           T=FFN_CHUNK_STEPS)
        pool_p.append(pp)
        re_p.append(hr)
        im_p.append(hi)
        conv_p.append(cv)
    st = jnp.stack
    pool_p = st(pool_p).reshape(depth, POOL_KEEP, bp, POOL_WIDTH)[:, 1:].transpose(0, 2, 1, 3)
    conv_p = st(conv_p).reshape(depth, CONV_HIST, bp, D_FF).transpose(0, 2, 1, 3)
    group = lambda h: h.reshape(h.shape[:-1] + (SSM_GROUPS, SSM_STATE))
    sample_ssm = lambda h: h.reshape(depth, SSM_GROUPS, SSM_STATE, bs).transpose(0, 3, 1, 2)
    return (xp, y_sample, pool_p, group(st(re_p)), group(st(im_p)), conv_p,
            pool_s.transpose(0, 2, 1, 3), sample_ssm(re_s), sample_ssm(im_s), conv_s)
```

```python
import functools

import jax
import jax.numpy as jnp
from jax import lax
from jax.experimental import pallas as pl
from jax.experimental.pallas import tpu as pltpu

D_MODEL = 1024
N_META = 16
POOL_WIDTH = 512
POOL_WINDOWS = (2, 4, 8, 16)
POOL_GROUPS = len(POOL_WINDOWS)
POOL_GROUP_IN = 128
POOL_GROUP_OUT = 256
POOL_HIST = 15
POOL_KEEP = 16
SSM_WIDTH = 512
SSM_GROUP_CH = 16
SSM_GROUPS = 32
SSM_STATE = 64
SSM_FLAT = SSM_GROUPS * SSM_STATE
SSM_TILE = 512
SSM_TILES = SSM_FLAT // SSM_TILE
SSM_TILE_GROUPS = SSM_TILE // SSM_STATE
SSM_TILE_CH = SSM_TILE_GROUPS * SSM_GROUP_CH
IN_COLS = POOL_WIDTH + SSM_WIDTH + 2 * D_MODEL
D_FF = 2816
FF_TILE = 256
FF_TILES = D_FF // FF_TILE
FF_LOOKAHEAD = 4
FF_SLOTS = FF_LOOKAHEAD + 1
CONV_W = 3
CONV_HIST = CONV_W - 1
W_RING = 3
META_G_ROW0 = 8
RMS_EPS = 1e-6
PAST_LEN = 16384

V7X_VMEM_LIMIT_BYTES = 56 * 1024 * 1024
MIXER_CHUNK_STEPS = 64
MIXER_SUB_BLOCKS = 1
FFN_CHUNK_STEPS = 64
FFN_SUB_BLOCKS = 2

_F32 = jnp.float32
_BF16 = jnp.bfloat16


def _rmsnorm(x, g):
    return x * lax.rsqrt(jnp.mean(x * x, axis=-1, keepdims=True) + RMS_EPS) * g


def _dot(a, b):
    return jnp.dot(a, b, preferred_element_type=_F32)


def _whole(shape):
    nd = len(shape)
    return pl.BlockSpec(shape, lambda *_: (0,) * nd, pipeline_mode=pl.Buffered(1))


def _layer(arr, l):
    nd = arr.ndim - 1
    return pl.BlockSpec((None,) + arr.shape[1:], lambda *_: (l,) + (0,) * nd,
                        pipeline_mode=pl.Buffered(1))


def _compiler_params():
    return pltpu.CompilerParams(dimension_semantics=("arbitrary",),
                                vmem_limit_bytes=V7X_VMEM_LIMIT_BYTES)


def _disc_kernel(are_ref, aim_ref, ldt_ref, bre_ref, bim_ref, ctre_ref, ctim_ref,
                 abre_ref, abim_ref, bbc_ref, cre_ref, cim_ref, *zero_refs):
    for z_ref in zero_refs:
        z_ref[...] = jnp.zeros(z_ref.shape, z_ref.dtype)
    lam_re = are_ref[...]
    lam_im = aim_ref[...]
    dt = jnp.exp(ldt_ref[...])
    mag = jnp.exp(lam_re * dt)
    ab_re = mag * jnp.cos(lam_im * dt)
    ab_im = mag * jnp.sin(lam_im * dt)
    den = lam_re * lam_re + lam_im * lam_im
    nr = ab_re - 1.0
    ni = ab_im
    q_re = (nr * lam_re + ni * lam_im) / den
    q_im = (ni * lam_re - nr * lam_im) / den
    b_re = bre_ref[...]
    b_im = bim_ref[...]
    abre_ref[...] = ab_re
    abim_ref[...] = ab_im
    bb_re = (q_re * b_re - q_im * b_im).astype(_BF16)
    bb_im = (q_re * b_im + q_im * b_re).astype(_BF16)
    ct_re = ctre_ref[...].astype(_BF16)
    ct_im = ctim_ref[...].astype(_BF16)
    bbc_ref[...] = jnp.zeros(bbc_ref.shape, _BF16)
    cre_ref[...] = jnp.zeros(cre_ref.shape, _BF16)
    cim_ref[...] = jnp.zeros(cim_ref.shape, _BF16)
    for i in range(SSM_TILES):
        for gl in range(SSM_TILE_GROUPS):
            g = i * SSM_TILE_GROUPS + gl
            ch = slice(gl * SSM_GROUP_CH, (gl + 1) * SSM_GROUP_CH)
            st = slice(gl * SSM_STATE, (gl + 1) * SSM_STATE)
            st_im = slice(SSM_TILE + gl * SSM_STATE, SSM_TILE + (gl + 1) * SSM_STATE)
            src_st = slice(g * SSM_STATE, (g + 1) * SSM_STATE)
            src_ch = slice(g * SSM_GROUP_CH, (g + 1) * SSM_GROUP_CH)
            bbc_ref[i, ch, st] = bb_re[:, src_st]
            bbc_ref[i, ch, st_im] = bb_im[:, src_st]
            cre_ref[i, st, ch] = ct_re[:, src_ch]
            cim_ref[i, st, ch] = ct_im[:, src_ch]


def _discretise(a_re, a_im, log_dt, b_re, b_im, c_re, c_im, zero_shapes):
    depth = a_re.shape[0]
    flat = lambda a: a.reshape(depth, 1, SSM_FLAT)
    ldt = jnp.broadcast_to(log_dt[:, :, None], (depth, SSM_GROUPS, SSM_STATE))
    bt = lambda b: b.transpose(0, 3, 1, 2).reshape(depth, SSM_GROUP_CH, SSM_FLAT)
    ct = lambda c: c.transpose(0, 3, 1, 2).reshape(depth, SSM_STATE, SSM_WIDTH)
    ins = [flat(a_re), flat(a_im), flat(ldt), bt(b_re), bt(b_im), ct(c_re), ct(c_im)]
    out_shape = [jax.ShapeDtypeStruct((depth, 1, SSM_FLAT), _F32)] * 2 + [
        jax.ShapeDtypeStruct((depth, SSM_TILES, SSM_TILE_CH, 2 * SSM_TILE), _BF16),
        jax.ShapeDtypeStruct((depth, SSM_TILES, SSM_TILE, SSM_TILE_CH), _BF16),
        jax.ShapeDtypeStruct((depth, SSM_TILES, SSM_TILE, SSM_TILE_CH), _BF16)]
    out_shape += [jax.ShapeDtypeStruct(s, _F32) for s in zero_shapes]
    per_layer = lambda s: pl.BlockSpec((None,) + s.shape[1:],
                                       lambda l: (l,) + (0,) * (len(s.shape) - 1))
    return pl.pallas_call(
        _disc_kernel,
        grid=(depth,),
        in_specs=[per_layer(a) for a in ins],
        out_specs=[per_layer(s) for s in out_shape],
        out_shape=out_shape,
        name="s5_discretise",
    )(*ins)


def _pool_branch(ext, n_hist, B, T, pos0, poolw, pscale):
    M = B * T
    base = n_hist * B
    pos = pos0 + lax.broadcasted_iota(jnp.int32, (M, 1), 0) // B
    tiles = []
    for g, w in enumerate(POOL_WINDOWS):
        ls = slice(g * POOL_GROUP_IN, (g + 1) * POOL_GROUP_IN)
        tok = ext[base:base + M, ls]
        win_sum = tok
        for k in range(1, w):
            win_sum = win_sum + ext[base - k * B:base - k * B + M, ls]
        cnt = jnp.minimum(w, pos + 1).astype(_F32)
        d = win_sum / cnt - tok
        tiles.append(_dot(d.astype(_BF16), poolw(g)))
    return jnp.concatenate(tiles, axis=-1) * pscale


def _ssm_inputs(u_ssm_b, bbc_ref):
    return [_dot(u_ssm_b[:, i * SSM_TILE_CH:(i + 1) * SSM_TILE_CH], bbc_ref[i])
            for i in range(SSM_TILES)]


def _ssm_tile(i, bu, h_re, h_im, B, T, are_ref, aim_ref, cre_ref, cim_ref):
    sl = slice(i * SSM_TILE, (i + 1) * SSM_TILE)
    a_re = jnp.broadcast_to(are_ref[:, sl], (B, SSM_TILE))
    a_im = jnp.broadcast_to(aim_ref[:, sl], (B, SSM_TILE))
    hs_re, hs_im = [], []
    for t in range(T):
        rows = slice(t * B, (t + 1) * B)
        n_re = a_re * h_re - a_im * h_im + bu[rows, 0:SSM_TILE]
        n_im = a_re * h_im + a_im * h_re + bu[rows, SSM_TILE:2 * SSM_TILE]
        h_re, h_im = n_re, n_im
        hs_re.append(h_re)
        hs_im.append(h_im)
    y = (_dot(jnp.concatenate(hs_re, axis=0).astype(_BF16), cre_ref[i])
         - _dot(jnp.concatenate(hs_im, axis=0).astype(_BF16), cim_ref[i]))
    return y, h_re, h_im


GLU_TILE = 256


def _glu_weight(wglu_f32):
    tiles = []
    for k in range(D_MODEL // GLU_TILE):
        tiles.append(wglu_f32[:, k * GLU_TILE:(k + 1) * GLU_TILE])
        tiles.append(wglu_f32[:, D_MODEL + k * GLU_TILE:D_MODEL + (k + 1) * GLU_TILE])
    return jnp.concatenate(tiles, axis=-1).astype(_BF16)


def _glu(y_tiles, u_ssm, dskip, wglu):
    y_s = jnp.concatenate(y_tiles, axis=-1) + dskip * u_ssm
    zz = _dot(jax.nn.gelu(y_s).astype(_BF16), wglu)
    outs = []
    for k in range(D_MODEL // GLU_TILE):
        lo = 2 * k * GLU_TILE
        outs.append(zz[:, lo:lo + GLU_TILE] * jax.nn.sigmoid(zz[:, lo + GLU_TILE:lo + 2 * GLU_TILE]))
    return jnp.concatenate(outs, axis=-1)


def _mixer_main_kernel(x_ref, hist_ref, h0re_ref, h0im_ref, n1g_ref, pscale_ref, are_ref, aim_ref,
                       bbc_ref, cre_ref, cim_ref, dskip_ref, win_ref, poolw_ref, wglu_ref, wout_ref,
                       xo_ref, poolst_ref, hre_o_ref, him_o_ref,
                       pool_ref, hst_ref, *scratch_refs, B, T, start_pos, n_chunks,
                       batch_major_in):
    c = pl.program_id(0)

    @pl.when(c == 0)
    def _():
        pool_ref[...] = hist_ref[...]
        hst_ref[0] = h0re_ref[...]
        hst_ref[1] = h0im_ref[...]

    Ts = T // MIXER_SUB_BLOCKS
    Ms = B * Ts
    bu_refs, dma_refs = scratch_refs[:SSM_TILES], scratch_refs[SSM_TILES:]
    staged_rows = pl.ds(pl.multiple_of(jnp.minimum(c, 0), B), Ms)
    gate_cols = 2 * D_MODEL // SSM_TILES

    if batch_major_in:
        xt_ref, x_sem = dma_refs
        slot = lax.rem(c, 2)

        def x_copy(chunk, buf, b):
            return pltpu.make_async_copy(x_ref.at[b, pl.ds(chunk * T, T), :],
                                         xt_ref.at[buf, :, b, :], x_sem.at[buf, b])

        @pl.when(c == 0)
        def _():
            for b in range(B):
                x_copy(0, 0, b).start()

        @pl.when(c + 1 < n_chunks)
        def _():
            for b in range(B):
                x_copy(c + 1, 1 - slot, b).start()

        for b in range(B):
            x_copy(c, slot, b).wait()

    for s in range(MIXER_SUB_BLOCKS):
        if batch_major_in:
            x = xt_ref[slot, s * Ts:(s + 1) * Ts].reshape(Ms, D_MODEL)
        else:
            x = x_ref[s * Ms:(s + 1) * Ms, :]
        hb = (x * n1g_ref[...]).astype(_BF16)
        r = lax.rsqrt(jnp.mean(x * x, axis=-1, keepdims=True) + RMS_EPS)

        u_ssm = r * _dot(hb, win_ref[:, POOL_WIDTH:POOL_WIDTH + SSM_WIDTH])
        u_ssm_b = u_ssm.astype(_BF16)
        def bbar(i):
            bu_refs[i][staged_rows, :] = _dot(u_ssm_b[:, i * SSM_TILE_CH:(i + 1) * SSM_TILE_CH],
                                              bbc_ref[i])

        def gate(i):
            lo = POOL_WIDTH + SSM_WIDTH + i * gate_cols
            return jax.nn.sigmoid(r * _dot(hb, win_ref[:, lo:lo + gate_cols]))

        def scan_c(i):
            sl = slice(i * SSM_TILE, (i + 1) * SSM_TILE)
            y, h_re, h_im = _ssm_tile(i, bu_refs[i], hst_ref[0, :, sl], hst_ref[1, :, sl], B, Ts,
                                      are_ref, aim_ref, cre_ref, cim_ref)
            hst_ref[0, :, sl] = h_re
            hst_ref[1, :, sl] = h_im
            return y

        gates, y_tiles = [], []
        u_pool = r * _dot(hb, win_ref[:, 0:POOL_WIDTH])
        bbar(0)
        gates.append(gate(0))
        bbar(1)
        gates.append(gate(1))
        bbar(2)
        bbar(3)
        y_tiles.append(scan_c(0))
        gates.append(gate(2))
        y_tiles.append(scan_c(1))
        gates.append(gate(3))
        y_tiles.append(scan_c(2))
        ext = jnp.concatenate([pool_ref[...], u_pool], axis=0)
        pool_ref[...] = ext[Ms:Ms + POOL_KEEP * B, :]
        y_pool = _pool_branch(ext, POOL_KEEP, B, Ts, start_pos + c * T + s * Ts,
                              lambda g: poolw_ref[g], pscale_ref[...])
        y_tiles.append(scan_c(3))
        gate_pool = jnp.concatenate(gates[0:SSM_TILES // 2], axis=-1)
        gate_ssm = jnp.concatenate(gates[SSM_TILES // 2:], axis=-1)
        merge_pool = gate_pool * y_pool
        merge = merge_pool + gate_ssm * _glu(y_tiles, u_ssm, dskip_ref[...], wglu_ref[...])
        xo_ref[s * Ms:(s + 1) * Ms, :] = x + _dot(merge.astype(_BF16), wout_ref[...])

    @pl.when(c == n_chunks - 1)
    def _():
        poolst_ref[...] = pool_ref[...]
        hre_o_ref[...] = hst_ref[0]
        him_o_ref[...] = hst_ref[1]


def _mixer_main(x, meta_state, p, l, wb, *, B, T, start_pos, batch_major_in):
    M = B * T
    if batch_major_in:
        n_chunks = x.shape[1] // T
        x_spec = pl.BlockSpec(memory_space=pl.ANY)
        dma_scratch = [pltpu.VMEM((2, T, B, D_MODEL), _F32), pltpu.SemaphoreType.DMA((2, B))]
    else:
        n_chunks = x.shape[0] // M
        x_spec = pl.BlockSpec((M, D_MODEL), lambda c: (c, 0))
        dma_scratch = []
    keep = POOL_KEEP * B
    kern = functools.partial(_mixer_main_kernel, B=B, T=T, start_pos=start_pos,
                             n_chunks=n_chunks, batch_major_in=batch_major_in)
    stacked = [p["n1g"], p["pool_scale"], p["a_re"], p["a_im"], p["bbc"], p["cre"], p["cim"],
               p["dskip"]]
    return pl.pallas_call(
        kern,
        grid=(n_chunks,),
        in_specs=[x_spec] + [_whole(a.shape) for a in meta_state]
        + [_layer(a, l) for a in stacked] + [_whole(a.shape) for a in wb],
        out_specs=[pl.BlockSpec((M, D_MODEL), lambda c: (c, 0)), _whole((keep, POOL_WIDTH)),
                   _whole((B, SSM_FLAT)), _whole((B, SSM_FLAT))],
        out_shape=[jax.ShapeDtypeStruct((n_chunks * M, D_MODEL), _F32),
                   jax.ShapeDtypeStruct((keep, POOL_WIDTH), _F32),
                   jax.ShapeDtypeStruct((B, SSM_FLAT), _F32),
                   jax.ShapeDtypeStruct((B, SSM_FLAT), _F32)],
        scratch_shapes=[pltpu.VMEM((keep, POOL_WIDTH), _F32),
                        pltpu.VMEM((2, B, SSM_FLAT), _F32),
                        ] + [pltpu.VMEM((M // MIXER_SUB_BLOCKS, 2 * SSM_TILE), _F32)] * SSM_TILES
        + dma_scratch,
        compiler_params=_compiler_params(),
        name="mixer_main",
    )(x, *meta_state, *stacked, *wb)


def _mixer_small_kernel(x_ref, pool_ref, sre_ref, sim_ref, n1g_ref, pscale_ref, are_ref, aim_ref,
                        bbc_ref, cre_ref, cim_ref, dskip_ref, win_ref, poolw_ref, wglu_ref,
                        wout_ref, *refs, Bs, Bm, Tm, n_alias):
    (xo_ref, pool_o_ref, sre_o_ref, sim_o_ref, mpool_o_ref, mre_o_ref, mim_o_ref,
     win_o_ref, poolw_o_ref, wglu_o_ref, wout_o_ref, mext_ref, mbu_ref, mhs_ref) = refs[n_alias:]
    Ms = Bs
    win_o_ref[...] = win_ref[...].astype(_BF16)
    poolw_o_ref[...] = poolw_ref[...].astype(_BF16)
    wglu_o_ref[...] = _glu_weight(wglu_ref[...])
    wout_o_ref[...] = wout_ref[...].astype(_BF16)

    x = x_ref[...]
    hb = _rmsnorm(x, n1g_ref[...]).astype(_BF16)
    proj = _dot(hb, win_o_ref[...])
    u_pool, u_ssm = proj[:, 0:POOL_WIDTH], proj[:, POOL_WIDTH:POOL_WIDTH + SSM_WIDTH]
    gate_pool = jax.nn.sigmoid(proj[:, POOL_WIDTH + SSM_WIDTH:POOL_WIDTH + SSM_WIDTH + D_MODEL])
    gate_ssm = jax.nn.sigmoid(proj[:, POOL_WIDTH + SSM_WIDTH + D_MODEL:])
    bus = _ssm_inputs(u_ssm.astype(_BF16), bbc_ref)
    poolw = lambda g: poolw_o_ref[g]

    hist = [pool_ref[j] for j in range(POOL_HIST)]
    new_hist = hist[1:] + [u_pool[0:Ms]]
    for j in range(POOL_HIST):
        pool_o_ref[j] = new_hist[j]
    y_pool_s = _pool_branch(jnp.concatenate(hist + [u_pool[0:Ms]], axis=0), POOL_HIST, Bs, 1,
                            PAST_LEN, poolw, pscale_ref[...])
    mext_ref[0:POOL_KEEP, :] = jnp.zeros((POOL_KEEP, POOL_WIDTH), _F32)
    mext_ref[POOL_KEEP:POOL_KEEP + Tm, :] = u_pool[Ms:]
    y_pool_m = _pool_branch(mext_ref, POOL_KEEP, 1, Tm, 0, poolw, pscale_ref[...])
    for t in range(POOL_KEEP):
        mpool_o_ref[t * Bm:(t + 1) * Bm, :] = jnp.broadcast_to(
            mext_ref[Tm + t:Tm + t + 1, :], (Bm, POOL_WIDTH))
    y_pool = jnp.concatenate([y_pool_s, y_pool_m], axis=0)

    y_tiles = []
    for i in range(SSM_TILES):
        sl = slice(i * SSM_TILE, (i + 1) * SSM_TILE)
        ys, h_re, h_im = _ssm_tile(i, bus[i][0:Ms], sre_ref[sl, :].T, sim_ref[sl, :].T, Bs, 1,
                                   are_ref, aim_ref, cre_ref, cim_ref)
        sre_o_ref[sl, :] = h_re.T
        sim_o_ref[sl, :] = h_im.T
        mbu_ref[...] = bus[i][Ms:]
        a_re, a_im = are_ref[:, sl], aim_ref[:, sl]
        h_re = h_im = jnp.zeros((1, SSM_TILE), _F32)
        for t in range(Tm):
            n_re = a_re * h_re - a_im * h_im + mbu_ref[t:t + 1, 0:SSM_TILE]
            n_im = a_re * h_im + a_im * h_re + mbu_ref[t:t + 1, SSM_TILE:]
            h_re, h_im = n_re, n_im
            mhs_ref[t:t + 1, 0:SSM_TILE] = h_re
            mhs_ref[t:t + 1, SSM_TILE:] = h_im
        ym = (_dot(mhs_ref[:, 0:SSM_TILE].astype(_BF16), cre_ref[i])
              - _dot(mhs_ref[:, SSM_TILE:].astype(_BF16), cim_ref[i]))
        mre_o_ref[:, sl] = jnp.broadcast_to(h_re, (Bm, SSM_TILE))
        mim_o_ref[:, sl] = jnp.broadcast_to(h_im, (Bm, SSM_TILE))
        y_tiles.append(jnp.concatenate([ys, ym], axis=0))
    merge = gate_pool * y_pool + gate_ssm * _glu(y_tiles, u_ssm, dskip_ref[...], wglu_o_ref[...])
    xo_ref[...] = x + _dot(merge.astype(_BF16), wout_o_ref[...])


def _mixer_small(x, p, l, prev, *, Bs, Bm, Tm):
    kern = functools.partial(_mixer_small_kernel, Bs=Bs, Bm=Bm, Tm=Tm, n_alias=len(prev))
    stacked = [p["state_pool"], p["state_re"], p["state_im"], p["n1g"], p["pool_scale"], p["a_re"],
               p["a_im"], p["bbc"], p["cre"], p["cim"], p["dskip"], p["w_in"], p["pool_w"],
               p["w_glu"], p["w_out"]]
    sds = jax.ShapeDtypeStruct
    state_shapes = [sds(p["state_pool"].shape, _F32), sds(p["state_re"].shape, _F32),
                    sds(p["state_im"].shape, _F32)]
    plain_shapes = [sds((POOL_KEEP * Bm, POOL_WIDTH), _F32), sds((Bm, SSM_FLAT), _F32),
                    sds((Bm, SSM_FLAT), _F32), sds((D_MODEL, IN_COLS), _BF16),
                    sds((POOL_GROUPS, POOL_GROUP_IN, POOL_GROUP_OUT), _BF16),
                    sds((SSM_WIDTH, 2 * D_MODEL), _BF16), sds((D_MODEL, D_MODEL), _BF16)]
    n_in = 1 + len(stacked)
    return pl.pallas_call(
        kern,
        grid=(1,),
        in_specs=[_whole(x.shape)] + [_layer(a, l) for a in stacked]
        + [pl.BlockSpec(memory_space=pl.ANY)] * len(prev),
        out_specs=[_whole(x.shape)] + [_layer(s, l) for s in state_shapes]
        + [_whole(s.shape) for s in plain_shapes],
        out_shape=[sds(x.shape, _F32)] + state_shapes + plain_shapes,
        scratch_shapes=[pltpu.VMEM((POOL_KEEP + Tm, POOL_WIDTH), _F32),
                        pltpu.VMEM((Tm, 2 * SSM_TILE), _F32), pltpu.VMEM((Tm, 2 * SSM_TILE), _F32)],
        input_output_aliases={n_in + k: 1 + k for k in range(len(prev))},
        compiler_params=_compiler_params(),
        name="mixer_small",
    )(x, *stacked, *prev)


GELU_C1 = 0.7978845608028654
GELU_C2 = GELU_C1 * 0.044715


def _gelu2_mul(x, v):
    t = jnp.tanh(x * (GELU_C1 + GELU_C2 * (x * x)))
    return (x * v) * (1.0 + t)


def _conv_tile(g_ext, B, M, convw, convb):
    conv = convb
    for k in range(CONV_W):
        conv = conv + convw[k:k + 1, :] * g_ext[k * B:k * B + M, :]
    return conv


def _ffn_main_kernel(x_ref, chist_ref, n2g_ref, convw_ref, convb_ref, nfg_ref, wgv_ref,
                     wdown_ref, xo_ref, convst_ref,
                     carry_ref, *scratch_refs, B, T, n_chunks, final_norm, batch_major_out):
    keep = CONV_HIST * B
    c = pl.program_id(0)

    @pl.when(c == 0)
    def _():
        carry_ref[...] = chist_ref[...]

    Ts = T // FFN_SUB_BLOCKS
    Ms = B * Ts
    gv_refs, dma_refs = scratch_refs[:FF_SLOTS], scratch_refs[FF_SLOTS:]
    if batch_major_out:
        yt_ref, y_sem = dma_refs
        slot_o = lax.rem(c, 2)

        def y_copy(chunk, buf, b):
            return pltpu.make_async_copy(yt_ref.at[buf, :, b, :],
                                         xo_ref.at[b, pl.ds(chunk * T, T), :], y_sem.at[buf, b])

        @pl.when(c >= 2)
        def _():
            for b in range(B):
                y_copy(c - 2, slot_o, b).wait()

    xs = [x_ref[s * Ms:(s + 1) * Ms, :] for s in range(FFN_SUB_BLOCKS)]
    hbs = [_rmsnorm(x, n2g_ref[...]).astype(_BF16) for x in xs]

    staged_rows = pl.ds(pl.multiple_of(keep + jnp.minimum(c, 0), B), Ms)
    n_tiles = FFN_SUB_BLOCKS * FF_TILES

    def up(k):
        s, j = divmod(k, FF_TILES)
        gv_refs[k % FF_SLOTS][staged_rows, :] = _dot(
            hbs[s], wgv_ref[:, 2 * j * FF_TILE:2 * (j + 1) * FF_TILE])

    for k in range(FF_LOOKAHEAD):
        up(k)
    acc = None
    for k in range(n_tiles):
        s, j = divmod(k, FF_TILES)
        fs = slice(j * FF_TILE, (j + 1) * FF_TILE)
        if k + FF_LOOKAHEAD < n_tiles:
            up(k + FF_LOOKAHEAD)
        slot = gv_refs[k % FF_SLOTS]
        slot[0:keep, 0:FF_TILE] = carry_ref[:, fs]
        conv = _conv_tile(slot.at[:, 0:FF_TILE], B, Ms, convw_ref[:, fs], convb_ref[:, fs])
        carry_ref[:, fs] = slot[Ms:Ms + keep, 0:FF_TILE]
        v = slot[keep:keep + Ms, FF_TILE:]
        contrib = _dot(_gelu2_mul(conv, v).astype(_BF16), wdown_ref[fs, :])
        acc = contrib if j == 0 else acc + contrib
        if j == FF_TILES - 1:
            xo = xs[s] + acc
            if final_norm:
                xo = _rmsnorm(xo, nfg_ref[...])
            if batch_major_out:
                yt_ref[slot_o, s * Ts:(s + 1) * Ts] = xo.reshape(Ts, B, D_MODEL)
            else:
                xo_ref[s * Ms:(s + 1) * Ms, :] = xo

    if batch_major_out:
        for b in range(B):
            y_copy(c, slot_o, b).start()

    @pl.when(c == n_chunks - 1)
    def _():
        convst_ref[...] = carry_ref[...]
        if batch_major_out:
            for b in range(B):
                y_copy(c - 1, 1 - slot_o, b).wait()
                y_copy(c, slot_o, b).wait()


def _ffn_main(x, chist, p, l, wb, *, B, T, final_norm, batch_major_out):
    M = B * T
    n_chunks = x.shape[0] // M
    keep = CONV_HIST * B
    kern = functools.partial(_ffn_main_kernel, B=B, T=T, n_chunks=n_chunks,
                             final_norm=final_norm, batch_major_out=batch_major_out)
    stacked = [p["n2g"], p["conv_w"], p["conv_b"]]
    if batch_major_out:
        assert n_chunks >= 2
        o_spec = pl.BlockSpec(memory_space=pl.ANY)
        o_shape = jax.ShapeDtypeStruct((B, n_chunks * T, D_MODEL), _F32)
        dma_scratch = [pltpu.VMEM((2, T, B, D_MODEL), _F32), pltpu.SemaphoreType.DMA((2, B))]
    else:
        o_spec = pl.BlockSpec((M, D_MODEL), lambda c: (c, 0))
        o_shape = jax.ShapeDtypeStruct((n_chunks * M, D_MODEL), _F32)
        dma_scratch = []
    return pl.pallas_call(
        kern,
        grid=(n_chunks,),
        in_specs=[pl.BlockSpec((M, D_MODEL), lambda c: (c, 0)), _whole(chist.shape)]
        + [_layer(a, l) for a in stacked] + [_whole(p["nfg"].shape)]
        + [_whole(a.shape) for a in wb],
        out_specs=[o_spec, _whole((keep, D_FF))],
        out_shape=[o_shape, jax.ShapeDtypeStruct((keep, D_FF), _F32)],
        scratch_shapes=[pltpu.VMEM((keep, D_FF), _F32),
                        ] + [pltpu.VMEM((keep + M // FFN_SUB_BLOCKS, 2 * FF_TILE), _F32)] * FF_SLOTS
        + dma_scratch,
        compiler_params=_compiler_params(),
        name="convffn_main",
    )(x, chist, *stacked, p["nfg"], *wb)


def _ffn_small_kernel(x_ref, conv_ref, n2g_ref, convw_ref, convb_ref, nfg_ref, wup_ref,
                      wdown_ref, *refs, Bs, Bm, Tm, final_norm, n_alias, layer):
    (xo_ref, conv_o_ref, mconv_o_ref, wgv_o_ref, wdown_o_ref,
     hb_ref, acc_ref, mg_ref, wg_ring, wv_ring, wd_ring, w_sem) = refs[n_alias:]
    Ms = Bs
    jt = pl.program_id(0)

    def tile_copies(tile, slot):
        col = pl.multiple_of(tile * FF_TILE, FF_TILE)
        return (pltpu.make_async_copy(wup_ref.at[layer, :, pl.ds(col, FF_TILE)],
                                      wg_ring.at[slot], w_sem.at[0, slot]),
                pltpu.make_async_copy(wup_ref.at[layer, :, pl.ds(D_FF + col, FF_TILE)],
                                      wv_ring.at[slot], w_sem.at[1, slot]),
                pltpu.make_async_copy(wdown_ref.at[layer, pl.ds(col, FF_TILE), :],
                                      wd_ring.at[slot], w_sem.at[2, slot]))

    @pl.when(jt == 0)
    def _():
        for tile in range(W_RING - 1):
            for cp in tile_copies(tile, tile):
                cp.start()
        hb_ref[...] = _rmsnorm(x_ref[...], n2g_ref[...]).astype(_BF16)
        acc_ref[...] = jnp.zeros(acc_ref.shape, _F32)

    @pl.when(jt + W_RING - 1 < FF_TILES)
    def _():
        for cp in tile_copies(jt + W_RING - 1, lax.rem(jt + W_RING - 1, W_RING)):
            cp.start()

    slot = lax.rem(jt, W_RING)
    for cp in tile_copies(jt, slot):
        cp.wait()

    wgv_o_ref[:, 0:FF_TILE] = wg_ring[slot].astype(_BF16)
    wgv_o_ref[:, FF_TILE:] = wv_ring[slot].astype(_BF16)
    wdown_o_ref[...] = (0.5 * wd_ring[slot]).astype(_BF16)
    hb = hb_ref[...]
    gv = _dot(hb, wgv_o_ref[...])
    g, v = gv[:, 0:FF_TILE], gv[:, FF_TILE:]
    hist = [conv_ref[:, j, :] for j in range(CONV_HIST)]
    new_hist = hist[1:] + [g[0:Ms]]
    for j in range(CONV_HIST):
        conv_o_ref[:, j, :] = new_hist[j]
    g_ext_s = jnp.concatenate(hist + [g[0:Ms]], axis=0)
    mg_ref[0:META_G_ROW0, :] = jnp.zeros((META_G_ROW0, FF_TILE), _F32)
    mg_ref[META_G_ROW0:META_G_ROW0 + Tm, :] = g[Ms:]
    conv_m = convb_ref[...]
    for k in range(CONV_W):
        lo = META_G_ROW0 - CONV_HIST + k
        conv_m = conv_m + convw_ref[k:k + 1, :] * mg_ref[lo:lo + Tm, :]
    for j in range(CONV_HIST):
        row = META_G_ROW0 + Tm - CONV_HIST + j
        mconv_o_ref[j * Bm:(j + 1) * Bm, :] = jnp.broadcast_to(mg_ref[row:row + 1, :],
                                                              (Bm, FF_TILE))
    conv = jnp.concatenate([_conv_tile(g_ext_s, Bs, Ms, convw_ref[...], convb_ref[...]), conv_m],
                           axis=0)
    acc_ref[...] += _dot(_gelu2_mul(conv, v).astype(_BF16), wdown_o_ref[...])

    @pl.when(jt == FF_TILES - 1)
    def _():
        xo = x_ref[...] + acc_ref[...]
        if final_norm:
            xo = _rmsnorm(xo, nfg_ref[...])
        xo_ref[...] = xo


def _ffn_small(x, p, l, prev, *, Bs, Bm, Tm, final_norm):
    kern = functools.partial(_ffn_small_kernel, Bs=Bs, Bm=Bm, Tm=Tm, final_norm=final_norm,
                             n_alias=len(prev), layer=l)
    sds = jax.ShapeDtypeStruct
    n_rows = x.shape[0]
    in_specs = [
        _whole(x.shape),
        pl.BlockSpec((None, Bs, CONV_HIST, FF_TILE), lambda j: (l, 0, 0, j)),
        _layer(p["n2g"], l),
        pl.BlockSpec((None, CONV_W, FF_TILE), lambda j: (l, 0, j)),
        pl.BlockSpec((None, 1, FF_TILE), lambda j: (l, 0, j)),
        _whole(p["nfg"].shape),
        pl.BlockSpec(memory_space=pl.ANY),
        pl.BlockSpec(memory_space=pl.ANY),
    ] + [pl.BlockSpec(memory_space=pl.ANY)] * len(prev)
    out_shape = [sds(x.shape, _F32), sds(p["state_conv"].shape, _F32),
                 sds((CONV_HIST * Bm, D_FF), _F32), sds((D_MODEL, 2 * D_FF), _BF16),
                 sds((D_FF, D_MODEL), _BF16)]
    out_specs = [
        _whole(x.shape),
        pl.BlockSpec((None, Bs, CONV_HIST, FF_TILE), lambda j: (l, 0, 0, j)),
        pl.BlockSpec((CONV_HIST * Bm, FF_TILE), lambda j: (0, j)),
        pl.BlockSpec((D_MODEL, 2 * FF_TILE), lambda j: (0, j)),
        pl.BlockSpec((FF_TILE, D_MODEL), lambda j: (j, 0)),
    ]
    n_in = 8
    return pl.pallas_call(
        kern,
        grid=(FF_TILES,),
        in_specs=in_specs,
        out_specs=out_specs,
        out_shape=out_shape,
        scratch_shapes=[pltpu.VMEM((n_rows, D_MODEL), _BF16), pltpu.VMEM((n_rows, D_MODEL), _F32),
                        pltpu.VMEM((META_G_ROW0 + Tm, FF_TILE), _F32),
                        pltpu.VMEM((W_RING, D_MODEL, FF_TILE), _F32),
                        pltpu.VMEM((W_RING, D_MODEL, FF_TILE), _F32),
                        pltpu.VMEM((W_RING, FF_TILE, D_MODEL), _F32),
                        pltpu.SemaphoreType.DMA((3, W_RING))],
        input_output_aliases={n_in + k: 1 + k for k in range(len(prev))},
        compiler_params=_compiler_params(),
        name="convffn_small",
    )(x, p["state_conv"], p["n2g"], p["conv_w"], p["conv_b"], p["nfg"], p["w_up"], p["w_down"],
      *prev)


def kernel(x_prompt, x_sample, state_pool, state_ssm_re, state_ssm_im, state_conv, meta_tokens,
           norm1_g, w_in, pool_w, pool_scale, ssm_A_re, ssm_A_im, ssm_log_dt, ssm_B_re, ssm_B_im,
           ssm_C_re, ssm_C_im, ssm_D, w_glu, w_out, norm2_g, w_up, conv_w, conv_b, w_down, norm_f_g):
    depth = w_in.shape[0]
    bp, seq, _ = x_prompt.shape
    bs = x_sample.shape[0]
    assert seq % MIXER_CHUNK_STEPS == 0 and seq % FFN_CHUNK_STEPS == 0
    state_shapes = [(depth, POOL_HIST, bs, POOL_WIDTH), (depth, SSM_FLAT, bs), (depth, SSM_FLAT, bs),
                    state_conv.shape]
    ab_re, ab_im, bbc, cre, cim, *new_states = _discretise(
        ssm_A_re, ssm_A_im, ssm_log_dt, ssm_B_re, ssm_B_im, ssm_C_re, ssm_C_im, state_shapes)
    row = lambda a: a[:, None, :]
    p = dict(state_pool=state_pool.transpose(0, 2, 1, 3), state_conv=state_conv,
             state_re=state_ssm_re.transpose(0, 2, 3, 1).reshape(depth, SSM_FLAT, bs),
             state_im=state_ssm_im.transpose(0, 2, 3, 1).reshape(depth, SSM_FLAT, bs),
             n1g=row(norm1_g), pool_scale=row(pool_scale), a_re=ab_re, a_im=ab_im, bbc=bbc,
             cre=cre, cim=cim, dskip=row(ssm_D), w_in=w_in, pool_w=pool_w, w_glu=w_glu,
             w_out=w_out, n2g=row(norm2_g), conv_w=conv_w, conv_b=row(conv_b), w_up=w_up,
             w_down=w_down, nfg=norm_f_g[None])

    xs = jnp.concatenate([x_sample.reshape(bs, D_MODEL), meta_tokens], axis=0)
    small = dict(Bs=bs, Bm=bp, Tm=N_META)
    mix_prev, ffn_prev = new_states[0:3], new_states[3:4]
    meta_state, mix_w, ffn_w = [], [], []
    for l in range(depth):
        xs, pool_s, re_s, im_s, mp, mr, mi, *wb = _mixer_small(xs, p, l, mix_prev, **small)
        mix_prev = [pool_s, re_s, im_s]
        mix_w.append(wb)
        xs, conv_s, mc, *wb = _ffn_small(xs, p, l, ffn_prev, final_norm=(l == depth - 1), **small)
        ffn_prev = [conv_s]
        ffn_w.append(wb)
        meta_state.append(((mp, mr, mi), mc))
    y_sample = xs[0:bs].reshape(bs, 1, D_MODEL)

    xp = x_prompt
    pool_p, re_p, im_p, conv_p = [], [], [], []
    for l in range(depth):
        mix_state, mc = meta_state[l]
        xp, pp, hr, hi = _mixer_main(xp, mix_state, p, l, mix_w[l], start_pos=N_META,
                                     batch_major_in=(l == 0), B=bp, T=MIXER_CHUNK_STEPS)
        last = l == depth - 1
        xp, cv = _ffn_main(xp, mc, p, l, ffn_w[l], final_norm=last, batch_major_out=last, B=bp,
                           T=FFN_CHUNK_STEPS)
        pool_p.append(pp)
        re_p.append(hr)
        im_p.append(hi)
        conv_p.append(cv)
    st = jnp.stack
    pool_p = st(pool_p).reshape(depth, POOL_KEEP, bp, POOL_WIDTH)[:, 1:].transpose(0, 2, 1, 3)
    conv_p = st(conv_p).reshape(depth, CONV_HIST, bp, D_FF).transpose(0, 2, 1, 3)
    group = lambda h: h.reshape(h.shape[:-1] + (SSM_GROUPS, SSM_STATE))
    sample_ssm = lambda h: h.reshape(depth, SSM_GROUPS, SSM_STATE, bs).transpose(0, 3, 1, 2)
    return (xp, y_sample, pool_p, group(st(re_p)), group(st(im_p)), conv_p,
            pool_s.transpose(0, 2, 1, 3), sample_ssm(re_s), sample_ssm(im_s), conv_s)
```

```python
import functools

import jax
import jax.numpy as jnp
from jax import lax
from jax.experimental import pallas as pl
from jax.experimental.pallas import tpu as pltpu

D_MODEL = 1024
N_META = 16
POOL_WIDTH = 512
POOL_WINDOWS = (2, 4, 8, 16)
POOL_GROUPS = len(POOL_WINDOWS)
POOL_GROUP_IN = 128
POOL_GROUP_OUT = 256
POOL_HIST = 15
POOL_KEEP = 16
SSM_WIDTH = 512
SSM_GROUP_CH = 16
SSM_GROUPS = 32
SSM_STATE = 64
SSM_FLAT = SSM_GROUPS * SSM_STATE
SSM_TILE = 512
SSM_TILES = SSM_FLAT // SSM_TILE
SSM_TILE_GROUPS = SSM_TILE // SSM_STATE
SSM_TILE_CH = SSM_TILE_GROUPS * SSM_GROUP_CH
IN_COLS = POOL_WIDTH + SSM_WIDTH + 2 * D_MODEL
D_FF = 2816
FF_TILE = 256
FF_TILES = D_FF // FF_TILE
FF_LOOKAHEAD = 4
FF_SLOTS = FF_LOOKAHEAD + 1
CONV_W = 3
CONV_HIST = CONV_W - 1
W_RING = 3
META_G_ROW0 = 8
RMS_EPS = 1e-6
PAST_LEN = 16384

V7X_VMEM_LIMIT_BYTES = 56 * 1024 * 1024
MIXER_CHUNK_STEPS = 64
MIXER_SUB_BLOCKS = 1
FFN_CHUNK_STEPS = 64
FFN_SUB_BLOCKS = 2

_F32 = jnp.float32
_BF16 = jnp.bfloat16


def _rmsnorm(x, g):
    return x * lax.rsqrt(jnp.mean(x * x, axis=-1, keepdims=True) + RMS_EPS) * g


def _dot(a, b):
    return jnp.dot(a, b, preferred_element_type=_F32)


def _whole(shape):
    nd = len(shape)
    return pl.BlockSpec(shape, lambda *_: (0,) * nd, pipeline_mode=pl.Buffered(1))


def _layer(arr, l):
    nd = arr.ndim - 1
    return pl.BlockSpec((None,) + arr.shape[1:], lambda *_: (l,) + (0,) * nd,
                        pipeline_mode=pl.Buffered(1))


def _compiler_params():
    return pltpu.CompilerParams(dimension_semantics=("arbitrary",),
                                vmem_limit_bytes=V7X_VMEM_LIMIT_BYTES)


def _disc_kernel(are_ref, aim_ref, ldt_ref, bre_ref, bim_ref, ctre_ref, ctim_ref,
                 abre_ref, abim_ref, bbc_ref, cre_ref, cim_ref, *zero_refs):
    for z_ref in zero_refs:
        z_ref[...] = jnp.zeros(z_ref.shape, z_ref.dtype)
    lam_re = are_ref[...]
    lam_im = aim_ref[...]
    dt = jnp.exp(ldt_ref[...])
    mag = jnp.exp(lam_re * dt)
    ab_re = mag * jnp.cos(lam_im * dt)
    ab_im = mag * jnp.sin(lam_im * dt)
    den = lam_re * lam_re + lam_im * lam_im
    nr = ab_re - 1.0
    ni = ab_im
    q_re = (nr * lam_re + ni * lam_im) / den
    q_im = (ni * lam_re - nr * lam_im) / den
    b_re = bre_ref[...]
    b_im = bim_ref[...]
    abre_ref[...] = ab_re
    abim_ref[...] = ab_im
    bb_re = (q_re * b_re - q_im * b_im).astype(_BF16)
    bb_im = (q_re * b_im + q_im * b_re).astype(_BF16)
    ct_re = ctre_ref[...].astype(_BF16)
    ct_im = ctim_ref[...].astype(_BF16)
    bbc_ref[...] = jnp.zeros(bbc_ref.shape, _BF16)
    cre_ref[...] = jnp.zeros(cre_ref.shape, _BF16)
    cim_ref[...] = jnp.zeros(cim_ref.shape, _BF16)
    for i in range(SSM_TILES):
        for gl in range(SSM_TILE_GROUPS):
            g = i * SSM_TILE_GROUPS + gl
            ch = slice(gl * SSM_GROUP_CH, (gl + 1) * SSM_GROUP_CH)
            st = slice(gl * SSM_STATE, (gl + 1) * SSM_STATE)
            st_im = slice(SSM_TILE + gl * SSM_STATE, SSM_TILE + (gl + 1) * SSM_STATE)
            src_st = slice(g * SSM_STATE, (g + 1) * SSM_STATE)
            src_ch = slice(g * SSM_GROUP_CH, (g + 1) * SSM_GROUP_CH)
            bbc_ref[i, ch, st] = bb_re[:, src_st]
            bbc_ref[i, ch, st_im] = bb_im[:, src_st]
            cre_ref[i, st, ch] = ct_re[:, src_ch]
            cim_ref[i, st, ch] = ct_im[:, src_ch]


def _discretise(a_re, a_im, log_dt, b_re, b_im, c_re, c_im, zero_shapes):
    depth = a_re.shape[0]
    flat = lambda a: a.reshape(depth, 1, SSM_FLAT)
    ldt = jnp.broadcast_to(log_dt[:, :, None], (depth, SSM_GROUPS, SSM_STATE))
    bt = lambda b: b.transpose(0, 3, 1, 2).reshape(depth, SSM_GROUP_CH, SSM_FLAT)
    ct = lambda c: c.transpose(0, 3, 1, 2).reshape(depth, SSM_STATE, SSM_WIDTH)
    ins = [flat(a_re), flat(a_im), flat(ldt), bt(b_re), bt(b_im), ct(c_re), ct(c_im)]
    out_shape = [jax.ShapeDtypeStruct((depth, 1, SSM_FLAT), _F32)] * 2 + [
        jax.ShapeDtypeStruct((depth, SSM_TILES, SSM_TILE_CH, 2 * SSM_TILE), _BF16),
        jax.ShapeDtypeStruct((depth, SSM_TILES, SSM_TILE, SSM_TILE_CH), _BF16),
        jax.ShapeDtypeStruct((depth, SSM_TILES, SSM_TILE, SSM_TILE_CH), _BF16)]
    out_shape += [jax.ShapeDtypeStruct(s, _F32) for s in zero_shapes]
    per_layer = lambda s: pl.BlockSpec((None,) + s.shape[1:],
                                       lambda l: (l,) + (0,) * (len(s.shape) - 1))
    return pl.pallas_call(
        _disc_kernel,
        grid=(depth,),
        in_specs=[per_layer(a) for a in ins],
        out_specs=[per_layer(s) for s in out_shape],
        out_shape=out_shape,
        name="s5_discretise",
    )(*ins)


def _pool_branch(ext, n_hist, B, T, pos0, poolw, pscale):
    M = B * T
    base = n_hist * B
    pos = pos0 + lax.broadcasted_iota(jnp.int32, (M, 1), 0) // B
    tiles = []
    for g, w in enumerate(POOL_WINDOWS):
        ls = slice(g * POOL_GROUP_IN, (g + 1) * POOL_GROUP_IN)
        tok = ext[base:base + M, ls]
        win_sum = tok
        for k in range(1, w):
            win_sum = win_sum + ext[base - k * B:base - k * B + M, ls]
        cnt = jnp.minimum(w, pos + 1).astype(_F32)
        d = win_sum / cnt - tok
        tiles.append(_dot(d.astype(_BF16), poolw(g)))
    return jnp.concatenate(tiles, axis=-1) * pscale


def _ssm_inputs(u_ssm_b, bbc_ref):
    return [_dot(u_ssm_b[:, i * SSM_TILE_CH:(i + 1) * SSM_TILE_CH], bbc_ref[i])
            for i in range(SSM_TILES)]


def _ssm_tile(i, bu, h_re, h_im, B, T, are_ref, aim_ref, cre_ref, cim_ref):
    sl = slice(i * SSM_TILE, (i + 1) * SSM_TILE)
    a_re = jnp.broadcast_to(are_ref[:, sl], (B, SSM_TILE))
    a_im = jnp.broadcast_to(aim_ref[:, sl], (B, SSM_TILE))
    hs_re, hs_im = [], []
    for t in range(T):
        rows = slice(t * B, (t + 1) * B)
        n_re = a_re * h_re - a_im * h_im + bu[rows, 0:SSM_TILE]
        n_im = a_re * h_im + a_im * h_re + bu[rows, SSM_TILE:2 * SSM_TILE]
        h_re, h_im = n_re, n_im
        hs_re.append(h_re)
        hs_im.append(h_im)
    y = (_dot(jnp.concatenate(hs_re, axis=0).astype(_BF16), cre_ref[i])
         - _dot(jnp.concatenate(hs_im, axis=0).astype(_BF16), cim_ref[i]))
    return y, h_re, h_im


GLU_TILE = 256


def _glu_weight(wglu_f32):
    tiles = []
    for k in range(D_MODEL // GLU_TILE):
        tiles.append(wglu_f32[:, k * GLU_TILE:(k + 1) * GLU_TILE])
        tiles.append(wglu_f32[:, D_MODEL + k * GLU_TILE:D_MODEL + (k + 1) * GLU_TILE])
    return jnp.concatenate(tiles, axis=-1).astype(_BF16)


def _glu(y_tiles, u_ssm, dskip, wglu):
    y_s = jnp.concatenate(y_tiles, axis=-1) + dskip * u_ssm
    zz = _dot(jax.nn.gelu(y_s).astype(_BF16), wglu)
    outs = []
    for k in range(D_MODEL // GLU_TILE):
        lo = 2 * k * GLU_TILE
        outs.append(zz[:, lo:lo + GLU_TILE] * jax.nn.sigmoid(zz[:, lo + GLU_TILE:lo + 2 * GLU_TILE]))
    return jnp.concatenate(outs, axis=-1)


def _mixer_main_kernel(x_ref, hist_ref, h0re_ref, h0im_ref, n1g_ref, pscale_ref, are_ref, aim_ref,
                       bbc_ref, cre_ref, cim_ref, dskip_ref, win_ref, poolw_ref, wglu_ref, wout_ref,
                       xo_ref, poolst_ref, hre_o_ref, him_o_ref,
                       pool_ref, hst_ref, *scratch_refs, B, T, start_pos, n_chunks,
                       batch_major_in):
    c = pl.program_id(0)

    @pl.when(c == 0)
    def _():
        pool_ref[...] = hist_ref[...]
        hst_ref[0] = h0re_ref[...]
        hst_ref[1] = h0im_ref[...]

    Ts = T // MIXER_SUB_BLOCKS
    Ms = B * Ts
    bu_refs, dma_refs = scratch_refs[:SSM_TILES], scratch_refs[SSM_TILES:]
    staged_rows = pl.ds(pl.multiple_of(jnp.minimum(c, 0), B), Ms)
    gate_cols = 2 * D_MODEL // SSM_TILES

    if batch_major_in:
        xt_ref, x_sem = dma_refs
        slot = lax.rem(c, 2)

        def x_copy(chunk, buf, b):
            return pltpu.make_async_copy(x_ref.at[b, pl.ds(chunk * T, T), :],
                                         xt_ref.at[buf, :, b, :], x_sem.at[buf, b])

        @pl.when(c == 0)
        def _():
            for b in range(B):
                x_copy(0, 0, b).start()

        @pl.when(c + 1 < n_chunks)
        def _():
            for b in range(B):
                x_copy(c + 1, 1 - slot, b).start()

        for b in range(B):
            x_copy(c, slot, b).wait()

    for s in range(MIXER_SUB_BLOCKS):
        if batch_major_in:
            x = xt_ref[slot, s * Ts:(s + 1) * Ts].reshape(Ms, D_MODEL)
        else:
            x = x_ref[s * Ms:(s + 1) * Ms, :]
        hb = (x * n1g_ref[...]).astype(_BF16)
        r = lax.rsqrt(jnp.mean(x * x, axis=-1, keepdims=True) + RMS_EPS)

        u_ssm = r * _dot(hb, win_ref[:, POOL_WIDTH:POOL_WIDTH + SSM_WIDTH])
        u_ssm_b = u_ssm.astype(_BF16)
        def bbar(i):
            bu_refs[i][staged_rows, :] = _dot(u_ssm_b[:, i * SSM_TILE_CH:(i + 1) * SSM_TILE_CH],
                                              bbc_ref[i])

        def gate(i):
            lo = POOL_WIDTH + SSM_WIDTH + i * gate_cols
            return jax.nn.sigmoid(r * _dot(hb, win_ref[:, lo:lo + gate_cols]))

        def scan_c(i):
            sl = slice(i * SSM_TILE, (i + 1) * SSM_TILE)
            y, h_re, h_im = _ssm_tile(i, bu_refs[i], hst_ref[0, :, sl], hst_ref[1, :, sl], B, Ts,
                                      are_ref, aim_ref, cre_ref, cim_ref)
            hst_ref[0, :, sl] = h_re
            hst_ref[1, :, sl] = h_im
            return y

        gates, y_tiles = [], []
        u_pool = r * _dot(hb, win_ref[:, 0:POOL_WIDTH])
        bbar(0)
        gates.append(gate(0))
        bbar(1)
        gates.append(gate(1))
        bbar(2)
        bbar(3)
        y_tiles.append(scan_c(0))
        gates.append(gate(2))
        y_tiles.append(scan_c(1))
        gates.append(gate(3))
        y_tiles.append(scan_c(2))
        ext = jnp.concatenate([pool_ref[...], u_pool], axis=0)
        pool_ref[...] = ext[Ms:Ms + POOL_KEEP * B, :]
        y_pool = _pool_branch(ext, POOL_KEEP, B, Ts, start_pos + c * T + s * Ts,
                              lambda g: poolw_ref[g], pscale_ref[...])
        y_tiles.append(scan_c(3))
        gate_pool = jnp.concatenate(gates[0:SSM_TILES // 2], axis=-1)
        gate_ssm = jnp.concatenate(gates[SSM_TILES // 2:], axis=-1)
        merge_pool = gate_pool * y_pool
        merge = merge_pool + gate_ssm * _glu(y_tiles, u_ssm, dskip_ref[...], wglu_ref[...])
        xo_ref[s * Ms:(s + 1) * Ms, :] = x + _dot(merge.astype(_BF16), wout_ref[...])

    @pl.when(c == n_chunks - 1)
    def _():
        poolst_ref[...] = pool_ref[...]
        hre_o_ref[...] = hst_ref[0]
        him_o_ref[...] = hst_ref[1]


def _mixer_main(x, meta_state, p, l, wb, *, B, T, start_pos, batch_major_in):
    M = B * T
    if batch_major_in:
        n_chunks = x.shape[1] // T
        x_spec = pl.BlockSpec(memory_space=pl.ANY)
        dma_scratch = [pltpu.VMEM((2, T, B, D_MODEL), _F32), pltpu.SemaphoreType.DMA((2, B))]
    else:
        n_chunks = x.shape[0] // M
        x_spec = pl.BlockSpec((M, D_MODEL), lambda c: (c, 0))
        dma_scratch = []
    keep = POOL_KEEP * B
    kern = functools.partial(_mixer_main_kernel, B=B, T=T, start_pos=start_pos,
                             n_chunks=n_chunks, batch_major_in=batch_major_in)
    stacked = [p["n1g"], p["pool_scale"], p["a_re"], p["a_im"], p["bbc"], p["cre"], p["cim"],
               p["dskip"]]
    return pl.pallas_call(
        kern,
        grid=(n_chunks,),
        in_specs=[x_spec] + [_whole(a.shape) for a in meta_state]
        + [_layer(a, l) for a in stacked] + [_whole(a.shape) for a in wb],
        out_specs=[pl.BlockSpec((M, D_MODEL), lambda c: (c, 0)), _whole((keep, POOL_WIDTH)),
                   _whole((B, SSM_FLAT)), _whole((B, SSM_FLAT))],
        out_shape=[jax.ShapeDtypeStruct((n_chunks * M, D_MODEL), _F32),
                   jax.ShapeDtypeStruct((keep, POOL_WIDTH), _F32),
                   jax.ShapeDtypeStruct((B, SSM_FLAT), _F32),
                   jax.ShapeDtypeStruct((B, SSM_FLAT), _F32)],
        scratch_shapes=[pltpu.VMEM((keep, POOL_WIDTH), _F32),
                        pltpu.VMEM((2, B, SSM_FLAT), _F32),
                        ] + [pltpu.VMEM((M // MIXER_SUB_BLOCKS, 2 * SSM_TILE), _F32)] * SSM_TILES
        + dma_scratch,
        compiler_params=_compiler_params(),
        name="mixer_main",
    )(x, *meta_state, *stacked, *wb)


def _mixer_small_kernel(x_ref, pool_ref, sre_ref, sim_ref, n1g_ref, pscale_ref, are_ref, aim_ref,
                        bbc_ref, cre_ref, cim_ref, dskip_ref, win_ref, poolw_ref, wglu_ref,
                        wout_ref, *refs, Bs, Bm, Tm, n_alias, layer):
    (xo_ref, pool_o_ref, sre_o_ref, sim_o_ref, mpool_o_ref, mre_o_ref, mim_o_ref,
     win_o_ref, poolw_o_ref, wglu_o_ref, wout_o_ref, mext_ref, mbu_ref, mhs_ref,
     wglu_buf, wout_buf, w_sem) = refs[n_alias:]
    Ms = Bs
    glu_copy = pltpu.make_async_copy(wglu_ref.at[layer], wglu_buf, w_sem.at[0])
    out_copy = pltpu.make_async_copy(wout_ref.at[layer], wout_buf, w_sem.at[1])
    glu_copy.start()
    out_copy.start()
    win_o_ref[...] = win_ref[...].astype(_BF16)
    poolw_o_ref[...] = poolw_ref[...].astype(_BF16)

    x = x_ref[...]
    hb = _rmsnorm(x, n1g_ref[...]).astype(_BF16)
    proj = _dot(hb, win_o_ref[...])
    u_pool, u_ssm = proj[:, 0:POOL_WIDTH], proj[:, POOL_WIDTH:POOL_WIDTH + SSM_WIDTH]
    gate_pool = jax.nn.sigmoid(proj[:, POOL_WIDTH + SSM_WIDTH:POOL_WIDTH + SSM_WIDTH + D_MODEL])
    gate_ssm = jax.nn.sigmoid(proj[:, POOL_WIDTH + SSM_WIDTH + D_MODEL:])
    bus = _ssm_inputs(u_ssm.astype(_BF16), bbc_ref)
    poolw = lambda g: poolw_o_ref[g]

    hist = [pool_ref[j] for j in range(POOL_HIST)]
    new_hist = hist[1:] + [u_pool[0:Ms]]
    for j in range(POOL_HIST):
        pool_o_ref[j] = new_hist[j]
    y_pool_s = _pool_branch(jnp.concatenate(hist + [u_pool[0:Ms]], axis=0), POOL_HIST, Bs, 1,
                            PAST_LEN, poolw, pscale_ref[...])
    mext_ref[0:POOL_KEEP, :] = jnp.zeros((POOL_KEEP, POOL_WIDTH), _F32)
    mext_ref[POOL_KEEP:POOL_KEEP + Tm, :] = u_pool[Ms:]
    y_pool_m = _pool_branch(mext_ref, POOL_KEEP, 1, Tm, 0, poolw, pscale_ref[...])
    for t in range(POOL_KEEP):
        mpool_o_ref[t * Bm:(t + 1) * Bm, :] = jnp.broadcast_to(
            mext_ref[Tm + t:Tm + t + 1, :], (Bm, POOL_WIDTH))
    y_pool = jnp.concatenate([y_pool_s, y_pool_m], axis=0)

    y_tiles = []
    for i in range(SSM_TILES):
        sl = slice(i * SSM_TILE, (i + 1) * SSM_TILE)
        ys, h_re, h_im = _ssm_tile(i, bus[i][0:Ms], sre_ref[sl, :].T, sim_ref[sl, :].T, Bs, 1,
                                   are_ref, aim_ref, cre_ref, cim_ref)
        sre_o_ref[sl, :] = h_re.T
        sim_o_ref[sl, :] = h_im.T
        mbu_ref[...] = bus[i][Ms:]
        a_re, a_im = are_ref[:, sl], aim_ref[:, sl]
        h_re = h_im = jnp.zeros((1, SSM_TILE), _F32)
        for t in range(Tm):
            n_re = a_re * h_re - a_im * h_im + mbu_ref[t:t + 1, 0:SSM_TILE]
            n_im = a_re * h_im + a_im * h_re + mbu_ref[t:t + 1, SSM_TILE:]
            h_re, h_im = n_re, n_im
            mhs_ref[t:t + 1, 0:SSM_TILE] = h_re
            mhs_ref[t:t + 1, SSM_TILE:] = h_im
        ym = (_dot(mhs_ref[:, 0:SSM_TILE].astype(_BF16), cre_ref[i])
              - _dot(mhs_ref[:, SSM_TILE:].astype(_BF16), cim_ref[i]))
        mre_o_ref[:, sl] = jnp.broadcast_to(h_re, (Bm, SSM_TILE))
        mim_o_ref[:, sl] = jnp.broadcast_to(h_im, (Bm, SSM_TILE))
        y_tiles.append(jnp.concatenate([ys, ym], axis=0))
    glu_copy.wait()
    wglu_o_ref[...] = _glu_weight(wglu_buf[...])
    merge = gate_pool * y_pool + gate_ssm * _glu(y_tiles, u_ssm, dskip_ref[...], wglu_o_ref[...])
    out_copy.wait()
    wout_o_ref[...] = wout_buf[...].astype(_BF16)
    xo_ref[...] = x + _dot(merge.astype(_BF16), wout_o_ref[...])


def _mixer_small(x, p, l, prev, *, Bs, Bm, Tm):
    kern = functools.partial(_mixer_small_kernel, Bs=Bs, Bm=Bm, Tm=Tm, n_alias=len(prev), layer=l)
    stacked = [p["state_pool"], p["state_re"], p["state_im"], p["n1g"], p["pool_scale"], p["a_re"],
               p["a_im"], p["bbc"], p["cre"], p["cim"], p["dskip"], p["w_in"], p["pool_w"],
               p["w_glu"], p["w_out"]]
    sds = jax.ShapeDtypeStruct
    state_shapes = [sds(p["state_pool"].shape, _F32), sds(p["state_re"].shape, _F32),
                    sds(p["state_im"].shape, _F32)]
    plain_shapes = [sds((POOL_KEEP * Bm, POOL_WIDTH), _F32), sds((Bm, SSM_FLAT), _F32),
                    sds((Bm, SSM_FLAT), _F32), sds((D_MODEL, IN_COLS), _BF16),
                    sds((POOL_GROUPS, POOL_GROUP_IN, POOL_GROUP_OUT), _BF16),
                    sds((SSM_WIDTH, 2 * D_MODEL), _BF16), sds((D_MODEL, D_MODEL), _BF16)]
    n_in = 1 + len(stacked)
    return pl.pallas_call(
        kern,
        grid=(1,),
        in_specs=[_whole(x.shape)] + [_layer(a, l) for a in stacked[:-2]]
        + [pl.BlockSpec(memory_space=pl.ANY)] * (2 + len(prev)),
        out_specs=[_whole(x.shape)] + [_layer(s, l) for s in state_shapes]
        + [_whole(s.shape) for s in plain_shapes],
        out_shape=[sds(x.shape, _F32)] + state_shapes + plain_shapes,
        scratch_shapes=[pltpu.VMEM((POOL_KEEP + Tm, POOL_WIDTH), _F32),
                        pltpu.VMEM((Tm, 2 * SSM_TILE), _F32), pltpu.VMEM((Tm, 2 * SSM_TILE), _F32),
                        pltpu.VMEM(p["w_glu"].shape[1:], _F32), pltpu.VMEM(p["w_out"].shape[1:], _F32),
                        pltpu.SemaphoreType.DMA((2,))],
        input_output_aliases={n_in + k: 1 + k for k in range(len(prev))},
        compiler_params=_compiler_params(),
        name="mixer_small",
    )(x, *stacked, *prev)


GELU_C1 = 0.7978845608028654
GELU_C2 = GELU_C1 * 0.044715


def _gelu2_mul(x, v):
    t = jnp.tanh(x * (GELU_C1 + GELU_C2 * (x * x)))
    return (x * v) * (1.0 + t)


def _conv_tile(g_ext, B, M, convw, convb):
    conv = convb
    for k in range(CONV_W):
        conv = conv + convw[k:k + 1, :] * g_ext[k * B:k * B + M, :]
    return conv


def _ffn_main_kernel(x_ref, chist_ref, n2g_ref, convw_ref, convb_ref, nfg_ref, wgv_ref,
                     wdown_ref, xo_ref, convst_ref,
                     carry_ref, *scratch_refs, B, T, n_chunks, final_norm, batch_major_out):
    keep = CONV_HIST * B
    c = pl.program_id(0)

    @pl.when(c == 0)
    def _():
        carry_ref[...] = chist_ref[...]

    Ts = T // FFN_SUB_BLOCKS
    Ms = B * Ts
    gv_refs, dma_refs = scratch_refs[:FF_SLOTS], scratch_refs[FF_SLOTS:]
    if batch_major_out:
        yt_ref, y_sem = dma_refs
        slot_o = lax.rem(c, 2)

        def y_copy(chunk, buf, b):
            return pltpu.make_async_copy(yt_ref.at[buf, :, b, :],
                                         xo_ref.at[b, pl.ds(chunk * T, T), :], y_sem.at[buf, b])

        @pl.when(c >= 2)
        def _():
            for b in range(B):
                y_copy(c - 2, slot_o, b).wait()

    xs = [x_ref[s * Ms:(s + 1) * Ms, :] for s in range(FFN_SUB_BLOCKS)]
    hbs = [_rmsnorm(x, n2g_ref[...]).astype(_BF16) for x in xs]

    staged_rows = pl.ds(pl.multiple_of(keep + jnp.minimum(c, 0), B), Ms)
    n_tiles = FFN_SUB_BLOCKS * FF_TILES

    def up(k):
        s, j = divmod(k, FF_TILES)
        gv_refs[k % FF_SLOTS][staged_rows, :] = _dot(
            hbs[s], wgv_ref[:, 2 * j * FF_TILE:2 * (j + 1) * FF_TILE])

    for k in range(FF_LOOKAHEAD):
        up(k)
    acc = None
    for k in range(n_tiles):
        s, j = divmod(k, FF_TILES)
        fs = slice(j * FF_TILE, (j + 1) * FF_TILE)
        if k + FF_LOOKAHEAD < n_tiles:
            up(k + FF_LOOKAHEAD)
        slot = gv_refs[k % FF_SLOTS]
        slot[0:keep, 0:FF_TILE] = carry_ref[:, fs]
        conv = _conv_tile(slot.at[:, 0:FF_TILE], B, Ms, convw_ref[:, fs], convb_ref[:, fs])
        carry_ref[:, fs] = slot[Ms:Ms + keep, 0:FF_TILE]
        v = slot[keep:keep + Ms, FF_TILE:]
        contrib = _dot(_gelu2_mul(conv, v).astype(_BF16), wdown_ref[fs, :])
        acc = contrib if j == 0 else acc + contrib
        if j == FF_TILES - 1:
            xo = xs[s] + acc
            if final_norm:
                xo = _rmsnorm(xo, nfg_ref[...])
            if batch_major_out:
                yt_ref[slot_o, s * Ts:(s + 1) * Ts] = xo.reshape(Ts, B, D_MODEL)
            else:
                xo_ref[s * Ms:(s + 1) * Ms, :] = xo

    if batch_major_out:
        for b in range(B):
            y_copy(c, slot_o, b).start()

    @pl.when(c == n_chunks - 1)
    def _():
        convst_ref[...] = carry_ref[...]
        if batch_major_out:
            for b in range(B):
                y_copy(c - 1, 1 - slot_o, b).wait()
                y_copy(c, slot_o, b).wait()


def _ffn_main(x, chist, p, l, wb, *, B, T, final_norm, batch_major_out):
    M = B * T
    n_chunks = x.shape[0] // M
    keep = CONV_HIST * B
    kern = functools.partial(_ffn_main_kernel, B=B, T=T, n_chunks=n_chunks,
                             final_norm=final_norm, batch_major_out=batch_major_out)
    stacked = [p["n2g"], p["conv_w"], p["conv_b"]]
    if batch_major_out:
        assert n_chunks >= 2
        o_spec = pl.BlockSpec(memory_space=pl.ANY)
        o_shape = jax.ShapeDtypeStruct((B, n_chunks * T, D_MODEL), _F32)
        dma_scratch = [pltpu.VMEM((2, T, B, D_MODEL), _F32), pltpu.SemaphoreType.DMA((2, B))]
    else:
        o_spec = pl.BlockSpec((M, D_MODEL), lambda c: (c, 0))
        o_shape = jax.ShapeDtypeStruct((n_chunks * M, D_MODEL), _F32)
        dma_scratch = []
    return pl.pallas_call(
        kern,
        grid=(n_chunks,),
        in_specs=[pl.BlockSpec((M, D_MODEL), lambda c: (c, 0)), _whole(chist.shape)]
        + [_layer(a, l) for a in stacked] + [_whole(p["nfg"].shape)]
        + [_whole(a.shape) for a in wb],
        out_specs=[o_spec, _whole((keep, D_FF))],
        out_shape=[o_shape, jax.ShapeDtypeStruct((keep, D_FF), _F32)],
        scratch_shapes=[pltpu.VMEM((keep, D_FF), _F32),
                        ] + [pltpu.VMEM((keep + M // FFN_SUB_BLOCKS, 2 * FF_TILE), _F32)] * FF_SLOTS
        + dma_scratch,
        compiler_params=_compiler_params(),
        name="convffn_main",
    )(x, chist, *stacked, p["nfg"], *wb)


def _ffn_small_kernel(x_ref, conv_ref, n2g_ref, convw_ref, convb_ref, nfg_ref, wup_ref,
                      wdown_ref, *refs, Bs, Bm, Tm, final_norm, n_alias, layer):
    (xo_ref, conv_o_ref, mconv_o_ref, wgv_o_ref, wdown_o_ref,
     hb_ref, acc_ref, mg_ref, wg_ring, wv_ring, wd_ring, w_sem) = refs[n_alias:]
    Ms = Bs
    jt = pl.program_id(0)

    def tile_copies(tile, slot):
        col = pl.multiple_of(tile * FF_TILE, FF_TILE)
        return (pltpu.make_async_copy(wup_ref.at[layer, :, pl.ds(col, FF_TILE)],
                                      wg_ring.at[slot], w_sem.at[0, slot]),
                pltpu.make_async_copy(wup_ref.at[layer, :, pl.ds(D_FF + col, FF_TILE)],
                                      wv_ring.at[slot], w_sem.at[1, slot]),
                pltpu.make_async_copy(wdown_ref.at[layer, pl.ds(col, FF_TILE), :],
                                      wd_ring.at[slot], w_sem.at[2, slot]))

    @pl.when(jt == 0)
    def _():
        for tile in range(W_RING - 1):
            for cp in tile_copies(tile, tile):
                cp.start()
        hb_ref[...] = _rmsnorm(x_ref[...], n2g_ref[...]).astype(_BF16)
        acc_ref[...] = jnp.zeros(acc_ref.shape, _F32)

    @pl.when(jt + W_RING - 1 < FF_TILES)
    def _():
        for cp in tile_copies(jt + W_RING - 1, lax.rem(jt + W_RING - 1, W_RING)):
            cp.start()

    slot = lax.rem(jt, W_RING)
    for cp in tile_copies(jt, slot):
        cp.wait()

    wgv_o_ref[:, 0:FF_TILE] = wg_ring[slot].astype(_BF16)
    wgv_o_ref[:, FF_TILE:] = wv_ring[slot].astype(_BF16)
    wdown_o_ref[...] = (0.5 * wd_ring[slot]).astype(_BF16)
    hb = hb_ref[...]
    gv = _dot(hb, wgv_o_ref[...])
    g, v = gv[:, 0:FF_TILE], gv[:, FF_TILE:]
    hist = [conv_ref[:, j, :] for j in range(CONV_HIST)]
    new_hist = hist[1:] + [g[0:Ms]]
    for j in range(CONV_HIST):
        conv_o_ref[:, j, :] = new_hist[j]
    g_ext_s = jnp.concatenate(hist + [g[0:Ms]], axis=0)
    mg_ref[0:META_G_ROW0, :] = jnp.zeros((META_G_ROW0, FF_TILE), _F32)
    mg_ref[META_G_ROW0:META_G_ROW0 + Tm, :] = g[Ms:]
    conv_m = convb_ref[...]
    for k in range(CONV_W):
        lo = META_G_ROW0 - CONV_HIST + k
        conv_m = conv_m + convw_ref[k:k + 1, :] * mg_ref[lo:lo + Tm, :]
    for j in range(CONV_HIST):
        row = META_G_ROW0 + Tm - CONV_HIST + j
        mconv_o_ref[j * Bm:(j + 1) * Bm, :] = jnp.broadcast_to(mg_ref[row:row + 1, :],
                                                              (Bm, FF_TILE))
    conv = jnp.concatenate([_conv_tile(g_ext_s, Bs, Ms, convw_ref[...], convb_ref[...]), conv_m],
                           axis=0)
    acc_ref[...] += _dot(_gelu2_mul(conv, v).astype(_BF16), wdown_o_ref[...])

    @pl.when(jt == FF_TILES - 1)
    def _():
        xo = x_ref[...] + acc_ref[...]
        if final_norm:
            xo = _rmsnorm(xo, nfg_ref[...])
        xo_ref[...] = xo


def _ffn_small(x, p, l, prev, *, Bs, Bm, Tm, final_norm):
    kern = functools.partial(_ffn_small_kernel, Bs=Bs, Bm=Bm, Tm=Tm, final_norm=final_norm,
                             n_alias=len(prev), layer=l)
    sds = jax.ShapeDtypeStruct
    n_rows = x.shape[0]
    in_specs = [
        _whole(x.shape),
        pl.BlockSpec((None, Bs, CONV_HIST, FF_TILE), lambda j: (l, 0, 0, j)),
        _layer(p["n2g"], l),
        pl.BlockSpec((None, CONV_W, FF_TILE), lambda j: (l, 0, j)),
        pl.BlockSpec((None, 1, FF_TILE), lambda j: (l, 0, j)),
        _whole(p["nfg"].shape),
        pl.BlockSpec(memory_space=pl.ANY),
        pl.BlockSpec(memory_space=pl.ANY),
    ] + [pl.BlockSpec(memory_space=pl.ANY)] * len(prev)
    out_shape = [sds(x.shape, _F32), sds(p["state_conv"].shape, _F32),
                 sds((CONV_HIST * Bm, D_FF), _F32), sds((D_MODEL, 2 * D_FF), _BF16),
                 sds((D_FF, D_MODEL), _BF16)]
    out_specs = [
        _whole(x.shape),
        pl.BlockSpec((None, Bs, CONV_HIST, FF_TILE), lambda j: (l, 0, 0, j)),
        pl.BlockSpec((CONV_HIST * Bm, FF_TILE), lambda j: (0, j)),
        pl.BlockSpec((D_MODEL, 2 * FF_TILE), lambda j: (0, j)),
        pl.BlockSpec((FF_TILE, D_MODEL), lambda j: (j, 0)),
    ]
    n_in = 8
    return pl.pallas_call(
        kern,
        grid=(FF_TILES,),
        in_specs=in_specs,
        out_specs=out_specs,
        out_shape=out_shape,
        scratch_shapes=[pltpu.VMEM((n_rows, D_MODEL), _BF16), pltpu.VMEM((n_rows, D_MODEL), _F32),
                        pltpu.VMEM((META_G_ROW0 + Tm, FF_TILE), _F32),
                        pltpu.VMEM((W_RING, D_MODEL, FF_TILE), _F32),
                        pltpu.VMEM((W_RING, D_MODEL, FF_TILE), _F32),
                        pltpu.VMEM((W_RING, FF_TILE, D_MODEL), _F32),
                        pltpu.SemaphoreType.DMA((3, W_RING))],
        input_output_aliases={n_in + k: 1 + k for k in range(len(prev))},
        compiler_params=_compiler_params(),
        name="convffn_small",
    )(x, p["state_conv"], p["n2g"], p["conv_w"], p["conv_b"], p["nfg"], p["w_up"], p["w_down"],
      *prev)


def kernel(x_prompt, x_sample, state_pool, state_ssm_re, state_ssm_im, state_conv, meta_tokens,
           norm1_g, w_in, pool_w, pool_scale, ssm_A_re, ssm_A_im, ssm_log_dt, ssm_B_re, ssm_B_im,
           ssm_C_re, ssm_C_im, ssm_D, w_glu, w_out, norm2_g, w_up, conv_w, conv_b, w_down, norm_f_g):
    depth = w_in.shape[0]
    bp, seq, _ = x_prompt.shape
    bs = x_sample.shape[0]
    assert seq % MIXER_CHUNK_STEPS == 0 and seq % FFN_CHUNK_STEPS == 0
    state_shapes = [(depth, POOL_HIST, bs, POOL_WIDTH), (depth, SSM_FLAT, bs), (depth, SSM_FLAT, bs),
                    state_conv.shape]
    ab_re, ab_im, bbc, cre, cim, *new_states = _discretise(
        ssm_A_re, ssm_A_im, ssm_log_dt, ssm_B_re, ssm_B_im, ssm_C_re, ssm_C_im, state_shapes)
    row = lambda a: a[:, None, :]
    p = dict(state_pool=state_pool.transpose(0, 2, 1, 3), state_conv=state_conv,
             state_re=state_ssm_re.transpose(0, 2, 3, 1).reshape(depth, SSM_FLAT, bs),
             state_im=state_ssm_im.transpose(0, 2, 3, 1).reshape(depth, SSM_FLAT, bs),
             n1g=row(norm1_g), pool_scale=row(pool_scale), a_re=ab_re, a_im=ab_im, bbc=bbc,
             cre=cre, cim=cim, dskip=row(ssm_D), w_in=w_in, pool_w=pool_w, w_glu=w_glu,
             w_out=w_out, n2g=row(norm2_g), conv_w=conv_w, conv_b=row(conv_b), w_up=w_up,
             w_down=w_down, nfg=norm_f_g[None])

    xs = jnp.concatenate([x_sample.reshape(bs, D_MODEL), meta_tokens], axis=0)
    small = dict(Bs=bs, Bm=bp, Tm=N_META)
    mix_prev, ffn_prev = new_states[0:3], new_states[3:4]
    meta_state, mix_w, ffn_w = [], [], []
    for l in range(depth):
        xs, pool_s, re_s, im_s, mp, mr, mi, *wb = _mixer_small(xs, p, l, mix_prev, **small)
        mix_prev = [pool_s, re_s, im_s]
        mix_w.append(wb)
        xs, conv_s, mc, *wb = _ffn_small(xs, p, l, ffn_prev, final_norm=(l == depth - 1), **small)
        ffn_prev = [conv_s]
        ffn_w.append(wb)
        meta_state.append(((mp, mr, mi), mc))
    y_sample = xs[0:bs].reshape(bs, 1, D_MODEL)

    xp = x_prompt
    pool_p, re_p, im_p, conv_p = [], [], [], []
    for l in range(depth):
        mix_state, mc = meta_state[l]
        xp, pp, hr, hi = _mixer_main(xp, mix_state, p, l, mix_w[l], start_pos=N_META,
                                     batch_major_in=(l == 0), B=bp, T=MIXER_CHUNK_STEPS)
        last = l == depth - 1
        xp, cv = _ffn_main(xp, mc, p, l, ffn_w[l], final_norm=last, batch_major_out=last, B=bp,
                           T=FFN_CHUNK_STEPS)
        pool_p.append(pp)
        re_p.append(hr)
        im_p.append(hi)
        conv_p.append(cv)
    st = jnp.stack
    pool_p = st(pool_p).reshape(depth, POOL_KEEP, bp, POOL_WIDTH)[:, 1:].transpose(0, 2, 1, 3)
    conv_p = st(conv_p).reshape(depth, CONV_HIST, bp, D_FF).transpose(0, 2, 1, 3)
    group = lambda h: h.reshape(h.shape[:-1] + (SSM_GROUPS, SSM_STATE))
    sample_ssm = lambda h: h.reshape(depth, SSM_GROUPS, SSM_STATE, bs).transpose(0, 3, 1, 2)
    return (xp, y_sample, pool_p, group(st(re_p)), group(st(im_p)), conv_p,
            pool_s.transpose(0, 2, 1, 3), sample_ssm(re_s), sample_ssm(im_s), conv_s)
```

```python
import functools

import jax
import jax.numpy as jnp
from jax import lax
from jax.experimental import pallas as pl
from jax.experimental.pallas import tpu as pltpu

D_MODEL = 1024
N_META = 16
POOL_WIDTH = 512
POOL_WINDOWS = (2, 4, 8, 16)
POOL_GROUPS = len(POOL_WINDOWS)
POOL_GROUP_IN = 128
POOL_GROUP_OUT = 256
POOL_HIST = 15
POOL_KEEP = 16
SSM_WIDTH = 512
SSM_GROUP_CH = 16
SSM_GROUPS = 32
SSM_STATE = 64
SSM_FLAT = SSM_GROUPS * SSM_STATE
SSM_TILE = 512
SSM_TILES = SSM_FLAT // SSM_TILE
SSM_TILE_GROUPS = SSM_TILE // SSM_STATE
SSM_TILE_CH = SSM_TILE_GROUPS * SSM_GROUP_CH
IN_COLS = POOL_WIDTH + SSM_WIDTH + 2 * D_MODEL
D_FF = 2816
FF_TILE = 256
FF_TILES = D_FF // FF_TILE
FF_LOOKAHEAD = 4
FF_SLOTS = FF_LOOKAHEAD + 1
CONV_W = 3
CONV_HIST = CONV_W - 1
W_RING = 3
META_G_ROW0 = 8
RMS_EPS = 1e-6
PAST_LEN = 16384

V7X_VMEM_LIMIT_BYTES = 56 * 1024 * 1024
MIXER_CHUNK_STEPS = 64
MIXER_SUB_BLOCKS = 1
FFN_CHUNK_STEPS = 64
FFN_SUB_BLOCKS = 2

_F32 = jnp.float32
_BF16 = jnp.bfloat16


def _rmsnorm(x, g):
    return x * lax.rsqrt(jnp.mean(x * x, axis=-1, keepdims=True) + RMS_EPS) * g


def _dot(a, b):
    return jnp.dot(a, b, preferred_element_type=_F32)


def _whole(shape):
    nd = len(shape)
    return pl.BlockSpec(shape, lambda *_: (0,) * nd, pipeline_mode=pl.Buffered(1))


def _layer(arr, l):
    nd = arr.ndim - 1
    return pl.BlockSpec((None,) + arr.shape[1:], lambda *_: (l,) + (0,) * nd,
                        pipeline_mode=pl.Buffered(1))


def _compiler_params():
    return pltpu.CompilerParams(dimension_semantics=("arbitrary",),
                                vmem_limit_bytes=V7X_VMEM_LIMIT_BYTES)


def _disc_kernel(are_ref, aim_ref, ldt_ref, bre_ref, bim_ref, ctre_ref, ctim_ref,
                 abre_ref, abim_ref, bbc_ref, cre_ref, cim_ref, *zero_refs):
    for z_ref in zero_refs:
        z_ref[...] = jnp.zeros(z_ref.shape, z_ref.dtype)
    lam_re = are_ref[...]
    lam_im = aim_ref[...]
    dt = jnp.exp(ldt_ref[...])
    mag = jnp.exp(lam_re * dt)
    ab_re = mag * jnp.cos(lam_im * dt)
    ab_im = mag * jnp.sin(lam_im * dt)
    den = lam_re * lam_re + lam_im * lam_im
    nr = ab_re - 1.0
    ni = ab_im
    q_re = (nr * lam_re + ni * lam_im) / den
    q_im = (ni * lam_re - nr * lam_im) / den
    b_re = bre_ref[...]
    b_im = bim_ref[...]
    abre_ref[...] = ab_re
    abim_ref[...] = ab_im
    bb_re = (q_re * b_re - q_im * b_im).astype(_BF16)
    bb_im = (q_re * b_im + q_im * b_re).astype(_BF16)
    ct_re = ctre_ref[...].astype(_BF16)
    ct_im = ctim_ref[...].astype(_BF16)
    bbc_ref[...] = jnp.zeros(bbc_ref.shape, _BF16)
    cre_ref[...] = jnp.zeros(cre_ref.shape, _BF16)
    cim_ref[...] = jnp.zeros(cim_ref.shape, _BF16)
    for i in range(SSM_TILES):
        for gl in range(SSM_TILE_GROUPS):
            g = i * SSM_TILE_GROUPS + gl
            ch = slice(gl * SSM_GROUP_CH, (gl + 1) * SSM_GROUP_CH)
            st = slice(gl * SSM_STATE, (gl + 1) * SSM_STATE)
            st_im = slice(SSM_TILE + gl * SSM_STATE, SSM_TILE + (gl + 1) * SSM_STATE)
            src_st = slice(g * SSM_STATE, (g + 1) * SSM_STATE)
            src_ch = slice(g * SSM_GROUP_CH, (g + 1) * SSM_GROUP_CH)
            bbc_ref[i, ch, st] = bb_re[:, src_st]
            bbc_ref[i, ch, st_im] = bb_im[:, src_st]
            cre_ref[i, st, ch] = ct_re[:, src_ch]
            cim_ref[i, st, ch] = ct_im[:, src_ch]


def _discretise(a_re, a_im, log_dt, b_re, b_im, c_re, c_im, zero_shapes):
    depth = a_re.shape[0]
    flat = lambda a: a.reshape(depth, 1, SSM_FLAT)
    ldt = jnp.broadcast_to(log_dt[:, :, None], (depth, SSM_GROUPS, SSM_STATE))
    bt = lambda b: b.transpose(0, 3, 1, 2).reshape(depth, SSM_GROUP_CH, SSM_FLAT)
    ct = lambda c: c.transpose(0, 3, 1, 2).reshape(depth, SSM_STATE, SSM_WIDTH)
    ins = [flat(a_re), flat(a_im), flat(ldt), bt(b_re), bt(b_im), ct(c_re), ct(c_im)]
    out_shape = [jax.ShapeDtypeStruct((depth, 1, SSM_FLAT), _F32)] * 2 + [
        jax.ShapeDtypeStruct((depth, SSM_TILES, SSM_TILE_CH, 2 * SSM_TILE), _BF16),
        jax.ShapeDtypeStruct((depth, SSM_TILES, SSM_TILE, SSM_TILE_CH), _BF16),
        jax.ShapeDtypeStruct((depth, SSM_TILES, SSM_TILE, SSM_TILE_CH), _BF16)]
    out_shape += [jax.ShapeDtypeStruct(s, _F32) for s in zero_shapes]
    per_layer = lambda s: pl.BlockSpec((None,) + s.shape[1:],
                                       lambda l: (l,) + (0,) * (len(s.shape) - 1))
    return pl.pallas_call(
        _disc_kernel,
        grid=(depth,),
        in_specs=[per_layer(a) for a in ins],
        out_specs=[per_layer(s) for s in out_shape],
        out_shape=out_shape,
        name="s5_discretise",
    )(*ins)


def _pool_branch(ext, n_hist, B, T, pos0, poolw, pscale):
    M = B * T
    base = n_hist * B
    pos = pos0 + lax.broadcasted_iota(jnp.int32, (M, 1), 0) // B
    tiles = []
    for g, w in enumerate(POOL_WINDOWS):
        ls = slice(g * POOL_GROUP_IN, (g + 1) * POOL_GROUP_IN)
        tok = ext[base:base + M, ls]
        win_sum = tok
        for k in range(1, w):
            win_sum = win_sum + ext[base - k * B:base - k * B + M, ls]
        cnt = jnp.minimum(w, pos + 1).astype(_F32)
        d = win_sum / cnt - tok
        tiles.append(_dot(d.astype(_BF16), poolw(g)))
    return jnp.concatenate(tiles, axis=-1) * pscale


def _ssm_inputs(u_ssm_b, bbc_ref):
    return [_dot(u_ssm_b[:, i * SSM_TILE_CH:(i + 1) * SSM_TILE_CH], bbc_ref[i])
            for i in range(SSM_TILES)]


def _ssm_tile(i, bu, h_re, h_im, B, T, are_ref, aim_ref, cre_ref, cim_ref):
    sl = slice(i * SSM_TILE, (i + 1) * SSM_TILE)
    a_re = jnp.broadcast_to(are_ref[:, sl], (B, SSM_TILE))
    a_im = jnp.broadcast_to(aim_ref[:, sl], (B, SSM_TILE))
    hs_re, hs_im = [], []
    for t in range(T):
        rows = slice(t * B, (t + 1) * B)
        n_re = a_re * h_re - a_im * h_im + bu[rows, 0:SSM_TILE]
        n_im = a_re * h_im + a_im * h_re + bu[rows, SSM_TILE:2 * SSM_TILE]
        h_re, h_im = n_re, n_im
        hs_re.append(h_re)
        hs_im.append(h_im)
    y = (_dot(jnp.concatenate(hs_re, axis=0).astype(_BF16), cre_ref[i])
         - _dot(jnp.concatenate(hs_im, axis=0).astype(_BF16), cim_ref[i]))
    return y, h_re, h_im


GLU_TILE = 256


def _glu_weight(wglu_f32):
    tiles = []
    for k in range(D_MODEL // GLU_TILE):
        tiles.append(wglu_f32[:, k * GLU_TILE:(k + 1) * GLU_TILE])
        tiles.append(wglu_f32[:, D_MODEL + k * GLU_TILE:D_MODEL + (k + 1) * GLU_TILE])
    return jnp.concatenate(tiles, axis=-1).astype(_BF16)


def _glu(y_tiles, u_ssm, dskip, wglu):
    y_s = jnp.concatenate(y_tiles, axis=-1) + dskip * u_ssm
    zz = _dot(jax.nn.gelu(y_s).astype(_BF16), wglu)
    outs = []
    for k in range(D_MODEL // GLU_TILE):
        lo = 2 * k * GLU_TILE
        outs.append(zz[:, lo:lo + GLU_TILE] * jax.nn.sigmoid(zz[:, lo + GLU_TILE:lo + 2 * GLU_TILE]))
    return jnp.concatenate(outs, axis=-1)


def _mixer_main_kernel(x_ref, hist_ref, h0re_ref, h0im_ref, n1g_ref, pscale_ref, are_ref, aim_ref,
                       bbc_ref, cre_ref, cim_ref, dskip_ref, win_ref, poolw_ref, wglu_ref, wout_ref,
                       xo_ref, poolst_ref, hre_o_ref, him_o_ref,
                       pool_ref, hst_ref, *scratch_refs, B, T, start_pos, n_chunks,
                       batch_major_in):
    c = pl.program_id(0)

    @pl.when(c == 0)
    def _():
        pool_ref[...] = hist_ref[...]
        hst_ref[0] = h0re_ref[...]
        hst_ref[1] = h0im_ref[...]

    Ts = T // MIXER_SUB_BLOCKS
    Ms = B * Ts
    bu_refs, dma_refs = scratch_refs[:SSM_TILES], scratch_refs[SSM_TILES:]
    staged_rows = pl.ds(pl.multiple_of(jnp.minimum(c, 0), B), Ms)
    gate_cols = 2 * D_MODEL // SSM_TILES

    if batch_major_in:
        xt_ref, x_sem = dma_refs
        slot = lax.rem(c, 2)

        def x_copy(chunk, buf, b):
            return pltpu.make_async_copy(x_ref.at[b, pl.ds(chunk * T, T), :],
                                         xt_ref.at[buf, :, b, :], x_sem.at[buf, b])

        @pl.when(c == 0)
        def _():
            for b in range(B):
                x_copy(0, 0, b).start()

        @pl.when(c + 1 < n_chunks)
        def _():
            for b in range(B):
                x_copy(c + 1, 1 - slot, b).start()

        for b in range(B):
            x_copy(c, slot, b).wait()

    for s in range(MIXER_SUB_BLOCKS):
        if batch_major_in:
            x = xt_ref[slot, s * Ts:(s + 1) * Ts].reshape(Ms, D_MODEL)
        else:
            x = x_ref[s * Ms:(s + 1) * Ms, :]
        hb = (x * n1g_ref[...]).astype(_BF16)
        r = lax.rsqrt(jnp.mean(x * x, axis=-1, keepdims=True) + RMS_EPS)

        u_ssm = r * _dot(hb, win_ref[:, POOL_WIDTH:POOL_WIDTH + SSM_WIDTH])
        u_ssm_b = u_ssm.astype(_BF16)
        def bbar(i):
            bu_refs[i][staged_rows, :] = _dot(u_ssm_b[:, i * SSM_TILE_CH:(i + 1) * SSM_TILE_CH],
                                              bbc_ref[i])

        def gate(i):
            lo = POOL_WIDTH + SSM_WIDTH + i * gate_cols
            return jax.nn.sigmoid(r * _dot(hb, win_ref[:, lo:lo + gate_cols]))

        def scan_c(i):
            sl = slice(i * SSM_TILE, (i + 1) * SSM_TILE)
            y, h_re, h_im = _ssm_tile(i, bu_refs[i], hst_ref[0, :, sl], hst_ref[1, :, sl], B, Ts,
                                      are_ref, aim_ref, cre_ref, cim_ref)
            hst_ref[0, :, sl] = h_re
            hst_ref[1, :, sl] = h_im
            return y

        gates, y_tiles = [], []
        u_pool = r * _dot(hb, win_ref[:, 0:POOL_WIDTH])
        bbar(0)
        gates.append(gate(0))
        bbar(1)
        gates.append(gate(1))
        bbar(2)
        bbar(3)
        y_tiles.append(scan_c(0))
        gates.append(gate(2))
        y_tiles.append(scan_c(1))
        gates.append(gate(3))
        y_tiles.append(scan_c(2))
        ext = jnp.concatenate([pool_ref[...], u_pool], axis=0)
        pool_ref[...] = ext[Ms:Ms + POOL_KEEP * B, :]
        y_pool = _pool_branch(ext, POOL_KEEP, B, Ts, start_pos + c * T + s * Ts,
                              lambda g: poolw_ref[g], pscale_ref[...])
        y_tiles.append(scan_c(3))
        gate_pool = jnp.concatenate(gates[0:SSM_TILES // 2], axis=-1)
        gate_ssm = jnp.concatenate(gates[SSM_TILES // 2:], axis=-1)
        merge_pool = gate_pool * y_pool
        merge = merge_pool + gate_ssm * _glu(y_tiles, u_ssm, dskip_ref[...], wglu_ref[...])
        xo_ref[s * Ms:(s + 1) * Ms, :] = x + _dot(merge.astype(_BF16), wout_ref[...])

    @pl.when(c == n_chunks - 1)
    def _():
        poolst_ref[...] = pool_ref[...]
        hre_o_ref[...] = hst_ref[0]
        him_o_ref[...] = hst_ref[1]


def _mixer_main(x, meta_state, p, l, wb, *, B, T, start_pos, batch_major_in):
    M = B * T
    if batch_major_in:
        n_chunks = x.shape[1] // T
        x_spec = pl.BlockSpec(memory_space=pl.ANY)
        dma_scratch = [pltpu.VMEM((2, T, B, D_MODEL), _F32), pltpu.SemaphoreType.DMA((2, B))]
    else:
        n_chunks = x.shape[0] // M
        x_spec = pl.BlockSpec((M, D_MODEL), lambda c: (c, 0))
        dma_scratch = []
    keep = POOL_KEEP * B
    kern = functools.partial(_mixer_main_kernel, B=B, T=T, start_pos=start_pos,
                             n_chunks=n_chunks, batch_major_in=batch_major_in)
    stacked = [p["n1g"], p["pool_scale"], p["a_re"], p["a_im"], p["bbc"], p["cre"], p["cim"],
               p["dskip"]]
    return pl.pallas_call(
        kern,
        grid=(n_chunks,),
        in_specs=[x_spec] + [_whole(a.shape) for a in meta_state]
        + [_layer(a, l) for a in stacked] + [_whole(a.shape) for a in wb],
        out_specs=[pl.BlockSpec((M, D_MODEL), lambda c: (c, 0)), _whole((keep, POOL_WIDTH)),
                   _whole((B, SSM_FLAT)), _whole((B, SSM_FLAT))],
        out_shape=[jax.ShapeDtypeStruct((n_chunks * M, D_MODEL), _F32),
                   jax.ShapeDtypeStruct((keep, POOL_WIDTH), _F32),
                   jax.ShapeDtypeStruct((B, SSM_FLAT), _F32),
                   jax.ShapeDtypeStruct((B, SSM_FLAT), _F32)],
        scratch_shapes=[pltpu.VMEM((keep, POOL_WIDTH), _F32),
                        pltpu.VMEM((2, B, SSM_FLAT), _F32),
                        ] + [pltpu.VMEM((M // MIXER_SUB_BLOCKS, 2 * SSM_TILE), _F32)] * SSM_TILES
        + dma_scratch,
        compiler_params=_compiler_params(),
        name="mixer_main",
    )(x, *meta_state, *stacked, *wb)


def _mixer_small_kernel(x_ref, pool_ref, sre_ref, sim_ref, n1g_ref, pscale_ref, are_ref, aim_ref,
                        bbc_ref, cre_ref, cim_ref, dskip_ref, win_ref, poolw_ref, wglu_ref,
                        wout_ref, *refs, Bs, Bm, Tm, n_alias):
    (xo_ref, pool_o_ref, sre_o_ref, sim_o_ref, mpool_o_ref, mre_o_ref, mim_o_ref,
     win_o_ref, poolw_o_ref, wglu_o_ref, wout_o_ref, mext_ref, mbu_ref, mhs_ref) = refs[n_alias:]
    Ms = Bs
    win_o_ref[...] = win_ref[...].astype(_BF16)
    poolw_o_ref[...] = poolw_ref[...].astype(_BF16)
    wglu_o_ref[...] = _glu_weight(wglu_ref[...])
    wout_o_ref[...] = wout_ref[...].astype(_BF16)

    x = x_ref[...]
    hb = _rmsnorm(x, n1g_ref[...]).astype(_BF16)
    proj = _dot(hb, win_o_ref[...])
    u_pool, u_ssm = proj[:, 0:POOL_WIDTH], proj[:, POOL_WIDTH:POOL_WIDTH + SSM_WIDTH]
    gate_pool = jax.nn.sigmoid(proj[:, POOL_WIDTH + SSM_WIDTH:POOL_WIDTH + SSM_WIDTH + D_MODEL])
    gate_ssm = jax.nn.sigmoid(proj[:, POOL_WIDTH + SSM_WIDTH + D_MODEL:])
    bus = _ssm_inputs(u_ssm.astype(_BF16), bbc_ref)
    poolw = lambda g: poolw_o_ref[g]

    hist = [pool_ref[j] for j in range(POOL_HIST)]
    new_hist = hist[1:] + [u_pool[0:Ms]]
    for j in range(POOL_HIST):
        pool_o_ref[j] = new_hist[j]
    y_pool_s = _pool_branch(jnp.concatenate(hist + [u_pool[0:Ms]], axis=0), POOL_HIST, Bs, 1,
                            PAST_LEN, poolw, pscale_ref[...])
    mext_ref[0:POOL_KEEP, :] = jnp.zeros((POOL_KEEP, POOL_WIDTH), _F32)
    mext_ref[POOL_KEEP:POOL_KEEP + Tm, :] = u_pool[Ms:]
    y_pool_m = _pool_branch(mext_ref, POOL_KEEP, 1, Tm, 0, poolw, pscale_ref[...])
    for t in range(POOL_KEEP):
        mpool_o_ref[t * Bm:(t + 1) * Bm, :] = jnp.broadcast_to(
            mext_ref[Tm + t:Tm + t + 1, :], (Bm, POOL_WIDTH))
    y_pool = jnp.concatenate([y_pool_s, y_pool_m], axis=0)

    y_tiles = []
    for i in range(SSM_TILES):
        sl = slice(i * SSM_TILE, (i + 1) * SSM_TILE)
        ys, h_re, h_im = _ssm_tile(i, bus[i][0:Ms], sre_ref[sl, :].T, sim_ref[sl, :].T, Bs, 1,
                                   are_ref, aim_ref, cre_ref, cim_ref)
        sre_o_ref[sl, :] = h_re.T
        sim_o_ref[sl, :] = h_im.T
        mbu_ref[...] = bus[i][Ms:]
        a_re, a_im = are_ref[:, sl], aim_ref[:, sl]
        h_re = h_im = jnp.zeros((1, SSM_TILE), _F32)
        for t in range(Tm):
            n_re = a_re * h_re - a_im * h_im + mbu_ref[t:t + 1, 0:SSM_TILE]
            n_im = a_re * h_im + a_im * h_re + mbu_ref[t:t + 1, SSM_TILE:]
            h_re, h_im = n_re, n_im
            mhs_ref[t:t + 1, 0:SSM_TILE] = h_re
            mhs_ref[t:t + 1, SSM_TILE:] = h_im
        ym = (_dot(mhs_ref[:, 0:SSM_TILE].astype(_BF16), cre_ref[i])
              - _dot(mhs_ref[:, SSM_TILE:].astype(_BF16), cim_ref[i]))
        mre_o_ref[:, sl] = jnp.broadcast_to(h_re, (Bm, SSM_TILE))
        mim_o_ref[:, sl] = jnp.broadcast_to(h_im, (Bm, SSM_TILE))
        y_tiles.append(jnp.concatenate([ys, ym], axis=0))
    merge = gate_pool * y_pool + gate_ssm * _glu(y_tiles, u_ssm, dskip_ref[...], wglu_o_ref[...])
    xo_ref[...] = x + _dot(merge.astype(_BF16), wout_o_ref[...])


def _mixer_small(x, p, l, prev, *, Bs, Bm, Tm):
    kern = functools.partial(_mixer_small_kernel, Bs=Bs, Bm=Bm, Tm=Tm, n_alias=len(prev))
    stacked = [p["state_pool"], p["state_re"], p["state_im"], p["n1g"], p["pool_scale"], p["a_re"],
               p["a_im"], p["bbc"], p["cre"], p["cim"], p["dskip"], p["w_in"], p["pool_w"],
               p["w_glu"], p["w_out"]]
    sds = jax.ShapeDtypeStruct
    state_shapes = [sds(p["state_pool"].shape, _F32), sds(p["state_re"].shape, _F32),
                    sds(p["state_im"].shape, _F32)]
    plain_shapes = [sds((POOL_KEEP * Bm, POOL_WIDTH), _F32), sds((Bm, SSM_FLAT), _F32),
                    sds((Bm, SSM_FLAT), _F32), sds((D_MODEL, IN_COLS), _BF16),
                    sds((POOL_GROUPS, POOL_GROUP_IN, POOL_GROUP_OUT), _BF16),
                    sds((SSM_WIDTH, 2 * D_MODEL), _BF16), sds((D_MODEL, D_MODEL), _BF16)]
    n_in = 1 + len(stacked)
    return pl.pallas_call(
        kern,
        grid=(1,),
        in_specs=[_whole(x.shape)] + [_layer(a, l) for a in stacked]
        + [pl.BlockSpec(memory_space=pl.ANY)] * len(prev),
        out_specs=[_whole(x.shape)] + [_layer(s, l) for s in state_shapes]
        + [_whole(s.shape) for s in plain_shapes],
        out_shape=[sds(x.shape, _F32)] + state_shapes + plain_shapes,
        scratch_shapes=[pltpu.VMEM((POOL_KEEP + Tm, POOL_WIDTH), _F32),
                        pltpu.VMEM((Tm, 2 * SSM_TILE), _F32), pltpu.VMEM((Tm, 2 * SSM_TILE), _F32)],
        input_output_aliases={n_in + k: 1 + k for k in range(len(prev))},
        compiler_params=_compiler_params(),
        name="mixer_small",
    )(x, *stacked, *prev)


GELU_C1 = 0.7978845608028654
GELU_C2 = GELU_C1 * 0.044715


def _gelu2_mul(x, v):
    t = jnp.tanh(x * (GELU_C1 + GELU_C2 * (x * x)))
    return (x * v) * (1.0 + t)


def _conv_tile(g_ext, B, M, convw, convb):
    conv = convb
    for k in range(CONV_W):
        conv = conv + convw[k:k + 1, :] * g_ext[k * B:k * B + M, :]
    return conv


def _ffn_main_kernel(x_ref, chist_ref, n2g_ref, convw_ref, convb_ref, nfg_ref, wgv_ref,
                     wdown_ref, xo_ref, convst_ref,
                     carry_ref, *scratch_refs, B, T, n_chunks, final_norm, batch_major_out):
    keep = CONV_HIST * B
    c = pl.program_id(0)

    @pl.when(c == 0)
    def _():
        carry_ref[...] = chist_ref[...]

    Ts = T // FFN_SUB_BLOCKS
    Ms = B * Ts
    gv_refs, dma_refs = scratch_refs[:FF_SLOTS], scratch_refs[FF_SLOTS:]
    if batch_major_out:
        yt_ref, y_sem = dma_refs
        slot_o = lax.rem(c, 2)

        def y_copy(chunk, buf, b):
            return pltpu.make_async_copy(yt_ref.at[buf, :, b, :],
                                         xo_ref.at[b, pl.ds(chunk * T, T), :], y_sem.at[buf, b])

        @pl.when(c >= 2)
        def _():
            for b in range(B):
                y_copy(c - 2, slot_o, b).wait()

    xs = [x_ref[s * Ms:(s + 1) * Ms, :] for s in range(FFN_SUB_BLOCKS)]
    hbs = [_rmsnorm(x, n2g_ref[...]).astype(_BF16) for x in xs]

    staged_rows = pl.ds(pl.multiple_of(keep + jnp.minimum(c, 0), B), Ms)
    n_tiles = FFN_SUB_BLOCKS * FF_TILES

    def up(k):
        s, j = divmod(k, FF_TILES)
        gv_refs[k % FF_SLOTS][staged_rows, :] = _dot(
            hbs[s], wgv_ref[:, 2 * j * FF_TILE:2 * (j + 1) * FF_TILE])

    for k in range(FF_LOOKAHEAD):
        up(k)
    acc = None
    for k in range(n_tiles):
        s, j = divmod(k, FF_TILES)
        fs = slice(j * FF_TILE, (j + 1) * FF_TILE)
        if k + FF_LOOKAHEAD < n_tiles:
            up(k + FF_LOOKAHEAD)
        slot = gv_refs[k % FF_SLOTS]
        slot[0:keep, 0:FF_TILE] = carry_ref[:, fs]
        conv = _conv_tile(slot.at[:, 0:FF_TILE], B, Ms, convw_ref[:, fs], convb_ref[:, fs])
        carry_ref[:, fs] = slot[Ms:Ms + keep, 0:FF_TILE]
        v = slot[keep:keep + Ms, FF_TILE:]
        contrib = _dot(_gelu2_mul(conv, v).astype(_BF16), wdown_ref[fs, :])
        acc = contrib if j == 0 else acc + contrib
        if j == FF_TILES - 1:
            xo = xs[s] + acc
            if final_norm:
                xo = _rmsnorm(xo, nfg_ref[...])
            if batch_major_out:
                yt_ref[slot_o, s * Ts:(s + 1) * Ts] = xo.reshape(Ts, B, D_MODEL)
            else:
                xo_ref[s * Ms:(s + 1) * Ms, :] = xo

    if batch_major_out:
        for b in range(B):
            y_copy(c, slot_o, b).start()

    @pl.when(c == n_chunks - 1)
    def _():
        convst_ref[...] = carry_ref[...]
        if batch_major_out:
            for b in range(B):
                y_copy(c - 1, 1 - slot_o, b).wait()
                y_copy(c, slot_o, b).wait()


def _ffn_main(x, chist, p, l, wb, *, B, T, final_norm, batch_major_out):
    M = B * T
    n_chunks = x.shape[0] // M
    keep = CONV_HIST * B
    kern = functools.partial(_ffn_main_kernel, B=B, T=T, n_chunks=n_chunks,
                             final_norm=final_norm, batch_major_out=batch_major_out)
    stacked = [p["n2g"], p["conv_w"], p["conv_b"]]
    if batch_major_out:
        assert n_chunks >= 2
        o_spec = pl.BlockSpec(memory_space=pl.ANY)
        o_shape = jax.ShapeDtypeStruct((B, n_chunks * T, D_MODEL), _F32)
        dma_scratch = [pltpu.VMEM((2, T, B, D_MODEL), _F32), pltpu.SemaphoreType.DMA((2, B))]
    else:
        o_spec = pl.BlockSpec((M, D_MODEL), lambda c: (c, 0))
        o_shape = jax.ShapeDtypeStruct((n_chunks * M, D_MODEL), _F32)
        dma_scratch = []
    return pl.pallas_call(
        kern,
        grid=(n_chunks,),
        in_specs=[pl.BlockSpec((M, D_MODEL), lambda c: (c, 0)), _whole(chist.shape)]
        + [_layer(a, l) for a in stacked] + [_whole(p["nfg"].shape)]
        + [_whole(a.shape) for a in wb],
        out_specs=[o_spec, _whole((keep, D_FF))],
        out_shape=[o_shape, jax.ShapeDtypeStruct((keep, D_FF), _F32)],
        scratch_shapes=[pltpu.VMEM((keep, D_FF), _F32),
                        ] + [pltpu.VMEM((keep + M // FFN_SUB_BLOCKS, 2 * FF_TILE), _F32)] * FF_SLOTS
        + dma_scratch,
        compiler_params=_compiler_params(),
        name="convffn_main",
    )(x, chist, *stacked, p["nfg"], *wb)


def _ffn_small_kernel(x_ref, conv_ref, n2g_ref, convw_ref, convb_ref, nfg_ref, wup_ref,
                      wdown_ref, *refs, Bs, Bm, Tm, final_norm, n_alias, layer):
    (xo_ref, conv_o_ref, mconv_o_ref, wgv_o_ref, wdown_o_ref,
     hb_ref, acc_ref, mg_ref, wg_ring, wv_ring, wd_ring, w_sem) = refs[n_alias:]
    Ms = Bs
    jt = pl.program_id(0)

    def tile_copies(tile, slot):
        col = pl.multiple_of(tile * FF_TILE, FF_TILE)
        return (pltpu.make_async_copy(wup_ref.at[layer, :, pl.ds(col, FF_TILE)],
                                      wg_ring.at[slot], w_sem.at[0, slot]),
                pltpu.make_async_copy(wup_ref.at[layer, :, pl.ds(D_FF + col, FF_TILE)],
                                      wv_ring.at[slot], w_sem.at[1, slot]),
                pltpu.make_async_copy(wdown_ref.at[layer, pl.ds(col, FF_TILE), :],
                                      wd_ring.at[slot], w_sem.at[2, slot]))

    @pl.when(jt == 0)
    def _():
        for tile in range(W_RING - 1):
            for i, cp in enumerate(tile_copies(tile, tile)):
                cp.start(priority=i // 2)
        hb_ref[...] = _rmsnorm(x_ref[...], n2g_ref[...]).astype(_BF16)
        acc_ref[...] = jnp.zeros(acc_ref.shape, _F32)

    @pl.when(jt + W_RING - 1 < FF_TILES)
    def _():
        for i, cp in enumerate(tile_copies(jt + W_RING - 1, lax.rem(jt + W_RING - 1, W_RING))):
            cp.start(priority=i // 2)

    slot = lax.rem(jt, W_RING)
    for cp in tile_copies(jt, slot):
        cp.wait()

    wgv_o_ref[:, 0:FF_TILE] = wg_ring[slot].astype(_BF16)
    wgv_o_ref[:, FF_TILE:] = wv_ring[slot].astype(_BF16)
    wdown_o_ref[...] = (0.5 * wd_ring[slot]).astype(_BF16)
    hb = hb_ref[...]
    gv = _dot(hb, wgv_o_ref[...])
    g, v = gv[:, 0:FF_TILE], gv[:, FF_TILE:]
    hist = [conv_ref[:, j, :] for j in range(CONV_HIST)]
    new_hist = hist[1:] + [g[0:Ms]]
    for j in range(CONV_HIST):
        conv_o_ref[:, j, :] = new_hist[j]
    g_ext_s = jnp.concatenate(hist + [g[0:Ms]], axis=0)
    mg_ref[0:META_G_ROW0, :] = jnp.zeros((META_G_ROW0, FF_TILE), _F32)
    mg_ref[META_G_ROW0:META_G_ROW0 + Tm, :] = g[Ms:]
    conv_m = convb_ref[...]
    for k in range(CONV_W):
        lo = META_G_ROW0 - CONV_HIST + k
        conv_m = conv_m + convw_ref[k:k + 1, :] * mg_ref[lo:lo + Tm, :]
    for j in range(CONV_HIST):
        row = META_G_ROW0 + Tm - CONV_HIST + j
        mconv_o_ref[j * Bm:(j + 1) * Bm, :] = jnp.broadcast_to(mg_ref[row:row + 1, :],
                                                              (Bm, FF_TILE))
    conv = jnp.concatenate([_conv_tile(g_ext_s, Bs, Ms, convw_ref[...], convb_ref[...]), conv_m],
                           axis=0)
    acc_ref[...] += _dot(_gelu2_mul(conv, v).astype(_BF16), wdown_o_ref[...])

    @pl.when(jt == FF_TILES - 1)
    def _():
        xo = x_ref[...] + acc_ref[...]
        if final_norm:
            xo = _rmsnorm(xo, nfg_ref[...])
        xo_ref[...] = xo


def _ffn_small(x, p, l, prev, *, Bs, Bm, Tm, final_norm):
    kern = functools.partial(_ffn_small_kernel, Bs=Bs, Bm=Bm, Tm=Tm, final_norm=final_norm,
                             n_alias=len(prev), layer=l)
    sds = jax.ShapeDtypeStruct
    n_rows = x.shape[0]
    in_specs = [
        _whole(x.shape),
        pl.BlockSpec((None, Bs, CONV_HIST, FF_TILE), lambda j: (l, 0, 0, j)),
        _layer(p["n2g"], l),
        pl.BlockSpec((None, CONV_W, FF_TILE), lambda j: (l, 0, j)),
        pl.BlockSpec((None, 1, FF_TILE), lambda j: (l, 0, j)),
        _whole(p["nfg"].shape),
        pl.BlockSpec(memory_space=pl.ANY),
        pl.BlockSpec(memory_space=pl.ANY),
    ] + [pl.BlockSpec(memory_space=pl.ANY)] * len(prev)
    out_shape = [sds(x.shape, _F32), sds(p["state_conv"].shape, _F32),
                 sds((CONV_HIST * Bm, D_FF), _F32), sds((D_MODEL, 2 * D_FF), _BF16),
                 sds((D_FF, D_MODEL), _BF16)]
    out_specs = [
        _whole(x.shape),
        pl.BlockSpec((None, Bs, CONV_HIST, FF_TILE), lambda j: (l, 0, 0, j)),
        pl.BlockSpec((CONV_HIST * Bm, FF_TILE), lambda j: (0, j)),
        pl.BlockSpec((D_MODEL, 2 * FF_TILE), lambda j: (0, j)),
        pl.BlockSpec((FF_TILE, D_MODEL), lambda j: (j, 0)),
    ]
    n_in = 8
    return pl.pallas_call(
        kern,
        grid=(FF_TILES,),
        in_specs=in_specs,
        out_specs=out_specs,
        out_shape=out_shape,
        scratch_shapes=[pltpu.VMEM((n_rows, D_MODEL), _BF16), pltpu.VMEM((n_rows, D_MODEL), _F32),
                        pltpu.VMEM((META_G_ROW0 + Tm, FF_TILE), _F32),
                        pltpu.VMEM((W_RING, D_MODEL, FF_TILE), _F32),
                        pltpu.VMEM((W_RING, D_MODEL, FF_TILE), _F32),
                        pltpu.VMEM((W_RING, FF_TILE, D_MODEL), _F32),
                        pltpu.SemaphoreType.DMA((3, W_RING))],
        input_output_aliases={n_in + k: 1 + k for k in range(len(prev))},
        compiler_params=_compiler_params(),
        name="convffn_small",
    )(x, p["state_conv"], p["n2g"], p["conv_w"], p["conv_b"], p["nfg"], p["w_up"], p["w_down"],
      *prev)


def kernel(x_prompt, x_sample, state_pool, state_ssm_re, state_ssm_im, state_conv, meta_tokens,
           norm1_g, w_in, pool_w, pool_scale, ssm_A_re, ssm_A_im, ssm_log_dt, ssm_B_re, ssm_B_im,
           ssm_C_re, ssm_C_im, ssm_D, w_glu, w_out, norm2_g, w_up, conv_w, conv_b, w_down, norm_f_g):
    depth = w_in.shape[0]
    bp, seq, _ = x_prompt.shape
    bs = x_sample.shape[0]
    assert seq % MIXER_CHUNK_STEPS == 0 and seq % FFN_CHUNK_STEPS == 0
    state_shapes = [(depth, POOL_HIST, bs, POOL_WIDTH), (depth, SSM_FLAT, bs), (depth, SSM_FLAT, bs),
                    state_conv.shape]
    ab_re, ab_im, bbc, cre, cim, *new_states = _discretise(
        ssm_A_re, ssm_A_im, ssm_log_dt, ssm_B_re, ssm_B_im, ssm_C_re, ssm_C_im, state_shapes)
    row = lambda a: a[:, None, :]
    p = dict(state_pool=state_pool.transpose(0, 2, 1, 3), state_conv=state_conv,
             state_re=state_ssm_re.transpose(0, 2, 3, 1).reshape(depth, SSM_FLAT, bs),
             state_im=state_ssm_im.transpose(0, 2, 3, 1).reshape(depth, SSM_FLAT, bs),
             n1g=row(norm1_g), pool_scale=row(pool_scale), a_re=ab_re, a_im=ab_im, bbc=bbc,
             cre=cre, cim=cim, dskip=row(ssm_D), w_in=w_in, pool_w=pool_w, w_glu=w_glu,
             w_out=w_out, n2g=row(norm2_g), conv_w=conv_w, conv_b=row(conv_b), w_up=w_up,
             w_down=w_down, nfg=norm_f_g[None])

    xs = jnp.concatenate([x_sample.reshape(bs, D_MODEL), meta_tokens], axis=0)
    small = dict(Bs=bs, Bm=bp, Tm=N_META)
    mix_prev, ffn_prev = new_states[0:3], new_states[3:4]
    meta_state, mix_w, ffn_w = [], [], []
    for l in range(depth):
        xs, pool_s, re_s, im_s, mp, mr, mi, *wb = _mixer_small(xs, p, l, mix_prev, **small)
        mix_prev = [pool_s, re_s, im_s]
        mix_w.append(wb)
        xs, conv_s, mc, *wb = _ffn_small(xs, p, l, ffn_prev, final_norm=(l == depth - 1), **small)
        ffn_prev = [conv_s]
        ffn_w.append(wb)
        meta_state.append(((mp, mr, mi), mc))
    y_sample = xs[0:bs].reshape(bs, 1, D_MODEL)

    xp = x_prompt
    pool_p, re_p, im_p, conv_p = [], [], [], []
    for l in range(depth):
        mix_state, mc = meta_state[l]
        xp, pp, hr, hi = _mixer_main(xp, mix_state, p, l, mix_w[l], start_pos=N_META,
                                     batch_major_in=(l == 0), B=bp, T=MIXER_CHUNK_STEPS)
        last = l == depth - 1
        xp, cv = _ffn_main(xp, mc, p, l, ffn_w[l], final_norm=last, batch_major_out=last, B=bp,
                           T=FFN_CHUNK_STEPS)
        pool_p.append(pp)
        re_p.append(hr)
        im_p.append(hi)
        conv_p.append(cv)
    st = jnp.stack
    pool_p = st(pool_p).reshape(depth, POOL_KEEP, bp, POOL_WIDTH)[:, 1:].transpose(0, 2, 1, 3)
    conv_p = st(conv_p).reshape(depth, CONV_HIST, bp, D_FF).transpose(0, 2, 1, 3)
    group = lambda h: h.reshape(h.shape[:-1] + (SSM_GROUPS, SSM_STATE))
    sample_ssm = lambda h: h.reshape(depth, SSM_GROUPS, SSM_STATE, bs).transpose(0, 3, 1, 2)
    return (xp, y_sample, pool_p, group(st(re_p)), group(st(im_p)), conv_p,
            pool_s.transpose(0, 2, 1, 3), sample_ssm(re_s), sample_ssm(im_s), conv_s)
```
